```python
import math
import jax, jax.numpy as jnp
from jax import lax
import numpy as np

D_MODEL = 2048
BATCH = 2
SEQ = 4096
DEPTH = 1

A_HEADS = 16
A_KV_HEADS = 4
A_HEAD_DIM = 64
A_GROUP = A_HEADS // A_KV_HEADS
WINDOW = 128
A_BLOCK = 128
A_Q_WIDTH = A_HEADS * A_HEAD_DIM
A_KV_WIDTH = A_KV_HEADS * A_HEAD_DIM
N_BUCKETS = 32
MAX_DISTANCE = 128
B_HEADS = 4
B_KEY_DIM = 128
B_VAL_DIM = 256
B_QK_WIDTH = B_HEADS * B_KEY_DIM
B_V_WIDTH = B_HEADS * B_VAL_DIM
B_GATE_RANK = 16
B_GATE_TAU = 16.0
B_CHUNK = 64
D_FF = -(-(8 * D_MODEL) // (3 * 256)) * 256
EPS = 1e-6
NEG_INF = -1e30

IN_SPLITS = (A_Q_WIDTH, A_KV_WIDTH, A_KV_WIDTH,
             B_QK_WIDTH, B_QK_WIDTH, B_V_WIDTH, B_GATE_RANK, B_V_WIDTH,
             D_MODEL, D_MODEL)
IN_WIDTH = sum(IN_SPLITS)

kernel_name = "hybrid_swa_sink_gla_gated_merge"


def rmsnorm(x, g):
    xf = x.astype(jnp.float32)
    y = xf * lax.rsqrt(jnp.mean(xf * xf, axis=-1, keepdims=True) + EPS)
    return (y * g.astype(jnp.float32)).astype(x.dtype)


def t5_causal_bucket(dist):
    max_exact = N_BUCKETS // 2
    d = jnp.maximum(dist, 0)
    large = max_exact + (jnp.log(jnp.maximum(d, 1).astype(jnp.float32) / max_exact)
                         / math.log(MAX_DISTANCE / max_exact)
                         * (N_BUCKETS - max_exact)).astype(jnp.int32)
    large = jnp.minimum(large, N_BUCKETS - 1)
    return jnp.where(d < max_exact, d, large)


def sliding_window_attention(q, k, v, sinks, rel_bias):
    B, S = q.shape[0], q.shape[1]
    nb = S // A_BLOCK
    qb = (q * (A_HEAD_DIM ** -0.5)).reshape(B, nb, A_BLOCK, A_KV_HEADS, A_GROUP, A_HEAD_DIM)
    pad = ((0, 0), (A_BLOCK, 0), (0, 0), (0, 0))
    kp = jnp.pad(k, pad).reshape(B, nb + 1, A_BLOCK, A_KV_HEADS, A_HEAD_DIM)
    vp = jnp.pad(v, pad).reshape(B, nb + 1, A_BLOCK, A_KV_HEADS, A_HEAD_DIM)
    kb = jnp.concatenate([kp[:, :-1], kp[:, 1:]], axis=2)
    vb = jnp.concatenate([vp[:, :-1], vp[:, 1:]], axis=2)
    s = jnp.einsum('bnqkgd,bnskd->bnkgqs', qb, kb).astype(jnp.float32)
    q_loc = jnp.arange(A_BLOCK)[:, None] + A_BLOCK
    k_loc = jnp.arange(2 * A_BLOCK)[None, :]
    dist = q_loc - k_loc
    band = (dist >= 0) & (dist < WINDOW)
    k_abs = jnp.arange(nb)[:, None, None] * A_BLOCK + k_loc[None] - A_BLOCK
    mask = band[None] & (k_abs >= 0)
    bias = rel_bias.astype(jnp.float32)[t5_causal_bucket(dist)]
    bias = bias.transpose(2, 0, 1).reshape(A_KV_HEADS, A_GROUP, A_BLOCK, 2 * A_BLOCK)
    s = jnp.where(mask[None, :, None, None], s + bias[None, None], NEG_INF)
    sink = sinks.astype(jnp.float32).reshape(A_KV_HEADS, A_GROUP)[None, None, :, :, None, None]
    m = jnp.maximum(jnp.max(s, axis=-1, keepdims=True), sink)
    p = jnp.exp(s - m)
    denom = jnp.sum(p, axis=-1, keepdims=True) + jnp.exp(sink - m)
    o = jnp.einsum('bnkgqs,bnskd->bnqkgd', (p / denom).astype(v.dtype), vb)
    return o.reshape(B, S, A_Q_WIDTH)


def gated_linear_attention(q, k, v, log_a):
    B, S = q.shape[0], q.shape[1]
    nc = S // B_CHUNK

    def chunks(t):
        return t.astype(jnp.float32).reshape(B, nc, B_CHUNK, B_HEADS, -1).transpose(0, 3, 1, 2, 4)

    qc = chunks(q) * (B_KEY_DIM ** -0.5)
    kc, vc, gc = chunks(k), chunks(v), chunks(log_a)
    b = jnp.cumsum(gc, axis=3)
    b_last = b[:, :, :, -1:, :]
    q_dec = qc * jnp.exp(b)
    k_dec = kc * jnp.exp(-b)
    k_state = kc * jnp.exp(b_last - b)
    causal = jnp.tril(jnp.ones((B_CHUNK, B_CHUNK), dtype=bool))
    att = jnp.where(causal, jnp.einsum('bhncd,bhnsd->bhncs', q_dec, k_dec), 0.0)
    o_intra = jnp.einsum('bhncs,bhnsv->bhncv', att, vc)
    dS = jnp.einsum('bhncd,bhncv->bhndv', k_state, vc)
    decay = jnp.exp(b_last[:, :, :, 0, :])

    def step(state, inp):
        dS_n, decay_n = inp
        return decay_n[..., None] * state + dS_n, state

    s0 = jnp.zeros((B, B_HEADS, B_KEY_DIM, B_VAL_DIM), jnp.float32)
    _, s_prev = lax.scan(step, s0, (dS.transpose(2, 0, 1, 3, 4), decay.transpose(2, 0, 1, 3)))
    s_prev = s_prev.transpose(1, 2, 0, 3, 4)
    o_inter = jnp.einsum('bhncd,bhndv->bhncv', q_dec, s_prev)
    o = o_intra + o_inter
    return o.transpose(0, 2, 3, 1, 4).reshape(B, S, B_HEADS, B_VAL_DIM)


def setup_inputs(seed: int = 0) -> dict:
    key = jax.random.key(seed)
    ks = jax.random.split(key, 20)
    L = DEPTH

    def w(k, shape, fan_in):
        return jax.random.normal(k, shape, jnp.float32) * (fan_in ** -0.5)

    def gain(k, shape):
        return 1.0 + 0.02 * jax.random.normal(k, shape, jnp.float32)

    return {
        "x": jax.random.normal(ks[0], (BATCH, SEQ, D_MODEL), jnp.float32),
        "norm_mix_g": gain(ks[1], (L, D_MODEL)),
        "w_in": w(ks[2], (L, D_MODEL, IN_WIDTH), D_MODEL),
        "sinks": 0.5 * jax.random.normal(ks[3], (L, A_HEADS), jnp.float32),
        "rel_bias": 0.5 * jax.random.normal(ks[4], (N_BUCKETS, A_HEADS), jnp.float32),
        "w_gate_up": w(ks[5], (L, B_GATE_RANK, B_QK_WIDTH), B_GATE_RANK),
        "b_gate": 0.1 * jax.random.normal(ks[6], (L, B_QK_WIDTH), jnp.float32),
        "gla_norm_g": gain(ks[7], (L, B_VAL_DIM)),
        "w_proj_a": w(ks[8], (L, A_Q_WIDTH, D_MODEL), A_Q_WIDTH),
        "w_proj_b": w(ks[9], (L, B_V_WIDTH, D_MODEL), B_V_WIDTH),
        "w_out": w(ks[10], (L, D_MODEL, D_MODEL), D_MODEL),
        "norm_ffn_g": gain(ks[11], (L, D_MODEL)),
        "w_ffn_gate": w(ks[12], (L, D_MODEL, D_FF), D_MODEL),
        "w_ffn_up": w(ks[13], (L, D_MODEL, D_FF), D_MODEL),
        "w_ffn_down": w(ks[14], (L, D_FF, D_MODEL), D_FF),
        "norm_final_g": gain(ks[15], (D_MODEL,)),
    }


def reference(x, norm_mix_g, w_in, sinks, rel_bias, w_gate_up, b_gate, gla_norm_g,
              w_proj_a, w_proj_b, w_out, norm_ffn_g, w_ffn_gate, w_ffn_up, w_ffn_down,
              norm_final_g):
    B, S = x.shape[0], x.shape[1]
    split_idx = [int(i) for i in np.cumsum(IN_SPLITS)[:-1]]
    h = x
    for l in range(DEPTH):
        u = rmsnorm(h, norm_mix_g[l])
        proj = u @ w_in[l]
        qa, ka, va, qb, kb, vb, g_low, ob_gate, gate_a, gate_b = jnp.split(proj, split_idx, axis=-1)
        ya = sliding_window_attention(
            qa.reshape(B, S, A_HEADS, A_HEAD_DIM),
            ka.reshape(B, S, A_KV_HEADS, A_HEAD_DIM),
            va.reshape(B, S, A_KV_HEADS, A_HEAD_DIM),
            sinks[l], rel_bias)
        ya = ya @ w_proj_a[l]
        log_a = jax.nn.log_sigmoid((g_low @ w_gate_up[l] + b_gate[l]).astype(jnp.float32)) / B_GATE_TAU
        ob = gated_linear_attention(
            qb.reshape(B, S, B_HEADS, B_KEY_DIM),
            kb.reshape(B, S, B_HEADS, B_KEY_DIM),
            vb.reshape(B, S, B_HEADS, B_VAL_DIM),
            log_a.reshape(B, S, B_HEADS, B_KEY_DIM))
        ob = rmsnorm(ob, gla_norm_g[l]).reshape(B, S, B_V_WIDTH).astype(x.dtype)
        yb = (ob * jax.nn.silu(ob_gate)) @ w_proj_b[l]
        merged = jax.nn.sigmoid(gate_a) * ya + jax.nn.sigmoid(gate_b) * yb
        h = h + (merged @ w_out[l]).astype(h.dtype)
        z = rmsnorm(h, norm_ffn_g[l])
        ff = (jax.nn.silu(z @ w_ffn_gate[l]) * (z @ w_ffn_up[l])) @ w_ffn_down[l]
        h = h + ff.astype(h.dtype)
    return rmsnorm(h, norm_final_g)
```

```python
import functools
import math

import numpy as np
import jax
import jax.numpy as jnp
from jax import lax
from jax.experimental import pallas as pl
from jax.experimental.pallas import tpu as pltpu

F32 = jnp.float32
BF16 = jnp.bfloat16

D_MODEL = 2048
A_HEADS = 16
A_KV_HEADS = 4
A_HEAD_DIM = 64
A_GROUP = A_HEADS // A_KV_HEADS
WINDOW = 128
A_BLOCK = 128
A_Q_WIDTH = A_HEADS * A_HEAD_DIM
A_KV_WIDTH = A_KV_HEADS * A_HEAD_DIM
N_BUCKETS = 32
MAX_DISTANCE = 128
B_HEADS = 4
B_KEY_DIM = 128
B_VAL_DIM = 256
B_QK_WIDTH = B_HEADS * B_KEY_DIM
B_V_WIDTH = B_HEADS * B_VAL_DIM
B_GATE_RANK = 16
B_GATE_TAU = 16.0
B_CHUNK = 64
EPS = 1e-6
NEG_INF = -1e30

LANES = 128

OFF_GATE_A = 0
OFF_GATE_B = OFF_GATE_A + D_MODEL
OFF_QA = OFF_GATE_B + D_MODEL
OFF_VB = OFF_QA + A_Q_WIDTH
OFF_OBG = OFF_VB + B_V_WIDTH
OFF_QB = OFF_OBG + B_V_WIDTH
OFF_KB = OFF_QB + B_QK_WIDTH
OFF_KA = OFF_KB + B_QK_WIDTH
OFF_VA = OFF_KA + A_KV_WIDTH
OFF_GLOW = OFF_VA + A_KV_WIDTH
PROJ_USED = OFF_GLOW + LANES

VMEM_LIMIT = 56 * 1024 * 1024


def _cparams(sem):
    return pltpu.CompilerParams(dimension_semantics=sem, vmem_limit_bytes=VMEM_LIMIT)


def _const_spec(shape):
    return pl.BlockSpec(shape, lambda *_: (0,) * len(shape), pipeline_mode=pl.Buffered(1))


def _rmsnorm_rows(x, g):
    ms = jnp.mean(x * x, axis=-1, keepdims=True)
    return x * lax.rsqrt(ms + EPS) * g


NORM_ROWS = 128


def _inproj_kernel(x_ref, g_ref, w_ref, o_ref, u_ref):
    @pl.when(pl.program_id(1) == 0)
    def _():
        g = g_ref[...]

        def body(c, carry):
            rows = pl.ds(pl.multiple_of(c * NORM_ROWS, NORM_ROWS), NORM_ROWS)
            u_ref[rows, :] = _rmsnorm_rows(x_ref[rows, :], g).astype(BF16)
            return carry

        lax.fori_loop(0, x_ref.shape[0] // NORM_ROWS, body, 0)

    o_ref[...] = jnp.dot(u_ref[...], w_ref[...], preferred_element_type=F32).astype(o_ref.dtype)


def _inproj(x2, g, w, tm, tn):
    t, d = x2.shape
    n = w.shape[1]
    return pl.pallas_call(
        _inproj_kernel,
        grid=(t // tm, n // tn),
        in_specs=[
            pl.BlockSpec((tm, d), lambda i, j: (i, 0)),
            pl.BlockSpec((1, d), lambda i, j: (0, 0)),
            pl.BlockSpec((d, tn), lambda i, j: (0, j)),
        ],
        out_specs=pl.BlockSpec((tm, tn), lambda i, j: (i, j)),
        out_shape=jax.ShapeDtypeStruct((t, n), BF16),
        scratch_shapes=[pltpu.VMEM((tm, d), BF16)],
        compiler_params=_cparams(("parallel", "arbitrary")),
        name="inproj",
    )(x2, g, w)


def _bucket_starts():
    max_exact = N_BUCKETS // 2
    d = np.arange(WINDOW)
    large = max_exact + (np.log(np.maximum(d, 1).astype(np.float32) / max_exact)
                         / math.log(MAX_DISTANCE / max_exact)
                         * (N_BUCKETS - max_exact)).astype(np.int32)
    bucket = np.where(d < max_exact, d, np.minimum(large, N_BUCKETS - 1))
    starts = []
    for b in range(N_BUCKETS):
        hit = np.nonzero(bucket == b)[0]
        if hit.size:
            assert np.all(np.diff(hit) == 1)
            starts.append((b, int(hit[0])))
    return starts


def _swa_kernel(nb, q_ref, kp_ref, kc_ref, vp_ref, vc_ref, sink_ref, rb_ref, o_ref, bias_ref):
    r = pl.program_id(0)
    row = lax.broadcasted_iota(jnp.int32, (A_BLOCK, 2 * A_BLOCK), 0)
    col = lax.broadcasted_iota(jnp.int32, (A_BLOCK, 2 * A_BLOCK), 1)
    dist = row + A_BLOCK - col

    @pl.when(r == 0)
    def _():
        starts = _bucket_starts()
        for h in range(A_HEADS):
            val = jnp.full(dist.shape, rb_ref[starts[0][0], h], F32)
            for b, s in starts[1:]:
                val = jnp.where(dist >= s, rb_ref[b, h], val)
            bias_ref[h] = val

    first_col = jnp.where(lax.rem(r, nb) == 0, A_BLOCK, 0)
    mask = (dist >= 0) & (dist < WINDOW) & (col >= first_col)
    mask_g = jnp.concatenate([mask] * A_GROUP, axis=0)

    q = q_ref[...]
    outs = []
    for kh in range(A_KV_HEADS):
        ks = slice(kh * A_HEAD_DIM, (kh + 1) * A_HEAD_DIM)
        k_cat = jnp.concatenate([kp_ref[:, ks], kc_ref[:, ks]], axis=0)
        v_cat = jnp.concatenate([vp_ref[:, ks], vc_ref[:, ks]], axis=0)
        heads = range(kh * A_GROUP, (kh + 1) * A_GROUP)
        q_g = jnp.concatenate([q[:, h * A_HEAD_DIM:(h + 1) * A_HEAD_DIM] for h in heads], axis=0)
        s = lax.dot_general(q_g, k_cat, (((1,), (1,)), ((), ())), preferred_element_type=F32)
        bias_g = jnp.concatenate([bias_ref[h] for h in heads], axis=0)
        s = jnp.where(mask_g, s * (A_HEAD_DIM ** -0.5) + bias_g, NEG_INF)
        sink_g = jnp.concatenate(
            [jnp.full((A_BLOCK, 1), sink_ref[0, h], F32) for h in heads], axis=0)
        m = jnp.maximum(jnp.max(s, axis=-1, keepdims=True), sink_g)
        p = jnp.exp(s - m)
        denom = jnp.sum(p, axis=-1, keepdims=True) + jnp.exp(sink_g - m)
        o = jnp.dot(p.astype(BF16), v_cat, preferred_element_type=F32) / denom
        outs.extend(o[g * A_BLOCK:(g + 1) * A_BLOCK] for g in range(A_GROUP))
    o_ref[...] = jnp.concatenate(outs, axis=1).astype(o_ref.dtype)


def _swa(proj, sinks, rel_bias, batch, seq):
    t = proj.shape[0]
    nb = seq // A_BLOCK
    qcol = OFF_QA // A_Q_WIDTH
    kcol = OFF_KA // A_KV_WIDTH
    vcol = OFF_VA // A_KV_WIDTH
    prev = lambda r: jnp.maximum(r - 1, 0)
    smem = functools.partial(pl.BlockSpec, memory_space=pltpu.SMEM)
    return pl.pallas_call(
        functools.partial(_swa_kernel, nb),
        grid=(t // A_BLOCK,),
        in_specs=[
            pl.BlockSpec((A_BLOCK, A_Q_WIDTH), lambda r: (r, qcol)),
            pl.BlockSpec((A_BLOCK, A_KV_WIDTH), lambda r: (prev(r), kcol)),
            pl.BlockSpec((A_BLOCK, A_KV_WIDTH), lambda r: (r, kcol)),
            pl.BlockSpec((A_BLOCK, A_KV_WIDTH), lambda r: (prev(r), vcol)),
            pl.BlockSpec((A_BLOCK, A_KV_WIDTH), lambda r: (r, vcol)),
            smem(), smem(),
        ],
        out_specs=pl.BlockSpec((A_BLOCK, A_Q_WIDTH), lambda r: (r, 0)),
        out_shape=jax.ShapeDtypeStruct((t, A_Q_WIDTH), BF16),
        scratch_shapes=[pltpu.VMEM((A_HEADS, A_BLOCK, 2 * A_BLOCK), F32)],
        compiler_params=_cparams(("arbitrary",)),
        name="swa",
    )(proj, proj, proj, proj, proj, sinks, rel_bias)


GLA_CHUNKS_PER_STEP = 4


def _split3(x):
    hi = x.astype(BF16)
    r1 = x - hi.astype(F32)
    mid = r1.astype(BF16)
    lo = (r1 - mid.astype(F32)).astype(BF16)
    return jnp.concatenate([hi, mid, lo], axis=0)


def _gla_kernel(q_ref, k_ref, v_ref, gl_ref, og_ref, wgu_ref, bg_ref, ng_ref, o_ref, s_ref):
    c = B_CHUNK

    @pl.when(pl.program_id(1) == 0)
    def _():
        s_ref[...] = jnp.zeros_like(s_ref)

    ri = lax.broadcasted_iota(jnp.int32, (c, c), 0)
    ci = lax.broadcasted_iota(jnp.int32, (c, c), 1)
    causal = ri >= ci
    ri3 = lax.broadcasted_iota(jnp.int32, (c, 3 * c), 0)
    ci3 = lax.broadcasted_iota(jnp.int32, (c, 3 * c), 1)
    ci3 = ci3 - jnp.where(ci3 >= c, c, 0) - jnp.where(ci3 >= 2 * c, c, 0)
    tri3 = (ri3 >= ci3).astype(BF16)
    ng = ng_ref[...]

    glin = jnp.dot(gl_ref[...], wgu_ref[...], preferred_element_type=F32) + bg_ref[...]
    log_a = (jnp.minimum(glin, 0.0) - jnp.log1p(jnp.exp(-jnp.abs(glin)))) / B_GATE_TAU

    for j in range(GLA_CHUNKS_PER_STEP):
        rows = slice(j * c, (j + 1) * c)
        for h in range(B_HEADS):
            kcols = slice(h * B_KEY_DIM, (h + 1) * B_KEY_DIM)
            vcols = slice(h * B_VAL_DIM, (h + 1) * B_VAL_DIM)
            g = log_a[rows, kcols]
            b = jnp.dot(tri3, _split3(g), preferred_element_type=F32)
            b_last = b[c - 1:c, :]
            qf = q_ref[rows, kcols].astype(F32) * (B_KEY_DIM ** -0.5)
            kf = k_ref[rows, kcols].astype(F32)
            vv = v_ref[rows, vcols]
            q_dec = (qf * jnp.exp(b)).astype(BF16)
            k_dec = (kf * jnp.exp(-b)).astype(BF16)
            k_state = kf * jnp.exp(b_last - b)
            att = lax.dot_general(q_dec, k_dec, (((1,), (1,)), ((), ())),
                                  preferred_element_type=F32)
            att = jnp.where(causal, att, 0.0).astype(BF16)
            state = s_ref[h]
            o = (jnp.dot(att, vv, preferred_element_type=F32)
                 + jnp.dot(q_dec, state.astype(BF16), preferred_element_type=F32))
            ds = jnp.dot(k_state.T.astype(BF16), vv, preferred_element_type=F32)
            decay = jnp.exp(jnp.broadcast_to(b_last, (8, B_KEY_DIM))).T[:, :1]
            s_ref[h] = decay * state + ds
            y = _rmsnorm_rows(o, ng)
            gate = og_ref[rows, vcols].astype(F32)
            o_ref[rows, vcols] = (y * (gate * jax.nn.sigmoid(gate))).astype(o_ref.dtype)


def _gla(proj, wgu, bg, ng, batch, seq):
    t = proj.shape[0]
    rows = GLA_CHUNKS_PER_STEP * B_CHUNK
    steps = seq // rows
    rb = lambda b, s: b * steps + s
    return pl.pallas_call(
        _gla_kernel,
        grid=(batch, steps),
        in_specs=[
            pl.BlockSpec((rows, B_QK_WIDTH), lambda b, s: (rb(b, s), OFF_QB // B_QK_WIDTH)),
            pl.BlockSpec((rows, B_QK_WIDTH), lambda b, s: (rb(b, s), OFF_KB // B_QK_WIDTH)),
            pl.BlockSpec((rows, B_V_WIDTH), lambda b, s: (rb(b, s), OFF_VB // B_V_WIDTH)),
            pl.BlockSpec((rows, LANES), lambda b, s: (rb(b, s), OFF_GLOW // LANES)),
            pl.BlockSpec((rows, B_V_WIDTH), lambda b, s: (rb(b, s), OFF_OBG // B_V_WIDTH)),
            _const_spec(wgu.shape), _const_spec(bg.shape), _const_spec(ng.shape),
        ],
        out_specs=pl.BlockSpec((rows, B_V_WIDTH), lambda b, s: (rb(b, s), 0)),
        out_shape=jax.ShapeDtypeStruct((t, B_V_WIDTH), BF16),
        scratch_shapes=[pltpu.VMEM((B_HEADS, B_KEY_DIM, B_VAL_DIM), F32)],
        compiler_params=_cparams(("parallel", "arbitrary")),
        name="gla",
    )(proj, proj, proj, proj, proj, wgu, bg, ng)


def _merge_kernel(a_ref, b_ref, ga_ref, gb_ref, x_ref, wa_ref, wb_ref, wo_ref, h_ref):
    ya = jnp.dot(a_ref[...], wa_ref[...], preferred_element_type=F32)
    yb = jnp.dot(b_ref[...], wb_ref[...], preferred_element_type=F32)
    merged = (jax.nn.sigmoid(ga_ref[...].astype(F32)) * ya
              + jax.nn.sigmoid(gb_ref[...].astype(F32)) * yb)
    h_ref[...] = x_ref[...] + jnp.dot(merged.astype(BF16), wo_ref[...], preferred_element_type=F32)


def _merge(attn, gla, proj, x2, wa, wb, wo, tm):
    t, d = x2.shape
    return pl.pallas_call(
        _merge_kernel,
        grid=(t // tm,),
        in_specs=[
            pl.BlockSpec((tm, A_Q_WIDTH), lambda i: (i, 0)),
            pl.BlockSpec((tm, B_V_WIDTH), lambda i: (i, 0)),
            pl.BlockSpec((tm, d), lambda i: (i, OFF_GATE_A // D_MODEL)),
            pl.BlockSpec((tm, d), lambda i: (i, OFF_GATE_B // D_MODEL)),
            pl.BlockSpec((tm, d), lambda i: (i, 0)),
            _const_spec(wa.shape), _const_spec(wb.shape), _const_spec(wo.shape),
        ],
        out_specs=pl.BlockSpec((tm, d), lambda i: (i, 0)),
        out_shape=jax.ShapeDtypeStruct((t, d), F32),
        compiler_params=_cparams(("parallel",)),
        name="merge",
    )(attn, gla, proj, proj, x2, wa, wb, wo)


def _ffn_kernel(h_ref, gz_ref, wg_ref, wu_ref, wd_ref, gf_ref, o_ref, z_ref, acc_ref):
    f = pl.program_id(1)
    n_chunks = h_ref.shape[0] // NORM_ROWS

    @pl.when(f == 0)
    def _():
        gz = gz_ref[...]

        def body(c, carry):
            rows = pl.ds(pl.multiple_of(c * NORM_ROWS, NORM_ROWS), NORM_ROWS)
            z_ref[rows, :] = _rmsnorm_rows(h_ref[rows, :], gz).astype(BF16)
            return carry

        lax.fori_loop(0, n_chunks, body, 0)
        acc_ref[...] = jnp.zeros_like(acc_ref)

    z = z_ref[...]
    g = jnp.dot(z, wg_ref[...], preferred_element_type=F32)
    u = jnp.dot(z, wu_ref[...], preferred_element_type=F32)
    act = (g * jax.nn.sigmoid(g) * u).astype(BF16)
    acc_ref[...] += jnp.dot(act, wd_ref[...], preferred_element_type=F32)

    @pl.when(f == pl.num_programs(1) - 1)
    def _():
        gf = gf_ref[...]

        def body(c, carry):
            rows = pl.ds(pl.multiple_of(c * NORM_ROWS, NORM_ROWS), NORM_ROWS)
            o_ref[rows, :] = _rmsnorm_rows(h_ref[rows, :] + acc_ref[rows, :], gf)
            return carry

        lax.fori_loop(0, n_chunks, body, 0)


def _ffn(h, gz, wg, wu, wd, gf, tm, tf):
    t, d = h.shape
    f = wg.shape[1]
    return pl.pallas_call(
        _ffn_kernel,
        grid=(t // tm, f // tf),
        in_specs=[
            pl.BlockSpec((tm, d), lambda i, j: (i, 0)),
            pl.BlockSpec((1, d), lambda i, j: (0, 0)),
            pl.BlockSpec((d, tf), lambda i, j: (0, j)),
            pl.BlockSpec((d, tf), lambda i, j: (0, j)),
            pl.BlockSpec((tf, d), lambda i, j: (j, 0)),
            pl.BlockSpec((1, d), lambda i, j: (0, 0)),
        ],
        out_specs=pl.BlockSpec((tm, d), lambda i, j: (i, 0)),
        out_shape=jax.ShapeDtypeStruct((t, d), F32),
        scratch_shapes=[pltpu.VMEM((tm, d), BF16), pltpu.VMEM((tm, d), F32)],
        compiler_params=_cparams(("parallel", "arbitrary")),
        name="ffn",
    )(h, gz, wg, wu, wd, gf)


def _pack_w_in(w, n_total):
    splits = (A_Q_WIDTH, A_KV_WIDTH, A_KV_WIDTH, B_QK_WIDTH, B_QK_WIDTH, B_V_WIDTH,
              B_GATE_RANK, B_V_WIDTH, D_MODEL, D_MODEL)
    qa, ka, va, qb, kb, vb, glow, obg, gate_a, gate_b = jnp.split(
        w, [int(i) for i in np.cumsum(splits)[:-1]], axis=1)
    d = w.shape[0]
    pad = jnp.zeros((d, n_total - PROJ_USED + LANES - B_GATE_RANK), w.dtype)
    return jnp.concatenate([gate_a, gate_b, qa, vb, obg, qb, kb, ka, va, glow, pad],
                           axis=1).astype(BF16)


def kernel(x, norm_mix_g, w_in, sinks, rel_bias, w_gate_up, b_gate, gla_norm_g, w_proj_a, w_proj_b,
           w_out, norm_ffn_g, w_ffn_gate, w_ffn_up, w_ffn_down, norm_final_g):
    batch, seq, d = x.shape
    assert d == D_MODEL and w_in.shape[0] == 1, "single-layer geometry"
    t = batch * seq
    x2 = x.reshape(t, d)

    tn_in = 1024
    n_total = -(-PROJ_USED // tn_in) * tn_in
    w_in_p = _pack_w_in(w_in[0], n_total)
    proj = _inproj(x2, norm_mix_g, w_in_p, tm=1024, tn=tn_in)

    attn = _swa(proj, sinks, rel_bias, batch, seq)

    wgu = jnp.zeros((LANES, B_QK_WIDTH), BF16).at[:B_GATE_RANK].set(w_gate_up[0].astype(BF16))
    gla = _gla(proj, wgu, b_gate, gla_norm_g, batch, seq)

    h = _merge(attn, gla, proj, x2, w_proj_a[0].astype(BF16), w_proj_b[0].astype(BF16),
               w_out[0].astype(BF16), tm=512)

    out = _ffn(h, norm_ffn_g, w_ffn_gate[0].astype(BF16), w_ffn_up[0].astype(BF16),
               w_ffn_down[0].astype(BF16), norm_final_g.reshape(1, d), tm=512, tf=512)
    return out.reshape(batch, seq, d)
```

```python
import functools
import math

import numpy as np
import jax
import jax.numpy as jnp
from jax import lax
from jax.experimental import pallas as pl
from jax.experimental.pallas import tpu as pltpu

F32 = jnp.float32
BF16 = jnp.bfloat16

D_MODEL = 2048
A_HEADS = 16
A_KV_HEADS = 4
A_HEAD_DIM = 64
A_GROUP = A_HEADS // A_KV_HEADS
WINDOW = 128
A_BLOCK = 128
A_Q_WIDTH = A_HEADS * A_HEAD_DIM
A_KV_WIDTH = A_KV_HEADS * A_HEAD_DIM
N_BUCKETS = 32
MAX_DISTANCE = 128
B_HEADS = 4
B_KEY_DIM = 128
B_VAL_DIM = 256
B_QK_WIDTH = B_HEADS * B_KEY_DIM
B_V_WIDTH = B_HEADS * B_VAL_DIM
B_GATE_RANK = 16
B_GATE_TAU = 16.0
B_CHUNK = 64
EPS = 1e-6
NEG_INF = -1e30

LANES = 128

OFF_GATE_A = 0
OFF_GATE_B = OFF_GATE_A + D_MODEL
OFF_QA = OFF_GATE_B + D_MODEL
OFF_KA = OFF_QA + A_Q_WIDTH
OFF_VA = OFF_KA + A_KV_WIDTH
OFF_QB = OFF_VA + A_KV_WIDTH
OFF_KB = OFF_QB + B_QK_WIDTH
OFF_VB = OFF_KB + B_QK_WIDTH
OFF_OBG = OFF_VB + B_V_WIDTH
OFF_GLOW = OFF_OBG + B_V_WIDTH
PROJ_USED = OFF_GLOW + LANES
HALF_V = B_V_WIDTH // 2

VMEM_LIMIT = 56 * 1024 * 1024


def _cparams(sem):
    return pltpu.CompilerParams(dimension_semantics=sem, vmem_limit_bytes=VMEM_LIMIT)


def _const_spec(shape):
    return pl.BlockSpec(shape, lambda *_: (0,) * len(shape), pipeline_mode=pl.Buffered(1))


def _rmsnorm_rows(x, g):
    ms = jnp.mean(x * x, axis=-1, keepdims=True)
    return x * lax.rsqrt(ms + EPS) * g


NORM_ROWS = 128


def _inproj_kernel(x_ref, g_ref, w_ref, o_ref, u_ref):
    @pl.when(pl.program_id(1) == 0)
    def _():
        g = g_ref[...]

        def body(c, carry):
            rows = pl.ds(pl.multiple_of(c * NORM_ROWS, NORM_ROWS), NORM_ROWS)
            u_ref[rows, :] = _rmsnorm_rows(x_ref[rows, :], g).astype(BF16)
            return carry

        lax.fori_loop(0, x_ref.shape[0] // NORM_ROWS, body, 0)

    o_ref[...] = jnp.dot(u_ref[...], w_ref[...], preferred_element_type=F32).astype(o_ref.dtype)


def _inproj(x2, g, w, tm, tn):
    t, d = x2.shape
    n = w.shape[1]
    return pl.pallas_call(
        _inproj_kernel,
        grid=(t // tm, n // tn),
        in_specs=[
            pl.BlockSpec((tm, d), lambda i, j: (i, 0)),
            pl.BlockSpec((1, d), lambda i, j: (0, 0)),
            pl.BlockSpec((d, tn), lambda i, j: (0, j)),
        ],
        out_specs=pl.BlockSpec((tm, tn), lambda i, j: (i, j)),
        out_shape=jax.ShapeDtypeStruct((t, n), BF16),
        scratch_shapes=[pltpu.VMEM((tm, d), BF16)],
        compiler_params=_cparams(("parallel", "arbitrary")),
        name="inproj",
    )(x2, g, w)


def _bucket_starts():
    max_exact = N_BUCKETS // 2
    d = np.arange(WINDOW)
    large = max_exact + (np.log(np.maximum(d, 1).astype(np.float32) / max_exact)
                         / math.log(MAX_DISTANCE / max_exact)
                         * (N_BUCKETS - max_exact)).astype(np.int32)
    bucket = np.where(d < max_exact, d, np.minimum(large, N_BUCKETS - 1))
    starts = []
    for b in range(N_BUCKETS):
        hit = np.nonzero(bucket == b)[0]
        if hit.size:
            assert np.all(np.diff(hit) == 1)
            starts.append((b, int(hit[0])))
    return starts


HEADS_PER_TILE = LANES // A_HEAD_DIM


def _swa_kernel(nb, q_ref, kp_ref, kc_ref, vp_ref, vc_ref, sink_ref, rb_ref, o_ref, bias_ref):
    r = pl.program_id(0)
    n_keys = 2 * A_BLOCK

    @pl.when(r == 0)
    def _():
        row = lax.broadcasted_iota(jnp.int32, (A_BLOCK, n_keys), 0)
        col = lax.broadcasted_iota(jnp.int32, (A_BLOCK, n_keys), 1)
        dist = row + A_BLOCK - col
        band = (dist >= 0) & (dist < WINDOW)
        starts = _bucket_starts()
        for h in range(A_HEADS):
            val = jnp.full(dist.shape, rb_ref[starts[0][0], h], F32)
            for b, s in starts[1:]:
                val = jnp.where(dist >= s, rb_ref[b, h], val)
            val = jnp.where(band, val, NEG_INF)
            sink = sink_ref[0, h]
            bias_ref[0, h] = jnp.where(col == 0, sink, val)
            bias_ref[1, h] = jnp.where(col == 0, sink, jnp.where(col >= A_BLOCK, val, NEG_INF))

    first = (lax.rem(r, nb) == 0).astype(jnp.int32)
    lane = lax.broadcasted_iota(jnp.int32, (1, LANES), 1)
    scale = A_HEAD_DIM ** -0.5
    q_keep = (jnp.where(lane < A_HEAD_DIM, scale, 0.0).astype(BF16),
              jnp.where(lane < A_HEAD_DIM, 0.0, scale).astype(BF16))
    lower_lanes = lax.broadcasted_iota(jnp.int32, (A_BLOCK, LANES), 1) < A_HEAD_DIM
    key0 = lax.broadcasted_iota(jnp.int32, (n_keys, LANES), 0) == 0
    ones = jnp.ones((n_keys, LANES), BF16)

    def both_blocks(prev_ref, cur_ref, tile):
        cols = slice(tile * LANES, (tile + 1) * LANES)
        cat = jnp.concatenate([prev_ref[:, cols], cur_ref[:, cols]], axis=0).astype(F32)
        cat = jnp.where(key0, 0.0, cat)
        return cat.astype(BF16), pltpu.roll(cat, A_HEAD_DIM, 1).astype(BF16)

    stacks = []
    for tile in range(A_KV_WIDTH // LANES):
        k_cat, k_swp = both_blocks(kp_ref, kc_ref, tile)
        v_cat, v_swp = both_blocks(vp_ref, vc_ref, tile)
        q0 = tile * HEADS_PER_TILE * A_GROUP // HEADS_PER_TILE
        stacks.append((k_cat, v_cat, [(q0, 0), (q0 + 1, 0), (q0 + 2, 1), (q0 + 3, 1)]))
        stacks.append((k_swp, v_swp, [(q0, 1), (q0 + 1, 1), (q0 + 2, 0), (q0 + 3, 0)]))

    scores = []
    for k_tile, _, members in stacks:
        q4 = jnp.concatenate(
            [q_ref[:, qt * LANES:(qt + 1) * LANES] * q_keep[half] for qt, half in members], axis=0)
        scores.append(lax.dot_general(q4, k_tile, (((1,), (1,)), ((), ())),
                                      preferred_element_type=F32))
    s = jnp.concatenate(scores, axis=0)
    s = s + jnp.concatenate(
        [bias_ref[first, qt * HEADS_PER_TILE + half] for _, _, members in stacks for qt, half in members],
        axis=0)
    p = jnp.exp(s - jnp.max(s, axis=-1, keepdims=True)).astype(BF16)

    normed = {}
    rows_per_stack = len(stacks[0][2]) * A_BLOCK
    for i, (_, v_tile, members) in enumerate(stacks):
        p_i = p[i * rows_per_stack:(i + 1) * rows_per_stack]
        o = jnp.dot(p_i, v_tile, preferred_element_type=F32)
        denom = jnp.dot(p_i, ones, preferred_element_type=F32)
        o = o / denom
        for j, member in enumerate(members):
            normed[member] = o[j * A_BLOCK:(j + 1) * A_BLOCK]
    for qt in range(A_Q_WIDTH // LANES):
        o_ref[:, qt * LANES:(qt + 1) * LANES] = jnp.where(
            lower_lanes, normed[(qt, 0)], normed[(qt, 1)]).astype(o_ref.dtype)


def _swa(proj, sinks, rel_bias, batch, seq):
    t = proj.shape[0]
    nb = seq // A_BLOCK
    qcol = OFF_QA // A_Q_WIDTH
    kcol = OFF_KA // A_KV_WIDTH
    vcol = OFF_VA // A_KV_WIDTH
    prev = lambda r: jnp.maximum(r - 1, 0)
    smem = functools.partial(pl.BlockSpec, memory_space=pltpu.SMEM)
    return pl.pallas_call(
        functools.partial(_swa_kernel, nb),
        grid=(t // A_BLOCK,),
        in_specs=[
            pl.BlockSpec((A_BLOCK, A_Q_WIDTH), lambda r: (r, qcol)),
            pl.BlockSpec((A_BLOCK, A_KV_WIDTH), lambda r: (prev(r), kcol)),
            pl.BlockSpec((A_BLOCK, A_KV_WIDTH), lambda r: (r, kcol)),
            pl.BlockSpec((A_BLOCK, A_KV_WIDTH), lambda r: (prev(r), vcol)),
            pl.BlockSpec((A_BLOCK, A_KV_WIDTH), lambda r: (r, vcol)),
            smem(), smem(),
        ],
        out_specs=pl.BlockSpec((A_BLOCK, A_Q_WIDTH), lambda r: (r, 0)),
        out_shape=jax.ShapeDtypeStruct((t, A_Q_WIDTH), BF16),
        scratch_shapes=[pltpu.VMEM((2, A_HEADS, A_BLOCK, 2 * A_BLOCK), F32)],
        compiler_params=_cparams(("arbitrary",)),
        name="swa",
    )(proj, proj, proj, proj, proj, sinks, rel_bias)


GLA_CHUNKS_PER_STEP = 4


def _split3(x):
    hi = x.astype(BF16)
    r1 = x - hi.astype(F32)
    mid = r1.astype(BF16)
    lo = (r1 - mid.astype(F32)).astype(BF16)
    return jnp.concatenate([hi, mid, lo], axis=0)


def _gla_kernel(q_ref, k_ref, v0_ref, v1_ref, gl_ref, og0_ref, og1_ref, wgu_ref, bg_ref, ng_ref,
                o_ref, s_ref):
    c = B_CHUNK
    heads_per_half = HALF_V // B_VAL_DIM
    v_refs = (v0_ref, v1_ref)
    og_refs = (og0_ref, og1_ref)

    @pl.when(pl.program_id(1) == 0)
    def _():
        s_ref[...] = jnp.zeros_like(s_ref)

    ri = lax.broadcasted_iota(jnp.int32, (c, c), 0)
    ci = lax.broadcasted_iota(jnp.int32, (c, c), 1)
    causal = ri >= ci
    ri3 = lax.broadcasted_iota(jnp.int32, (c, 3 * c), 0)
    ci3 = lax.broadcasted_iota(jnp.int32, (c, 3 * c), 1)
    ci3 = ci3 - jnp.where(ci3 >= c, c, 0) - jnp.where(ci3 >= 2 * c, c, 0)
    tri3 = (ri3 >= ci3).astype(BF16)
    ng = ng_ref[...]

    glin = jnp.dot(gl_ref[...], wgu_ref[...], preferred_element_type=F32) + bg_ref[...]
    log_a = (jnp.minimum(glin, 0.0) - jnp.log1p(jnp.exp(-jnp.abs(glin)))) / B_GATE_TAU

    for j in range(GLA_CHUNKS_PER_STEP):
        rows = slice(j * c, (j + 1) * c)
        for h in range(B_HEADS):
            kcols = slice(h * B_KEY_DIM, (h + 1) * B_KEY_DIM)
            vcols = slice(h * B_VAL_DIM, (h + 1) * B_VAL_DIM)
            hcols = slice((h % heads_per_half) * B_VAL_DIM, (h % heads_per_half + 1) * B_VAL_DIM)
            g = log_a[rows, kcols]
            b = jnp.dot(tri3, _split3(g), preferred_element_type=F32)
            b_last = b[c - 1:c, :]
            qf = q_ref[rows, kcols].astype(F32) * (B_KEY_DIM ** -0.5)
            kf = k_ref[rows, kcols].astype(F32)
            vv = v_refs[h // heads_per_half][rows, hcols]
            q_dec = (qf * jnp.exp(b)).astype(BF16)
            k_dec = (kf * jnp.exp(-b)).astype(BF16)
            k_state = kf * jnp.exp(b_last - b)
            att = lax.dot_general(q_dec, k_dec, (((1,), (1,)), ((), ())),
                                  preferred_element_type=F32)
            att = jnp.where(causal, att, 0.0).astype(BF16)
            state = s_ref[h]
            o = (jnp.dot(att, vv, preferred_element_type=F32)
                 + jnp.dot(q_dec, state.astype(BF16), preferred_element_type=F32))
            ds = jnp.dot(k_state.T.astype(BF16), vv, preferred_element_type=F32)
            decay = jnp.exp(jnp.broadcast_to(b_last, (8, B_KEY_DIM))).T[:, :1]
            s_ref[h] = decay * state + ds
            y = _rmsnorm_rows(o, ng)
            gate = og_refs[h // heads_per_half][rows, hcols].astype(F32)
            o_ref[rows, vcols] = (y * (gate * jax.nn.sigmoid(gate))).astype(o_ref.dtype)


def _gla(proj, wgu, bg, ng, batch, seq):
    t = proj.shape[0]
    rows = GLA_CHUNKS_PER_STEP * B_CHUNK
    steps = seq // rows
    rb = lambda b, s: b * steps + s
    return pl.pallas_call(
        _gla_kernel,
        grid=(batch, steps),
        in_specs=[
            pl.BlockSpec((rows, B_QK_WIDTH), lambda b, s: (rb(b, s), OFF_QB // B_QK_WIDTH)),
            pl.BlockSpec((rows, B_QK_WIDTH), lambda b, s: (rb(b, s), OFF_KB // B_QK_WIDTH)),
            pl.BlockSpec((rows, HALF_V), lambda b, s: (rb(b, s), OFF_VB // HALF_V)),
            pl.BlockSpec((rows, HALF_V), lambda b, s: (rb(b, s), OFF_VB // HALF_V + 1)),
            pl.BlockSpec((rows, LANES), lambda b, s: (rb(b, s), OFF_GLOW // LANES)),
            pl.BlockSpec((rows, HALF_V), lambda b, s: (rb(b, s), OFF_OBG // HALF_V)),
            pl.BlockSpec((rows, HALF_V), lambda b, s: (rb(b, s), OFF_OBG // HALF_V + 1)),
            _const_spec(wgu.shape), _const_spec(bg.shape), _const_spec(ng.shape),
        ],
        out_specs=pl.BlockSpec((rows, B_V_WIDTH), lambda b, s: (rb(b, s), 0)),
        out_shape=jax.ShapeDtypeStruct((t, B_V_WIDTH), BF16),
        scratch_shapes=[pltpu.VMEM((B_HEADS, B_KEY_DIM, B_VAL_DIM), F32)],
        compiler_params=_cparams(("parallel", "arbitrary")),
        name="gla",
    )(proj, proj, proj, proj, proj, proj, proj, wgu, bg, ng)


def _merge_kernel(a_ref, b_ref, ga_ref, gb_ref, x_ref, wa_ref, wb_ref, wo_ref, h_ref):
    ya = jnp.dot(a_ref[...], wa_ref[...], preferred_element_type=F32)
    yb = jnp.dot(b_ref[...], wb_ref[...], preferred_element_type=F32)
    merged = (jax.nn.sigmoid(ga_ref[...].astype(F32)) * ya
              + jax.nn.sigmoid(gb_ref[...].astype(F32)) * yb)
    h_ref[...] = x_ref[...] + jnp.dot(merged.astype(BF16), wo_ref[...], preferred_element_type=F32)


def _merge(attn, gla, proj, x2, wa, wb, wo, tm):
    t, d = x2.shape
    return pl.pallas_call(
        _merge_kernel,
        grid=(t // tm,),
        in_specs=[
            pl.BlockSpec((tm, A_Q_WIDTH), lambda i: (i, 0)),
            pl.BlockSpec((tm, B_V_WIDTH), lambda i: (i, 0)),
            pl.BlockSpec((tm, d), lambda i: (i, OFF_GATE_A // D_MODEL)),
            pl.BlockSpec((tm, d), lambda i: (i, OFF_GATE_B // D_MODEL)),
            pl.BlockSpec((tm, d), lambda i: (i, 0)),
            _const_spec(wa.shape), _const_spec(wb.shape), _const_spec(wo.shape),
        ],
        out_specs=pl.BlockSpec((tm, d), lambda i: (i, 0)),
        out_shape=jax.ShapeDtypeStruct((t, d), F32),
        compiler_params=_cparams(("parallel",)),
        name="merge",
    )(attn, gla, proj, proj, x2, wa, wb, wo)


def _ffn_kernel(h_ref, gz_ref, wg_ref, wu_ref, wd_ref, gf_ref, o_ref, z_ref, acc_ref):
    f = pl.program_id(1)
    n_chunks = h_ref.shape[0] // NORM_ROWS

    @pl.when(f == 0)
    def _():
        gz = gz_ref[...]

        def body(c, carry):
            rows = pl.ds(pl.multiple_of(c * NORM_ROWS, NORM_ROWS), NORM_ROWS)
            z_ref[rows, :] = _rmsnorm_rows(h_ref[rows, :], gz).astype(BF16)
            return carry

        lax.fori_loop(0, n_chunks, body, 0)
        acc_ref[...] = jnp.zeros_like(acc_ref)

    z = z_ref[...]
    g = jnp.dot(z, wg_ref[...], preferred_element_type=F32)
    u = jnp.dot(z, wu_ref[...], preferred_element_type=F32)
    act = (g * jax.nn.sigmoid(g) * u).astype(BF16)
    acc_ref[...] += jnp.dot(act, wd_ref[...], preferred_element_type=F32)

    @pl.when(f == pl.num_programs(1) - 1)
    def _():
        gf = gf_ref[...]

        def body(c, carry):
            rows = pl.ds(pl.multiple_of(c * NORM_ROWS, NORM_ROWS), NORM_ROWS)
            o_ref[rows, :] = _rmsnorm_rows(h_ref[rows, :] + acc_ref[rows, :], gf)
            return carry

        lax.fori_loop(0, n_chunks, body, 0)


def _ffn(h, gz, wg, wu, wd, gf, tm, tf):
    t, d = h.shape
    f = wg.shape[1]
    return pl.pallas_call(
        _ffn_kernel,
        grid=(t // tm, f // tf),
        in_specs=[
            pl.BlockSpec((tm, d), lambda i, j: (i, 0)),
            pl.BlockSpec((1, d), lambda i, j: (0, 0)),
            pl.BlockSpec((d, tf), lambda i, j: (0, j)),
            pl.BlockSpec((d, tf), lambda i, j: (0, j)),
            pl.BlockSpec((tf, d), lambda i, j: (j, 0)),
            pl.BlockSpec((1, d), lambda i, j: (0, 0)),
        ],
        out_specs=pl.BlockSpec((tm, d), lambda i, j: (i, 0)),
        out_shape=jax.ShapeDtypeStruct((t, d), F32),
        scratch_shapes=[pltpu.VMEM((tm, d), BF16), pltpu.VMEM((tm, d), F32)],
        compiler_params=_cparams(("parallel", "arbitrary")),
        name="ffn",
    )(h, gz, wg, wu, wd, gf)


def _pack_w_in(w, n_total):
    n_mix = A_Q_WIDTH + 2 * A_KV_WIDTH + 2 * B_QK_WIDTH + B_V_WIDTH
    glow_end = n_mix + B_GATE_RANK
    obg_end = glow_end + B_V_WIDTH
    d = w.shape[0]
    pad = jnp.zeros((d, n_total - OFF_GLOW - B_GATE_RANK), BF16)
    return jnp.concatenate(
        [w[:, obg_end:].astype(BF16), w[:, :n_mix].astype(BF16), w[:, glow_end:obg_end].astype(BF16),
         w[:, n_mix:glow_end].astype(BF16), pad], axis=1)


def kernel(x, norm_mix_g, w_in, sinks, rel_bias, w_gate_up, b_gate, gla_norm_g, w_proj_a, w_proj_b,
           w_out, norm_ffn_g, w_ffn_gate, w_ffn_up, w_ffn_down, norm_final_g):
    batch, seq, d = x.shape
    assert d == D_MODEL and w_in.shape[0] == 1, "single-layer geometry"
    t = batch * seq
    x2 = x.reshape(t, d)

    tn_in = 1024
    n_total = -(-PROJ_USED // tn_in) * tn_in
    w_in_p = _pack_w_in(w_in[0], n_total)
    proj = _inproj(x2, norm_mix_g, w_in_p, tm=1024, tn=tn_in)

    attn = _swa(proj, sinks, rel_bias, batch, seq)

    wgu = jnp.zeros((LANES, B_QK_WIDTH), BF16).at[:B_GATE_RANK].set(w_gate_up[0].astype(BF16))
    gla = _gla(proj, wgu, b_gate, gla_norm_g, batch, seq)

    h = _merge(attn, gla, proj, x2, w_proj_a[0].astype(BF16), w_proj_b[0].astype(BF16),
               w_out[0].astype(BF16), tm=512)

    out = _ffn(h, norm_ffn_g, w_ffn_gate[0].astype(BF16), w_ffn_up[0].astype(BF16),
               w_ffn_down[0].astype(BF16), norm_final_g.reshape(1, d), tm=512, tf=512)
    return out.reshape(batch, seq, d)
```

```python
import functools
import math

import numpy as np
import jax
import jax.numpy as jnp
from jax import lax
from jax.experimental import pallas as pl
from jax.experimental.pallas import tpu as pltpu

F32 = jnp.float32
BF16 = jnp.bfloat16

D_MODEL = 2048
A_HEADS = 16
A_KV_HEADS = 4
A_HEAD_DIM = 64
A_GROUP = A_HEADS // A_KV_HEADS
WINDOW = 128
A_BLOCK = 128
A_Q_WIDTH = A_HEADS * A_HEAD_DIM
A_KV_WIDTH = A_KV_HEADS * A_HEAD_DIM
N_BUCKETS = 32
MAX_DISTANCE = 128
B_HEADS = 4
B_KEY_DIM = 128
B_VAL_DIM = 256
B_QK_WIDTH = B_HEADS * B_KEY_DIM
B_V_WIDTH = B_HEADS * B_VAL_DIM
B_GATE_RANK = 16
B_GATE_TAU = 16.0
B_CHUNK = 64
EPS = 1e-6
NEG_INF = -1e30

LANES = 128

OFF_GATE_A = 0
OFF_GATE_B = OFF_GATE_A + D_MODEL
OFF_QA = OFF_GATE_B + D_MODEL
OFF_KA = OFF_QA + A_Q_WIDTH
OFF_VA = OFF_KA + A_KV_WIDTH
OFF_QB = OFF_VA + A_KV_WIDTH
OFF_KB = OFF_QB + B_QK_WIDTH
OFF_VB = OFF_KB + B_QK_WIDTH
OFF_OBG = OFF_VB + B_V_WIDTH
OFF_GLOW = OFF_OBG + B_V_WIDTH
PROJ_USED = OFF_GLOW + LANES
HALF_V = B_V_WIDTH // 2

VMEM_LIMIT = 56 * 1024 * 1024


def _cparams(sem):
    return pltpu.CompilerParams(dimension_semantics=sem, vmem_limit_bytes=VMEM_LIMIT)


def _const_spec(shape):
    return pl.BlockSpec(shape, lambda *_: (0,) * len(shape), pipeline_mode=pl.Buffered(1))


def _rmsnorm_rows(x, g):
    ms = jnp.mean(x * x, axis=-1, keepdims=True)
    return x * lax.rsqrt(ms + EPS) * g


NORM_ROWS = 128


def _inproj_kernel(x_ref, g_ref, w_ref, o_ref, u_ref):
    @pl.when(pl.program_id(1) == 0)
    def _():
        g = g_ref[...]

        def body(c, carry):
            rows = pl.ds(pl.multiple_of(c * NORM_ROWS, NORM_ROWS), NORM_ROWS)
            u_ref[rows, :] = _rmsnorm_rows(x_ref[rows, :], g).astype(BF16)
            return carry

        lax.fori_loop(0, x_ref.shape[0] // NORM_ROWS, body, 0)

    o_ref[...] = jnp.dot(u_ref[...], w_ref[...], preferred_element_type=F32).astype(o_ref.dtype)


def _inproj(x2, g, w, tm, tn):
    t, d = x2.shape
    n = w.shape[1]
    return pl.pallas_call(
        _inproj_kernel,
        grid=(t // tm, n // tn),
        in_specs=[
            pl.BlockSpec((tm, d), lambda i, j: (i, 0)),
            pl.BlockSpec((1, d), lambda i, j: (0, 0)),
            pl.BlockSpec((d, tn), lambda i, j: (0, j)),
        ],
        out_specs=pl.BlockSpec((tm, tn), lambda i, j: (i, j)),
        out_shape=jax.ShapeDtypeStruct((t, n), BF16),
        scratch_shapes=[pltpu.VMEM((tm, d), BF16)],
        compiler_params=_cparams(("parallel", "arbitrary")),
        name="inproj",
    )(x2, g, w)


def _bucket_starts():
    max_exact = N_BUCKETS // 2
    d = np.arange(WINDOW)
    large = max_exact + (np.log(np.maximum(d, 1).astype(np.float32) / max_exact)
                         / math.log(MAX_DISTANCE / max_exact)
                         * (N_BUCKETS - max_exact)).astype(np.int32)
    bucket = np.where(d < max_exact, d, np.minimum(large, N_BUCKETS - 1))
    starts = []
    for b in range(N_BUCKETS):
        hit = np.nonzero(bucket == b)[0]
        if hit.size:
            assert np.all(np.diff(hit) == 1)
            starts.append((b, int(hit[0])))
    return starts


HEADS_PER_TILE = LANES // A_HEAD_DIM


def _swa_kernel(nb, q_ref, kp_ref, kc_ref, vp_ref, vc_ref, sink_ref, rb_ref, o_ref, bias_ref):
    r = pl.program_id(0)
    n_keys = 2 * A_BLOCK

    @pl.when(r == 0)
    def _():
        row = lax.broadcasted_iota(jnp.int32, (A_BLOCK, n_keys), 0)
        col = lax.broadcasted_iota(jnp.int32, (A_BLOCK, n_keys), 1)
        dist = row + A_BLOCK - col
        band = (dist >= 0) & (dist < WINDOW)
        starts = _bucket_starts()
        for h in range(A_HEADS):
            val = jnp.full(dist.shape, rb_ref[starts[0][0], h], F32)
            for b, s in starts[1:]:
                val = jnp.where(dist >= s, rb_ref[b, h], val)
            val = jnp.where(band, val, NEG_INF)
            sink = sink_ref[0, h]
            bias_ref[0, h] = jnp.where(col == 0, sink, val)
            bias_ref[1, h] = jnp.where(col == 0, sink, jnp.where(col >= A_BLOCK, val, NEG_INF))

    first = (lax.rem(r, nb) == 0).astype(jnp.int32)
    lane = lax.broadcasted_iota(jnp.int32, (1, LANES), 1)
    scale = A_HEAD_DIM ** -0.5
    q_keep = (jnp.where(lane < A_HEAD_DIM, scale, 0.0).astype(BF16),
              jnp.where(lane < A_HEAD_DIM, 0.0, scale).astype(BF16))
    lower_lanes = lax.broadcasted_iota(jnp.int32, (A_BLOCK, LANES), 1) < A_HEAD_DIM
    key0 = lax.broadcasted_iota(jnp.int32, (n_keys, LANES), 0) == 0
    ones = jnp.ones((n_keys, LANES), BF16)

    def both_blocks(prev_ref, cur_ref, tile):
        cols = slice(tile * LANES, (tile + 1) * LANES)
        cat = jnp.concatenate([prev_ref[:, cols], cur_ref[:, cols]], axis=0).astype(F32)
        cat = jnp.where(key0, 0.0, cat)
        return cat.astype(BF16), pltpu.roll(cat, A_HEAD_DIM, 1).astype(BF16)

    stacks = []
    for tile in range(A_KV_WIDTH // LANES):
        k_cat, k_swp = both_blocks(kp_ref, kc_ref, tile)
        v_cat, v_swp = both_blocks(vp_ref, vc_ref, tile)
        q0 = tile * HEADS_PER_TILE * A_GROUP // HEADS_PER_TILE
        stacks.append((k_cat, v_cat, [(q0, 0), (q0 + 1, 0), (q0 + 2, 1), (q0 + 3, 1)]))
        stacks.append((k_swp, v_swp, [(q0, 1), (q0 + 1, 1), (q0 + 2, 0), (q0 + 3, 0)]))

    scores = []
    for k_tile, _, members in stacks:
        q4 = jnp.concatenate(
            [q_ref[:, qt * LANES:(qt + 1) * LANES] * q_keep[half] for qt, half in members], axis=0)
        scores.append(lax.dot_general(q4, k_tile, (((1,), (1,)), ((), ())),
                                      preferred_element_type=F32))
    s = jnp.concatenate(scores, axis=0)
    s = s + jnp.concatenate(
        [bias_ref[first, qt * HEADS_PER_TILE + half] for _, _, members in stacks for qt, half in members],
        axis=0)
    p = jnp.exp(s - jnp.max(s, axis=-1, keepdims=True)).astype(BF16)

    normed = {}
    rows_per_stack = len(stacks[0][2]) * A_BLOCK
    for i, (_, v_tile, members) in enumerate(stacks):
        p_i = p[i * rows_per_stack:(i + 1) * rows_per_stack]
        o = jnp.dot(p_i, v_tile, preferred_element_type=F32)
        denom = jnp.dot(p_i, ones, preferred_element_type=F32)
        o = o / denom
        for j, member in enumerate(members):
            normed[member] = o[j * A_BLOCK:(j + 1) * A_BLOCK]
    for qt in range(A_Q_WIDTH // LANES):
        o_ref[:, qt * LANES:(qt + 1) * LANES] = jnp.where(
            lower_lanes, normed[(qt, 0)], normed[(qt, 1)]).astype(o_ref.dtype)


def _swa(proj, sinks, rel_bias, batch, seq):
    t = proj.shape[0]
    nb = seq // A_BLOCK
    qcol = OFF_QA // A_Q_WIDTH
    kcol = OFF_KA // A_KV_WIDTH
    vcol = OFF_VA // A_KV_WIDTH
    prev = lambda r: jnp.maximum(r - 1, 0)
    smem = functools.partial(pl.BlockSpec, memory_space=pltpu.SMEM)
    return pl.pallas_call(
        functools.partial(_swa_kernel, nb),
        grid=(t // A_BLOCK,),
        in_specs=[
            pl.BlockSpec((A_BLOCK, A_Q_WIDTH), lambda r: (r, qcol)),
            pl.BlockSpec((A_BLOCK, A_KV_WIDTH), lambda r: (prev(r), kcol)),
            pl.BlockSpec((A_BLOCK, A_KV_WIDTH), lambda r: (r, kcol)),
            pl.BlockSpec((A_BLOCK, A_KV_WIDTH), lambda r: (prev(r), vcol)),
            pl.BlockSpec((A_BLOCK, A_KV_WIDTH), lambda r: (r, vcol)),
            smem(), smem(),
        ],
        out_specs=pl.BlockSpec((A_BLOCK, A_Q_WIDTH), lambda r: (r, 0)),
        out_shape=jax.ShapeDtypeStruct((t, A_Q_WIDTH), BF16),
        scratch_shapes=[pltpu.VMEM((2, A_HEADS, A_BLOCK, 2 * A_BLOCK), F32)],
        compiler_params=_cparams(("arbitrary",)),
        name="swa",
    )(proj, proj, proj, proj, proj, sinks, rel_bias)


GLA_CHUNKS_PER_STEP = 4


def _split3(x):
    hi = x.astype(BF16)
    r1 = x - hi.astype(F32)
    mid = r1.astype(BF16)
    lo = (r1 - mid.astype(F32)).astype(BF16)
    return jnp.concatenate([hi, mid, lo], axis=0)


def _gla_kernel(q_ref, k_ref, v0_ref, v1_ref, gl_ref, og0_ref, og1_ref, wgu_ref, bg_ref, ng_ref,
                o_ref, s_ref):
    c = B_CHUNK
    n_chunks = GLA_CHUNKS_PER_STEP
    n_rows = n_chunks * c
    heads_per_half = HALF_V // B_VAL_DIM
    v_refs = (v0_ref, v1_ref)
    og_refs = (og0_ref, og1_ref)
    chunk_rows = [slice(j * c, (j + 1) * c) for j in range(n_chunks)]
    key_cols = [slice(h * B_KEY_DIM, (h + 1) * B_KEY_DIM) for h in range(B_HEADS)]
    units = [(j, h) for j in range(n_chunks) for h in range(B_HEADS)]

    def v_of(refs, j, h):
        lo = (h % heads_per_half) * B_VAL_DIM
        rows = slice(None) if j is None else chunk_rows[j]
        return refs[h // heads_per_half][rows, lo:lo + B_VAL_DIM]

    @pl.when(pl.program_id(1) == 0)
    def _():
        s_ref[...] = jnp.zeros_like(s_ref)

    glin = jnp.dot(gl_ref[...], wgu_ref[...], preferred_element_type=F32) + bg_ref[...]
    log_a = (jnp.minimum(glin, 0.0) - jnp.log1p(jnp.exp(-jnp.abs(glin)))) / B_GATE_TAU

    ri = lax.broadcasted_iota(jnp.int32, (n_rows, 3 * n_rows), 0)
    ci = lax.broadcasted_iota(jnp.int32, (n_rows, 3 * n_rows), 1)
    ci = ci - jnp.where(ci >= n_rows, n_rows, 0) - jnp.where(ci >= 2 * n_rows, n_rows, 0)
    shift = int(math.log2(c))
    same_chunk = lax.shift_right_logical(ri, shift) == lax.shift_right_logical(ci, shift)
    tri3 = ((ri >= ci) & same_chunk).astype(BF16)
    b = jnp.dot(tri3, _split3(log_a), preferred_element_type=F32)
    last_rows = [b[(j + 1) * c - 1:(j + 1) * c, :] for j in range(n_chunks)]
    b_last = jnp.concatenate([jnp.broadcast_to(r, (c, b.shape[1])) for r in last_rows], axis=0)

    qf = q_ref[...].astype(F32) * (B_KEY_DIM ** -0.5)
    kf = k_ref[...].astype(F32)
    q_dec = (qf * jnp.exp(b)).astype(BF16)
    k_dec = (kf * jnp.exp(-b)).astype(BF16)
    k_state = kf * jnp.exp(b_last - b)
    pad = jnp.zeros((8 - n_chunks, b.shape[1]), F32)
    decay_rows = jnp.exp(jnp.concatenate(last_rows + [pad], axis=0))

    ri = lax.broadcasted_iota(jnp.int32, (c, c), 0)
    ci = lax.broadcasted_iota(jnp.int32, (c, c), 1)
    causal = ri >= ci
    att = {}
    for j, h in units:
        a = lax.dot_general(q_dec[chunk_rows[j], key_cols[h]], k_dec[chunk_rows[j], key_cols[h]],
                            (((1,), (1,)), ((), ())), preferred_element_type=F32)
        att[j, h] = jnp.where(causal, a, 0.0).astype(BF16)
    o_intra = {u: jnp.dot(att[u], v_of(v_refs, *u), preferred_element_type=F32) for u in units}
    ds = {(j, h): jnp.dot(k_state[chunk_rows[j], key_cols[h]].T.astype(BF16), v_of(v_refs, j, h),
                          preferred_element_type=F32) for j, h in units}

    entering = {}
    for h in range(B_HEADS):
        decay_t = decay_rows[:, key_cols[h]].T
        state = s_ref[h]
        for j in range(n_chunks):
            entering[j, h] = state.astype(BF16)
            state = decay_t[:, j:j + 1] * state + ds[j, h]
        s_ref[h] = state
    o_inter = {(j, h): jnp.dot(q_dec[chunk_rows[j], key_cols[h]], entering[j, h],
                               preferred_element_type=F32) for j, h in units}

    ng = ng_ref[...]
    for h in range(B_HEADS):
        o = jnp.concatenate([o_intra[j, h] + o_inter[j, h] for j in range(n_chunks)], axis=0)
        gate = v_of(og_refs, None, h).astype(F32)
        y = _rmsnorm_rows(o, ng) * (gate * jax.nn.sigmoid(gate))
        o_ref[:, h * B_VAL_DIM:(h + 1) * B_VAL_DIM] = y.astype(o_ref.dtype)


def _gla(proj, wgu, bg, ng, batch, seq):
    t = proj.shape[0]
    rows = GLA_CHUNKS_PER_STEP * B_CHUNK
    steps = seq // rows
    rb = lambda b, s: b * steps + s
    return pl.pallas_call(
        _gla_kernel,
        grid=(batch, steps),
        in_specs=[
            pl.BlockSpec((rows, B_QK_WIDTH), lambda b, s: (rb(b, s), OFF_QB // B_QK_WIDTH)),
            pl.BlockSpec((rows, B_QK_WIDTH), lambda b, s: (rb(b, s), OFF_KB // B_QK_WIDTH)),
            pl.BlockSpec((rows, HALF_V), lambda b, s: (rb(b, s), OFF_VB // HALF_V)),
            pl.BlockSpec((rows, HALF_V), lambda b, s: (rb(b, s), OFF_VB // HALF_V + 1)),
            pl.BlockSpec((rows, LANES), lambda b, s: (rb(b, s), OFF_GLOW // LANES)),
            pl.BlockSpec((rows, HALF_V), lambda b, s: (rb(b, s), OFF_OBG // HALF_V)),
            pl.BlockSpec((rows, HALF_V), lambda b, s: (rb(b, s), OFF_OBG // HALF_V + 1)),
            _const_spec(wgu.shape), _const_spec(bg.shape), _const_spec(ng.shape),
        ],
        out_specs=pl.BlockSpec((rows, B_V_WIDTH), lambda b, s: (rb(b, s), 0)),
        out_shape=jax.ShapeDtypeStruct((t, B_V_WIDTH), BF16),
        scratch_shapes=[pltpu.VMEM((B_HEADS, B_KEY_DIM, B_VAL_DIM), F32)],
        compiler_params=_cparams(("parallel", "arbitrary")),
        name="gla",
    )(proj, proj, proj, proj, proj, proj, proj, wgu, bg, ng)


def _merge_kernel(a_ref, b_ref, ga_ref, gb_ref, x_ref, wa_ref, wb_ref, wo_ref, h_ref):
    ya = jnp.dot(a_ref[...], wa_ref[...], preferred_element_type=F32)
    yb = jnp.dot(b_ref[...], wb_ref[...], preferred_element_type=F32)
    merged = (jax.nn.sigmoid(ga_ref[...].astype(F32)) * ya
              + jax.nn.sigmoid(gb_ref[...].astype(F32)) * yb)
    h_ref[...] = x_ref[...] + jnp.dot(merged.astype(BF16), wo_ref[...], preferred_element_type=F32)


def _merge(attn, gla, proj, x2, wa, wb, wo, tm):
    t, d = x2.shape
    return pl.pallas_call(
        _merge_kernel,
        grid=(t // tm,),
        in_specs=[
            pl.BlockSpec((tm, A_Q_WIDTH), lambda i: (i, 0)),
            pl.BlockSpec((tm, B_V_WIDTH), lambda i: (i, 0)),
            pl.BlockSpec((tm, d), lambda i: (i, OFF_GATE_A // D_MODEL)),
            pl.BlockSpec((tm, d), lambda i: (i, OFF_GATE_B // D_MODEL)),
            pl.BlockSpec((tm, d), lambda i: (i, 0)),
            _const_spec(wa.shape), _const_spec(wb.shape), _const_spec(wo.shape),
        ],
        out_specs=pl.BlockSpec((tm, d), lambda i: (i, 0)),
        out_shape=jax.ShapeDtypeStruct((t, d), F32),
        compiler_params=_cparams(("parallel",)),
        name="merge",
    )(attn, gla, proj, proj, x2, wa, wb, wo)


def _ffn_kernel(h_ref, gz_ref, wg_ref, wu_ref, wd_ref, gf_ref, o_ref, z_ref, acc_ref):
    f = pl.program_id(1)
    n_chunks = h_ref.shape[0] // NORM_ROWS

    @pl.when(f == 0)
    def _():
        gz = gz_ref[...]

        def body(c, carry):
            rows = pl.ds(pl.multiple_of(c * NORM_ROWS, NORM_ROWS), NORM_ROWS)
            z_ref[rows, :] = _rmsnorm_rows(h_ref[rows, :], gz).astype(BF16)
            return carry

        lax.fori_loop(0, n_chunks, body, 0)
        acc_ref[...] = jnp.zeros_like(acc_ref)

    z = z_ref[...]
    g = jnp.dot(z, wg_ref[...], preferred_element_type=F32)
    u = jnp.dot(z, wu_ref[...], preferred_element_type=F32)
    act = (g * jax.nn.sigmoid(g) * u).astype(BF16)
    acc_ref[...] += jnp.dot(act, wd_ref[...], preferred_element_type=F32)

    @pl.when(f == pl.num_programs(1) - 1)
    def _():
        gf = gf_ref[...]

        def body(c, carry):
            rows = pl.ds(pl.multiple_of(c * NORM_ROWS, NORM_ROWS), NORM_ROWS)
            o_ref[rows, :] = _rmsnorm_rows(h_ref[rows, :] + acc_ref[rows, :], gf)
            return carry

        lax.fori_loop(0, n_chunks, body, 0)


def _ffn(h, gz, wg, wu, wd, gf, tm, tf):
    t, d = h.shape
    f = wg.shape[1]
    return pl.pallas_call(
        _ffn_kernel,
        grid=(t // tm, f // tf),
        in_specs=[
            pl.BlockSpec((tm, d), lambda i, j: (i, 0)),
            pl.BlockSpec((1, d), lambda i, j: (0, 0)),
            pl.BlockSpec((d, tf), lambda i, j: (0, j)),
            pl.BlockSpec((d, tf), lambda i, j: (0, j)),
            pl.BlockSpec((tf, d), lambda i, j: (j, 0)),
            pl.BlockSpec((1, d), lambda i, j: (0, 0)),
        ],
        out_specs=pl.BlockSpec((tm, d), lambda i, j: (i, 0)),
        out_shape=jax.ShapeDtypeStruct((t, d), F32),
        scratch_shapes=[pltpu.VMEM((tm, d), BF16), pltpu.VMEM((tm, d), F32)],
        compiler_params=_cparams(("parallel", "arbitrary")),
        name="ffn",
    )(h, gz, wg, wu, wd, gf)


PACK_ROWS = 128


def _pack_w_in_kernel(w_ref, o_ref):
    n_mix = OFF_OBG - OFF_QA
    glow_end = n_mix + B_GATE_RANK
    obg_end = glow_end + B_V_WIDTH
    n_in = w_ref.shape[1]
    o_ref[:, OFF_GATE_A:OFF_QA] = w_ref[:, obg_end:n_in].astype(BF16)
    o_ref[:, OFF_QA:OFF_OBG] = w_ref[:, 0:n_mix].astype(BF16)
    o_ref[:, OFF_OBG:OFF_GLOW] = w_ref[:, glow_end:obg_end].astype(BF16)
    lane = lax.broadcasted_iota(jnp.int32, (w_ref.shape[0], LANES), 1)
    glow = jnp.where(lane < B_GATE_RANK, w_ref[:, n_mix:n_mix + LANES], 0.0)
    o_ref[:, OFF_GLOW:PROJ_USED] = glow.astype(BF16)
    o_ref[:, PROJ_USED:] = jnp.zeros((o_ref.shape[0], o_ref.shape[1] - PROJ_USED), BF16)


def _pack_w_in(w, n_total):
    d, n_in = w.shape
    return pl.pallas_call(
        _pack_w_in_kernel,
        grid=(d // PACK_ROWS,),
        in_specs=[pl.BlockSpec((PACK_ROWS, n_in), lambda i: (i, 0))],
        out_specs=pl.BlockSpec((PACK_ROWS, n_total), lambda i: (i, 0)),
        out_shape=jax.ShapeDtypeStruct((d, n_total), BF16),
        compiler_params=_cparams(("parallel",)),
        name="pack_w_in",
    )(w)


def kernel(x, norm_mix_g, w_in, sinks, rel_bias, w_gate_up, b_gate, gla_norm_g, w_proj_a, w_proj_b,
           w_out, norm_ffn_g, w_ffn_gate, w_ffn_up, w_ffn_down, norm_final_g):
    batch, seq, d = x.shape
    assert d == D_MODEL and w_in.shape[0] == 1, "single-layer geometry"
    t = batch * seq
    x2 = x.reshape(t, d)

    tn_in = 1024
    n_total = -(-PROJ_USED // tn_in) * tn_in
    w_in_p = _pack_w_in(w_in[0], n_total)
    proj = _inproj(x2, norm_mix_g, w_in_p, tm=1024, tn=tn_in)

    attn = _swa(proj, sinks, rel_bias, batch, seq)

    wgu = jnp.zeros((LANES, B_QK_WIDTH), BF16).at[:B_GATE_RANK].set(w_gate_up[0].astype(BF16))
    gla = _gla(proj, wgu, b_gate, gla_norm_g, batch, seq)

    h = _merge(attn, gla, proj, x2, w_proj_a[0].astype(BF16), w_proj_b[0].astype(BF16),
               w_out[0].astype(BF16), tm=512)

    out = _ffn(h, norm_ffn_g, w_ffn_gate[0].astype(BF16), w_ffn_up[0].astype(BF16),
               w_ffn_down[0].astype(BF16), norm_final_g.reshape(1, d), tm=512, tf=512)
    return out.reshape(batch, seq, d)
```

```python
import functools
import math

import numpy as np
import jax
import jax.numpy as jnp
from jax import lax
from jax.experimental import pallas as pl
from jax.experimental.pallas import tpu as pltpu

F32 = jnp.float32
BF16 = jnp.bfloat16

D_MODEL = 2048
A_HEADS = 16
A_KV_HEADS = 4
A_HEAD_DIM = 64
A_GROUP = A_HEADS // A_KV_HEADS
WINDOW = 128
A_BLOCK = 128
A_Q_WIDTH = A_HEADS * A_HEAD_DIM
A_KV_WIDTH = A_KV_HEADS * A_HEAD_DIM
N_BUCKETS = 32
MAX_DISTANCE = 128
B_HEADS = 4
B_KEY_DIM = 128
B_VAL_DIM = 256
B_QK_WIDTH = B_HEADS * B_KEY_DIM
B_V_WIDTH = B_HEADS * B_VAL_DIM
B_GATE_RANK = 16
B_GATE_TAU = 16.0
B_CHUNK = 64
EPS = 1e-6
NEG_INF = -1e30

LANES = 128

OFF_GATE_A = 0
OFF_GATE_B = OFF_GATE_A + D_MODEL
OFF_QA = OFF_GATE_B + D_MODEL
OFF_KA = OFF_QA + A_Q_WIDTH
OFF_VA = OFF_KA + A_KV_WIDTH
OFF_QB = OFF_VA + A_KV_WIDTH
OFF_KB = OFF_QB + B_QK_WIDTH
OFF_VB = OFF_KB + B_QK_WIDTH
OFF_OBG = OFF_VB + B_V_WIDTH
OFF_GLOW = OFF_OBG + B_V_WIDTH
PROJ_USED = OFF_GLOW + LANES
HALF_V = B_V_WIDTH // 2

VMEM_LIMIT = 56 * 1024 * 1024


def _cparams(sem):
    return pltpu.CompilerParams(dimension_semantics=sem, vmem_limit_bytes=VMEM_LIMIT)


def _const_spec(shape):
    return pl.BlockSpec(shape, lambda *_: (0,) * len(shape), pipeline_mode=pl.Buffered(1))


def _rmsnorm_rows(x, g):
    ms = jnp.mean(x * x, axis=-1, keepdims=True)
    return x * lax.rsqrt(ms + EPS) * g


NORM_ROWS = 128


def _inproj_kernel(x_ref, g_ref, w_ref, o_ref, u_ref):
    @pl.when(pl.program_id(1) == 0)
    def _():
        g = g_ref[...]

        def body(c, carry):
            rows = pl.ds(pl.multiple_of(c * NORM_ROWS, NORM_ROWS), NORM_ROWS)
            u_ref[rows, :] = _rmsnorm_rows(x_ref[rows, :], g).astype(BF16)
            return carry

        lax.fori_loop(0, x_ref.shape[0] // NORM_ROWS, body, 0)

    o_ref[...] = lax.dot_general(u_ref[...], w_ref[...], (((1,), (1,)), ((), ())),
                                 preferred_element_type=F32).astype(o_ref.dtype)


def _inproj(x2, g, w_t, tm, tn):
    t, d = x2.shape
    n = w_t.shape[0]
    return pl.pallas_call(
        _inproj_kernel,
        grid=(t // tm, n // tn),
        in_specs=[
            pl.BlockSpec((tm, d), lambda i, j: (i, 0)),
            pl.BlockSpec((1, d), lambda i, j: (0, 0)),
            pl.BlockSpec((tn, d), lambda i, j: (j, 0)),
        ],
        out_specs=pl.BlockSpec((tm, tn), lambda i, j: (i, j)),
        out_shape=jax.ShapeDtypeStruct((t, n), BF16),
        scratch_shapes=[pltpu.VMEM((tm, d), BF16)],
        compiler_params=_cparams(("parallel", "arbitrary")),
        name="inproj",
    )(x2, g, w_t)


def _bucket_starts():
    max_exact = N_BUCKETS // 2
    d = np.arange(WINDOW)
    large = max_exact + (np.log(np.maximum(d, 1).astype(np.float32) / max_exact)
                         / math.log(MAX_DISTANCE / max_exact)
                         * (N_BUCKETS - max_exact)).astype(np.int32)
    bucket = np.where(d < max_exact, d, np.minimum(large, N_BUCKETS - 1))
    starts = []
    for b in range(N_BUCKETS):
        hit = np.nonzero(bucket == b)[0]
        if hit.size:
            assert np.all(np.diff(hit) == 1)
            starts.append((b, int(hit[0])))
    return starts


HEADS_PER_TILE = LANES // A_HEAD_DIM


def _swa_kernel(nb, q_ref, kp_ref, kc_ref, vp_ref, vc_ref, sink_ref, rb_ref, o_ref, bias_ref):
    r = pl.program_id(0)
    n_keys = 2 * A_BLOCK

    @pl.when(r == 0)
    def _():
        row = lax.broadcasted_iota(jnp.int32, (A_BLOCK, n_keys), 0)
        col = lax.broadcasted_iota(jnp.int32, (A_BLOCK, n_keys), 1)
        dist = row + A_BLOCK - col
        band = (dist >= 0) & (dist < WINDOW)
        starts = _bucket_starts()
        for h in range(A_HEADS):
            val = jnp.full(dist.shape, rb_ref[starts[0][0], h], F32)
            for b, s in starts[1:]:
                val = jnp.where(dist >= s, rb_ref[b, h], val)
            val = jnp.where(band, val, NEG_INF)
            sink = sink_ref[0, h]
            bias_ref[0, h] = jnp.where(col == 0, sink, val)
            bias_ref[1, h] = jnp.where(col == 0, sink, jnp.where(col >= A_BLOCK, val, NEG_INF))

    first = (lax.rem(r, nb) == 0).astype(jnp.int32)
    lane = lax.broadcasted_iota(jnp.int32, (1, LANES), 1)
    scale = A_HEAD_DIM ** -0.5
    q_keep = (jnp.where(lane < A_HEAD_DIM, scale, 0.0).astype(BF16),
              jnp.where(lane < A_HEAD_DIM, 0.0, scale).astype(BF16))
    lower_lanes = lax.broadcasted_iota(jnp.int32, (A_BLOCK, LANES), 1) < A_HEAD_DIM
    key0 = lax.broadcasted_iota(jnp.int32, (n_keys, LANES), 0) == 0
    ones = jnp.ones((n_keys, LANES), BF16)

    def both_blocks(prev_ref, cur_ref, tile):
        cols = slice(tile * LANES, (tile + 1) * LANES)
        cat = jnp.concatenate([prev_ref[:, cols], cur_ref[:, cols]], axis=0).astype(F32)
        cat = jnp.where(key0, 0.0, cat)
        return cat.astype(BF16), pltpu.roll(cat, A_HEAD_DIM, 1).astype(BF16)

    stacks = []
    for tile in range(A_KV_WIDTH // LANES):
        k_cat, k_swp = both_blocks(kp_ref, kc_ref, tile)
        v_cat, v_swp = both_blocks(vp_ref, vc_ref, tile)
        q0 = tile * HEADS_PER_TILE * A_GROUP // HEADS_PER_TILE
        stacks.append((k_cat, v_cat, [(q0, 0), (q0 + 1, 0), (q0 + 2, 1), (q0 + 3, 1)]))
        stacks.append((k_swp, v_swp, [(q0, 1), (q0 + 1, 1), (q0 + 2, 0), (q0 + 3, 0)]))

    scores = []
    for k_tile, _, members in stacks:
        q4 = jnp.concatenate(
            [q_ref[:, qt * LANES:(qt + 1) * LANES] * q_keep[half] for qt, half in members], axis=0)
        scores.append(lax.dot_general(q4, k_tile, (((1,), (1,)), ((), ())),
                                      preferred_element_type=F32))
    s = jnp.concatenate(scores, axis=0)
    s = s + jnp.concatenate(
        [bias_ref[first, qt * HEADS_PER_TILE + half] for _, _, members in stacks for qt, half in members],
        axis=0)
    p = jnp.exp(s - jnp.max(s, axis=-1, keepdims=True)).astype(BF16)

    normed = {}
    rows_per_stack = len(stacks[0][2]) * A_BLOCK
    for i, (_, v_tile, members) in enumerate(stacks):
        p_i = p[i * rows_per_stack:(i + 1) * rows_per_stack]
        o = jnp.dot(p_i, v_tile, preferred_element_type=F32)
        denom = jnp.dot(p_i, ones, preferred_element_type=F32)
        o = o / denom
        for j, member in enumerate(members):
            normed[member] = o[j * A_BLOCK:(j + 1) * A_BLOCK]
    for qt in range(A_Q_WIDTH // LANES):
        o_ref[:, qt * LANES:(qt + 1) * LANES] = jnp.where(
            lower_lanes, normed[(qt, 0)], normed[(qt, 1)]).astype(o_ref.dtype)


def _swa(proj, sinks, rel_bias, batch, seq):
    t = proj.shape[0]
    nb = seq // A_BLOCK
    qcol = OFF_QA // A_Q_WIDTH
    kcol = OFF_KA // A_KV_WIDTH
    vcol = OFF_VA // A_KV_WIDTH
    prev = lambda r: jnp.maximum(r - 1, 0)
    smem = functools.partial(pl.BlockSpec, memory_space=pltpu.SMEM)
    return pl.pallas_call(
        functools.partial(_swa_kernel, nb),
        grid=(t // A_BLOCK,),
        in_specs=[
            pl.BlockSpec((A_BLOCK, A_Q_WIDTH), lambda r: (r, qcol)),
            pl.BlockSpec((A_BLOCK, A_KV_WIDTH), lambda r: (prev(r), kcol)),
            pl.BlockSpec((A_BLOCK, A_KV_WIDTH), lambda r: (r, kcol)),
            pl.BlockSpec((A_BLOCK, A_KV_WIDTH), lambda r: (prev(r), vcol)),
            pl.BlockSpec((A_BLOCK, A_KV_WIDTH), lambda r: (r, vcol)),
            smem(), smem(),
        ],
        out_specs=pl.BlockSpec((A_BLOCK, A_Q_WIDTH), lambda r: (r, 0)),
        out_shape=jax.ShapeDtypeStruct((t, A_Q_WIDTH), BF16),
        scratch_shapes=[pltpu.VMEM((2, A_HEADS, A_BLOCK, 2 * A_BLOCK), F32)],
        compiler_params=_cparams(("arbitrary",)),
        name="swa",
    )(proj, proj, proj, proj, proj, sinks, rel_bias)


GLA_CHUNKS_PER_STEP = 4


def _split3(x):
    hi = x.astype(BF16)
    r1 = x - hi.astype(F32)
    mid = r1.astype(BF16)
    lo = (r1 - mid.astype(F32)).astype(BF16)
    return jnp.concatenate([hi, mid, lo], axis=0)


def _gla_kernel(q_ref, k_ref, v0_ref, v1_ref, gl_ref, og0_ref, og1_ref, wgu_ref, bg_ref, ng_ref,
                o_ref, s_ref):
    c = B_CHUNK
    n_chunks = GLA_CHUNKS_PER_STEP
    n_rows = n_chunks * c
    heads_per_half = HALF_V // B_VAL_DIM
    v_refs = (v0_ref, v1_ref)
    og_refs = (og0_ref, og1_ref)
    chunk_rows = [slice(j * c, (j + 1) * c) for j in range(n_chunks)]
    key_cols = [slice(h * B_KEY_DIM, (h + 1) * B_KEY_DIM) for h in range(B_HEADS)]
    units = [(j, h) for j in range(n_chunks) for h in range(B_HEADS)]

    def v_of(refs, j, h):
        lo = (h % heads_per_half) * B_VAL_DIM
        rows = slice(None) if j is None else chunk_rows[j]
        return refs[h // heads_per_half][rows, lo:lo + B_VAL_DIM]

    @pl.when(pl.program_id(1) == 0)
    def _():
        s_ref[...] = jnp.zeros_like(s_ref)

    glin = jnp.dot(gl_ref[...], wgu_ref[...], preferred_element_type=F32) + bg_ref[...]
    log_a = (jnp.minimum(glin, 0.0) - jnp.log(1.0 + jnp.exp(-jnp.abs(glin)))) / B_GATE_TAU

    ri = lax.broadcasted_iota(jnp.int32, (n_rows, 3 * n_rows), 0)
    ci = lax.broadcasted_iota(jnp.int32, (n_rows, 3 * n_rows), 1)
    ci = ci - jnp.where(ci >= n_rows, n_rows, 0) - jnp.where(ci >= 2 * n_rows, n_rows, 0)
    shift = int(math.log2(c))
    same_chunk = lax.shift_right_logical(ri, shift) == lax.shift_right_logical(ci, shift)
    tri3 = ((ri >= ci) & same_chunk).astype(BF16)
    b = jnp.dot(tri3, _split3(log_a), preferred_element_type=F32)
    last_rows = [b[(j + 1) * c - 1:(j + 1) * c, :] for j in range(n_chunks)]
    b_last = jnp.concatenate([jnp.broadcast_to(r, (c, b.shape[1])) for r in last_rows], axis=0)

    qf = q_ref[...].astype(F32) * (B_KEY_DIM ** -0.5)
    kf = k_ref[...].astype(F32)
    q_dec = (qf * jnp.exp(b)).astype(BF16)
    k_dec = (kf * jnp.exp(-b)).astype(BF16)
    k_state = kf * jnp.exp(b_last - b)
    pad = jnp.zeros((8 - n_chunks, b.shape[1]), F32)
    decay_rows = jnp.exp(jnp.concatenate(last_rows + [pad], axis=0))

    ri = lax.broadcasted_iota(jnp.int32, (c, c), 0)
    ci = lax.broadcasted_iota(jnp.int32, (c, c), 1)
    causal = ri >= ci
    att = {}
    for j, h in units:
        a = lax.dot_general(q_dec[chunk_rows[j], key_cols[h]], k_dec[chunk_rows[j], key_cols[h]],
                            (((1,), (1,)), ((), ())), preferred_element_type=F32)
        att[j, h] = jnp.where(causal, a, 0.0).astype(BF16)
    o_intra = {u: jnp.dot(att[u], v_of(v_refs, *u), preferred_element_type=F32) for u in units}
    ds = {(j, h): jnp.dot(k_state[chunk_rows[j], key_cols[h]].T.astype(BF16), v_of(v_refs, j, h),
                          preferred_element_type=F32) for j, h in units}

    entering = {}
    for h in range(B_HEADS):
        decay_t = decay_rows[:, key_cols[h]].T
        state = s_ref[h]
        for j in range(n_chunks):
            entering[j, h] = state.astype(BF16)
            state = decay_t[:, j:j + 1] * state + ds[j, h]
        s_ref[h] = state
    o_inter = {(j, h): jnp.dot(q_dec[chunk_rows[j], key_cols[h]], entering[j, h],
                               preferred_element_type=F32) for j, h in units}

    ng = ng_ref[...]
    for h in range(B_HEADS):
        o = jnp.concatenate([o_intra[j, h] + o_inter[j, h] for j in range(n_chunks)], axis=0)
        gate = v_of(og_refs, None, h).astype(F32)
        y = _rmsnorm_rows(o, ng) * (gate * jax.nn.sigmoid(gate))
        o_ref[:, h * B_VAL_DIM:(h + 1) * B_VAL_DIM] = y.astype(o_ref.dtype)


def _gla(proj, wgu, bg, ng, batch, seq):
    t = proj.shape[0]
    rows = GLA_CHUNKS_PER_STEP * B_CHUNK
    steps = seq // rows
    rb = lambda b, s: b * steps + s
    return pl.pallas_call(
        _gla_kernel,
        grid=(batch, steps),
        in_specs=[
            pl.BlockSpec((rows, B_QK_WIDTH), lambda b, s: (rb(b, s), OFF_QB // B_QK_WIDTH)),
            pl.BlockSpec((rows, B_QK_WIDTH), lambda b, s: (rb(b, s), OFF_KB // B_QK_WIDTH)),
            pl.BlockSpec((rows, HALF_V), lambda b, s: (rb(b, s), OFF_VB // HALF_V)),
            pl.BlockSpec((rows, HALF_V), lambda b, s: (rb(b, s), OFF_VB // HALF_V + 1)),
            pl.BlockSpec((rows, LANES), lambda b, s: (rb(b, s), OFF_GLOW // LANES)),
            pl.BlockSpec((rows, HALF_V), lambda b, s: (rb(b, s), OFF_OBG // HALF_V)),
            pl.BlockSpec((rows, HALF_V), lambda b, s: (rb(b, s), OFF_OBG // HALF_V + 1)),
            _const_spec(wgu.shape), _const_spec(bg.shape), _const_spec(ng.shape),
        ],
        out_specs=pl.BlockSpec((rows, B_V_WIDTH), lambda b, s: (rb(b, s), 0)),
        out_shape=jax.ShapeDtypeStruct((t, B_V_WIDTH), BF16),
        scratch_shapes=[pltpu.VMEM((B_HEADS, B_KEY_DIM, B_VAL_DIM), F32)],
        compiler_params=_cparams(("parallel", "arbitrary")),
        name="gla",
    )(proj, proj, proj, proj, proj, proj, proj, wgu, bg, ng)


def _merge_kernel(a_ref, b_ref, ga_ref, gb_ref, x_ref, wa_ref, wb_ref, wo_ref, h_ref):
    ya = jnp.dot(a_ref[...], wa_ref[...], preferred_element_type=F32)
    yb = jnp.dot(b_ref[...], wb_ref[...], preferred_element_type=F32)
    merged = (jax.nn.sigmoid(ga_ref[...].astype(F32)) * ya
              + jax.nn.sigmoid(gb_ref[...].astype(F32)) * yb)
    h_ref[...] = x_ref[...] + jnp.dot(merged.astype(BF16), wo_ref[...], preferred_element_type=F32)


def _merge(attn, gla, proj, x2, wa, wb, wo, tm):
    t, d = x2.shape
    return pl.pallas_call(
        _merge_kernel,
        grid=(t // tm,),
        in_specs=[
            pl.BlockSpec((tm, A_Q_WIDTH), lambda i: (i, 0)),
            pl.BlockSpec((tm, B_V_WIDTH), lambda i: (i, 0)),
            pl.BlockSpec((tm, d), lambda i: (i, OFF_GATE_A // D_MODEL)),
            pl.BlockSpec((tm, d), lambda i: (i, OFF_GATE_B // D_MODEL)),
            pl.BlockSpec((tm, d), lambda i: (i, 0)),
            _const_spec(wa.shape), _const_spec(wb.shape), _const_spec(wo.shape),
        ],
        out_specs=pl.BlockSpec((tm, d), lambda i: (i, 0)),
        out_shape=jax.ShapeDtypeStruct((t, d), F32),
        compiler_params=_cparams(("parallel",)),
        name="merge",
    )(attn, gla, proj, proj, x2, wa, wb, wo)


def _ffn_kernel(h_ref, gz_ref, wg_ref, wu_ref, wd_ref, gf_ref, o_ref, z_ref, acc_ref):
    f = pl.program_id(1)
    n_chunks = h_ref.shape[0] // NORM_ROWS

    @pl.when(f == 0)
    def _():
        gz = gz_ref[...]

        def body(c, carry):
            rows = pl.ds(pl.multiple_of(c * NORM_ROWS, NORM_ROWS), NORM_ROWS)
            z_ref[rows, :] = _rmsnorm_rows(h_ref[rows, :], gz).astype(BF16)
            return carry

        lax.fori_loop(0, n_chunks, body, 0)
        acc_ref[...] = jnp.zeros_like(acc_ref)

    z = z_ref[...]
    g = jnp.dot(z, wg_ref[...], preferred_element_type=F32)
    u = jnp.dot(z, wu_ref[...], preferred_element_type=F32)
    act = (g * jax.nn.sigmoid(g) * u).astype(BF16)
    acc_ref[...] += jnp.dot(act, wd_ref[...], preferred_element_type=F32)

    @pl.when(f == pl.num_programs(1) - 1)
    def _():
        gf = gf_ref[...]

        def body(c, carry):
            rows = pl.ds(pl.multiple_of(c * NORM_ROWS, NORM_ROWS), NORM_ROWS)
            o_ref[rows, :] = _rmsnorm_rows(h_ref[rows, :] + acc_ref[rows, :], gf)
            return carry

        lax.fori_loop(0, n_chunks, body, 0)


def _ffn(h, gz, wg, wu, wd, gf, tm, tf):
    t, d = h.shape
    f = wg.shape[1]
    return pl.pallas_call(
        _ffn_kernel,
        grid=(t // tm, f // tf),
        in_specs=[
            pl.BlockSpec((tm, d), lambda i, j: (i, 0)),
            pl.BlockSpec((1, d), lambda i, j: (0, 0)),
            pl.BlockSpec((d, tf), lambda i, j: (0, j)),
            pl.BlockSpec((d, tf), lambda i, j: (0, j)),
            pl.BlockSpec((tf, d), lambda i, j: (j, 0)),
            pl.BlockSpec((1, d), lambda i, j: (0, 0)),
        ],
        out_specs=pl.BlockSpec((tm, d), lambda i, j: (i, 0)),
        out_shape=jax.ShapeDtypeStruct((t, d), F32),
        scratch_shapes=[pltpu.VMEM((tm, d), BF16), pltpu.VMEM((tm, d), F32)],
        compiler_params=_cparams(("parallel", "arbitrary")),
        name="ffn",
    )(h, gz, wg, wu, wd, gf)


PACK_COLS = 256


def _pack_w_in_kernel(w_ref, o_ref):
    n_mix = OFF_OBG - OFF_QA
    glow_end = n_mix + B_GATE_RANK
    obg_end = glow_end + B_V_WIDTH
    n_in = w_ref.shape[0]
    o_ref[OFF_GATE_A:OFF_QA, :] = w_ref[obg_end:n_in, :].astype(BF16)
    o_ref[OFF_QA:OFF_OBG, :] = w_ref[0:n_mix, :].astype(BF16)
    o_ref[OFF_OBG:OFF_GLOW, :] = w_ref[glow_end:obg_end, :].astype(BF16)
    o_ref[OFF_GLOW:OFF_GLOW + B_GATE_RANK, :] = w_ref[n_mix:glow_end, :].astype(BF16)
    n_zero = o_ref.shape[0] - OFF_GLOW - B_GATE_RANK
    o_ref[OFF_GLOW + B_GATE_RANK:, :] = jnp.zeros((n_zero, o_ref.shape[1]), BF16)


def _pack_w_in(w_t, n_total):
    n_in, d = w_t.shape
    return pl.pallas_call(
        _pack_w_in_kernel,
        grid=(d // PACK_COLS,),
        in_specs=[pl.BlockSpec((n_in, PACK_COLS), lambda i: (0, i))],
        out_specs=pl.BlockSpec((n_total, PACK_COLS), lambda i: (0, i)),
        out_shape=jax.ShapeDtypeStruct((n_total, d), BF16),
        compiler_params=_cparams(("parallel",)),
        name="pack_w_in",
    )(w_t)


def kernel(x, norm_mix_g, w_in, sinks, rel_bias, w_gate_up, b_gate, gla_norm_g, w_proj_a, w_proj_b,
           w_out, norm_ffn_g, w_ffn_gate, w_ffn_up, w_ffn_down, norm_final_g):
    batch, seq, d = x.shape
    assert d == D_MODEL and w_in.shape[0] == 1, "single-layer geometry"
    t = batch * seq
    x2 = x.reshape(t, d)

    tn_in = 1024
    n_total = -(-PROJ_USED // tn_in) * tn_in
    w_in_p = _pack_w_in(w_in[0].T, n_total)
    proj = _inproj(x2, norm_mix_g, w_in_p, tm=1024, tn=tn_in)

    attn = _swa(proj, sinks, rel_bias, batch, seq)

    wgu = jnp.zeros((LANES, B_QK_WIDTH), BF16).at[:B_GATE_RANK].set(w_gate_up[0].astype(BF16))
    gla = _gla(proj, wgu, b_gate, gla_norm_g, batch, seq)

    h = _merge(attn, gla, proj, x2, w_proj_a[0].astype(BF16), w_proj_b[0].astype(BF16),
               w_out[0].astype(BF16), tm=512)

    out = _ffn(h, norm_ffn_g, w_ffn_gate[0].astype(BF16), w_ffn_up[0].astype(BF16),
               w_ffn_down[0].astype(BF16), norm_final_g.reshape(1, d), tm=512, tf=512)
    return out.reshape(batch, seq, d)
```

```python
import functools
import math

import numpy as np
import jax
import jax.numpy as jnp
from jax import lax
from jax.experimental import pallas as pl
from jax.experimental.pallas import tpu as pltpu

F32 = jnp.float32
BF16 = jnp.bfloat16

D_MODEL = 2048
A_HEADS = 16
A_KV_HEADS = 4
A_HEAD_DIM = 64
A_GROUP = A_HEADS // A_KV_HEADS
WINDOW = 128
A_BLOCK = 128
A_Q_WIDTH = A_HEADS * A_HEAD_DIM
A_KV_WIDTH = A_KV_HEADS * A_HEAD_DIM
N_BUCKETS = 32
MAX_DISTANCE = 128
B_HEADS = 4
B_KEY_DIM = 128
B_VAL_DIM = 256
B_QK_WIDTH = B_HEADS * B_KEY_DIM
B_V_WIDTH = B_HEADS * B_VAL_DIM
B_GATE_RANK = 16
B_GATE_TAU = 16.0
B_CHUNK = 64
EPS = 1e-6
NEG_INF = -1e30

LANES = 128

OFF_GATE_A = 0
OFF_GATE_B = OFF_GATE_A + D_MODEL
OFF_QA = OFF_GATE_B + D_MODEL
OFF_KA = OFF_QA + A_Q_WIDTH
OFF_VA = OFF_KA + A_KV_WIDTH
OFF_QB = OFF_VA + A_KV_WIDTH
OFF_KB = OFF_QB + B_QK_WIDTH
OFF_VB = OFF_KB + B_QK_WIDTH
OFF_OBG = OFF_VB + B_V_WIDTH
OFF_GLOW = OFF_OBG + B_V_WIDTH
PROJ_USED = OFF_GLOW + LANES
HALF_V = B_V_WIDTH // 2

VMEM_LIMIT = 56 * 1024 * 1024


def _cparams(sem):
    return pltpu.CompilerParams(dimension_semantics=sem, vmem_limit_bytes=VMEM_LIMIT)


def _const_spec(shape):
    return pl.BlockSpec(shape, lambda *_: (0,) * len(shape), pipeline_mode=pl.Buffered(1))


def _rmsnorm_rows(x, g):
    ms = jnp.mean(x * x, axis=-1, keepdims=True)
    return x * lax.rsqrt(ms + EPS) * g


NORM_ROWS = 128


BF16_SUBLANES = 16


def _cast_specs(weights, n_chunks, chunk_of):
    in_specs, out_specs, shapes = [], [], []
    for w in weights:
        rows, rem = divmod(w.shape[0], n_chunks)
        assert rem == 0 and rows % BF16_SUBLANES == 0, (w.shape, n_chunks)
        for specs in (in_specs, out_specs):
            specs.append(pl.BlockSpec((rows, w.shape[1]), lambda *idx: (chunk_of(*idx), 0)))
        shapes.append(jax.ShapeDtypeStruct(w.shape, BF16))
    return in_specs, out_specs, shapes


def _cast_blocks(in_refs, out_refs):
    for src, dst in zip(in_refs, out_refs):
        dst[...] = src[...].astype(dst.dtype)


def _inproj_kernel(n_cast, tail, x_ref, g_ref, w_ref, *rest):
    cast_in, o_ref, cast_out, u_ref = rest[:n_cast], rest[n_cast], rest[n_cast + 1:-1], rest[-1]
    j = pl.program_id(1)

    @pl.when(j == 0)
    def _():
        g = g_ref[...]

        def body(c, carry):
            rows = pl.ds(pl.multiple_of(c * NORM_ROWS, NORM_ROWS), NORM_ROWS)
            u_ref[rows, :] = _rmsnorm_rows(x_ref[rows, :], g).astype(BF16)
            return carry

        lax.fori_loop(0, x_ref.shape[0] // NORM_ROWS, body, 0)
        _cast_blocks(cast_in, cast_out)

    def project(n_cols):
        o_ref[:, :n_cols] = lax.dot_general(u_ref[...], w_ref[:n_cols, :], (((1,), (1,)), ((), ())),
                                            preferred_element_type=F32).astype(o_ref.dtype)

    last = pl.num_programs(1) - 1
    pl.when(j != last)(lambda: project(o_ref.shape[1]))
    pl.when(j == last)(lambda: project(tail))


def _inproj(x2, g, w_t, cast_weights, tm, tn):
    t, d = x2.shape
    n = w_t.shape[0]
    n_tiles = pl.cdiv(n, tn)
    tail = n - (n_tiles - 1) * tn
    cast_in, cast_out, cast_shapes = _cast_specs(cast_weights, t // tm, lambda i, j: i)
    outs = pl.pallas_call(
        functools.partial(_inproj_kernel, len(cast_weights), tail),
        grid=(t // tm, n_tiles),
        in_specs=[
            pl.BlockSpec((tm, d), lambda i, j: (i, 0)),
            pl.BlockSpec((1, d), lambda i, j: (0, 0)),
            pl.BlockSpec((tn, d), lambda i, j: (j, 0)),
        ] + cast_in,
        out_specs=[pl.BlockSpec((tm, tn), lambda i, j: (i, j))] + cast_out,
        out_shape=[jax.ShapeDtypeStruct((t, n), BF16)] + cast_shapes,
        scratch_shapes=[pltpu.VMEM((tm, d), BF16)],
        compiler_params=_cparams(("parallel", "arbitrary")),
        name="inproj",
    )(x2, g, w_t, *cast_weights)
    return outs[0], outs[1:]


def _bucket_starts():
    max_exact = N_BUCKETS // 2
    d = np.arange(WINDOW)
    large = max_exact + (np.log(np.maximum(d, 1).astype(np.float32) / max_exact)
                         / math.log(MAX_DISTANCE / max_exact)
                         * (N_BUCKETS - max_exact)).astype(np.int32)
    bucket = np.where(d < max_exact, d, np.minimum(large, N_BUCKETS - 1))
    starts = []
    for b in range(N_BUCKETS):
        hit = np.nonzero(bucket == b)[0]
        if hit.size:
            assert np.all(np.diff(hit) == 1)
            starts.append((b, int(hit[0])))
    return starts


HEADS_PER_TILE = LANES // A_HEAD_DIM


def _swa_kernel(nb, n_cast, q_ref, kp_ref, kc_ref, vp_ref, vc_ref, sink_ref, rb_ref, *rest):
    cast_in, o_ref, cast_out, bias_ref = rest[:n_cast], rest[n_cast], rest[n_cast + 1:-1], rest[-1]
    _cast_blocks(cast_in, cast_out)
    r = pl.program_id(0)
    n_keys = 2 * A_BLOCK

    @pl.when(r == 0)
    def _():
        row = lax.broadcasted_iota(jnp.int32, (A_BLOCK, n_keys), 0)
        col = lax.broadcasted_iota(jnp.int32, (A_BLOCK, n_keys), 1)
        dist = row + A_BLOCK - col
        band = (dist >= 0) & (dist < WINDOW)
        starts = _bucket_starts()
        for h in range(A_HEADS):
            val = jnp.full(dist.shape, rb_ref[starts[0][0], h], F32)
            for b, s in starts[1:]:
                val = jnp.where(dist >= s, rb_ref[b, h], val)
            val = jnp.where(band, val, NEG_INF)
            sink = sink_ref[0, h]
            bias_ref[0, h] = jnp.where(col == 0, sink, val)
            bias_ref[1, h] = jnp.where(col == 0, sink, jnp.where(col >= A_BLOCK, val, NEG_INF))

    first = (lax.rem(r, nb) == 0).astype(jnp.int32)
    lane = lax.broadcasted_iota(jnp.int32, (1, LANES), 1)
    scale = A_HEAD_DIM ** -0.5
    q_keep = (jnp.where(lane < A_HEAD_DIM, scale, 0.0).astype(BF16),
              jnp.where(lane < A_HEAD_DIM, 0.0, scale).astype(BF16))
    lower_lanes = lax.broadcasted_iota(jnp.int32, (A_BLOCK, LANES), 1) < A_HEAD_DIM
    key0 = lax.broadcasted_iota(jnp.int32, (n_keys, LANES), 0) == 0
    ones = jnp.ones((n_keys, LANES), BF16)

    def both_blocks(prev_ref, cur_ref, tile):
        cols = slice(tile * LANES, (tile + 1) * LANES)
        cat = jnp.concatenate([prev_ref[:, cols], cur_ref[:, cols]], axis=0).astype(F32)
        cat = jnp.where(key0, 0.0, cat)
        return cat.astype(BF16), pltpu.roll(cat, A_HEAD_DIM, 1).astype(BF16)

    stacks = []
    for tile in range(A_KV_WIDTH // LANES):
        k_cat, k_swp = both_blocks(kp_ref, kc_ref, tile)
        v_cat, v_swp = both_blocks(vp_ref, vc_ref, tile)
        q0 = tile * HEADS_PER_TILE * A_GROUP // HEADS_PER_TILE
        stacks.append((k_cat, v_cat, [(q0, 0), (q0 + 1, 0), (q0 + 2, 1), (q0 + 3, 1)]))
        stacks.append((k_swp, v_swp, [(q0, 1), (q0 + 1, 1), (q0 + 2, 0), (q0 + 3, 0)]))

    scores = []
    for k_tile, _, members in stacks:
        q4 = jnp.concatenate(
            [q_ref[:, qt * LANES:(qt + 1) * LANES] * q_keep[half] for qt, half in members], axis=0)
        scores.append(lax.dot_general(q4, k_tile, (((1,), (1,)), ((), ())),
                                      preferred_element_type=F32))
    s = jnp.concatenate(scores, axis=0)
    s = s + jnp.concatenate(
        [bias_ref[first, qt * HEADS_PER_TILE + half] for _, _, members in stacks for qt, half in members],
        axis=0)
    p = jnp.exp(s - jnp.max(s, axis=-1, keepdims=True)).astype(BF16)

    normed = {}
    rows_per_stack = len(stacks[0][2]) * A_BLOCK
    for i, (_, v_tile, members) in enumerate(stacks):
        p_i = p[i * rows_per_stack:(i + 1) * rows_per_stack]
        o = jnp.dot(p_i, v_tile, preferred_element_type=F32)
        denom = jnp.dot(p_i, ones, preferred_element_type=F32)
        o = o / denom
        for j, member in enumerate(members):
            normed[member] = o[j * A_BLOCK:(j + 1) * A_BLOCK]
    for qt in range(A_Q_WIDTH // LANES):
        o_ref[:, qt * LANES:(qt + 1) * LANES] = jnp.where(
            lower_lanes, normed[(qt, 0)], normed[(qt, 1)]).astype(o_ref.dtype)


def _swa(proj, sinks, rel_bias, cast_weights, batch, seq):
    t = proj.shape[0]
    nb = seq // A_BLOCK
    qcol = OFF_QA // A_Q_WIDTH
    kcol = OFF_KA // A_KV_WIDTH
    vcol = OFF_VA // A_KV_WIDTH
    prev = lambda r: jnp.maximum(r - 1, 0)
    smem = functools.partial(pl.BlockSpec, memory_space=pltpu.SMEM)
    cast_in, cast_out, cast_shapes = _cast_specs(cast_weights, t // A_BLOCK, lambda r: r)
    outs = pl.pallas_call(
        functools.partial(_swa_kernel, nb, len(cast_weights)),
        grid=(t // A_BLOCK,),
        in_specs=[
            pl.BlockSpec((A_BLOCK, A_Q_WIDTH), lambda r: (r, qcol)),
            pl.BlockSpec((A_BLOCK, A_KV_WIDTH), lambda r: (prev(r), kcol)),
            pl.BlockSpec((A_BLOCK, A_KV_WIDTH), lambda r: (r, kcol)),
            pl.BlockSpec((A_BLOCK, A_KV_WIDTH), lambda r: (prev(r), vcol)),
            pl.BlockSpec((A_BLOCK, A_KV_WIDTH), lambda r: (r, vcol)),
            smem(), smem(),
        ] + cast_in,
        out_specs=[pl.BlockSpec((A_BLOCK, A_Q_WIDTH), lambda r: (r, 0))] + cast_out,
        out_shape=[jax.ShapeDtypeStruct((t, A_Q_WIDTH), BF16)] + cast_shapes,
        scratch_shapes=[pltpu.VMEM((2, A_HEADS, A_BLOCK, 2 * A_BLOCK), F32)],
        compiler_params=_cparams(("arbitrary",)),
        name="swa",
    )(proj, proj, proj, proj, proj, sinks, rel_bias, *cast_weights)
    return outs[0], outs[1:]


GLA_CHUNKS_PER_STEP = 4


def _split3(x):
    hi = x.astype(BF16)
    r1 = x - hi.astype(F32)
    mid = r1.astype(BF16)
    lo = (r1 - mid.astype(F32)).astype(BF16)
    return jnp.concatenate([hi, mid, lo], axis=0)


def _gla_kernel(n_cast, q_ref, k_ref, v0_ref, v1_ref, gl_ref, og0_ref, og1_ref, wgu_ref, bg_ref, ng_ref,
                *rest):
    cast_in, o_ref, cast_out, s_ref = rest[:n_cast], rest[n_cast], rest[n_cast + 1:-1], rest[-1]
    _cast_blocks(cast_in, cast_out)
    c = B_CHUNK
    n_chunks = GLA_CHUNKS_PER_STEP
    n_rows = n_chunks * c
    heads_per_half = HALF_V // B_VAL_DIM
    v_refs = (v0_ref, v1_ref)
    og_refs = (og0_ref, og1_ref)
    chunk_rows = [slice(j * c, (j + 1) * c) for j in range(n_chunks)]
    key_cols = [slice(h * B_KEY_DIM, (h + 1) * B_KEY_DIM) for h in range(B_HEADS)]
    units = [(j, h) for j in range(n_chunks) for h in range(B_HEADS)]

    def v_of(refs, j, h):
        lo = (h % heads_per_half) * B_VAL_DIM
        rows = slice(None) if j is None else chunk_rows[j]
        return refs[h // heads_per_half][rows, lo:lo + B_VAL_DIM]

    @pl.when(pl.program_id(1) == 0)
    def _():
        s_ref[...] = jnp.zeros_like(s_ref)

    glin = jnp.dot(gl_ref[...], wgu_ref[...], preferred_element_type=F32) + bg_ref[...]
    log_a = (jnp.minimum(glin, 0.0) - jnp.log(1.0 + jnp.exp(-jnp.abs(glin)))) / B_GATE_TAU

    ri = lax.broadcasted_iota(jnp.int32, (n_rows, 3 * n_rows), 0)
    ci = lax.broadcasted_iota(jnp.int32, (n_rows, 3 * n_rows), 1)
    ci = ci - jnp.where(ci >= n_rows, n_rows, 0) - jnp.where(ci >= 2 * n_rows, n_rows, 0)
    shift = int(math.log2(c))
    same_chunk = lax.shift_right_logical(ri, shift) == lax.shift_right_logical(ci, shift)
    tri3 = ((ri >= ci) & same_chunk).astype(BF16)
    b = jnp.dot(tri3, _split3(log_a), preferred_element_type=F32)
    last_rows = [b[(j + 1) * c - 1:(j + 1) * c, :] for j in range(n_chunks)]
    b_last = jnp.concatenate([jnp.broadcast_to(r, (c, b.shape[1])) for r in last_rows], axis=0)

    qf = q_ref[...].astype(F32) * (B_KEY_DIM ** -0.5)
    kf = k_ref[...].astype(F32)
    q_dec = (qf * jnp.exp(b)).astype(BF16)
    k_dec = (kf * jnp.exp(-b)).astype(BF16)
    k_state = kf * jnp.exp(b_last - b)
    pad = jnp.zeros((8 - n_chunks, b.shape[1]), F32)
    decay_rows = jnp.exp(jnp.concatenate(last_rows + [pad], axis=0))

    ri = lax.broadcasted_iota(jnp.int32, (c, c), 0)
    ci = lax.broadcasted_iota(jnp.int32, (c, c), 1)
    causal = ri >= ci
    att = {}
    for j, h in units:
        a = lax.dot_general(q_dec[chunk_rows[j], key_cols[h]], k_dec[chunk_rows[j], key_cols[h]],
                            (((1,), (1,)), ((), ())), preferred_element_type=F32)
        att[j, h] = jnp.where(causal, a, 0.0).astype(BF16)
    o_intra = {u: jnp.dot(att[u], v_of(v_refs, *u), preferred_element_type=F32) for u in units}
    ds = {(j, h): jnp.dot(k_state[chunk_rows[j], key_cols[h]].T.astype(BF16), v_of(v_refs, j, h),
                          preferred_element_type=F32) for j, h in units}

    entering = {}
    for h in range(B_HEADS):
        decay_t = decay_rows[:, key_cols[h]].T
        state = s_ref[h]
        for j in range(n_chunks):
            entering[j, h] = state.astype(BF16)
            state = decay_t[:, j:j + 1] * state + ds[j, h]
        s_ref[h] = state
    o_inter = {(j, h): jnp.dot(q_dec[chunk_rows[j], key_cols[h]], entering[j, h],
                               preferred_element_type=F32) for j, h in units}

    ng = ng_ref[...]
    for h in range(B_HEADS):
        o = jnp.concatenate([o_intra[j, h] + o_inter[j, h] for j in range(n_chunks)], axis=0)
        gate = v_of(og_refs, None, h).astype(F32)
        y = _rmsnorm_rows(o, ng) * (gate * jax.nn.sigmoid(gate))
        o_ref[:, h * B_VAL_DIM:(h + 1) * B_VAL_DIM] = y.astype(o_ref.dtype)


def _gla(proj, wgu, bg, ng, cast_weights, batch, seq):
    t = proj.shape[0]
    rows = GLA_CHUNKS_PER_STEP * B_CHUNK
    steps = seq // rows
    rb = lambda b, s: b * steps + s
    cast_in, cast_out, cast_shapes = _cast_specs(cast_weights, batch * steps, rb)
    outs = pl.pallas_call(
        functools.partial(_gla_kernel, len(cast_weights)),
        grid=(batch, steps),
        in_specs=[
            pl.BlockSpec((rows, B_QK_WIDTH), lambda b, s: (rb(b, s), OFF_QB // B_QK_WIDTH)),
            pl.BlockSpec((rows, B_QK_WIDTH), lambda b, s: (rb(b, s), OFF_KB // B_QK_WIDTH)),
            pl.BlockSpec((rows, HALF_V), lambda b, s: (rb(b, s), OFF_VB // HALF_V)),
            pl.BlockSpec((rows, HALF_V), lambda b, s: (rb(b, s), OFF_VB // HALF_V + 1)),
            pl.BlockSpec((rows, LANES), lambda b, s: (rb(b, s), OFF_GLOW // LANES)),
            pl.BlockSpec((rows, HALF_V), lambda b, s: (rb(b, s), OFF_OBG // HALF_V)),
            pl.BlockSpec((rows, HALF_V), lambda b, s: (rb(b, s), OFF_OBG // HALF_V + 1)),
            _const_spec(wgu.shape), _const_spec(bg.shape), _const_spec(ng.shape),
        ] + cast_in,
        out_specs=[pl.BlockSpec((rows, B_V_WIDTH), lambda b, s: (rb(b, s), 0))] + cast_out,
        out_shape=[jax.ShapeDtypeStruct((t, B_V_WIDTH), BF16)] + cast_shapes,
        scratch_shapes=[pltpu.VMEM((B_HEADS, B_KEY_DIM, B_VAL_DIM), F32)],
        compiler_params=_cparams(("parallel", "arbitrary")),
        name="gla",
    )(proj, proj, proj, proj, proj, proj, proj, wgu, bg, ng, *cast_weights)
    return outs[0], outs[1:]


def _merge_kernel(a_ref, b_ref, ga_ref, gb_ref, x_ref, wa_ref, wb_ref, wo_ref, h_ref):
    ya = jnp.dot(a_ref[...], wa_ref[...], preferred_element_type=F32)
    yb = jnp.dot(b_ref[...], wb_ref[...], preferred_element_type=F32)
    merged = (jax.nn.sigmoid(ga_ref[...].astype(F32)) * ya
              + jax.nn.sigmoid(gb_ref[...].astype(F32)) * yb)
    h_ref[...] = x_ref[...] + jnp.dot(merged.astype(BF16), wo_ref[...], preferred_element_type=F32)


def _merge(attn, gla, proj, x2, wa, wb, wo, tm):
    t, d = x2.shape
    return pl.pallas_call(
        _merge_kernel,
        grid=(t // tm,),
        in_specs=[
            pl.BlockSpec((tm, A_Q_WIDTH), lambda i: (i, 0)),
            pl.BlockSpec((tm, B_V_WIDTH), lambda i: (i, 0)),
            pl.BlockSpec((tm, d), lambda i: (i, OFF_GATE_A // D_MODEL)),
            pl.BlockSpec((tm, d), lambda i: (i, OFF_GATE_B // D_MODEL)),
            pl.BlockSpec((tm, d), lambda i: (i, 0)),
            _const_spec(wa.shape), _const_spec(wb.shape), _const_spec(wo.shape),
        ],
        out_specs=pl.BlockSpec((tm, d), lambda i: (i, 0)),
        out_shape=jax.ShapeDtypeStruct((t, d), F32),
        compiler_params=_cparams(("parallel",)),
        name="merge",
    )(attn, gla, proj, proj, x2, wa, wb, wo)


def _ffn_kernel(h_ref, gz_ref, wg_ref, wu_ref, wd_ref, gf_ref, o_ref, z_ref, acc_ref):
    f = pl.program_id(1)
    n_chunks = h_ref.shape[0] // NORM_ROWS

    @pl.when(f == 0)
    def _():
        gz = gz_ref[...]

        def body(c, carry):
            rows = pl.ds(pl.multiple_of(c * NORM_ROWS, NORM_ROWS), NORM_ROWS)
            z_ref[rows, :] = _rmsnorm_rows(h_ref[rows, :], gz).astype(BF16)
            return carry

        lax.fori_loop(0, n_chunks, body, 0)
        acc_ref[...] = jnp.zeros_like(acc_ref)

    z = z_ref[...]
    g = jnp.dot(z, wg_ref[...], preferred_element_type=F32)
    u = jnp.dot(z, wu_ref[...], preferred_element_type=F32)
    act = (g * jax.nn.sigmoid(g) * u).astype(BF16)
    acc_ref[...] += jnp.dot(act, wd_ref[...], preferred_element_type=F32)

    @pl.when(f == pl.num_programs(1) - 1)
    def _():
        gf = gf_ref[...]

        def body(c, carry):
            rows = pl.ds(pl.multiple_of(c * NORM_ROWS, NORM_ROWS), NORM_ROWS)
            o_ref[rows, :] = _rmsnorm_rows(h_ref[rows, :] + acc_ref[rows, :], gf)
            return carry

        lax.fori_loop(0, n_chunks, body, 0)


def _ffn(h, gz, wg, wu, wd, gf, tm, tf):
    t, d = h.shape
    f = wg.shape[1]
    return pl.pallas_call(
        _ffn_kernel,
        grid=(t // tm, f // tf),
        in_specs=[
            pl.BlockSpec((tm, d), lambda i, j: (i, 0)),
            pl.BlockSpec((1, d), lambda i, j: (0, 0)),
            pl.BlockSpec((d, tf), lambda i, j: (0, j)),
            pl.BlockSpec((d, tf), lambda i, j: (0, j)),
            pl.BlockSpec((tf, d), lambda i, j: (j, 0)),
            pl.BlockSpec((1, d), lambda i, j: (0, 0)),
        ],
        out_specs=pl.BlockSpec((tm, d), lambda i, j: (i, 0)),
        out_shape=jax.ShapeDtypeStruct((t, d), F32),
        scratch_shapes=[pltpu.VMEM((tm, d), BF16), pltpu.VMEM((tm, d), F32)],
        compiler_params=_cparams(("parallel", "arbitrary")),
        name="ffn",
    )(h, gz, wg, wu, wd, gf)


PACK_COLS = 256


def _pack_w_in_kernel(w_ref, o_ref):
    n_mix = OFF_OBG - OFF_QA
    glow_end = n_mix + B_GATE_RANK
    obg_end = glow_end + B_V_WIDTH
    n_in = w_ref.shape[0]
    o_ref[OFF_GATE_A:OFF_QA, :] = w_ref[obg_end:n_in, :].astype(BF16)
    o_ref[OFF_QA:OFF_OBG, :] = w_ref[0:n_mix, :].astype(BF16)
    o_ref[OFF_OBG:OFF_GLOW, :] = w_ref[glow_end:obg_end, :].astype(BF16)
    o_ref[OFF_GLOW:OFF_GLOW + B_GATE_RANK, :] = w_ref[n_mix:glow_end, :].astype(BF16)
    n_zero = o_ref.shape[0] - OFF_GLOW - B_GATE_RANK
    o_ref[OFF_GLOW + B_GATE_RANK:, :] = jnp.zeros((n_zero, o_ref.shape[1]), BF16)


def _pack_w_in(w_t, n_total):
    n_in, d = w_t.shape
    return pl.pallas_call(
        _pack_w_in_kernel,
        grid=(d // PACK_COLS,),
        in_specs=[pl.BlockSpec((n_in, PACK_COLS), lambda i: (0, i))],
        out_specs=pl.BlockSpec((n_total, PACK_COLS), lambda i: (0, i)),
        out_shape=jax.ShapeDtypeStruct((n_total, d), BF16),
        compiler_params=_cparams(("parallel",)),
        name="pack_w_in",
    )(w_t)


def kernel(x, norm_mix_g, w_in, sinks, rel_bias, w_gate_up, b_gate, gla_norm_g, w_proj_a, w_proj_b,
           w_out, norm_ffn_g, w_ffn_gate, w_ffn_up, w_ffn_down, norm_final_g):
    batch, seq, d = x.shape
    assert d == D_MODEL and w_in.shape[0] == 1, "single-layer geometry"
    t = batch * seq
    x2 = x.reshape(t, d)

    w_in_p = _pack_w_in(w_in[0].T, PROJ_USED)
    proj, (wa, wb, wo) = _inproj(x2, norm_mix_g, w_in_p, (w_proj_a[0], w_proj_b[0], w_out[0]),
                                 tm=1024, tn=1024)

    attn, (wg, wu) = _swa(proj, sinks, rel_bias, (w_ffn_gate[0], w_ffn_up[0]), batch, seq)

    wgu = jnp.zeros((LANES, B_QK_WIDTH), BF16).at[:B_GATE_RANK].set(w_gate_up[0].astype(BF16))
    gla, (wd,) = _gla(proj, wgu, b_gate, gla_norm_g, (w_ffn_down[0],), batch, seq)

    h = _merge(attn, gla, proj, x2, wa, wb, wo, tm=512)

    out = _ffn(h, norm_ffn_g, wg, wu, wd, norm_final_g.reshape(1, d), tm=512, tf=512)
    return out.reshape(batch, seq, d)
```

```python
import functools
import math

import numpy as np
import jax
import jax.numpy as jnp
from jax import lax
from jax.experimental import pallas as pl
from jax.experimental.pallas import tpu as pltpu

F32 = jnp.float32
BF16 = jnp.bfloat16

D_MODEL = 2048
A_HEADS = 16
A_KV_HEADS = 4
A_HEAD_DIM = 64
A_GROUP = A_HEADS // A_KV_HEADS
WINDOW = 128
A_BLOCK = 128
A_Q_WIDTH = A_HEADS * A_HEAD_DIM
A_KV_WIDTH = A_KV_HEADS * A_HEAD_DIM
N_BUCKETS = 32
MAX_DISTANCE = 128
B_HEADS = 4
B_KEY_DIM = 128
B_VAL_DIM = 256
B_QK_WIDTH = B_HEADS * B_KEY_DIM
B_V_WIDTH = B_HEADS * B_VAL_DIM
B_GATE_RANK = 16
B_GATE_TAU = 16.0
B_CHUNK = 64
EPS = 1e-6
NEG_INF = -1e30

LANES = 128

OFF_GATE_A = 0
OFF_GATE_B = OFF_GATE_A + D_MODEL
OFF_QA = OFF_GATE_B + D_MODEL
OFF_KA = OFF_QA + A_Q_WIDTH
OFF_VA = OFF_KA + A_KV_WIDTH
OFF_QB = OFF_VA + A_KV_WIDTH
OFF_KB = OFF_QB + B_QK_WIDTH
OFF_VB = OFF_KB + B_QK_WIDTH
OFF_OBG = OFF_VB + B_V_WIDTH
OFF_GLOW = OFF_OBG + B_V_WIDTH
PROJ_USED = OFF_GLOW + LANES
HALF_V = B_V_WIDTH // 2

VMEM_LIMIT = 60 * 1024 * 1024


def _cparams(sem):
    return pltpu.CompilerParams(dimension_semantics=sem, vmem_limit_bytes=VMEM_LIMIT)


def _const_spec(shape):
    return pl.BlockSpec(shape, lambda *_: (0,) * len(shape), pipeline_mode=pl.Buffered(1))


def _rmsnorm_rows(x, g):
    ms = jnp.mean(x * x, axis=-1, keepdims=True)
    return x * lax.rsqrt(ms + EPS) * g


NORM_ROWS = 128


BF16_SUBLANES = 16


def _cast_specs(weights, n_chunks, chunk_of):
    in_specs, out_specs, shapes = [], [], []
    for w in weights:
        rows, rem = divmod(w.shape[0], n_chunks)
        assert rem == 0 and rows % BF16_SUBLANES == 0, (w.shape, n_chunks)
        for specs in (in_specs, out_specs):
            specs.append(pl.BlockSpec((rows, w.shape[1]), lambda *idx: (chunk_of(*idx), 0)))
        shapes.append(jax.ShapeDtypeStruct(w.shape, BF16))
    return in_specs, out_specs, shapes


def _cast_blocks(in_refs, out_refs):
    for src, dst in zip(in_refs, out_refs):
        dst[...] = src[...].astype(dst.dtype)


def _inproj_kernel(n_cast, tail, x_ref, g_ref, w_ref, *rest):
    cast_in, o_ref, cast_out, u_ref = rest[:n_cast], rest[n_cast], rest[n_cast + 1:-1], rest[-1]
    j = pl.program_id(1)

    @pl.when(j == 0)
    def _():
        g = g_ref[...]

        def body(c, carry):
            rows = pl.ds(pl.multiple_of(c * NORM_ROWS, NORM_ROWS), NORM_ROWS)
            u_ref[rows, :] = _rmsnorm_rows(x_ref[rows, :], g).astype(BF16)
            return carry

        lax.fori_loop(0, x_ref.shape[0] // NORM_ROWS, body, 0)
        _cast_blocks(cast_in, cast_out)

    def project(n_cols):
        o_ref[:, :n_cols] = lax.dot_general(u_ref[...], w_ref[:n_cols, :], (((1,), (1,)), ((), ())),
                                            preferred_element_type=F32).astype(o_ref.dtype)

    last = pl.num_programs(1) - 1
    pl.when(j != last)(lambda: project(o_ref.shape[1]))
    pl.when(j == last)(lambda: project(tail))


def _inproj(x2, g, w_t, cast_weights, tm, tn):
    t, d = x2.shape
    n = w_t.shape[0]
    n_tiles = pl.cdiv(n, tn)
    tail = n - (n_tiles - 1) * tn
    cast_in, cast_out, cast_shapes = _cast_specs(cast_weights, t // tm, lambda i, j: i)
    outs = pl.pallas_call(
        functools.partial(_inproj_kernel, len(cast_weights), tail),
        grid=(t // tm, n_tiles),
        in_specs=[
            pl.BlockSpec((tm, d), lambda i, j: (i, 0)),
            pl.BlockSpec((1, d), lambda i, j: (0, 0)),
            pl.BlockSpec((tn, d), lambda i, j: (j, 0)),
        ] + cast_in,
        out_specs=[pl.BlockSpec((tm, tn), lambda i, j: (i, j))] + cast_out,
        out_shape=[jax.ShapeDtypeStruct((t, n), BF16)] + cast_shapes,
        scratch_shapes=[pltpu.VMEM((tm, d), BF16)],
        compiler_params=_cparams(("parallel", "arbitrary")),
        name="inproj",
    )(x2, g, w_t, *cast_weights)
    return outs[0], outs[1:]


def _bucket_starts():
    max_exact = N_BUCKETS // 2
    d = np.arange(WINDOW)
    large = max_exact + (np.log(np.maximum(d, 1).astype(np.float32) / max_exact)
                         / math.log(MAX_DISTANCE / max_exact)
                         * (N_BUCKETS - max_exact)).astype(np.int32)
    bucket = np.where(d < max_exact, d, np.minimum(large, N_BUCKETS - 1))
    starts = []
    for b in range(N_BUCKETS):
        hit = np.nonzero(bucket == b)[0]
        if hit.size:
            assert np.all(np.diff(hit) == 1)
            starts.append((b, int(hit[0])))
    return starts


HEADS_PER_TILE = LANES // A_HEAD_DIM
SWA_BLOCKS_PER_STEP = 2


def _swa_kernel(nb, n_cast, q_ref, kp_ref, kc_ref, vp_ref, vc_ref, sink_ref, rb_ref, *rest):
    cast_in, o_ref, cast_out, bias_ref = rest[:n_cast], rest[n_cast], rest[n_cast + 1:-1], rest[-1]
    _cast_blocks(cast_in, cast_out)
    r = pl.program_id(0)
    n_keys = 2 * A_BLOCK

    @pl.when(r == 0)
    def _():
        row = lax.broadcasted_iota(jnp.int32, (A_BLOCK, n_keys), 0)
        col = lax.broadcasted_iota(jnp.int32, (A_BLOCK, n_keys), 1)
        dist = row + A_BLOCK - col
        band = (dist >= 0) & (dist < WINDOW)
        starts = _bucket_starts()
        for h in range(A_HEADS):
            val = jnp.full(dist.shape, rb_ref[starts[0][0], h], F32)
            for b, s in starts[1:]:
                val = jnp.where(dist >= s, rb_ref[b, h], val)
            val = jnp.where(band, val, NEG_INF)
            sink = sink_ref[0, h]
            bias_ref[0, h] = jnp.where(col == 0, sink, val)
            bias_ref[1, h] = jnp.where(col == 0, sink, jnp.where(col >= A_BLOCK, val, NEG_INF))

    lane = lax.broadcasted_iota(jnp.int32, (1, LANES), 1)
    scale = A_HEAD_DIM ** -0.5
    q_keep = (jnp.where(lane < A_HEAD_DIM, scale, 0.0).astype(BF16),
              jnp.where(lane < A_HEAD_DIM, 0.0, scale).astype(BF16))
    lower_lanes = lax.broadcasted_iota(jnp.int32, (A_BLOCK, LANES), 1) < A_HEAD_DIM
    key0 = lax.broadcasted_iota(jnp.int32, (n_keys, LANES), 0) == 0
    ones = jnp.ones((n_keys, LANES), BF16)

    def attend(q_rows, prev, cur, first):
        def both_blocks(which, tile):
            cols = slice(tile * LANES, (tile + 1) * LANES)
            cat = jnp.concatenate([prev[which][:, cols], cur[which][:, cols]], axis=0).astype(F32)
            cat = jnp.where(key0, 0.0, cat)
            return cat.astype(BF16), pltpu.roll(cat, A_HEAD_DIM, 1).astype(BF16)

        stacks = []
        for tile in range(A_KV_WIDTH // LANES):
            k_cat, k_swp = both_blocks(0, tile)
            v_cat, v_swp = both_blocks(1, tile)
            q0 = tile * A_GROUP
            stacks.append((k_cat, jnp.concatenate([v_cat, ones], axis=1),
                           [(q0, 0), (q0 + 1, 0), (q0 + 2, 1), (q0 + 3, 1)]))
            stacks.append((k_swp, jnp.concatenate([v_swp, ones], axis=1),
                           [(q0, 1), (q0 + 1, 1), (q0 + 2, 0), (q0 + 3, 0)]))

        scores = []
        for k_tile, _, members in stacks:
            q4 = jnp.concatenate(
                [q_ref[q_rows, qt * LANES:(qt + 1) * LANES] * q_keep[half] for qt, half in members],
                axis=0)
            scores.append(lax.dot_general(q4, k_tile, (((1,), (1,)), ((), ())),
                                          preferred_element_type=F32))
        s = jnp.concatenate(scores, axis=0)
        s = s + jnp.concatenate(
            [bias_ref[first, qt * HEADS_PER_TILE + half]
             for _, _, members in stacks for qt, half in members], axis=0)
        p = jnp.exp(s - jnp.max(s, axis=-1, keepdims=True)).astype(BF16)

        normed = {}
        rows_per_stack = len(stacks[0][2]) * A_BLOCK
        for i, (_, v_ones, members) in enumerate(stacks):
            ov = jnp.dot(p[i * rows_per_stack:(i + 1) * rows_per_stack], v_ones,
                         preferred_element_type=F32)
            o = ov[:, :LANES] / ov[:, LANES:]
            for j, member in enumerate(members):
                normed[member] = o[j * A_BLOCK:(j + 1) * A_BLOCK]
        for qt in range(A_Q_WIDTH // LANES):
            o_ref[q_rows, qt * LANES:(qt + 1) * LANES] = jnp.where(
                lower_lanes, normed[(qt, 0)], normed[(qt, 1)]).astype(o_ref.dtype)

    blocks_per_step = q_ref.shape[0] // A_BLOCK
    kv_prev = (kp_ref[...], vp_ref[...])
    for sub in range(blocks_per_step):
        rows = slice(sub * A_BLOCK, (sub + 1) * A_BLOCK)
        kv_cur = (kc_ref[rows, :], vc_ref[rows, :])
        if sub == 0:
            first = (lax.rem(r * blocks_per_step, nb) == 0).astype(jnp.int32)
        else:
            first = 0
        attend(rows, kv_prev, kv_cur, first)
        kv_prev = kv_cur


def _swa(proj, sinks, rel_bias, cast_weights, batch, seq):
    t = proj.shape[0]
    nb = seq // A_BLOCK
    qcol = OFF_QA // A_Q_WIDTH
    kcol = OFF_KA // A_KV_WIDTH
    vcol = OFF_VA // A_KV_WIDTH
    bps = SWA_BLOCKS_PER_STEP
    assert nb % bps == 0
    rows = bps * A_BLOCK
    prev = lambda r: jnp.maximum(r * bps - 1, 0)
    smem = functools.partial(pl.BlockSpec, memory_space=pltpu.SMEM)
    cast_in, cast_out, cast_shapes = _cast_specs(cast_weights, t // rows, lambda r: r)
    outs = pl.pallas_call(
        functools.partial(_swa_kernel, nb, len(cast_weights)),
        grid=(t // rows,),
        in_specs=[
            pl.BlockSpec((rows, A_Q_WIDTH), lambda r: (r, qcol)),
            pl.BlockSpec((A_BLOCK, A_KV_WIDTH), lambda r: (prev(r), kcol)),
            pl.BlockSpec((rows, A_KV_WIDTH), lambda r: (r, kcol)),
            pl.BlockSpec((A_BLOCK, A_KV_WIDTH), lambda r: (prev(r), vcol)),
            pl.BlockSpec((rows, A_KV_WIDTH), lambda r: (r, vcol)),
            smem(), smem(),
        ] + cast_in,
        out_specs=[pl.BlockSpec((rows, A_Q_WIDTH), lambda r: (r, 0))] + cast_out,
        out_shape=[jax.ShapeDtypeStruct((t, A_Q_WIDTH), BF16)] + cast_shapes,
        scratch_shapes=[pltpu.VMEM((2, A_HEADS, A_BLOCK, 2 * A_BLOCK), F32)],
        compiler_params=_cparams(("arbitrary",)),
        name="swa",
    )(proj, proj, proj, proj, proj, sinks, rel_bias, *cast_weights)
    return outs[0], outs[1:]


GLA_CHUNKS_PER_STEP = 4


def _split3(x):
    hi = x.astype(BF16)
    r1 = x - hi.astype(F32)
    mid = r1.astype(BF16)
    lo = (r1 - mid.astype(F32)).astype(BF16)
    return jnp.concatenate([hi, mid, lo], axis=0)


def _gla_kernel(n_cast, q_ref, k_ref, v0_ref, v1_ref, gl_ref, og0_ref, og1_ref, wgu_ref, bg_ref, ng_ref,
                *rest):
    cast_in, o_ref, cast_out, s_ref = rest[:n_cast], rest[n_cast], rest[n_cast + 1:-1], rest[-1]
    _cast_blocks(cast_in, cast_out)
    c = B_CHUNK
    n_chunks = GLA_CHUNKS_PER_STEP
    n_rows = n_chunks * c
    heads_per_half = HALF_V // B_VAL_DIM
    v_refs = (v0_ref, v1_ref)
    og_refs = (og0_ref, og1_ref)
    chunk_rows = [slice(j * c, (j + 1) * c) for j in range(n_chunks)]
    key_cols = [slice(h * B_KEY_DIM, (h + 1) * B_KEY_DIM) for h in range(B_HEADS)]
    units = [(j, h) for j in range(n_chunks) for h in range(B_HEADS)]

    def v_of(refs, j, h):
        lo = (h % heads_per_half) * B_VAL_DIM
        rows = slice(None) if j is None else chunk_rows[j]
        return refs[h // heads_per_half][rows, lo:lo + B_VAL_DIM]

    @pl.when(pl.program_id(1) == 0)
    def _():
        s_ref[...] = jnp.zeros_like(s_ref)

    glin = jnp.dot(gl_ref[...], wgu_ref[...], preferred_element_type=F32) + bg_ref[...]
    log_a = (jnp.minimum(glin, 0.0) - jnp.log(1.0 + jnp.exp(-jnp.abs(glin)))) / B_GATE_TAU

    ri = lax.broadcasted_iota(jnp.int32, (n_rows, 3 * n_rows), 0)
    ci = lax.broadcasted_iota(jnp.int32, (n_rows, 3 * n_rows), 1)
    ci = ci - jnp.where(ci >= n_rows, n_rows, 0) - jnp.where(ci >= 2 * n_rows, n_rows, 0)
    shift = int(math.log2(c))
    same_chunk = lax.shift_right_logical(ri, shift) == lax.shift_right_logical(ci, shift)
    tri3 = ((ri >= ci) & same_chunk).astype(BF16)
    b = jnp.dot(tri3, _split3(log_a), preferred_element_type=F32)
    last_rows = [b[(j + 1) * c - 1:(j + 1) * c, :] for j in range(n_chunks)]
    b_last = jnp.concatenate([jnp.broadcast_to(r, (c, b.shape[1])) for r in last_rows], axis=0)

    qf = q_ref[...].astype(F32) * (B_KEY_DIM ** -0.5)
    kf = k_ref[...].astype(F32)
    q_dec = (qf * jnp.exp(b)).astype(BF16)
    k_dec = (kf * jnp.exp(-b)).astype(BF16)
    k_state = kf * jnp.exp(b_last - b)
    pad = jnp.zeros((8 - n_chunks, b.shape[1]), F32)
    decay_rows = jnp.exp(jnp.concatenate(last_rows + [pad], axis=0))

    ri = lax.broadcasted_iota(jnp.int32, (c, c), 0)
    ci = lax.broadcasted_iota(jnp.int32, (c, c), 1)
    causal = ri >= ci
    att = {}
    for j, h in units:
        a = lax.dot_general(q_dec[chunk_rows[j], key_cols[h]], k_dec[chunk_rows[j], key_cols[h]],
                            (((1,), (1,)), ((), ())), preferred_element_type=F32)
        att[j, h] = jnp.where(causal, a, 0.0).astype(BF16)
    o_intra = {u: jnp.dot(att[u], v_of(v_refs, *u), preferred_element_type=F32) for u in units}
    ds = {(j, h): jnp.dot(k_state[chunk_rows[j], key_cols[h]].T.astype(BF16), v_of(v_refs, j, h),
                          preferred_element_type=F32) for j, h in units}

    entering = {}
    for h in range(B_HEADS):
        decay_t = decay_rows[:, key_cols[h]].T
        state = s_ref[h]
        for j in range(n_chunks):
            entering[j, h] = state.astype(BF16)
            state = decay_t[:, j:j + 1] * state + ds[j, h]
        s_ref[h] = state
    o_inter = {(j, h): jnp.dot(q_dec[chunk_rows[j], key_cols[h]], entering[j, h],
                               preferred_element_type=F32) for j, h in units}

    ng = ng_ref[...]
    for h in range(B_HEADS):
        o = jnp.concatenate([o_intra[j, h] + o_inter[j, h] for j in range(n_chunks)], axis=0)
        gate = v_of(og_refs, None, h).astype(F32)
        y = _rmsnorm_rows(o, ng) * (gate * jax.nn.sigmoid(gate))
        o_ref[:, h * B_VAL_DIM:(h + 1) * B_VAL_DIM] = y.astype(o_ref.dtype)


def _gla(proj, wgu, bg, ng, cast_weights, batch, seq):
    t = proj.shape[0]
    rows = GLA_CHUNKS_PER_STEP * B_CHUNK
    steps = seq // rows
    rb = lambda b, s: b * steps + s
    cast_in, cast_out, cast_shapes = _cast_specs(cast_weights, batch * steps, rb)
    outs = pl.pallas_call(
        functools.partial(_gla_kernel, len(cast_weights)),
        grid=(batch, steps),
        in_specs=[
            pl.BlockSpec((rows, B_QK_WIDTH), lambda b, s: (rb(b, s), OFF_QB // B_QK_WIDTH)),
            pl.BlockSpec((rows, B_QK_WIDTH), lambda b, s: (rb(b, s), OFF_KB // B_QK_WIDTH)),
            pl.BlockSpec((rows, HALF_V), lambda b, s: (rb(b, s), OFF_VB // HALF_V)),
            pl.BlockSpec((rows, HALF_V), lambda b, s: (rb(b, s), OFF_VB // HALF_V + 1)),
            pl.BlockSpec((rows, LANES), lambda b, s: (rb(b, s), OFF_GLOW // LANES)),
            pl.BlockSpec((rows, HALF_V), lambda b, s: (rb(b, s), OFF_OBG // HALF_V)),
            pl.BlockSpec((rows, HALF_V), lambda b, s: (rb(b, s), OFF_OBG // HALF_V + 1)),
            _const_spec(wgu.shape), _const_spec(bg.shape), _const_spec(ng.shape),
        ] + cast_in,
        out_specs=[pl.BlockSpec((rows, B_V_WIDTH), lambda b, s: (rb(b, s), 0))] + cast_out,
        out_shape=[jax.ShapeDtypeStruct((t, B_V_WIDTH), BF16)] + cast_shapes,
        scratch_shapes=[pltpu.VMEM((B_HEADS, B_KEY_DIM, B_VAL_DIM), F32)],
        compiler_params=_cparams(("parallel", "arbitrary")),
        name="gla",
    )(proj, proj, proj, proj, proj, proj, proj, wgu, bg, ng, *cast_weights)
    return outs[0], outs[1:]


def _merge_kernel(a_ref, b_ref, ga_ref, gb_ref, x_ref, wa_ref, wb_ref, wo_ref, gz_ref, h_ref, z_ref):
    ya = jnp.dot(a_ref[...], wa_ref[...], preferred_element_type=F32)
    yb = jnp.dot(b_ref[...], wb_ref[...], preferred_element_type=F32)
    merged = (jax.nn.sigmoid(ga_ref[...].astype(F32)) * ya
              + jax.nn.sigmoid(gb_ref[...].astype(F32)) * yb)
    h = x_ref[...] + jnp.dot(merged.astype(BF16), wo_ref[...], preferred_element_type=F32)
    h_ref[...] = h
    z_ref[...] = _rmsnorm_rows(h, gz_ref[...]).astype(z_ref.dtype)


def _merge(attn, gla, proj, x2, wa, wb, wo, gz, tm):
    t, d = x2.shape
    row_block = pl.BlockSpec((tm, d), lambda i: (i, 0))
    return pl.pallas_call(
        _merge_kernel,
        grid=(t // tm,),
        in_specs=[
            pl.BlockSpec((tm, A_Q_WIDTH), lambda i: (i, 0)),
            pl.BlockSpec((tm, B_V_WIDTH), lambda i: (i, 0)),
            pl.BlockSpec((tm, d), lambda i: (i, OFF_GATE_A // D_MODEL)),
            pl.BlockSpec((tm, d), lambda i: (i, OFF_GATE_B // D_MODEL)),
            row_block,
            _const_spec(wa.shape), _const_spec(wb.shape), _const_spec(wo.shape), _const_spec(gz.shape),
        ],
        out_specs=[row_block, row_block],
        out_shape=[jax.ShapeDtypeStruct((t, d), F32), jax.ShapeDtypeStruct((t, d), BF16)],
        compiler_params=_cparams(("parallel",)),
        name="merge",
    )(attn, gla, proj, proj, x2, wa, wb, wo, gz)


def _ffn_kernel(z_ref, h_ref, wg_ref, wu_ref, wd_ref, gf_ref, o_ref, acc_ref):
    f = pl.program_id(1)
    z = z_ref[...]
    g = jnp.dot(z, wg_ref[...], preferred_element_type=F32)
    u = jnp.dot(z, wu_ref[...], preferred_element_type=F32)
    act = (g * jax.nn.sigmoid(g) * u).astype(BF16)
    acc = jnp.where(f == 0, 0.0, acc_ref[...])
    acc_ref[...] = acc + jnp.dot(act, wd_ref[...], preferred_element_type=F32)

    @pl.when(f == pl.num_programs(1) - 1)
    def _():
        gf = gf_ref[...]

        def body(c, carry):
            rows = pl.ds(pl.multiple_of(c * NORM_ROWS, NORM_ROWS), NORM_ROWS)
            o_ref[rows, :] = _rmsnorm_rows(h_ref[rows, :] + acc_ref[rows, :], gf)
            return carry

        lax.fori_loop(0, h_ref.shape[0] // NORM_ROWS, body, 0)


def _ffn(z, h, wg, wu, wd, gf, tm, tf):
    t, d = h.shape
    f = wg.shape[1]
    row_block = pl.BlockSpec((tm, d), lambda i, j: (i, 0))
    return pl.pallas_call(
        _ffn_kernel,
        grid=(t // tm, f // tf),
        in_specs=[
            row_block, row_block,
            pl.BlockSpec((d, tf), lambda i, j: (0, j)),
            pl.BlockSpec((d, tf), lambda i, j: (0, j)),
            pl.BlockSpec((tf, d), lambda i, j: (j, 0)),
            pl.BlockSpec((1, d), lambda i, j: (0, 0)),
        ],
        out_specs=row_block,
        out_shape=jax.ShapeDtypeStruct((t, d), F32),
        scratch_shapes=[pltpu.VMEM((tm, d), F32)],
        compiler_params=_cparams(("parallel", "arbitrary")),
        name="ffn",
    )(z, h, wg, wu, wd, gf)


PACK_COLS = 256


def _pack_w_in_kernel(w_ref, o_ref):
    n_mix = OFF_OBG - OFF_QA
    glow_end = n_mix + B_GATE_RANK
    obg_end = glow_end + B_V_WIDTH
    n_in = w_ref.shape[0]
    o_ref[OFF_GATE_A:OFF_QA, :] = w_ref[obg_end:n_in, :].astype(BF16)
    o_ref[OFF_QA:OFF_OBG, :] = w_ref[0:n_mix, :].astype(BF16)
    o_ref[OFF_OBG:OFF_GLOW, :] = w_ref[glow_end:obg_end, :].astype(BF16)
    o_ref[OFF_GLOW:OFF_GLOW + B_GATE_RANK, :] = w_ref[n_mix:glow_end, :].astype(BF16)
    n_zero = o_ref.shape[0] - OFF_GLOW - B_GATE_RANK
    o_ref[OFF_GLOW + B_GATE_RANK:, :] = jnp.zeros((n_zero, o_ref.shape[1]), BF16)


def _pack_w_in(w_t, n_total):
    n_in, d = w_t.shape
    return pl.pallas_call(
        _pack_w_in_kernel,
        grid=(d // PACK_COLS,),
        in_specs=[pl.BlockSpec((n_in, PACK_COLS), lambda i: (0, i))],
        out_specs=pl.BlockSpec((n_total, PACK_COLS), lambda i: (0, i)),
        out_shape=jax.ShapeDtypeStruct((n_total, d), BF16),
        compiler_params=_cparams(("parallel",)),
        name="pack_w_in",
    )(w_t)


def kernel(x, norm_mix_g, w_in, sinks, rel_bias, w_gate_up, b_gate, gla_norm_g, w_proj_a, w_proj_b,
           w_out, norm_ffn_g, w_ffn_gate, w_ffn_up, w_ffn_down, norm_final_g):
    batch, seq, d = x.shape
    assert d == D_MODEL and w_in.shape[0] == 1, "single-layer geometry"
    t = batch * seq
    x2 = x.reshape(t, d)

    w_in_p = _pack_w_in(w_in[0].T, PROJ_USED)
    proj, _ = _inproj(x2, norm_mix_g, w_in_p, (), tm=1024, tn=1024)

    attn, (wg, wu) = _swa(proj, sinks, rel_bias, (w_ffn_gate[0], w_ffn_up[0]), batch, seq)

    wgu = jnp.zeros((LANES, B_QK_WIDTH), BF16).at[:B_GATE_RANK].set(w_gate_up[0].astype(BF16))
    gla, (wa, wb, wo, wd) = _gla(proj, wgu, b_gate, gla_norm_g,
                                 (w_proj_a[0], w_proj_b[0], w_out[0], w_ffn_down[0]), batch, seq)

    h, z = _merge(attn, gla, proj, x2, wa, wb, wo, norm_ffn_g, tm=512)

    out = _ffn(z, h, wg, wu, wd, norm_final_g.reshape(1, d), tm=512, tf=512)
    return out.reshape(batch, seq, d)
```

```python
import functools
import math

import numpy as np
import jax
import jax.numpy as jnp
from jax import lax
from jax.experimental import pallas as pl
from jax.experimental.pallas import tpu as pltpu

F32 = jnp.float32
BF16 = jnp.bfloat16

D_MODEL = 2048
A_HEADS = 16
A_KV_HEADS = 4
A_HEAD_DIM = 64
A_GROUP = A_HEADS // A_KV_HEADS
WINDOW = 128
A_BLOCK = 128
A_Q_WIDTH = A_HEADS * A_HEAD_DIM
A_KV_WIDTH = A_KV_HEADS * A_HEAD_DIM
N_BUCKETS = 32
MAX_DISTANCE = 128
B_HEADS = 4
B_KEY_DIM = 128
B_VAL_DIM = 256
B_QK_WIDTH = B_HEADS * B_KEY_DIM
B_V_WIDTH = B_HEADS * B_VAL_DIM
B_GATE_RANK = 16
B_GATE_TAU = 16.0
B_CHUNK = 64
EPS = 1e-6
NEG_INF = -1e30

LANES = 128

OFF_GATE_A = 0
OFF_GATE_B = OFF_GATE_A + D_MODEL
OFF_QA = OFF_GATE_B + D_MODEL
OFF_KA = OFF_QA + A_Q_WIDTH
OFF_VA = OFF_KA + A_KV_WIDTH
OFF_QB = OFF_VA + A_KV_WIDTH
OFF_KB = OFF_QB + B_QK_WIDTH
OFF_VB = OFF_KB + B_QK_WIDTH
OFF_OBG = OFF_VB + B_V_WIDTH
OFF_GLOW = OFF_OBG + B_V_WIDTH
PROJ_USED = OFF_GLOW + LANES
HALF_V = B_V_WIDTH // 2

VMEM_LIMIT = 60 * 1024 * 1024


def _cparams(sem):
    return pltpu.CompilerParams(dimension_semantics=sem, vmem_limit_bytes=VMEM_LIMIT)


def _const_spec(shape):
    return pl.BlockSpec(shape, lambda *_: (0,) * len(shape), pipeline_mode=pl.Buffered(1))


def _rmsnorm_rows(x, g):
    ms = jnp.mean(x * x, axis=-1, keepdims=True)
    return x * lax.rsqrt(ms + EPS) * g


NORM_ROWS = 128


BF16_SUBLANES = 16


def _cast_specs(weights, n_chunks, chunk_of):
    in_specs, out_specs, shapes = [], [], []
    for w in weights:
        rows, rem = divmod(w.shape[0], n_chunks)
        assert rem == 0 and rows % BF16_SUBLANES == 0, (w.shape, n_chunks)
        for specs in (in_specs, out_specs):
            specs.append(pl.BlockSpec((rows, w.shape[1]), lambda *idx: (chunk_of(*idx), 0)))
        shapes.append(jax.ShapeDtypeStruct(w.shape, BF16))
    return in_specs, out_specs, shapes


def _cast_blocks(in_refs, out_refs):
    for src, dst in zip(in_refs, out_refs):
        dst[...] = src[...].astype(dst.dtype)


def _inproj_kernel(x_ref, g_ref, w_ref, o_ref, u_ref):
    @pl.when(pl.program_id(1) == 0)
    def _():
        g = g_ref[...]

        def body(c, carry):
            rows = pl.ds(pl.multiple_of(c * NORM_ROWS, NORM_ROWS), NORM_ROWS)
            u_ref[rows, :] = _rmsnorm_rows(x_ref[rows, :], g).astype(BF16)
            return carry

        lax.fori_loop(0, x_ref.shape[0] // NORM_ROWS, body, 0)

    o_ref[...] = lax.dot_general(u_ref[...], w_ref[...], (((1,), (1,)), ((), ())),
                                 preferred_element_type=F32).astype(o_ref.dtype)


def _inproj(x2, g, w_t, tm, tn):
    t, d = x2.shape
    n = w_t.shape[0]
    return pl.pallas_call(
        _inproj_kernel,
        grid=(t // tm, n // tn),
        in_specs=[
            pl.BlockSpec((tm, d), lambda i, j: (i, 0)),
            pl.BlockSpec((1, d), lambda i, j: (0, 0)),
            pl.BlockSpec((tn, d), lambda i, j: (j, 0)),
        ],
        out_specs=pl.BlockSpec((tm, tn), lambda i, j: (i, j)),
        out_shape=jax.ShapeDtypeStruct((t, n), BF16),
        scratch_shapes=[pltpu.VMEM((tm, d), BF16)],
        compiler_params=_cparams(("parallel", "arbitrary")),
        name="inproj",
    )(x2, g, w_t)


def _bucket_starts():
    max_exact = N_BUCKETS // 2
    d = np.arange(WINDOW)
    large = max_exact + (np.log(np.maximum(d, 1).astype(np.float32) / max_exact)
                         / math.log(MAX_DISTANCE / max_exact)
                         * (N_BUCKETS - max_exact)).astype(np.int32)
    bucket = np.where(d < max_exact, d, np.minimum(large, N_BUCKETS - 1))
    starts = []
    for b in range(N_BUCKETS):
        hit = np.nonzero(bucket == b)[0]
        if hit.size:
            assert np.all(np.diff(hit) == 1)
            starts.append((b, int(hit[0])))
    return starts


HEADS_PER_TILE = LANES // A_HEAD_DIM
SWA_BLOCKS_PER_STEP = 2


def _swa_kernel(nb, n_cast, q_ref, kp_ref, kc_ref, vp_ref, vc_ref, sink_ref, rb_ref, *rest):
    cast_in, o_ref, cast_out, bias_ref = rest[:n_cast], rest[n_cast], rest[n_cast + 1:-1], rest[-1]
    _cast_blocks(cast_in, cast_out)
    r = pl.program_id(0)
    n_keys = 2 * A_BLOCK

    @pl.when(r == 0)
    def _():
        row = lax.broadcasted_iota(jnp.int32, (A_BLOCK, n_keys), 0)
        col = lax.broadcasted_iota(jnp.int32, (A_BLOCK, n_keys), 1)
        dist = row + A_BLOCK - col
        band = (dist >= 0) & (dist < WINDOW)
        starts = _bucket_starts()
        for h in range(A_HEADS):
            val = jnp.full(dist.shape, rb_ref[starts[0][0], h], F32)
            for b, s in starts[1:]:
                val = jnp.where(dist >= s, rb_ref[b, h], val)
            val = jnp.where(band, val, NEG_INF)
            sink = sink_ref[0, h]
            bias_ref[0, h] = jnp.where(col == 0, sink, val)
            bias_ref[1, h] = jnp.where(col == 0, sink, jnp.where(col >= A_BLOCK, val, NEG_INF))

    lane = lax.broadcasted_iota(jnp.int32, (1, LANES), 1)
    scale = A_HEAD_DIM ** -0.5
    q_keep = (jnp.where(lane < A_HEAD_DIM, scale, 0.0).astype(BF16),
              jnp.where(lane < A_HEAD_DIM, 0.0, scale).astype(BF16))
    lower_lanes = lax.broadcasted_iota(jnp.int32, (A_BLOCK, LANES), 1) < A_HEAD_DIM
    key0 = lax.broadcasted_iota(jnp.int32, (n_keys, LANES), 0) == 0
    ones = jnp.ones((n_keys, LANES), BF16)

    def attend(q_rows, prev, cur, first):
        def both_blocks(which, tile):
            cols = slice(tile * LANES, (tile + 1) * LANES)
            cat = jnp.concatenate([prev[which][:, cols], cur[which][:, cols]], axis=0).astype(F32)
            cat = jnp.where(key0, 0.0, cat)
            return cat.astype(BF16), pltpu.roll(cat, A_HEAD_DIM, 1).astype(BF16)

        stacks = []
        for tile in range(A_KV_WIDTH // LANES):
            k_cat, k_swp = both_blocks(0, tile)
            v_cat, v_swp = both_blocks(1, tile)
            q0 = tile * A_GROUP
            stacks.append((k_cat, jnp.concatenate([v_cat, ones], axis=1),
                           [(q0, 0), (q0 + 1, 0), (q0 + 2, 1), (q0 + 3, 1)]))
            stacks.append((k_swp, jnp.concatenate([v_swp, ones], axis=1),
                           [(q0, 1), (q0 + 1, 1), (q0 + 2, 0), (q0 + 3, 0)]))

        scores = []
        for k_tile, _, members in stacks:
            q4 = jnp.concatenate(
                [q_ref[q_rows, qt * LANES:(qt + 1) * LANES] * q_keep[half] for qt, half in members],
                axis=0)
            scores.append(lax.dot_general(q4, k_tile, (((1,), (1,)), ((), ())),
                                          preferred_element_type=F32))
        s = jnp.concatenate(scores, axis=0)
        s = s + jnp.concatenate(
            [bias_ref[first, qt * HEADS_PER_TILE + half]
             for _, _, members in stacks for qt, half in members], axis=0)
        p = jnp.exp(s - jnp.max(s, axis=-1, keepdims=True)).astype(BF16)

        normed = {}
        rows_per_stack = len(stacks[0][2]) * A_BLOCK
        for i, (_, v_ones, members) in enumerate(stacks):
            ov = jnp.dot(p[i * rows_per_stack:(i + 1) * rows_per_stack], v_ones,
                         preferred_element_type=F32)
            o = ov[:, :LANES] / ov[:, LANES:]
            for j, member in enumerate(members):
                normed[member] = o[j * A_BLOCK:(j + 1) * A_BLOCK]
        for qt in range(A_Q_WIDTH // LANES):
            o_ref[q_rows, qt * LANES:(qt + 1) * LANES] = jnp.where(
                lower_lanes, normed[(qt, 0)], normed[(qt, 1)]).astype(o_ref.dtype)

    blocks_per_step = q_ref.shape[0] // A_BLOCK
    kv_prev = (kp_ref[...], vp_ref[...])
    for sub in range(blocks_per_step):
        rows = slice(sub * A_BLOCK, (sub + 1) * A_BLOCK)
        kv_cur = (kc_ref[rows, :], vc_ref[rows, :])
        if sub == 0:
            first = (lax.rem(r * blocks_per_step, nb) == 0).astype(jnp.int32)
        else:
            first = 0
        attend(rows, kv_prev, kv_cur, first)
        kv_prev = kv_cur


def _swa(proj, sinks, rel_bias, cast_weights, batch, seq):
    t = proj.shape[0]
    nb = seq // A_BLOCK
    qcol = OFF_QA // A_Q_WIDTH
    kcol = OFF_KA // A_KV_WIDTH
    vcol = OFF_VA // A_KV_WIDTH
    bps = SWA_BLOCKS_PER_STEP
    assert nb % bps == 0
    rows = bps * A_BLOCK
    prev = lambda r: jnp.maximum(r * bps - 1, 0)
    smem = functools.partial(pl.BlockSpec, memory_space=pltpu.SMEM)
    cast_in, cast_out, cast_shapes = _cast_specs(cast_weights, t // rows, lambda r: r)
    outs = pl.pallas_call(
        functools.partial(_swa_kernel, nb, len(cast_weights)),
        grid=(t // rows,),
        in_specs=[
            pl.BlockSpec((rows, A_Q_WIDTH), lambda r: (r, qcol)),
            pl.BlockSpec((A_BLOCK, A_KV_WIDTH), lambda r: (prev(r), kcol)),
            pl.BlockSpec((rows, A_KV_WIDTH), lambda r: (r, kcol)),
            pl.BlockSpec((A_BLOCK, A_KV_WIDTH), lambda r: (prev(r), vcol)),
            pl.BlockSpec((rows, A_KV_WIDTH), lambda r: (r, vcol)),
            smem(), smem(),
        ] + cast_in,
        out_specs=[pl.BlockSpec((rows, A_Q_WIDTH), lambda r: (r, 0))] + cast_out,
        out_shape=[jax.ShapeDtypeStruct((t, A_Q_WIDTH), BF16)] + cast_shapes,
        scratch_shapes=[pltpu.VMEM((2, A_HEADS, A_BLOCK, 2 * A_BLOCK), F32)],
        compiler_params=_cparams(("arbitrary",)),
        name="swa",
    )(proj, proj, proj, proj, proj, sinks, rel_bias, *cast_weights)
    return outs[0], outs[1:]


GLA_CHUNKS_PER_STEP = 4


def _split3(x):
    hi = x.astype(BF16)
    r1 = x - hi.astype(F32)
    mid = r1.astype(BF16)
    lo = (r1 - mid.astype(F32)).astype(BF16)
    return jnp.concatenate([hi, mid, lo], axis=0)


def _gla_kernel(n_cast, q_ref, k_ref, v0_ref, v1_ref, gl_ref, og0_ref, og1_ref, wgu_ref, bg_ref, ng_ref,
                *rest):
    cast_in, o_ref, cast_out, s_ref = rest[:n_cast], rest[n_cast], rest[n_cast + 1:-1], rest[-1]
    _cast_blocks(cast_in, cast_out)
    c = B_CHUNK
    n_chunks = GLA_CHUNKS_PER_STEP
    n_rows = n_chunks * c
    heads_per_half = HALF_V // B_VAL_DIM
    v_refs = (v0_ref, v1_ref)
    og_refs = (og0_ref, og1_ref)
    chunk_rows = [slice(j * c, (j + 1) * c) for j in range(n_chunks)]
    key_cols = [slice(h * B_KEY_DIM, (h + 1) * B_KEY_DIM) for h in range(B_HEADS)]
    units = [(j, h) for j in range(n_chunks) for h in range(B_HEADS)]

    def v_of(refs, j, h):
        lo = (h % heads_per_half) * B_VAL_DIM
        rows = slice(None) if j is None else chunk_rows[j]
        return refs[h // heads_per_half][rows, lo:lo + B_VAL_DIM]

    @pl.when(pl.program_id(1) == 0)
    def _():
        s_ref[...] = jnp.zeros_like(s_ref)

    glin = jnp.dot(gl_ref[...], wgu_ref[...], preferred_element_type=F32) + bg_ref[...]
    log_a = (jnp.minimum(glin, 0.0) - jnp.log(1.0 + jnp.exp(-jnp.abs(glin)))) / B_GATE_TAU

    ri = lax.broadcasted_iota(jnp.int32, (n_rows, 3 * n_rows), 0)
    ci = lax.broadcasted_iota(jnp.int32, (n_rows, 3 * n_rows), 1)
    ci = ci - jnp.where(ci >= n_rows, n_rows, 0) - jnp.where(ci >= 2 * n_rows, n_rows, 0)
    shift = int(math.log2(c))
    same_chunk = lax.shift_right_logical(ri, shift) == lax.shift_right_logical(ci, shift)
    tri3 = ((ri >= ci) & same_chunk).astype(BF16)
    b = jnp.dot(tri3, _split3(log_a), preferred_element_type=F32)
    last_rows = [b[(j + 1) * c - 1:(j + 1) * c, :] for j in range(n_chunks)]
    b_last = jnp.concatenate([jnp.broadcast_to(r, (c, b.shape[1])) for r in last_rows], axis=0)

    qf = q_ref[...].astype(F32) * (B_KEY_DIM ** -0.5)
    kf = k_ref[...].astype(F32)
    q_dec = (qf * jnp.exp(b)).astype(BF16)
    k_dec = (kf * jnp.exp(-b)).astype(BF16)
    k_state = kf * jnp.exp(b_last - b)
    pad = jnp.zeros((8 - n_chunks, b.shape[1]), F32)
    decay_rows = jnp.exp(jnp.concatenate(last_rows + [pad], axis=0))

    ri = lax.broadcasted_iota(jnp.int32, (c, c), 0)
    ci = lax.broadcasted_iota(jnp.int32, (c, c), 1)
    causal = ri >= ci
    att = {}
    for j, h in units:
        a = lax.dot_general(q_dec[chunk_rows[j], key_cols[h]], k_dec[chunk_rows[j], key_cols[h]],
                            (((1,), (1,)), ((), ())), preferred_element_type=F32)
        att[j, h] = jnp.where(causal, a, 0.0).astype(BF16)
    o_intra = {u: jnp.dot(att[u], v_of(v_refs, *u), preferred_element_type=F32) for u in units}
    ds = {(j, h): jnp.dot(k_state[chunk_rows[j], key_cols[h]].T.astype(BF16), v_of(v_refs, j, h),
                          preferred_element_type=F32) for j, h in units}

    entering = {}
    for h in range(B_HEADS):
        decay_t = decay_rows[:, key_cols[h]].T
        state = s_ref[h]
        for j in range(n_chunks):
            entering[j, h] = state.astype(BF16)
            state = decay_t[:, j:j + 1] * state + ds[j, h]
        s_ref[h] = state
    o_inter = {(j, h): jnp.dot(q_dec[chunk_rows[j], key_cols[h]], entering[j, h],
                               preferred_element_type=F32) for j, h in units}

    ng = ng_ref[...]
    for h in range(B_HEADS):
        o = jnp.concatenate([o_intra[j, h] + o_inter[j, h] for j in range(n_chunks)], axis=0)
        gate = v_of(og_refs, None, h).astype(F32)
        y = _rmsnorm_rows(o, ng) * (gate * jax.nn.sigmoid(gate))
        o_ref[:, h * B_VAL_DIM:(h + 1) * B_VAL_DIM] = y.astype(o_ref.dtype)


def _gla(proj, wgu, bg, ng, cast_weights, batch, seq):
    t = proj.shape[0]
    rows = GLA_CHUNKS_PER_STEP * B_CHUNK
    steps = seq // rows
    rb = lambda b, s: b * steps + s
    cast_in, cast_out, cast_shapes = _cast_specs(cast_weights, batch * steps, rb)
    outs = pl.pallas_call(
        functools.partial(_gla_kernel, len(cast_weights)),
        grid=(batch, steps),
        in_specs=[
            pl.BlockSpec((rows, B_QK_WIDTH), lambda b, s: (rb(b, s), OFF_QB // B_QK_WIDTH)),
            pl.BlockSpec((rows, B_QK_WIDTH), lambda b, s: (rb(b, s), OFF_KB // B_QK_WIDTH)),
            pl.BlockSpec((rows, HALF_V), lambda b, s: (rb(b, s), OFF_VB // HALF_V)),
            pl.BlockSpec((rows, HALF_V), lambda b, s: (rb(b, s), OFF_VB // HALF_V + 1)),
            pl.BlockSpec((rows, LANES), lambda b, s: (rb(b, s), OFF_GLOW // LANES)),
            pl.BlockSpec((rows, HALF_V), lambda b, s: (rb(b, s), OFF_OBG // HALF_V)),
            pl.BlockSpec((rows, HALF_V), lambda b, s: (rb(b, s), OFF_OBG // HALF_V + 1)),
            _const_spec(wgu.shape), _const_spec(bg.shape), _const_spec(ng.shape),
        ] + cast_in,
        out_specs=[pl.BlockSpec((rows, B_V_WIDTH), lambda b, s: (rb(b, s), 0))] + cast_out,
        out_shape=[jax.ShapeDtypeStruct((t, B_V_WIDTH), BF16)] + cast_shapes,
        scratch_shapes=[pltpu.VMEM((B_HEADS, B_KEY_DIM, B_VAL_DIM), F32)],
        compiler_params=_cparams(("parallel", "arbitrary")),
        name="gla",
    )(proj, proj, proj, proj, proj, proj, proj, wgu, bg, ng, *cast_weights)
    return outs[0], outs[1:]


def _merge_kernel(a_ref, b_ref, ga_ref, gb_ref, x_ref, wa_ref, wb_ref, wo_ref, gz_ref, h_ref, z_ref):
    ya = jnp.dot(a_ref[...], wa_ref[...], preferred_element_type=F32)
    yb = jnp.dot(b_ref[...], wb_ref[...], preferred_element_type=F32)
    merged = (jax.nn.sigmoid(ga_ref[...].astype(F32)) * ya
              + jax.nn.sigmoid(gb_ref[...].astype(F32)) * yb)
    h = x_ref[...] + jnp.dot(merged.astype(BF16), wo_ref[...], preferred_element_type=F32)
    h_ref[...] = h
    z_ref[...] = _rmsnorm_rows(h, gz_ref[...]).astype(z_ref.dtype)


def _merge(attn, gla, proj, x2, wa, wb, wo, gz, tm):
    t, d = x2.shape
    row_block = pl.BlockSpec((tm, d), lambda i: (i, 0))
    return pl.pallas_call(
        _merge_kernel,
        grid=(t // tm,),
        in_specs=[
            pl.BlockSpec((tm, A_Q_WIDTH), lambda i: (i, 0)),
            pl.BlockSpec((tm, B_V_WIDTH), lambda i: (i, 0)),
            pl.BlockSpec((tm, d), lambda i: (i, OFF_GATE_A // D_MODEL)),
            pl.BlockSpec((tm, d), lambda i: (i, OFF_GATE_B // D_MODEL)),
            row_block,
            _const_spec(wa.shape), _const_spec(wb.shape), _const_spec(wo.shape), _const_spec(gz.shape),
        ],
        out_specs=[row_block, row_block],
        out_shape=[jax.ShapeDtypeStruct((t, d), F32), jax.ShapeDtypeStruct((t, d), BF16)],
        compiler_params=_cparams(("parallel",)),
        name="merge",
    )(attn, gla, proj, proj, x2, wa, wb, wo, gz)


def _ffn_kernel(z_ref, h_ref, wg_ref, wu_ref, wd_ref, gf_ref, o_ref):
    f = pl.program_id(1)
    z = z_ref[...]
    g = jnp.dot(z, wg_ref[...], preferred_element_type=F32)
    u = jnp.dot(z, wu_ref[...], preferred_element_type=F32)
    act = (g * jax.nn.sigmoid(g) * u).astype(BF16)
    acc = jnp.where(f == 0, 0.0, o_ref[...])
    o_ref[...] = acc + jnp.dot(act, wd_ref[...], preferred_element_type=F32)

    @pl.when(f == pl.num_programs(1) - 1)
    def _():
        gf = gf_ref[...]

        def body(c, carry):
            rows = pl.ds(pl.multiple_of(c * NORM_ROWS, NORM_ROWS), NORM_ROWS)
            o_ref[rows, :] = _rmsnorm_rows(h_ref[rows, :] + o_ref[rows, :], gf)
            return carry

        lax.fori_loop(0, h_ref.shape[0] // NORM_ROWS, body, 0)


def _ffn(z, h, wg, wu, wd, gf, tm, tf):
    t, d = h.shape
    f = wg.shape[1]
    row_block = pl.BlockSpec((tm, d), lambda i, j: (i, 0))
    return pl.pallas_call(
        _ffn_kernel,
        grid=(t // tm, f // tf),
        in_specs=[
            row_block, row_block,
            pl.BlockSpec((d, tf), lambda i, j: (0, j)),
            pl.BlockSpec((d, tf), lambda i, j: (0, j)),
            pl.BlockSpec((tf, d), lambda i, j: (j, 0)),
            pl.BlockSpec((1, d), lambda i, j: (0, 0)),
        ],
        out_specs=row_block,
        out_shape=jax.ShapeDtypeStruct((t, d), F32),
        compiler_params=_cparams(("parallel", "arbitrary")),
        name="ffn",
    )(z, h, wg, wu, wd, gf)


PACK_COLS = 256


def _pack_w_in_kernel(w_ref, o_ref):
    n_mix = OFF_OBG - OFF_QA
    glow_end = n_mix + B_GATE_RANK
    obg_end = glow_end + B_V_WIDTH
    n_in = w_ref.shape[0]
    o_ref[OFF_GATE_A:OFF_QA, :] = w_ref[obg_end:n_in, :].astype(BF16)
    o_ref[OFF_QA:OFF_OBG, :] = w_ref[0:n_mix, :].astype(BF16)
    o_ref[OFF_OBG:OFF_GLOW, :] = w_ref[glow_end:obg_end, :].astype(BF16)
    o_ref[OFF_GLOW:OFF_GLOW + B_GATE_RANK, :] = w_ref[n_mix:glow_end, :].astype(BF16)
    n_zero = o_ref.shape[0] - OFF_GLOW - B_GATE_RANK
    o_ref[OFF_GLOW + B_GATE_RANK:, :] = jnp.zeros((n_zero, o_ref.shape[1]), BF16)


def _pack_w_in(w_t, n_total):
    n_in, d = w_t.shape
    return pl.pallas_call(
        _pack_w_in_kernel,
        grid=(d // PACK_COLS,),
        in_specs=[pl.BlockSpec((n_in, PACK_COLS), lambda i: (0, i))],
        out_specs=pl.BlockSpec((n_total, PACK_COLS), lambda i: (0, i)),
        out_shape=jax.ShapeDtypeStruct((n_total, d), BF16),
        compiler_params=_cparams(("parallel",)),
        name="pack_w_in",
    )(w_t)


def kernel(x, norm_mix_g, w_in, sinks, rel_bias, w_gate_up, b_gate, gla_norm_g, w_proj_a, w_proj_b,
           w_out, norm_ffn_g, w_ffn_gate, w_ffn_up, w_ffn_down, norm_final_g):
    batch, seq, d = x.shape
    assert d == D_MODEL and w_in.shape[0] == 1, "single-layer geometry"
    t = batch * seq
    x2 = x.reshape(t, d)

    tn_in = 1024
    w_in_p = _pack_w_in(w_in[0].T, pl.cdiv(PROJ_USED, tn_in) * tn_in)
    proj = _inproj(x2, norm_mix_g, w_in_p, tm=1024, tn=tn_in)

    attn, (wg, wu) = _swa(proj, sinks, rel_bias, (w_ffn_gate[0], w_ffn_up[0]), batch, seq)

    wgu = jnp.zeros((LANES, B_QK_WIDTH), BF16).at[:B_GATE_RANK].set(w_gate_up[0].astype(BF16))
    gla, (wa, wb, wo, wd) = _gla(proj, wgu, b_gate, gla_norm_g,
                                 (w_proj_a[0], w_proj_b[0], w_out[0], w_ffn_down[0]), batch, seq)

    h, z = _merge(attn, gla, proj, x2, wa, wb, wo, norm_ffn_g, tm=512)

    out = _ffn(z, h, wg, wu, wd, norm_final_g.reshape(1, d), tm=1024, tf=512)
    return out.reshape(batch, seq, d)
```

```python
import functools
import math

import numpy as np
import jax
import jax.numpy as jnp
from jax import lax
from jax.experimental import pallas as pl
from jax.experimental.pallas import tpu as pltpu

F32 = jnp.float32
BF16 = jnp.bfloat16

D_MODEL = 2048
A_HEADS = 16
A_KV_HEADS = 4
A_HEAD_DIM = 64
A_GROUP = A_HEADS // A_KV_HEADS
WINDOW = 128
A_BLOCK = 128
A_Q_WIDTH = A_HEADS * A_HEAD_DIM
A_KV_WIDTH = A_KV_HEADS * A_HEAD_DIM
N_BUCKETS = 32
MAX_DISTANCE = 128
B_HEADS = 4
B_KEY_DIM = 128
B_VAL_DIM = 256
B_QK_WIDTH = B_HEADS * B_KEY_DIM
B_V_WIDTH = B_HEADS * B_VAL_DIM
B_GATE_RANK = 16
B_GATE_TAU = 16.0
B_CHUNK = 64
EPS = 1e-6
NEG_INF = -1e30

LANES = 128

OFF_GATE_A = 0
OFF_GATE_B = OFF_GATE_A + D_MODEL
OFF_QA = OFF_GATE_B + D_MODEL
OFF_KA = OFF_QA + A_Q_WIDTH
OFF_VA = OFF_KA + A_KV_WIDTH
OFF_QB = OFF_VA + A_KV_WIDTH
OFF_KB = OFF_QB + B_QK_WIDTH
OFF_VB = OFF_KB + B_QK_WIDTH
OFF_OBG = OFF_VB + B_V_WIDTH
OFF_GLOW = OFF_OBG + B_V_WIDTH
PROJ_USED = OFF_GLOW + LANES
HALF_V = B_V_WIDTH // 2

VMEM_LIMIT = 60 * 1024 * 1024


def _cparams(sem):
    return pltpu.CompilerParams(dimension_semantics=sem, vmem_limit_bytes=VMEM_LIMIT)


def _const_spec(shape):
    return pl.BlockSpec(shape, lambda *_: (0,) * len(shape), pipeline_mode=pl.Buffered(1))


def _rmsnorm_rows(x, g):
    ms = jnp.mean(x * x, axis=-1, keepdims=True)
    return x * lax.rsqrt(ms + EPS) * g


NORM_ROWS = 128


BF16_SUBLANES = 16


def _cast_specs(weights, n_chunks, chunk_of):
    in_specs, out_specs, shapes = [], [], []
    for w in weights:
        rows, rem = divmod(w.shape[0], n_chunks)
        assert rem == 0 and rows % BF16_SUBLANES == 0, (w.shape, n_chunks)
        for specs in (in_specs, out_specs):
            specs.append(pl.BlockSpec((rows, w.shape[1]), lambda *idx: (chunk_of(*idx), 0)))
        shapes.append(jax.ShapeDtypeStruct(w.shape, BF16))
    return in_specs, out_specs, shapes


def _cast_blocks(in_refs, out_refs):
    for src, dst in zip(in_refs, out_refs):
        dst[...] = src[...].astype(dst.dtype)


def _inproj_kernel(x_ref, g_ref, w_ref, o_ref, u_ref):
    @pl.when(pl.program_id(1) == 0)
    def _():
        g = g_ref[...]

        def body(c, carry):
            rows = pl.ds(pl.multiple_of(c * NORM_ROWS, NORM_ROWS), NORM_ROWS)
            u_ref[rows, :] = _rmsnorm_rows(x_ref[rows, :], g).astype(BF16)
            return carry

        lax.fori_loop(0, x_ref.shape[0] // NORM_ROWS, body, 0)

    o_ref[...] = lax.dot_general(u_ref[...], w_ref[...], (((1,), (1,)), ((), ())),
                                 preferred_element_type=F32).astype(o_ref.dtype)


def _inproj(x2, g, w_t, tm, tn):
    t, d = x2.shape
    n = w_t.shape[0]
    return pl.pallas_call(
        _inproj_kernel,
        grid=(t // tm, n // tn),
        in_specs=[
            pl.BlockSpec((tm, d), lambda i, j: (i, 0)),
            pl.BlockSpec((1, d), lambda i, j: (0, 0)),
            pl.BlockSpec((tn, d), lambda i, j: (j, 0)),
        ],
        out_specs=pl.BlockSpec((tm, tn), lambda i, j: (i, j)),
        out_shape=jax.ShapeDtypeStruct((t, n), BF16),
        scratch_shapes=[pltpu.VMEM((tm, d), BF16)],
        compiler_params=_cparams(("parallel", "arbitrary")),
        name="inproj",
    )(x2, g, w_t)


def _bucket_starts():
    max_exact = N_BUCKETS // 2
    d = np.arange(WINDOW)
    large = max_exact + (np.log(np.maximum(d, 1).astype(np.float32) / max_exact)
                         / math.log(MAX_DISTANCE / max_exact)
                         * (N_BUCKETS - max_exact)).astype(np.int32)
    bucket = np.where(d < max_exact, d, np.minimum(large, N_BUCKETS - 1))
    starts = []
    for b in range(N_BUCKETS):
        hit = np.nonzero(bucket == b)[0]
        if hit.size:
            assert np.all(np.diff(hit) == 1)
            starts.append((b, int(hit[0])))
    return starts


HEADS_PER_TILE = LANES // A_HEAD_DIM
SWA_BLOCKS_PER_STEP = 4


def _swa_kernel(nb, n_cast, q_ref, kp_ref, kc_ref, vp_ref, vc_ref, sink_ref, rb_ref, *rest):
    cast_in, o_ref, cast_out, bias_ref = rest[:n_cast], rest[n_cast], rest[n_cast + 1:-1], rest[-1]
    _cast_blocks(cast_in, cast_out)
    r = pl.program_id(0)
    n_keys = 2 * A_BLOCK

    @pl.when(r == 0)
    def _():
        row = lax.broadcasted_iota(jnp.int32, (A_BLOCK, n_keys), 0)
        col = lax.broadcasted_iota(jnp.int32, (A_BLOCK, n_keys), 1)
        dist = row + A_BLOCK - col
        band = (dist >= 0) & (dist < WINDOW)
        starts = _bucket_starts()
        for h in range(A_HEADS):
            val = jnp.full(dist.shape, rb_ref[starts[0][0], h], F32)
            for b, s in starts[1:]:
                val = jnp.where(dist >= s, rb_ref[b, h], val)
            val = jnp.where(band, val, NEG_INF)
            sink = sink_ref[0, h]
            bias_ref[0, h] = jnp.where(col == 0, sink, val)
            bias_ref[1, h] = jnp.where(col == 0, sink, jnp.where(col >= A_BLOCK, val, NEG_INF))

    lane = lax.broadcasted_iota(jnp.int32, (1, LANES), 1)
    scale = A_HEAD_DIM ** -0.5
    q_keep = (jnp.where(lane < A_HEAD_DIM, scale, 0.0).astype(BF16),
              jnp.where(lane < A_HEAD_DIM, 0.0, scale).astype(BF16))
    lower_lanes = lax.broadcasted_iota(jnp.int32, (A_BLOCK, LANES), 1) < A_HEAD_DIM
    key0 = lax.broadcasted_iota(jnp.int32, (n_keys, LANES), 0) == 0
    ones = jnp.ones((n_keys, LANES), BF16)

    def attend(q_rows, prev, cur, first):
        def both_blocks(which, tile):
            cols = slice(tile * LANES, (tile + 1) * LANES)
            cat = jnp.concatenate([prev[which][:, cols], cur[which][:, cols]], axis=0).astype(F32)
            cat = jnp.where(key0, 0.0, cat)
            return cat.astype(BF16), pltpu.roll(cat, A_HEAD_DIM, 1).astype(BF16)

        stacks = []
        for tile in range(A_KV_WIDTH // LANES):
            k_cat, k_swp = both_blocks(0, tile)
            v_cat, v_swp = both_blocks(1, tile)
            q0 = tile * A_GROUP
            stacks.append((k_cat, jnp.concatenate([v_cat, ones], axis=1),
                           [(q0, 0), (q0 + 1, 0), (q0 + 2, 1), (q0 + 3, 1)]))
            stacks.append((k_swp, jnp.concatenate([v_swp, ones], axis=1),
                           [(q0, 1), (q0 + 1, 1), (q0 + 2, 0), (q0 + 3, 0)]))

        scores = []
        for k_tile, _, members in stacks:
            q4 = jnp.concatenate(
                [q_ref[q_rows, qt * LANES:(qt + 1) * LANES] * q_keep[half] for qt, half in members],
                axis=0)
            scores.append(lax.dot_general(q4, k_tile, (((1,), (1,)), ((), ())),
                                          preferred_element_type=F32))
        s = jnp.concatenate(scores, axis=0)
        s = s + jnp.concatenate(
            [bias_ref[first, qt * HEADS_PER_TILE + half]
             for _, _, members in stacks for qt, half in members], axis=0)
        p = jnp.exp(s - jnp.max(s, axis=-1, keepdims=True)).astype(BF16)

        normed = {}
        rows_per_stack = len(stacks[0][2]) * A_BLOCK
        for i, (_, v_ones, members) in enumerate(stacks):
            ov = jnp.dot(p[i * rows_per_stack:(i + 1) * rows_per_stack], v_ones,
                         preferred_element_type=F32)
            o = ov[:, :LANES] / ov[:, LANES:]
            for j, member in enumerate(members):
                normed[member] = o[j * A_BLOCK:(j + 1) * A_BLOCK]
        for qt in range(A_Q_WIDTH // LANES):
            o_ref[q_rows, qt * LANES:(qt + 1) * LANES] = jnp.where(
                lower_lanes, normed[(qt, 0)], normed[(qt, 1)]).astype(o_ref.dtype)

    blocks_per_step = q_ref.shape[0] // A_BLOCK
    kv_prev = (kp_ref[...], vp_ref[...])
    for sub in range(blocks_per_step):
        rows = slice(sub * A_BLOCK, (sub + 1) * A_BLOCK)
        kv_cur = (kc_ref[rows, :], vc_ref[rows, :])
        if sub == 0:
            first = (lax.rem(r * blocks_per_step, nb) == 0).astype(jnp.int32)
        else:
            first = 0
        attend(rows, kv_prev, kv_cur, first)
        kv_prev = kv_cur


def _swa(proj, sinks, rel_bias, cast_weights, batch, seq):
    t = proj.shape[0]
    nb = seq // A_BLOCK
    qcol = OFF_QA // A_Q_WIDTH
    kcol = OFF_KA // A_KV_WIDTH
    vcol = OFF_VA // A_KV_WIDTH
    bps = SWA_BLOCKS_PER_STEP
    assert nb % bps == 0
    rows = bps * A_BLOCK
    prev = lambda r: jnp.maximum(r * bps - 1, 0)
    smem = functools.partial(pl.BlockSpec, memory_space=pltpu.SMEM)
    cast_in, cast_out, cast_shapes = _cast_specs(cast_weights, t // rows, lambda r: r)
    outs = pl.pallas_call(
        functools.partial(_swa_kernel, nb, len(cast_weights)),
        grid=(t // rows,),
        in_specs=[
            pl.BlockSpec((rows, A_Q_WIDTH), lambda r: (r, qcol)),
            pl.BlockSpec((A_BLOCK, A_KV_WIDTH), lambda r: (prev(r), kcol)),
            pl.BlockSpec((rows, A_KV_WIDTH), lambda r: (r, kcol)),
            pl.BlockSpec((A_BLOCK, A_KV_WIDTH), lambda r: (prev(r), vcol)),
            pl.BlockSpec((rows, A_KV_WIDTH), lambda r: (r, vcol)),
            smem(), smem(),
        ] + cast_in,
        out_specs=[pl.BlockSpec((rows, A_Q_WIDTH), lambda r: (r, 0))] + cast_out,
        out_shape=[jax.ShapeDtypeStruct((t, A_Q_WIDTH), BF16)] + cast_shapes,
        scratch_shapes=[pltpu.VMEM((2, A_HEADS, A_BLOCK, 2 * A_BLOCK), F32)],
        compiler_params=_cparams(("arbitrary",)),
        name="swa",
    )(proj, proj, proj, proj, proj, sinks, rel_bias, *cast_weights)
    return outs[0], outs[1:]


GLA_CHUNKS_PER_STEP = 8


def _split3(x):
    hi = x.astype(BF16)
    r1 = x - hi.astype(F32)
    mid = r1.astype(BF16)
    lo = (r1 - mid.astype(F32)).astype(BF16)
    return jnp.concatenate([hi, mid, lo], axis=0)


def _gla_kernel(n_cast, q_ref, k_ref, v0_ref, v1_ref, gl_ref, og0_ref, og1_ref, wgu_ref, bg_ref, ng_ref,
                *rest):
    cast_in, o_ref, cast_out, s_ref = rest[:n_cast], rest[n_cast], rest[n_cast + 1:-1], rest[-1]
    _cast_blocks(cast_in, cast_out)
    c = B_CHUNK
    n_chunks = GLA_CHUNKS_PER_STEP
    n_rows = n_chunks * c
    heads_per_half = HALF_V // B_VAL_DIM
    v_refs = (v0_ref, v1_ref)
    og_refs = (og0_ref, og1_ref)
    chunk_rows = [slice(j * c, (j + 1) * c) for j in range(n_chunks)]
    key_cols = [slice(h * B_KEY_DIM, (h + 1) * B_KEY_DIM) for h in range(B_HEADS)]
    units = [(j, h) for j in range(n_chunks) for h in range(B_HEADS)]

    def v_of(refs, j, h):
        lo = (h % heads_per_half) * B_VAL_DIM
        rows = slice(None) if j is None else chunk_rows[j]
        return refs[h // heads_per_half][rows, lo:lo + B_VAL_DIM]

    @pl.when(pl.program_id(1) == 0)
    def _():
        s_ref[...] = jnp.zeros_like(s_ref)

    glin = jnp.dot(gl_ref[...], wgu_ref[...], preferred_element_type=F32) + bg_ref[...]
    log_a = (jnp.minimum(glin, 0.0) - jnp.log(1.0 + jnp.exp(-jnp.abs(glin)))) / B_GATE_TAU

    ri = lax.broadcasted_iota(jnp.int32, (n_rows, 3 * n_rows), 0)
    ci = lax.broadcasted_iota(jnp.int32, (n_rows, 3 * n_rows), 1)
    ci = ci - jnp.where(ci >= n_rows, n_rows, 0) - jnp.where(ci >= 2 * n_rows, n_rows, 0)
    shift = int(math.log2(c))
    same_chunk = lax.shift_right_logical(ri, shift) == lax.shift_right_logical(ci, shift)
    tri3 = ((ri >= ci) & same_chunk).astype(BF16)
    b = jnp.dot(tri3, _split3(log_a), preferred_element_type=F32)
    last_rows = [b[(j + 1) * c - 1:(j + 1) * c, :] for j in range(n_chunks)]
    b_last = jnp.concatenate([jnp.broadcast_to(r, (c, b.shape[1])) for r in last_rows], axis=0)

    qf = q_ref[...].astype(F32) * (B_KEY_DIM ** -0.5)
    kf = k_ref[...].astype(F32)
    q_dec = (qf * jnp.exp(b)).astype(BF16)
    k_dec = (kf * jnp.exp(-b)).astype(BF16)
    k_state = kf * jnp.exp(b_last - b)
    sublanes = 8
    pad = [jnp.zeros((sublanes - n_chunks, b.shape[1]), F32)] if n_chunks < sublanes else []
    decay_rows = jnp.exp(jnp.concatenate(last_rows + pad, axis=0))

    ri = lax.broadcasted_iota(jnp.int32, (c, c), 0)
    ci = lax.broadcasted_iota(jnp.int32, (c, c), 1)
    causal = ri >= ci
    att = {}
    for j, h in units:
        a = lax.dot_general(q_dec[chunk_rows[j], key_cols[h]], k_dec[chunk_rows[j], key_cols[h]],
                            (((1,), (1,)), ((), ())), preferred_element_type=F32)
        att[j, h] = jnp.where(causal, a, 0.0).astype(BF16)
    o_intra = {u: jnp.dot(att[u], v_of(v_refs, *u), preferred_element_type=F32) for u in units}
    ds = {(j, h): jnp.dot(k_state[chunk_rows[j], key_cols[h]].T.astype(BF16), v_of(v_refs, j, h),
                          preferred_element_type=F32) for j, h in units}

    entering = {}
    for h in range(B_HEADS):
        decay_t = decay_rows[:, key_cols[h]].T
        state = s_ref[h]
        for j in range(n_chunks):
            entering[j, h] = state.astype(BF16)
            state = decay_t[:, j:j + 1] * state + ds[j, h]
        s_ref[h] = state
    o_inter = {(j, h): jnp.dot(q_dec[chunk_rows[j], key_cols[h]], entering[j, h],
                               preferred_element_type=F32) for j, h in units}

    ng = ng_ref[...]
    for h in range(B_HEADS):
        o = jnp.concatenate([o_intra[j, h] + o_inter[j, h] for j in range(n_chunks)], axis=0)
        gate = v_of(og_refs, None, h).astype(F32)
        y = _rmsnorm_rows(o, ng) * (gate * jax.nn.sigmoid(gate))
        o_ref[:, h * B_VAL_DIM:(h + 1) * B_VAL_DIM] = y.astype(o_ref.dtype)


def _gla(proj, wgu, bg, ng, cast_weights, batch, seq):
    t = proj.shape[0]
    rows = GLA_CHUNKS_PER_STEP * B_CHUNK
    steps = seq // rows
    rb = lambda b, s: b * steps + s
    cast_in, cast_out, cast_shapes = _cast_specs(cast_weights, batch * steps, rb)
    outs = pl.pallas_call(
        functools.partial(_gla_kernel, len(cast_weights)),
        grid=(batch, steps),
        in_specs=[
            pl.BlockSpec((rows, B_QK_WIDTH), lambda b, s: (rb(b, s), OFF_QB // B_QK_WIDTH)),
            pl.BlockSpec((rows, B_QK_WIDTH), lambda b, s: (rb(b, s), OFF_KB // B_QK_WIDTH)),
            pl.BlockSpec((rows, HALF_V), lambda b, s: (rb(b, s), OFF_VB // HALF_V)),
            pl.BlockSpec((rows, HALF_V), lambda b, s: (rb(b, s), OFF_VB // HALF_V + 1)),
            pl.BlockSpec((rows, LANES), lambda b, s: (rb(b, s), OFF_GLOW // LANES)),
            pl.BlockSpec((rows, HALF_V), lambda b, s: (rb(b, s), OFF_OBG // HALF_V)),
            pl.BlockSpec((rows, HALF_V), lambda b, s: (rb(b, s), OFF_OBG // HALF_V + 1)),
            _const_spec(wgu.shape), _const_spec(bg.shape), _const_spec(ng.shape),
        ] + cast_in,
        out_specs=[pl.BlockSpec((rows, B_V_WIDTH), lambda b, s: (rb(b, s), 0))] + cast_out,
        out_shape=[jax.ShapeDtypeStruct((t, B_V_WIDTH), BF16)] + cast_shapes,
        scratch_shapes=[pltpu.VMEM((B_HEADS, B_KEY_DIM, B_VAL_DIM), F32)],
        compiler_params=_cparams(("parallel", "arbitrary")),
        name="gla",
    )(proj, proj, proj, proj, proj, proj, proj, wgu, bg, ng, *cast_weights)
    return outs[0], outs[1:]


def _merge_kernel(a_ref, b_ref, ga_ref, gb_ref, x_ref, wa_ref, wb_ref, wo_ref, gz_ref, h_ref, z_ref):
    ya = jnp.dot(a_ref[...], wa_ref[...], preferred_element_type=F32)
    yb = jnp.dot(b_ref[...], wb_ref[...], preferred_element_type=F32)
    merged = (jax.nn.sigmoid(ga_ref[...].astype(F32)) * ya
              + jax.nn.sigmoid(gb_ref[...].astype(F32)) * yb)
    h = x_ref[...] + jnp.dot(merged.astype(BF16), wo_ref[...], preferred_element_type=F32)
    h_ref[...] = h
    z_ref[...] = _rmsnorm_rows(h, gz_ref[...]).astype(z_ref.dtype)


def _merge(attn, gla, proj, x2, wa, wb, wo, gz, tm):
    t, d = x2.shape
    row_block = pl.BlockSpec((tm, d), lambda i: (i, 0))
    return pl.pallas_call(
        _merge_kernel,
        grid=(t // tm,),
        in_specs=[
            pl.BlockSpec((tm, A_Q_WIDTH), lambda i: (i, 0)),
            pl.BlockSpec((tm, B_V_WIDTH), lambda i: (i, 0)),
            pl.BlockSpec((tm, d), lambda i: (i, OFF_GATE_A // D_MODEL)),
            pl.BlockSpec((tm, d), lambda i: (i, OFF_GATE_B // D_MODEL)),
            row_block,
            _const_spec(wa.shape), _const_spec(wb.shape), _const_spec(wo.shape), _const_spec(gz.shape),
        ],
        out_specs=[row_block, row_block],
        out_shape=[jax.ShapeDtypeStruct((t, d), F32), jax.ShapeDtypeStruct((t, d), BF16)],
        compiler_params=_cparams(("parallel",)),
        name="merge",
    )(attn, gla, proj, proj, x2, wa, wb, wo, gz)


def _ffn_kernel(z_ref, h_ref, wg_ref, wu_ref, wd_ref, gf_ref, o_ref):
    f = pl.program_id(1)
    z = z_ref[...]
    g = jnp.dot(z, wg_ref[...], preferred_element_type=F32)
    u = jnp.dot(z, wu_ref[...], preferred_element_type=F32)
    act = (g * jax.nn.sigmoid(g) * u).astype(BF16)
    acc = jnp.where(f == 0, 0.0, o_ref[...])
    o_ref[...] = acc + jnp.dot(act, wd_ref[...], preferred_element_type=F32)

    @pl.when(f == pl.num_programs(1) - 1)
    def _():
        gf = gf_ref[...]

        def body(c, carry):
            rows = pl.ds(pl.multiple_of(c * NORM_ROWS, NORM_ROWS), NORM_ROWS)
            o_ref[rows, :] = _rmsnorm_rows(h_ref[rows, :] + o_ref[rows, :], gf)
            return carry

        lax.fori_loop(0, h_ref.shape[0] // NORM_ROWS, body, 0)


def _ffn(z, h, wg, wu, wd, gf, tm, tf):
    t, d = h.shape
    f = wg.shape[1]
    row_block = pl.BlockSpec((tm, d), lambda i, j: (i, 0))
    return pl.pallas_call(
        _ffn_kernel,
        grid=(t // tm, f // tf),
        in_specs=[
            row_block, row_block,
            pl.BlockSpec((d, tf), lambda i, j: (0, j)),
            pl.BlockSpec((d, tf), lambda i, j: (0, j)),
            pl.BlockSpec((tf, d), lambda i, j: (j, 0)),
            pl.BlockSpec((1, d), lambda i, j: (0, 0)),
        ],
        out_specs=row_block,
        out_shape=jax.ShapeDtypeStruct((t, d), F32),
        compiler_params=_cparams(("parallel", "arbitrary")),
        name="ffn",
    )(z, h, wg, wu, wd, gf)


PACK_COLS = 256


def _pack_w_in_kernel(w_ref, o_ref):
    n_mix = OFF_OBG - OFF_QA
    glow_end = n_mix + B_GATE_RANK
    obg_end = glow_end + B_V_WIDTH
    n_in = w_ref.shape[0]
    o_ref[OFF_GATE_A:OFF_QA, :] = w_ref[obg_end:n_in, :].astype(BF16)
    o_ref[OFF_QA:OFF_OBG, :] = w_ref[0:n_mix, :].astype(BF16)
    o_ref[OFF_OBG:OFF_GLOW, :] = w_ref[glow_end:obg_end, :].astype(BF16)
    o_ref[OFF_GLOW:OFF_GLOW + B_GATE_RANK, :] = w_ref[n_mix:glow_end, :].astype(BF16)
    n_zero = o_ref.shape[0] - OFF_GLOW - B_GATE_RANK
    o_ref[OFF_GLOW + B_GATE_RANK:, :] = jnp.zeros((n_zero, o_ref.shape[1]), BF16)


def _pack_w_in(w_t, n_total):
    n_in, d = w_t.shape
    return pl.pallas_call(
        _pack_w_in_kernel,
        grid=(d // PACK_COLS,),
        in_specs=[pl.BlockSpec((n_in, PACK_COLS), lambda i: (0, i))],
        out_specs=pl.BlockSpec((n_total, PACK_COLS), lambda i: (0, i)),
        out_shape=jax.ShapeDtypeStruct((n_total, d), BF16),
        compiler_params=_cparams(("parallel",)),
        name="pack_w_in",
    )(w_t)


def kernel(x, norm_mix_g, w_in, sinks, rel_bias, w_gate_up, b_gate, gla_norm_g, w_proj_a, w_proj_b,
           w_out, norm_ffn_g, w_ffn_gate, w_ffn_up, w_ffn_down, norm_final_g):
    batch, seq, d = x.shape
    assert d == D_MODEL and w_in.shape[0] == 1, "single-layer geometry"
    t = batch * seq
    x2 = x.reshape(t, d)

    tn_in = 2304
    w_in_p = _pack_w_in(w_in[0].T, pl.cdiv(PROJ_USED, tn_in) * tn_in)
    proj = _inproj(x2, norm_mix_g, w_in_p, tm=1024, tn=tn_in)

    attn, (wg, wu) = _swa(proj, sinks, rel_bias, (w_ffn_gate[0], w_ffn_up[0]), batch, seq)

    wgu = jnp.zeros((LANES, B_QK_WIDTH), BF16).at[:B_GATE_RANK].set(w_gate_up[0].astype(BF16))
    gla, (wa, wb, wo, wd) = _gla(proj, wgu, b_gate, gla_norm_g,
                                 (w_proj_a[0], w_proj_b[0], w_out[0], w_ffn_down[0]), batch, seq)

    h, z = _merge(attn, gla, proj, x2, wa, wb, wo, norm_ffn_g, tm=512)

    out = _ffn(z, h, wg, wu, wd, norm_final_g.reshape(1, d), tm=1024, tf=512)
    return out.reshape(batch, seq, d)
```

```python
import functools
import math

import numpy as np
import jax
import jax.numpy as jnp
from jax import lax
from jax.experimental import pallas as pl
from jax.experimental.pallas import tpu as pltpu

F32 = jnp.float32
BF16 = jnp.bfloat16

D_MODEL = 2048
A_HEADS = 16
A_KV_HEADS = 4
A_HEAD_DIM = 64
A_GROUP = A_HEADS // A_KV_HEADS
WINDOW = 128
A_BLOCK = 128
A_Q_WIDTH = A_HEADS * A_HEAD_DIM
A_KV_WIDTH = A_KV_HEADS * A_HEAD_DIM
N_BUCKETS = 32
MAX_DISTANCE = 128
B_HEADS = 4
B_KEY_DIM = 128
B_VAL_DIM = 256
B_QK_WIDTH = B_HEADS * B_KEY_DIM
B_V_WIDTH = B_HEADS * B_VAL_DIM
B_GATE_RANK = 16
B_GATE_TAU = 16.0
B_CHUNK = 64
EPS = 1e-6
NEG_INF = -1e30

LANES = 128

OFF_GATE_A = 0
OFF_GATE_B = OFF_GATE_A + D_MODEL
OFF_QA = OFF_GATE_B + D_MODEL
OFF_KA = OFF_QA + A_Q_WIDTH
OFF_VA = OFF_KA + A_KV_WIDTH
OFF_QB = OFF_VA + A_KV_WIDTH
OFF_KB = OFF_QB + B_QK_WIDTH
OFF_VB = OFF_KB + B_QK_WIDTH
OFF_OBG = OFF_VB + B_V_WIDTH
OFF_GLOW = OFF_OBG + B_V_WIDTH
PROJ_USED = OFF_GLOW + LANES
HALF_V = B_V_WIDTH // 2

VMEM_LIMIT = 60 * 1024 * 1024


def _cparams(sem):
    return pltpu.CompilerParams(dimension_semantics=sem, vmem_limit_bytes=VMEM_LIMIT)


def _const_spec(shape):
    return pl.BlockSpec(shape, lambda *_: (0,) * len(shape), pipeline_mode=pl.Buffered(1))


def _rmsnorm_rows(x, g):
    ms = jnp.mean(x * x, axis=-1, keepdims=True)
    return x * lax.rsqrt(ms + EPS) * g


NORM_ROWS = 128


BF16_SUBLANES = 16


def _cast_specs(weights, n_chunks, chunk_of):
    in_specs, out_specs, shapes = [], [], []
    for w in weights:
        rows, rem = divmod(w.shape[0], n_chunks)
        assert rem == 0 and rows % BF16_SUBLANES == 0, (w.shape, n_chunks)
        for specs in (in_specs, out_specs):
            specs.append(pl.BlockSpec((rows, w.shape[1]), lambda *idx: (chunk_of(*idx), 0)))
        shapes.append(jax.ShapeDtypeStruct(w.shape, BF16))
    return in_specs, out_specs, shapes


def _cast_blocks(in_refs, out_refs):
    for src, dst in zip(in_refs, out_refs):
        dst[...] = src[...].astype(dst.dtype)


def _inproj_kernel(x_ref, g_ref, w_ref, o_ref, u_ref):
    @pl.when(pl.program_id(1) == 0)
    def _():
        g = g_ref[...]

        def body(c, carry):
            rows = pl.ds(pl.multiple_of(c * NORM_ROWS, NORM_ROWS), NORM_ROWS)
            u_ref[rows, :] = _rmsnorm_rows(x_ref[rows, :], g).astype(BF16)
            return carry

        lax.fori_loop(0, x_ref.shape[0] // NORM_ROWS, body, 0)

    o_ref[...] = lax.dot_general(u_ref[...], w_ref[...], (((1,), (1,)), ((), ())),
                                 preferred_element_type=F32).astype(o_ref.dtype)


def _inproj(x2, g, w_t, tm, tn):
    t, d = x2.shape
    n = w_t.shape[0]
    return pl.pallas_call(
        _inproj_kernel,
        grid=(t // tm, n // tn),
        in_specs=[
            pl.BlockSpec((tm, d), lambda i, j: (i, 0)),
            pl.BlockSpec((1, d), lambda i, j: (0, 0)),
            pl.BlockSpec((tn, d), lambda i, j: (j, 0)),
        ],
        out_specs=pl.BlockSpec((tm, tn), lambda i, j: (i, j)),
        out_shape=jax.ShapeDtypeStruct((t, n), BF16),
        scratch_shapes=[pltpu.VMEM((tm, d), BF16)],
        compiler_params=_cparams(("parallel", "arbitrary")),
        name="inproj",
    )(x2, g, w_t)


def _bucket_starts():
    max_exact = N_BUCKETS // 2
    d = np.arange(WINDOW)
    large = max_exact + (np.log(np.maximum(d, 1).astype(np.float32) / max_exact)
                         / math.log(MAX_DISTANCE / max_exact)
                         * (N_BUCKETS - max_exact)).astype(np.int32)
    bucket = np.where(d < max_exact, d, np.minimum(large, N_BUCKETS - 1))
    starts = []
    for b in range(N_BUCKETS):
        hit = np.nonzero(bucket == b)[0]
        if hit.size:
            assert np.all(np.diff(hit) == 1)
            starts.append((b, int(hit[0])))
    return starts


HEADS_PER_TILE = LANES // A_HEAD_DIM
SWA_BLOCKS_PER_STEP = 4


def _swa_body(q_ref, kp_ref, kc_ref, vp_ref, vc_ref, sink_ref, rb_ref, o_ref, bias_ref):
    n_keys = 2 * A_BLOCK

    @pl.when((pl.program_id(0) == 0) & (pl.program_id(1) == 0))
    def _():
        row = lax.broadcasted_iota(jnp.int32, (A_BLOCK, n_keys), 0)
        col = lax.broadcasted_iota(jnp.int32, (A_BLOCK, n_keys), 1)
        dist = row + A_BLOCK - col
        band = (dist >= 0) & (dist < WINDOW)
        starts = _bucket_starts()
        for h in range(A_HEADS):
            val = jnp.full(dist.shape, rb_ref[starts[0][0], h], F32)
            for b, s in starts[1:]:
                val = jnp.where(dist >= s, rb_ref[b, h], val)
            val = jnp.where(band, val, NEG_INF)
            sink = sink_ref[0, h]
            bias_ref[0, h] = jnp.where(col == 0, sink, val)
            bias_ref[1, h] = jnp.where(col == 0, sink, jnp.where(col >= A_BLOCK, val, NEG_INF))

    lane = lax.broadcasted_iota(jnp.int32, (1, LANES), 1)
    scale = A_HEAD_DIM ** -0.5
    q_keep = (jnp.where(lane < A_HEAD_DIM, scale, 0.0).astype(BF16),
              jnp.where(lane < A_HEAD_DIM, 0.0, scale).astype(BF16))
    lower_lanes = lax.broadcasted_iota(jnp.int32, (A_BLOCK, LANES), 1) < A_HEAD_DIM
    key0 = lax.broadcasted_iota(jnp.int32, (n_keys, LANES), 0) == 0
    ones = jnp.ones((n_keys, LANES), BF16)

    def attend(q_rows, prev, cur, first):
        def both_blocks(which, tile):
            cols = slice(tile * LANES, (tile + 1) * LANES)
            cat = jnp.concatenate([prev[which][:, cols], cur[which][:, cols]], axis=0).astype(F32)
            cat = jnp.where(key0, 0.0, cat)
            return cat.astype(BF16), pltpu.roll(cat, A_HEAD_DIM, 1).astype(BF16)

        stacks = []
        for tile in range(A_KV_WIDTH // LANES):
            k_cat, k_swp = both_blocks(0, tile)
            v_cat, v_swp = both_blocks(1, tile)
            q0 = tile * A_GROUP
            stacks.append((k_cat, jnp.concatenate([v_cat, ones], axis=1),
                           [(q0, 0), (q0 + 1, 0), (q0 + 2, 1), (q0 + 3, 1)]))
            stacks.append((k_swp, jnp.concatenate([v_swp, ones], axis=1),
                           [(q0, 1), (q0 + 1, 1), (q0 + 2, 0), (q0 + 3, 0)]))

        scores = []
        for k_tile, _, members in stacks:
            q4 = jnp.concatenate(
                [q_ref[q_rows, qt * LANES:(qt + 1) * LANES] * q_keep[half] for qt, half in members],
                axis=0)
            scores.append(lax.dot_general(q4, k_tile, (((1,), (1,)), ((), ())),
                                          preferred_element_type=F32))
        s = jnp.concatenate(scores, axis=0)
        s = s + jnp.concatenate(
            [bias_ref[first, qt * HEADS_PER_TILE + half]
             for _, _, members in stacks for qt, half in members], axis=0)
        p = jnp.exp(s - jnp.max(s, axis=-1, keepdims=True)).astype(BF16)

        normed = {}
        rows_per_stack = len(stacks[0][2]) * A_BLOCK
        for i, (_, v_ones, members) in enumerate(stacks):
            ov = jnp.dot(p[i * rows_per_stack:(i + 1) * rows_per_stack], v_ones,
                         preferred_element_type=F32)
            o = ov[:, :LANES] / ov[:, LANES:]
            for j, member in enumerate(members):
                normed[member] = o[j * A_BLOCK:(j + 1) * A_BLOCK]
        for qt in range(A_Q_WIDTH // LANES):
            o_ref[q_rows, qt * LANES:(qt + 1) * LANES] = jnp.where(
                lower_lanes, normed[(qt, 0)], normed[(qt, 1)]).astype(o_ref.dtype)

    blocks_per_step = q_ref.shape[0] // A_BLOCK
    kv_prev = (kp_ref[...], vp_ref[...])
    for sub in range(blocks_per_step):
        rows = slice(sub * A_BLOCK, (sub + 1) * A_BLOCK)
        kv_cur = (kc_ref[rows, :], vc_ref[rows, :])
        first = (pl.program_id(1) == 0).astype(jnp.int32) if sub == 0 else 0
        attend(rows, kv_prev, kv_cur, first)
        kv_prev = kv_cur


GLA_CHUNKS_PER_STEP = 8


def _split3(x):
    hi = x.astype(BF16)
    r1 = x - hi.astype(F32)
    mid = r1.astype(BF16)
    lo = (r1 - mid.astype(F32)).astype(BF16)
    return jnp.concatenate([hi, mid, lo], axis=0)


def _gla_body(q_ref, k_ref, v0_ref, v1_ref, gl_ref, og0_ref, og1_ref, wgu_ref, bg_ref, ng_ref,
              o_ref, s_ref):
    c = B_CHUNK
    n_chunks = GLA_CHUNKS_PER_STEP
    n_rows = n_chunks * c
    heads_per_half = HALF_V // B_VAL_DIM
    v_refs = (v0_ref, v1_ref)
    og_refs = (og0_ref, og1_ref)
    chunk_rows = [slice(j * c, (j + 1) * c) for j in range(n_chunks)]
    key_cols = [slice(h * B_KEY_DIM, (h + 1) * B_KEY_DIM) for h in range(B_HEADS)]
    units = [(j, h) for j in range(n_chunks) for h in range(B_HEADS)]

    def v_of(refs, j, h):
        lo = (h % heads_per_half) * B_VAL_DIM
        rows = slice(None) if j is None else chunk_rows[j]
        return refs[h // heads_per_half][rows, lo:lo + B_VAL_DIM]

    @pl.when(pl.program_id(1) == 0)
    def _():
        s_ref[...] = jnp.zeros_like(s_ref)

    glin = jnp.dot(gl_ref[...], wgu_ref[...], preferred_element_type=F32) + bg_ref[...]
    log_a = (jnp.minimum(glin, 0.0) - jnp.log(1.0 + jnp.exp(-jnp.abs(glin)))) / B_GATE_TAU

    ri = lax.broadcasted_iota(jnp.int32, (n_rows, 3 * n_rows), 0)
    ci = lax.broadcasted_iota(jnp.int32, (n_rows, 3 * n_rows), 1)
    ci = ci - jnp.where(ci >= n_rows, n_rows, 0) - jnp.where(ci >= 2 * n_rows, n_rows, 0)
    shift = int(math.log2(c))
    same_chunk = lax.shift_right_logical(ri, shift) == lax.shift_right_logical(ci, shift)
    tri3 = ((ri >= ci) & same_chunk).astype(BF16)
    b = jnp.dot(tri3, _split3(log_a), preferred_element_type=F32)
    last_rows = [b[(j + 1) * c - 1:(j + 1) * c, :] for j in range(n_chunks)]
    b_last = jnp.concatenate([jnp.broadcast_to(r, (c, b.shape[1])) for r in last_rows], axis=0)

    qf = q_ref[...].astype(F32) * (B_KEY_DIM ** -0.5)
    kf = k_ref[...].astype(F32)
    q_dec = (qf * jnp.exp(b)).astype(BF16)
    k_dec = (kf * jnp.exp(-b)).astype(BF16)
    k_state = kf * jnp.exp(b_last - b)
    sublanes = 8
    pad = [jnp.zeros((sublanes - n_chunks, b.shape[1]), F32)] if n_chunks < sublanes else []
    decay_rows = jnp.exp(jnp.concatenate(last_rows + pad, axis=0))

    ri = lax.broadcasted_iota(jnp.int32, (c, c), 0)
    ci = lax.broadcasted_iota(jnp.int32, (c, c), 1)
    causal = ri >= ci
    att = {}
    for j, h in units:
        a = lax.dot_general(q_dec[chunk_rows[j], key_cols[h]], k_dec[chunk_rows[j], key_cols[h]],
                            (((1,), (1,)), ((), ())), preferred_element_type=F32)
        att[j, h] = jnp.where(causal, a, 0.0).astype(BF16)
    o_intra = {u: jnp.dot(att[u], v_of(v_refs, *u), preferred_element_type=F32) for u in units}
    ds = {(j, h): jnp.dot(k_state[chunk_rows[j], key_cols[h]].T.astype(BF16), v_of(v_refs, j, h),
                          preferred_element_type=F32) for j, h in units}

    entering = {}
    for h in range(B_HEADS):
        decay_t = decay_rows[:, key_cols[h]].T
        state = s_ref[h]
        for j in range(n_chunks):
            entering[j, h] = state.astype(BF16)
            state = decay_t[:, j:j + 1] * state + ds[j, h]
        s_ref[h] = state
    o_inter = {(j, h): jnp.dot(q_dec[chunk_rows[j], key_cols[h]], entering[j, h],
                               preferred_element_type=F32) for j, h in units}

    ng = ng_ref[...]
    for h in range(B_HEADS):
        o = jnp.concatenate([o_intra[j, h] + o_inter[j, h] for j in range(n_chunks)], axis=0)
        gate = v_of(og_refs, None, h).astype(F32)
        y = _rmsnorm_rows(o, ng) * (gate * jax.nn.sigmoid(gate))
        o_ref[:, h * B_VAL_DIM:(h + 1) * B_VAL_DIM] = y.astype(o_ref.dtype)


N_SWA_IN, N_GLA_IN = 7, 10


def _mixers_kernel(n_cast, *refs):
    swa_in, refs = refs[:N_SWA_IN], refs[N_SWA_IN:]
    gla_in, refs = refs[:N_GLA_IN], refs[N_GLA_IN:]
    cast_in, refs = refs[:n_cast], refs[n_cast:]
    attn_ref, gla_ref = refs[:2]
    cast_out, (bias_ref, s_ref) = refs[2:2 + n_cast], refs[2 + n_cast:]
    _cast_blocks(cast_in, cast_out)
    _swa_body(*swa_in, attn_ref, bias_ref)
    _gla_body(*gla_in, gla_ref, s_ref)


def _mixers(proj, sinks, rel_bias, wgu, bg, ng, cast_weights, batch, seq):
    t = proj.shape[0]
    rows = GLA_CHUNKS_PER_STEP * B_CHUNK
    assert rows == SWA_BLOCKS_PER_STEP * A_BLOCK and seq % rows == 0
    steps = seq // rows
    rb = lambda b, s: b * steps + s
    prev = lambda b, s: jnp.maximum(rb(b, s) * SWA_BLOCKS_PER_STEP - 1, 0)
    col_block = lambda width, off: pl.BlockSpec((rows, width), lambda b, s: (rb(b, s), off // width))
    prev_block = lambda off: pl.BlockSpec((A_BLOCK, A_KV_WIDTH), lambda b, s: (prev(b, s), off // A_KV_WIDTH))
    smem = functools.partial(pl.BlockSpec, memory_space=pltpu.SMEM)
    cast_in, cast_out, cast_shapes = _cast_specs(cast_weights, batch * steps, rb)
    swa_specs = [col_block(A_Q_WIDTH, OFF_QA), prev_block(OFF_KA), col_block(A_KV_WIDTH, OFF_KA),
                 prev_block(OFF_VA), col_block(A_KV_WIDTH, OFF_VA), smem(), smem()]
    gla_specs = [col_block(B_QK_WIDTH, OFF_QB), col_block(B_QK_WIDTH, OFF_KB),
                 col_block(HALF_V, OFF_VB), col_block(HALF_V, OFF_VB + HALF_V),
                 col_block(LANES, OFF_GLOW),
                 col_block(HALF_V, OFF_OBG), col_block(HALF_V, OFF_OBG + HALF_V),
                 _const_spec(wgu.shape), _const_spec(bg.shape), _const_spec(ng.shape)]
    assert len(swa_specs) == N_SWA_IN and len(gla_specs) == N_GLA_IN
    out_block = lambda width: pl.BlockSpec((rows, width), lambda b, s: (rb(b, s), 0))
    outs = pl.pallas_call(
        functools.partial(_mixers_kernel, len(cast_weights)),
        grid=(batch, steps),
        in_specs=swa_specs + gla_specs + cast_in,
        out_specs=[out_block(A_Q_WIDTH), out_block(B_V_WIDTH)] + cast_out,
        out_shape=[jax.ShapeDtypeStruct((t, A_Q_WIDTH), BF16),
                   jax.ShapeDtypeStruct((t, B_V_WIDTH), BF16)] + cast_shapes,
        scratch_shapes=[pltpu.VMEM((2, A_HEADS, A_BLOCK, 2 * A_BLOCK), F32),
                        pltpu.VMEM((B_HEADS, B_KEY_DIM, B_VAL_DIM), F32)],
        compiler_params=_cparams(("arbitrary", "arbitrary")),
        name="mixers",
    )(*([proj] * 5), sinks, rel_bias, *([proj] * 7), wgu, bg, ng, *cast_weights)
    return outs[0], outs[1], outs[2:]


def _merge_kernel(a_ref, b_ref, ga_ref, gb_ref, x_ref, wa_ref, wb_ref, wo_ref, gz_ref, h_ref, z_ref):
    ya = jnp.dot(a_ref[...], wa_ref[...], preferred_element_type=F32)
    yb = jnp.dot(b_ref[...], wb_ref[...], preferred_element_type=F32)
    merged = (jax.nn.sigmoid(ga_ref[...].astype(F32)) * ya
              + jax.nn.sigmoid(gb_ref[...].astype(F32)) * yb)
    h = x_ref[...] + jnp.dot(merged.astype(BF16), wo_ref[...], preferred_element_type=F32)
    h_ref[...] = h
    z_ref[...] = _rmsnorm_rows(h, gz_ref[...]).astype(z_ref.dtype)


def _merge(attn, gla, proj, x2, wa, wb, wo, gz, tm):
    t, d = x2.shape
    row_block = pl.BlockSpec((tm, d), lambda i: (i, 0))
    return pl.pallas_call(
        _merge_kernel,
        grid=(t // tm,),
        in_specs=[
            pl.BlockSpec((tm, A_Q_WIDTH), lambda i: (i, 0)),
            pl.BlockSpec((tm, B_V_WIDTH), lambda i: (i, 0)),
            pl.BlockSpec((tm, d), lambda i: (i, OFF_GATE_A // D_MODEL)),
            pl.BlockSpec((tm, d), lambda i: (i, OFF_GATE_B // D_MODEL)),
            row_block,
            _const_spec(wa.shape), _const_spec(wb.shape), _const_spec(wo.shape), _const_spec(gz.shape),
        ],
        out_specs=[row_block, row_block],
        out_shape=[jax.ShapeDtypeStruct((t, d), F32), jax.ShapeDtypeStruct((t, d), BF16)],
        compiler_params=_cparams(("parallel",)),
        name="merge",
    )(attn, gla, proj, proj, x2, wa, wb, wo, gz)


def _ffn_kernel(z_ref, h_ref, wg_ref, wu_ref, wd_ref, gf_ref, o_ref):
    f = pl.program_id(1)
    z = z_ref[...]
    g = jnp.dot(z, wg_ref[...], preferred_element_type=F32)
    u = jnp.dot(z, wu_ref[...], preferred_element_type=F32)
    act = (g * jax.nn.sigmoid(g) * u).astype(BF16)
    acc = jnp.where(f == 0, 0.0, o_ref[...])
    o_ref[...] = acc + jnp.dot(act, wd_ref[...], preferred_element_type=F32)

    @pl.when(f == pl.num_programs(1) - 1)
    def _():
        gf = gf_ref[...]

        def body(c, carry):
            rows = pl.ds(pl.multiple_of(c * NORM_ROWS, NORM_ROWS), NORM_ROWS)
            o_ref[rows, :] = _rmsnorm_rows(h_ref[rows, :] + o_ref[rows, :], gf)
            return carry

        lax.fori_loop(0, h_ref.shape[0] // NORM_ROWS, body, 0)


def _ffn(z, h, wg, wu, wd, gf, tm, tf):
    t, d = h.shape
    f = wg.shape[1]
    row_block = pl.BlockSpec((tm, d), lambda i, j: (i, 0))
    return pl.pallas_call(
        _ffn_kernel,
        grid=(t // tm, f // tf),
        in_specs=[
            row_block, row_block,
            pl.BlockSpec((d, tf), lambda i, j: (0, j)),
            pl.BlockSpec((d, tf), lambda i, j: (0, j)),
            pl.BlockSpec((tf, d), lambda i, j: (j, 0)),
            pl.BlockSpec((1, d), lambda i, j: (0, 0)),
        ],
        out_specs=row_block,
        out_shape=jax.ShapeDtypeStruct((t, d), F32),
        compiler_params=_cparams(("parallel", "arbitrary")),
        name="ffn",
    )(z, h, wg, wu, wd, gf)


PACK_COLS = 256


def _pack_w_in_kernel(w_ref, o_ref):
    n_mix = OFF_OBG - OFF_QA
    glow_end = n_mix + B_GATE_RANK
    obg_end = glow_end + B_V_WIDTH
    n_in = w_ref.shape[0]
    o_ref[OFF_GATE_A:OFF_QA, :] = w_ref[obg_end:n_in, :].astype(BF16)
    o_ref[OFF_QA:OFF_OBG, :] = w_ref[0:n_mix, :].astype(BF16)
    o_ref[OFF_OBG:OFF_GLOW, :] = w_ref[glow_end:obg_end, :].astype(BF16)
    o_ref[OFF_GLOW:OFF_GLOW + B_GATE_RANK, :] = w_ref[n_mix:glow_end, :].astype(BF16)
    n_zero = o_ref.shape[0] - OFF_GLOW - B_GATE_RANK
    o_ref[OFF_GLOW + B_GATE_RANK:, :] = jnp.zeros((n_zero, o_ref.shape[1]), BF16)


def _pack_w_in(w_t, n_total):
    n_in, d = w_t.shape
    return pl.pallas_call(
        _pack_w_in_kernel,
        grid=(d // PACK_COLS,),
        in_specs=[pl.BlockSpec((n_in, PACK_COLS), lambda i: (0, i))],
        out_specs=pl.BlockSpec((n_total, PACK_COLS), lambda i: (0, i)),
        out_shape=jax.ShapeDtypeStruct((n_total, d), BF16),
        compiler_params=_cparams(("parallel",)),
        name="pack_w_in",
    )(w_t)


def kernel(x, norm_mix_g, w_in, sinks, rel_bias, w_gate_up, b_gate, gla_norm_g, w_proj_a, w_proj_b,
           w_out, norm_ffn_g, w_ffn_gate, w_ffn_up, w_ffn_down, norm_final_g):
    batch, seq, d = x.shape
    assert d == D_MODEL and w_in.shape[0] == 1, "single-layer geometry"
    t = batch * seq
    x2 = x.reshape(t, d)

    tn_in = 2304
    w_in_p = _pack_w_in(w_in[0].T, pl.cdiv(PROJ_USED, tn_in) * tn_in)
    proj = _inproj(x2, norm_mix_g, w_in_p, tm=1024, tn=tn_in)

    wgu = jnp.zeros((LANES, B_QK_WIDTH), BF16).at[:B_GATE_RANK].set(w_gate_up[0].astype(BF16))
    later_weights = (w_proj_a[0], w_proj_b[0], w_out[0], w_ffn_gate[0], w_ffn_up[0], w_ffn_down[0])
    attn, gla, (wa, wb, wo, wg, wu, wd) = _mixers(proj, sinks, rel_bias, wgu, b_gate, gla_norm_g,
                                                  later_weights, batch, seq)

    h, z = _merge(attn, gla, proj, x2, wa, wb, wo, norm_ffn_g, tm=512)

    out = _ffn(z, h, wg, wu, wd, norm_final_g.reshape(1, d), tm=1024, tf=512)
    return out.reshape(batch, seq, d)
```

```python
import functools
import math

import numpy as np
import jax
import jax.numpy as jnp
from jax import lax
from jax.experimental import pallas as pl
from jax.experimental.pallas import tpu as pltpu

F32 = jnp.float32
BF16 = jnp.bfloat16

D_MODEL = 2048
A_HEADS = 16
A_KV_HEADS = 4
A_HEAD_DIM = 64
A_GROUP = A_HEADS // A_KV_HEADS
WINDOW = 128
A_BLOCK = 128
A_Q_WIDTH = A_HEADS * A_HEAD_DIM
A_KV_WIDTH = A_KV_HEADS * A_HEAD_DIM
N_BUCKETS = 32
MAX_DISTANCE = 128
B_HEADS = 4
B_KEY_DIM = 128
B_VAL_DIM = 256
B_QK_WIDTH = B_HEADS * B_KEY_DIM
B_V_WIDTH = B_HEADS * B_VAL_DIM
B_GATE_RANK = 16
B_GATE_TAU = 16.0
B_CHUNK = 64
EPS = 1e-6
NEG_INF = -1e30

LANES = 128

OFF_GATE_A = 0
OFF_GATE_B = OFF_GATE_A + D_MODEL
OFF_QA = OFF_GATE_B + D_MODEL
OFF_KA = OFF_QA + A_Q_WIDTH
OFF_VA = OFF_KA + A_KV_WIDTH
OFF_QB = OFF_VA + A_KV_WIDTH
OFF_KB = OFF_QB + B_QK_WIDTH
OFF_VB = OFF_KB + B_QK_WIDTH
OFF_OBG = OFF_VB + B_V_WIDTH
OFF_GLOW = OFF_OBG + B_V_WIDTH
PROJ_USED = OFF_GLOW + LANES
HALF_V = B_V_WIDTH // 2

VMEM_LIMIT = 60 * 1024 * 1024


def _cparams(sem):
    return pltpu.CompilerParams(dimension_semantics=sem, vmem_limit_bytes=VMEM_LIMIT)


def _const_spec(shape):
    return pl.BlockSpec(shape, lambda *_: (0,) * len(shape), pipeline_mode=pl.Buffered(1))


def _rmsnorm_rows(x, g):
    ms = jnp.mean(x * x, axis=-1, keepdims=True)
    return x * lax.rsqrt(ms + EPS) * g


NORM_ROWS = 128


BF16_SUBLANES = 16


def _cast_specs(weights, n_chunks, chunk_of):
    in_specs, out_specs, shapes = [], [], []
    for w in weights:
        rows, rem = divmod(w.shape[0], n_chunks)
        assert rem == 0 and rows % BF16_SUBLANES == 0, (w.shape, n_chunks)
        for specs in (in_specs, out_specs):
            specs.append(pl.BlockSpec((rows, w.shape[1]), lambda *idx: (chunk_of(*idx), 0)))
        shapes.append(jax.ShapeDtypeStruct(w.shape, BF16))
    return in_specs, out_specs, shapes


def _cast_blocks(in_refs, out_refs):
    for src, dst in zip(in_refs, out_refs):
        dst[...] = src[...].astype(dst.dtype)


def _normalize_rows(x_ref, g_ref, u_ref):
    g = g_ref[...]

    def body(c, carry):
        rows = pl.ds(pl.multiple_of(c * NORM_ROWS, NORM_ROWS), NORM_ROWS)
        u_ref[rows, :] = _rmsnorm_rows(x_ref[rows, :], g).astype(BF16)
        return carry

    lax.fori_loop(0, x_ref.shape[0] // NORM_ROWS, body, 0)


def _project(u_ref, w_ref, o_ref):
    o_ref[...] = lax.dot_general(u_ref[...], w_ref[...], (((1,), (1,)), ((), ())),
                                 preferred_element_type=F32).astype(o_ref.dtype)


_N_MIX = OFF_OBG - OFF_QA
_FEATURE_RUNS = (
    (OFF_GATE_A, _N_MIX + B_GATE_RANK + B_V_WIDTH, 2 * D_MODEL),
    (OFF_QA, 0, _N_MIX),
    (OFF_OBG, _N_MIX + B_GATE_RANK, B_V_WIDTH),
    (OFF_GLOW, _N_MIX, B_GATE_RANK),
)
PACK_TILE = 1024
PACK_WINDOW = PACK_TILE + B_GATE_RANK


def _pack_plan(n_native):
    plan = []
    for tile in range(pl.cdiv(PROJ_USED, PACK_TILE)):
        lo, hi = tile * PACK_TILE, (tile + 1) * PACK_TILE
        pieces = []
        for dst, src, n in _FEATURE_RUNS:
            a, b = max(lo, dst), min(hi, dst + n)
            if a < b:
                pieces.append((a - lo, src + a - dst, b - a))
        start = min(min(p[1] for p in pieces), n_native - PACK_WINDOW)
        assert all(start <= s and s + n <= start + PACK_WINDOW for _, s, n in pieces), (tile, pieces)
        assert start % BF16_SUBLANES == 0 and all(d % BF16_SUBLANES == 0 and (s - start) % BF16_SUBLANES == 0
                                                   for d, s, _ in pieces)
        plan.append((start, [(d, s - start, n) for d, s, n in pieces]))
    return plan


def _inproj_head_kernel(plan, x_ref, g_ref, w_ref, o_ref, wt_ref, u_ref):
    j = pl.program_id(0)
    pl.when(j == 0)(lambda: _normalize_rows(x_ref, g_ref, u_ref))
    for tile, (_, pieces) in enumerate(plan):
        @pl.when(j == tile)
        def _(pieces=pieces):
            covered = 0
            for dst, src, n in sorted(pieces):
                assert dst == covered
                wt_ref[dst:dst + n, :] = w_ref[src:src + n, :].astype(BF16)
                covered += n
            if covered < PACK_TILE:
                wt_ref[covered:, :] = jnp.zeros((PACK_TILE - covered, wt_ref.shape[1]), BF16)
    _project(u_ref, wt_ref, o_ref)


def _inproj_tail_kernel(x_ref, g_ref, w_ref, _, o_ref, u_ref):
    pl.when(pl.program_id(1) == 0)(lambda: _normalize_rows(x_ref, g_ref, u_ref))
    _project(u_ref, w_ref, o_ref)


def _inproj(x2, g, w_t, tm, tn):
    t, d = x2.shape
    n_native = w_t.shape[0]
    plan = _pack_plan(n_native)
    n = len(plan) * PACK_TILE
    assert n % tn == 0 and t % tm == 0

    def window_start(j):
        units = sum(jnp.where(j == tile, start // BF16_SUBLANES, 0) for tile, (start, _) in enumerate(plan))
        return units * BF16_SUBLANES

    proj, w_p = pl.pallas_call(
        functools.partial(_inproj_head_kernel, plan),
        grid=(len(plan),),
        in_specs=[
            pl.BlockSpec((tm, d), lambda j: (0, 0), pipeline_mode=pl.Buffered(1)),
            _const_spec(g.shape),
            pl.BlockSpec((pl.Element(PACK_WINDOW), pl.Element(d)), lambda j: (window_start(j), 0)),
        ],
        out_specs=[pl.BlockSpec((tm, PACK_TILE), lambda j: (0, j)),
                   pl.BlockSpec((PACK_TILE, d), lambda j: (j, 0))],
        out_shape=[jax.ShapeDtypeStruct((t, n), BF16), jax.ShapeDtypeStruct((n, d), BF16)],
        scratch_shapes=[pltpu.VMEM((tm, d), BF16)],
        compiler_params=_cparams(("arbitrary",)),
        name="inproj_head",
    )(x2, g, w_t)

    return pl.pallas_call(
        _inproj_tail_kernel,
        grid=(t // tm - 1, n // tn),
        in_specs=[
            pl.BlockSpec((tm, d), lambda i, j: (i + 1, 0)),
            pl.BlockSpec((1, d), lambda i, j: (0, 0)),
            pl.BlockSpec((tn, d), lambda i, j: (j, 0)),
            pl.BlockSpec(memory_space=pl.ANY),
        ],
        out_specs=pl.BlockSpec((tm, tn), lambda i, j: (i + 1, j)),
        out_shape=jax.ShapeDtypeStruct((t, n), BF16),
        input_output_aliases={3: 0},
        scratch_shapes=[pltpu.VMEM((tm, d), BF16)],
        compiler_params=_cparams(("parallel", "arbitrary")),
        name="inproj_tail",
    )(x2, g, w_p, proj)


def _bucket_starts():
    max_exact = N_BUCKETS // 2
    d = np.arange(WINDOW)
    large = max_exact + (np.log(np.maximum(d, 1).astype(np.float32) / max_exact)
                         / math.log(MAX_DISTANCE / max_exact)
                         * (N_BUCKETS - max_exact)).astype(np.int32)
    bucket = np.where(d < max_exact, d, np.minimum(large, N_BUCKETS - 1))
    starts = []
    for b in range(N_BUCKETS):
        hit = np.nonzero(bucket == b)[0]
        if hit.size:
            assert np.all(np.diff(hit) == 1)
            starts.append((b, int(hit[0])))
    return starts


HEADS_PER_TILE = LANES // A_HEAD_DIM
SWA_BLOCKS_PER_STEP = 4


def _swa_body(q_ref, kp_ref, kc_ref, vp_ref, vc_ref, sink_ref, rb_ref, o_ref, bias_ref):
    n_keys = 2 * A_BLOCK

    @pl.when((pl.program_id(0) == 0) & (pl.program_id(1) == 0))
    def _():
        row = lax.broadcasted_iota(jnp.int32, (A_BLOCK, n_keys), 0)
        col = lax.broadcasted_iota(jnp.int32, (A_BLOCK, n_keys), 1)
        dist = row + A_BLOCK - col
        band = (dist >= 0) & (dist < WINDOW)
        starts = _bucket_starts()
        for h in range(A_HEADS):
            val = jnp.full(dist.shape, rb_ref[starts[0][0], h], F32)
            for b, s in starts[1:]:
                val = jnp.where(dist >= s, rb_ref[b, h], val)
            val = jnp.where(band, val, NEG_INF)
            sink = sink_ref[0, h]
            bias_ref[0, h] = jnp.where(col == 0, sink, val)
            bias_ref[1, h] = jnp.where(col == 0, sink, jnp.where(col >= A_BLOCK, val, NEG_INF))

    lane = lax.broadcasted_iota(jnp.int32, (1, LANES), 1)
    scale = A_HEAD_DIM ** -0.5
    q_keep = (jnp.where(lane < A_HEAD_DIM, scale, 0.0).astype(BF16),
              jnp.where(lane < A_HEAD_DIM, 0.0, scale).astype(BF16))
    lower_lanes = lax.broadcasted_iota(jnp.int32, (A_BLOCK, LANES), 1) < A_HEAD_DIM
    key0 = lax.broadcasted_iota(jnp.int32, (n_keys, LANES), 0) == 0
    ones = jnp.ones((n_keys, LANES), BF16)

    def attend(q_rows, prev, cur, first):
        def both_blocks(which, tile):
            cols = slice(tile * LANES, (tile + 1) * LANES)
            cat = jnp.concatenate([prev[which][:, cols], cur[which][:, cols]], axis=0).astype(F32)
            cat = jnp.where(key0, 0.0, cat)
            return cat.astype(BF16), pltpu.roll(cat, A_HEAD_DIM, 1).astype(BF16)

        stacks = []
        for tile in range(A_KV_WIDTH // LANES):
            k_cat, k_swp = both_blocks(0, tile)
            v_cat, v_swp = both_blocks(1, tile)
            q0 = tile * A_GROUP
            stacks.append((k_cat, jnp.concatenate([v_cat, ones], axis=1),
                           [(q0, 0), (q0 + 1, 0), (q0 + 2, 1), (q0 + 3, 1)]))
            stacks.append((k_swp, jnp.concatenate([v_swp, ones], axis=1),
                           [(q0, 1), (q0 + 1, 1), (q0 + 2, 0), (q0 + 3, 0)]))

        scores = []
        for k_tile, _, members in stacks:
            q4 = jnp.concatenate(
                [q_ref[q_rows, qt * LANES:(qt + 1) * LANES] * q_keep[half] for qt, half in members],
                axis=0)
            scores.append(lax.dot_general(q4, k_tile, (((1,), (1,)), ((), ())),
                                          preferred_element_type=F32))
        s = jnp.concatenate(scores, axis=0)
        s = s + jnp.concatenate(
            [bias_ref[first, qt * HEADS_PER_TILE + half]
             for _, _, members in stacks for qt, half in members], axis=0)
        p = jnp.exp(s - jnp.max(s, axis=-1, keepdims=True)).astype(BF16)

        normed = {}
        rows_per_stack = len(stacks[0][2]) * A_BLOCK
        for i, (_, v_ones, members) in enumerate(stacks):
            ov = jnp.dot(p[i * rows_per_stack:(i + 1) * rows_per_stack], v_ones,
                         preferred_element_type=F32)
            o = ov[:, :LANES] / ov[:, LANES:]
            for j, member in enumerate(members):
                normed[member] = o[j * A_BLOCK:(j + 1) * A_BLOCK]
        for qt in range(A_Q_WIDTH // LANES):
            o_ref[q_rows, qt * LANES:(qt + 1) * LANES] = jnp.where(
                lower_lanes, normed[(qt, 0)], normed[(qt, 1)]).astype(o_ref.dtype)

    blocks_per_step = q_ref.shape[0] // A_BLOCK
    kv_prev = (kp_ref[...], vp_ref[...])
    for sub in range(blocks_per_step):
        rows = slice(sub * A_BLOCK, (sub + 1) * A_BLOCK)
        kv_cur = (kc_ref[rows, :], vc_ref[rows, :])
        first = (pl.program_id(1) == 0).astype(jnp.int32) if sub == 0 else 0
        attend(rows, kv_prev, kv_cur, first)
        kv_prev = kv_cur


GLA_CHUNKS_PER_STEP = 8


def _split3(x):
    hi = x.astype(BF16)
    r1 = x - hi.astype(F32)
    mid = r1.astype(BF16)
    lo = (r1 - mid.astype(F32)).astype(BF16)
    return jnp.concatenate([hi, mid, lo], axis=0)


def _gla_body(q_ref, k_ref, v0_ref, v1_ref, gl_ref, og0_ref, og1_ref, wgu_ref, bg_ref, ng_ref,
              o_ref, s_ref):
    c = B_CHUNK
    n_chunks = GLA_CHUNKS_PER_STEP
    n_rows = n_chunks * c
    heads_per_half = HALF_V // B_VAL_DIM
    v_refs = (v0_ref, v1_ref)
    og_refs = (og0_ref, og1_ref)
    chunk_rows = [slice(j * c, (j + 1) * c) for j in range(n_chunks)]
    key_cols = [slice(h * B_KEY_DIM, (h + 1) * B_KEY_DIM) for h in range(B_HEADS)]
    units = [(j, h) for j in range(n_chunks) for h in range(B_HEADS)]

    def v_of(refs, j, h):
        lo = (h % heads_per_half) * B_VAL_DIM
        rows = slice(None) if j is None else chunk_rows[j]
        return refs[h // heads_per_half][rows, lo:lo + B_VAL_DIM]

    @pl.when(pl.program_id(1) == 0)
    def _():
        s_ref[...] = jnp.zeros_like(s_ref)

    glin = jnp.dot(gl_ref[...], wgu_ref[...], preferred_element_type=F32) + bg_ref[...]
    log_a = (jnp.minimum(glin, 0.0) - jnp.log(1.0 + jnp.exp(-jnp.abs(glin)))) / B_GATE_TAU

    ri = lax.broadcasted_iota(jnp.int32, (n_rows, 3 * n_rows), 0)
    ci = lax.broadcasted_iota(jnp.int32, (n_rows, 3 * n_rows), 1)
    ci = ci - jnp.where(ci >= n_rows, n_rows, 0) - jnp.where(ci >= 2 * n_rows, n_rows, 0)
    shift = int(math.log2(c))
    same_chunk = lax.shift_right_logical(ri, shift) == lax.shift_right_logical(ci, shift)
    tri3 = ((ri >= ci) & same_chunk).astype(BF16)
    b = jnp.dot(tri3, _split3(log_a), preferred_element_type=F32)
    last_rows = [b[(j + 1) * c - 1:(j + 1) * c, :] for j in range(n_chunks)]
    b_last = jnp.concatenate([jnp.broadcast_to(r, (c, b.shape[1])) for r in last_rows], axis=0)

    qf = q_ref[...].astype(F32) * (B_KEY_DIM ** -0.5)
    kf = k_ref[...].astype(F32)
    q_dec = (qf * jnp.exp(b)).astype(BF16)
    k_dec = (kf * jnp.exp(-b)).astype(BF16)
    k_state = kf * jnp.exp(b_last - b)
    sublanes = 8
    pad = [jnp.zeros((sublanes - n_chunks, b.shape[1]), F32)] if n_chunks < sublanes else []
    decay_rows = jnp.exp(jnp.concatenate(last_rows + pad, axis=0))

    ri = lax.broadcasted_iota(jnp.int32, (c, c), 0)
    ci = lax.broadcasted_iota(jnp.int32, (c, c), 1)
    causal = ri >= ci
    att = {}
    for j, h in units:
        a = lax.dot_general(q_dec[chunk_rows[j], key_cols[h]], k_dec[chunk_rows[j], key_cols[h]],
                            (((1,), (1,)), ((), ())), preferred_element_type=F32)
        att[j, h] = jnp.where(causal, a, 0.0).astype(BF16)
    o_intra = {u: jnp.dot(att[u], v_of(v_refs, *u), preferred_element_type=F32) for u in units}
    ds = {(j, h): jnp.dot(k_state[chunk_rows[j], key_cols[h]].T.astype(BF16), v_of(v_refs, j, h),
                          preferred_element_type=F32) for j, h in units}

    entering = {}
    for h in range(B_HEADS):
        decay_t = decay_rows[:, key_cols[h]].T
        state = s_ref[h]
        for j in range(n_chunks):
            entering[j, h] = state.astype(BF16)
            state = decay_t[:, j:j + 1] * state + ds[j, h]
        s_ref[h] = state
    o_inter = {(j, h): jnp.dot(q_dec[chunk_rows[j], key_cols[h]], entering[j, h],
                               preferred_element_type=F32) for j, h in units}

    ng = ng_ref[...]
    for h in range(B_HEADS):
        o = jnp.concatenate([o_intra[j, h] + o_inter[j, h] for j in range(n_chunks)], axis=0)
        gate = v_of(og_refs, None, h).astype(F32)
        y = _rmsnorm_rows(o, ng) * (gate * jax.nn.sigmoid(gate))
        o_ref[:, h * B_VAL_DIM:(h + 1) * B_VAL_DIM] = y.astype(o_ref.dtype)


N_SWA_IN, N_GLA_IN = 7, 10


def _mixers_kernel(n_cast, *refs):
    swa_in, refs = refs[:N_SWA_IN], refs[N_SWA_IN:]
    gla_in, refs = refs[:N_GLA_IN], refs[N_GLA_IN:]
    cast_in, refs = refs[:n_cast], refs[n_cast:]
    attn_ref, gla_ref = refs[:2]
    cast_out, (bias_ref, s_ref) = refs[2:2 + n_cast], refs[2 + n_cast:]
    _cast_blocks(cast_in, cast_out)
    _swa_body(*swa_in, attn_ref, bias_ref)
    _gla_body(*gla_in, gla_ref, s_ref)


def _mixers(proj, sinks, rel_bias, wgu, bg, ng, cast_weights, batch, seq):
    t = proj.shape[0]
    rows = GLA_CHUNKS_PER_STEP * B_CHUNK
    assert rows == SWA_BLOCKS_PER_STEP * A_BLOCK and seq % rows == 0
    steps = seq // rows
    rb = lambda b, s: b * steps + s
    prev = lambda b, s: jnp.maximum(rb(b, s) * SWA_BLOCKS_PER_STEP - 1, 0)
    col_block = lambda width, off: pl.BlockSpec((rows, width), lambda b, s: (rb(b, s), off // width))
    prev_block = lambda off: pl.BlockSpec((A_BLOCK, A_KV_WIDTH), lambda b, s: (prev(b, s), off // A_KV_WIDTH))
    smem = functools.partial(pl.BlockSpec, memory_space=pltpu.SMEM)
    cast_in, cast_out, cast_shapes = _cast_specs(cast_weights, batch * steps, rb)
    swa_specs = [col_block(A_Q_WIDTH, OFF_QA), prev_block(OFF_KA), col_block(A_KV_WIDTH, OFF_KA),
                 prev_block(OFF_VA), col_block(A_KV_WIDTH, OFF_VA), smem(), smem()]
    gla_specs = [col_block(B_QK_WIDTH, OFF_QB), col_block(B_QK_WIDTH, OFF_KB),
                 col_block(HALF_V, OFF_VB), col_block(HALF_V, OFF_VB + HALF_V),
                 col_block(LANES, OFF_GLOW),
                 col_block(HALF_V, OFF_OBG), col_block(HALF_V, OFF_OBG + HALF_V),
                 _const_spec(wgu.shape), _const_spec(bg.shape), _const_spec(ng.shape)]
    assert len(swa_specs) == N_SWA_IN and len(gla_specs) == N_GLA_IN
    out_block = lambda width: pl.BlockSpec((rows, width), lambda b, s: (rb(b, s), 0))
    outs = pl.pallas_call(
        functools.partial(_mixers_kernel, len(cast_weights)),
        grid=(batch, steps),
        in_specs=swa_specs + gla_specs + cast_in,
        out_specs=[out_block(A_Q_WIDTH), out_block(B_V_WIDTH)] + cast_out,
        out_shape=[jax.ShapeDtypeStruct((t, A_Q_WIDTH), BF16),
                   jax.ShapeDtypeStruct((t, B_V_WIDTH), BF16)] + cast_shapes,
        scratch_shapes=[pltpu.VMEM((2, A_HEADS, A_BLOCK, 2 * A_BLOCK), F32),
                        pltpu.VMEM((B_HEADS, B_KEY_DIM, B_VAL_DIM), F32)],
        compiler_params=_cparams(("arbitrary", "arbitrary")),
        name="mixers",
    )(*([proj] * 5), sinks, rel_bias, *([proj] * 7), wgu, bg, ng, *cast_weights)
    return outs[0], outs[1], outs[2:]


def _merge_kernel(a_ref, b_ref, ga_ref, gb_ref, x_ref, wa_ref, wb_ref, wo_ref, gz_ref, h_ref, z_ref):
    ya = jnp.dot(a_ref[...], wa_ref[...], preferred_element_type=F32)
    yb = jnp.dot(b_ref[...], wb_ref[...], preferred_element_type=F32)
    merged = (jax.nn.sigmoid(ga_ref[...].astype(F32)) * ya
              + jax.nn.sigmoid(gb_ref[...].astype(F32)) * yb)
    h = x_ref[...] + jnp.dot(merged.astype(BF16), wo_ref[...], preferred_element_type=F32)
    h_ref[...] = h
    z_ref[...] = _rmsnorm_rows(h, gz_ref[...]).astype(z_ref.dtype)


def _merge(attn, gla, proj, x2, wa, wb, wo, gz, tm):
    t, d = x2.shape
    row_block = pl.BlockSpec((tm, d), lambda i: (i, 0))
    return pl.pallas_call(
        _merge_kernel,
        grid=(t // tm,),
        in_specs=[
            pl.BlockSpec((tm, A_Q_WIDTH), lambda i: (i, 0)),
            pl.BlockSpec((tm, B_V_WIDTH), lambda i: (i, 0)),
            pl.BlockSpec((tm, d), lambda i: (i, OFF_GATE_A // D_MODEL)),
            pl.BlockSpec((tm, d), lambda i: (i, OFF_GATE_B // D_MODEL)),
            row_block,
            _const_spec(wa.shape), _const_spec(wb.shape), _const_spec(wo.shape), _const_spec(gz.shape),
        ],
        out_specs=[row_block, row_block],
        out_shape=[jax.ShapeDtypeStruct((t, d), F32), jax.ShapeDtypeStruct((t, d), BF16)],
        compiler_params=_cparams(("parallel",)),
        name="merge",
    )(attn, gla, proj, proj, x2, wa, wb, wo, gz)


def _ffn_kernel(z_ref, h_ref, wg_ref, wu_ref, wd_ref, gf_ref, o_ref):
    f = pl.program_id(1)
    z = z_ref[...]
    g = jnp.dot(z, wg_ref[...], preferred_element_type=F32)
    u = jnp.dot(z, wu_ref[...], preferred_element_type=F32)
    act = (g * jax.nn.sigmoid(g) * u).astype(BF16)
    acc = jnp.where(f == 0, 0.0, o_ref[...])
    o_ref[...] = acc + jnp.dot(act, wd_ref[...], preferred_element_type=F32)

    @pl.when(f == pl.num_programs(1) - 1)
    def _():
        gf = gf_ref[...]

        def body(c, carry):
            rows = pl.ds(pl.multiple_of(c * NORM_ROWS, NORM_ROWS), NORM_ROWS)
            o_ref[rows, :] = _rmsnorm_rows(h_ref[rows, :] + o_ref[rows, :], gf)
            return carry

        lax.fori_loop(0, h_ref.shape[0] // NORM_ROWS, body, 0)


def _ffn(z, h, wg, wu, wd, gf, tm, tf):
    t, d = h.shape
    f = wg.shape[1]
    row_block = pl.BlockSpec((tm, d), lambda i, j: (i, 0))
    return pl.pallas_call(
        _ffn_kernel,
        grid=(t // tm, f // tf),
        in_specs=[
            row_block, row_block,
            pl.BlockSpec((d, tf), lambda i, j: (0, j)),
            pl.BlockSpec((d, tf), lambda i, j: (0, j)),
            pl.BlockSpec((tf, d), lambda i, j: (j, 0)),
            pl.BlockSpec((1, d), lambda i, j: (0, 0)),
        ],
        out_specs=row_block,
        out_shape=jax.ShapeDtypeStruct((t, d), F32),
        compiler_params=_cparams(("parallel", "arbitrary")),
        name="ffn",
    )(z, h, wg, wu, wd, gf)


def kernel(x, norm_mix_g, w_in, sinks, rel_bias, w_gate_up, b_gate, gla_norm_g, w_proj_a, w_proj_b,
           w_out, norm_ffn_g, w_ffn_gate, w_ffn_up, w_ffn_down, norm_final_g):
    batch, seq, d = x.shape
    assert d == D_MODEL and w_in.shape[0] == 1, "single-layer geometry"
    t = batch * seq
    x2 = x.reshape(t, d)

    proj = _inproj(x2, norm_mix_g, w_in[0].T, tm=1024, tn=2304)

    wgu = jnp.zeros((LANES, B_QK_WIDTH), BF16).at[:B_GATE_RANK].set(w_gate_up[0].astype(BF16))
    later_weights = (w_proj_a[0], w_proj_b[0], w_out[0], w_ffn_gate[0], w_ffn_up[0], w_ffn_down[0])
    attn, gla, (wa, wb, wo, wg, wu, wd) = _mixers(proj, sinks, rel_bias, wgu, b_gate, gla_norm_g,
                                                  later_weights, batch, seq)

    h, z = _merge(attn, gla, proj, x2, wa, wb, wo, norm_ffn_g, tm=512)

    out = _ffn(z, h, wg, wu, wd, norm_final_g.reshape(1, d), tm=1024, tf=512)
    return out.reshape(batch, seq, d)
```

```python
import functools
import math

import numpy as np
import jax
import jax.numpy as jnp
from jax import lax
from jax.experimental import pallas as pl
from jax.experimental.pallas import tpu as pltpu

F32 = jnp.float32
BF16 = jnp.bfloat16

D_MODEL = 2048
A_HEADS = 16
A_KV_HEADS = 4
A_HEAD_DIM = 64
A_GROUP = A_HEADS // A_KV_HEADS
WINDOW = 128
A_BLOCK = 128
A_Q_WIDTH = A_HEADS * A_HEAD_DIM
A_KV_WIDTH = A_KV_HEADS * A_HEAD_DIM
N_BUCKETS = 32
MAX_DISTANCE = 128
B_HEADS = 4
B_KEY_DIM = 128
B_VAL_DIM = 256
B_QK_WIDTH = B_HEADS * B_KEY_DIM
B_V_WIDTH = B_HEADS * B_VAL_DIM
B_GATE_RANK = 16
B_GATE_TAU = 16.0
B_CHUNK = 64
EPS = 1e-6
NEG_INF = -1e30

LANES = 128

OFF_GATE_A = 0
OFF_GATE_B = OFF_GATE_A + D_MODEL
OFF_QA = OFF_GATE_B + D_MODEL
OFF_KA = OFF_QA + A_Q_WIDTH
OFF_VA = OFF_KA + A_KV_WIDTH
OFF_QB = OFF_VA + A_KV_WIDTH
OFF_KB = OFF_QB + B_QK_WIDTH
OFF_VB = OFF_KB + B_QK_WIDTH
OFF_OBG = OFF_VB + B_V_WIDTH
OFF_GLOW = OFF_OBG + B_V_WIDTH
PROJ_USED = OFF_GLOW + LANES
HALF_V = B_V_WIDTH // 2

VMEM_LIMIT = 60 * 1024 * 1024


def _cparams(sem):
    return pltpu.CompilerParams(dimension_semantics=sem, vmem_limit_bytes=VMEM_LIMIT)


def _const_spec(shape):
    return pl.BlockSpec(shape, lambda *_: (0,) * len(shape), pipeline_mode=pl.Buffered(1))


def _rmsnorm_rows(x, g):
    ms = jnp.mean(x * x, axis=-1, keepdims=True)
    return x * lax.rsqrt(ms + EPS) * g


NORM_ROWS = 128


BF16_SUBLANES = 16


def _cast_specs(weights, n_chunks, chunk_of):
    in_specs, out_specs, shapes = [], [], []
    for w in weights:
        rows, rem = divmod(w.shape[0], n_chunks)
        assert rem == 0 and rows % BF16_SUBLANES == 0, (w.shape, n_chunks)
        for specs in (in_specs, out_specs):
            specs.append(pl.BlockSpec((rows, w.shape[1]), lambda *idx: (chunk_of(*idx), 0)))
        shapes.append(jax.ShapeDtypeStruct(w.shape, BF16))
    return in_specs, out_specs, shapes


def _cast_blocks(in_refs, out_refs):
    for src, dst in zip(in_refs, out_refs):
        dst[...] = src[...].astype(dst.dtype)


def _normalize_rows(x_ref, g_ref, u_ref):
    g = g_ref[...]

    def body(c, carry):
        rows = pl.ds(pl.multiple_of(c * NORM_ROWS, NORM_ROWS), NORM_ROWS)
        u_ref[rows, :] = _rmsnorm_rows(x_ref[rows, :], g).astype(BF16)
        return carry

    lax.fori_loop(0, x_ref.shape[0] // NORM_ROWS, body, 0)


def _project(u_ref, w_ref, o_ref):
    o_ref[...] = lax.dot_general(u_ref[...], w_ref[...], (((1,), (1,)), ((), ())),
                                 preferred_element_type=F32).astype(o_ref.dtype)


_N_MIX = OFF_OBG - OFF_QA
_FEATURE_RUNS = (
    (OFF_GATE_A, _N_MIX + B_GATE_RANK + B_V_WIDTH, 2 * D_MODEL),
    (OFF_QA, 0, _N_MIX),
    (OFF_OBG, _N_MIX + B_GATE_RANK, B_V_WIDTH),
    (OFF_GLOW, _N_MIX, B_GATE_RANK),
)
PACK_TILE = 1024
PACK_WINDOW = PACK_TILE + B_GATE_RANK


def _pack_plan(n_native):
    plan = []
    for tile in range(pl.cdiv(PROJ_USED, PACK_TILE)):
        lo, hi = tile * PACK_TILE, (tile + 1) * PACK_TILE
        pieces = []
        for dst, src, n in _FEATURE_RUNS:
            a, b = max(lo, dst), min(hi, dst + n)
            if a < b:
                pieces.append((a - lo, src + a - dst, b - a))
        start = min(min(p[1] for p in pieces), n_native - PACK_WINDOW)
        assert all(start <= s and s + n <= start + PACK_WINDOW for _, s, n in pieces), (tile, pieces)
        assert start % BF16_SUBLANES == 0 and all(d % BF16_SUBLANES == 0 and (s - start) % BF16_SUBLANES == 0
                                                   for d, s, _ in pieces)
        plan.append((start, [(d, s - start, n) for d, s, n in pieces]))
    return plan


def _inproj_head_kernel(plan, x_ref, g_ref, w_ref, o_ref, wt_ref, u_ref):
    j = pl.program_id(0)
    pl.when(j == 0)(lambda: _normalize_rows(x_ref, g_ref, u_ref))
    for tile, (_, pieces) in enumerate(plan):
        @pl.when(j == tile)
        def _(pieces=pieces):
            covered = 0
            for dst, src, n in sorted(pieces):
                assert dst == covered
                wt_ref[dst:dst + n, :] = w_ref[src:src + n, :].astype(BF16)
                covered += n
            if covered < PACK_TILE:
                wt_ref[covered:, :] = jnp.zeros((PACK_TILE - covered, wt_ref.shape[1]), BF16)
    _project(u_ref, wt_ref, o_ref)


def _inproj_tail_kernel(x_ref, g_ref, w_ref, _, o_ref, u_ref):
    pl.when(pl.program_id(1) == 0)(lambda: _normalize_rows(x_ref, g_ref, u_ref))
    _project(u_ref, w_ref, o_ref)


def _inproj(x2, g, w_t, tm, tn):
    t, d = x2.shape
    n_native = w_t.shape[0]
    plan = _pack_plan(n_native)
    n = len(plan) * PACK_TILE
    assert n % tn == 0 and t % tm == 0

    def window_start(j):
        units = sum(jnp.where(j == tile, start // BF16_SUBLANES, 0) for tile, (start, _) in enumerate(plan))
        return units * BF16_SUBLANES

    proj, w_p = pl.pallas_call(
        functools.partial(_inproj_head_kernel, plan),
        grid=(len(plan),),
        in_specs=[
            pl.BlockSpec((tm, d), lambda j: (0, 0), pipeline_mode=pl.Buffered(1)),
            _const_spec(g.shape),
            pl.BlockSpec((pl.Element(PACK_WINDOW), pl.Element(d)), lambda j: (window_start(j), 0)),
        ],
        out_specs=[pl.BlockSpec((tm, PACK_TILE), lambda j: (0, j)),
                   pl.BlockSpec((PACK_TILE, d), lambda j: (j, 0))],
        out_shape=[jax.ShapeDtypeStruct((t, n), BF16), jax.ShapeDtypeStruct((n, d), BF16)],
        scratch_shapes=[pltpu.VMEM((tm, d), BF16)],
        compiler_params=_cparams(("arbitrary",)),
        name="inproj_head",
    )(x2, g, w_t)

    return pl.pallas_call(
        _inproj_tail_kernel,
        grid=(t // tm - 1, n // tn),
        in_specs=[
            pl.BlockSpec((tm, d), lambda i, j: (i + 1, 0)),
            pl.BlockSpec((1, d), lambda i, j: (0, 0)),
            pl.BlockSpec((tn, d), lambda i, j: (j, 0)),
            pl.BlockSpec(memory_space=pl.ANY),
        ],
        out_specs=pl.BlockSpec((tm, tn), lambda i, j: (i + 1, j)),
        out_shape=jax.ShapeDtypeStruct((t, n), BF16),
        input_output_aliases={3: 0},
        scratch_shapes=[pltpu.VMEM((tm, d), BF16)],
        compiler_params=_cparams(("parallel", "arbitrary")),
        name="inproj_tail",
    )(x2, g, w_p, proj)


def _bucket_starts():
    max_exact = N_BUCKETS // 2
    d = np.arange(WINDOW)
    large = max_exact + (np.log(np.maximum(d, 1).astype(np.float32) / max_exact)
                         / math.log(MAX_DISTANCE / max_exact)
                         * (N_BUCKETS - max_exact)).astype(np.int32)
    bucket = np.where(d < max_exact, d, np.minimum(large, N_BUCKETS - 1))
    starts = []
    for b in range(N_BUCKETS):
        hit = np.nonzero(bucket == b)[0]
        if hit.size:
            assert np.all(np.diff(hit) == 1)
            starts.append((b, int(hit[0])))
    return starts


HEADS_PER_TILE = LANES // A_HEAD_DIM
SWA_BLOCKS_PER_STEP = 4


def _swa_body(q_ref, kp_ref, kc_ref, vp_ref, vc_ref, sink_ref, rb_ref, o_ref, bias_ref):
    n_keys = 2 * A_BLOCK

    @pl.when((pl.program_id(0) == 0) & (pl.program_id(1) == 0))
    def _():
        row = lax.broadcasted_iota(jnp.int32, (A_BLOCK, n_keys), 0)
        col = lax.broadcasted_iota(jnp.int32, (A_BLOCK, n_keys), 1)
        dist = row + A_BLOCK - col
        band = (dist >= 0) & (dist < WINDOW)
        starts = _bucket_starts()
        for h in range(A_HEADS):
            val = jnp.full(dist.shape, rb_ref[starts[0][0], h], F32)
            for b, s in starts[1:]:
                val = jnp.where(dist >= s, rb_ref[b, h], val)
            val = jnp.where(band, val, NEG_INF)
            sink = sink_ref[0, h]
            bias_ref[0, h] = jnp.where(col == 0, sink, val)
            bias_ref[1, h] = jnp.where(col == 0, sink, jnp.where(col >= A_BLOCK, val, NEG_INF))

    lane = lax.broadcasted_iota(jnp.int32, (1, LANES), 1)
    scale = A_HEAD_DIM ** -0.5
    q_keep = (jnp.where(lane < A_HEAD_DIM, scale, 0.0).astype(BF16),
              jnp.where(lane < A_HEAD_DIM, 0.0, scale).astype(BF16))
    lower_lanes = lax.broadcasted_iota(jnp.int32, (A_BLOCK, LANES), 1) < A_HEAD_DIM
    key0 = lax.broadcasted_iota(jnp.int32, (n_keys, LANES), 0) == 0
    ones = jnp.ones((n_keys, LANES), BF16)

    def attend(q_rows, prev, cur, first):
        def both_blocks(which, tile):
            cols = slice(tile * LANES, (tile + 1) * LANES)
            cat = jnp.concatenate([prev[which][:, cols], cur[which][:, cols]], axis=0).astype(F32)
            cat = jnp.where(key0, 0.0, cat)
            return cat.astype(BF16), pltpu.roll(cat, A_HEAD_DIM, 1).astype(BF16)

        stacks = []
        for tile in range(A_KV_WIDTH // LANES):
            k_cat, k_swp = both_blocks(0, tile)
            v_cat, v_swp = both_blocks(1, tile)
            q0 = tile * A_GROUP
            stacks.append((k_cat, jnp.concatenate([v_cat, ones], axis=1),
                           [(q0, 0), (q0 + 1, 0), (q0 + 2, 1), (q0 + 3, 1)]))
            stacks.append((k_swp, jnp.concatenate([v_swp, ones], axis=1),
                           [(q0, 1), (q0 + 1, 1), (q0 + 2, 0), (q0 + 3, 0)]))

        scores = []
        for k_tile, _, members in stacks:
            q4 = jnp.concatenate(
                [q_ref[q_rows, qt * LANES:(qt + 1) * LANES] * q_keep[half] for qt, half in members],
                axis=0)
            scores.append(lax.dot_general(q4, k_tile, (((1,), (1,)), ((), ())),
                                          preferred_element_type=F32))
        s = jnp.concatenate(scores, axis=0)
        s = s + jnp.concatenate(
            [bias_ref[first, qt * HEADS_PER_TILE + half]
             for _, _, members in stacks for qt, half in members], axis=0)
        p = jnp.exp(s - jnp.max(s, axis=-1, keepdims=True)).astype(BF16)

        normed = {}
        rows_per_stack = len(stacks[0][2]) * A_BLOCK
        for i, (_, v_ones, members) in enumerate(stacks):
            ov = jnp.dot(p[i * rows_per_stack:(i + 1) * rows_per_stack], v_ones,
                         preferred_element_type=F32)
            o = ov[:, :LANES] / ov[:, LANES:]
            for j, member in enumerate(members):
                normed[member] = o[j * A_BLOCK:(j + 1) * A_BLOCK]
        for qt in range(A_Q_WIDTH // LANES):
            o_ref[q_rows, qt * LANES:(qt + 1) * LANES] = jnp.where(
                lower_lanes, normed[(qt, 0)], normed[(qt, 1)]).astype(o_ref.dtype)

    blocks_per_step = q_ref.shape[0] // A_BLOCK
    kv_prev = (kp_ref[...], vp_ref[...])
    for sub in range(blocks_per_step):
        rows = slice(sub * A_BLOCK, (sub + 1) * A_BLOCK)
        kv_cur = (kc_ref[rows, :], vc_ref[rows, :])
        first = (pl.program_id(1) == 0).astype(jnp.int32) if sub == 0 else 0
        attend(rows, kv_prev, kv_cur, first)
        kv_prev = kv_cur


GLA_CHUNKS_PER_STEP = 8
GLA_PREFIX_GROUP = 1


def _split3(x):
    hi = x.astype(BF16)
    r1 = x - hi.astype(F32)
    mid = r1.astype(BF16)
    lo = (r1 - mid.astype(F32)).astype(BF16)
    return jnp.concatenate([hi, mid, lo], axis=0)


def _gla_body(q_ref, k_ref, v0_ref, v1_ref, gl_ref, og0_ref, og1_ref, wgu_ref, bg_ref, ng_ref,
              o_ref, s_ref):
    c = B_CHUNK
    n_chunks = GLA_CHUNKS_PER_STEP
    n_rows = n_chunks * c
    heads_per_half = HALF_V // B_VAL_DIM
    v_refs = (v0_ref, v1_ref)
    og_refs = (og0_ref, og1_ref)
    chunk_rows = [slice(j * c, (j + 1) * c) for j in range(n_chunks)]
    key_cols = [slice(h * B_KEY_DIM, (h + 1) * B_KEY_DIM) for h in range(B_HEADS)]
    units = [(j, h) for j in range(n_chunks) for h in range(B_HEADS)]

    def v_of(refs, j, h):
        lo = (h % heads_per_half) * B_VAL_DIM
        rows = slice(None) if j is None else chunk_rows[j]
        return refs[h // heads_per_half][rows, lo:lo + B_VAL_DIM]

    @pl.when(pl.program_id(1) == 0)
    def _():
        s_ref[...] = jnp.zeros_like(s_ref)

    glin = jnp.dot(gl_ref[...], wgu_ref[...], preferred_element_type=F32) + bg_ref[...]
    log_a = (jnp.minimum(glin, 0.0) - jnp.log(1.0 + jnp.exp(-jnp.abs(glin)))) / B_GATE_TAU

    g_rows = GLA_PREFIX_GROUP * c
    ri = lax.broadcasted_iota(jnp.int32, (g_rows, 3 * g_rows), 0)
    ci = lax.broadcasted_iota(jnp.int32, (g_rows, 3 * g_rows), 1)
    ci = ci - jnp.where(ci >= g_rows, g_rows, 0) - jnp.where(ci >= 2 * g_rows, g_rows, 0)
    shift = int(math.log2(c))
    same_chunk = lax.shift_right_logical(ri, shift) == lax.shift_right_logical(ci, shift)
    tri3 = ((ri >= ci) & same_chunk).astype(BF16)
    b = jnp.concatenate(
        [jnp.dot(tri3, _split3(log_a[r0:r0 + g_rows]), preferred_element_type=F32)
         for r0 in range(0, n_rows, g_rows)], axis=0)
    last_rows = [b[(j + 1) * c - 1:(j + 1) * c, :] for j in range(n_chunks)]
    b_last = jnp.concatenate([jnp.broadcast_to(r, (c, b.shape[1])) for r in last_rows], axis=0)

    qf = q_ref[...].astype(F32) * (B_KEY_DIM ** -0.5)
    kf = k_ref[...].astype(F32)
    q_dec = (qf * jnp.exp(b)).astype(BF16)
    k_dec = (kf * jnp.exp(-b)).astype(BF16)
    k_state = kf * jnp.exp(b_last - b)
    sublanes = 8
    pad = [jnp.zeros((sublanes - n_chunks, b.shape[1]), F32)] if n_chunks < sublanes else []
    decay_rows = jnp.exp(jnp.concatenate(last_rows + pad, axis=0))

    ri = lax.broadcasted_iota(jnp.int32, (c, c), 0)
    ci = lax.broadcasted_iota(jnp.int32, (c, c), 1)
    causal = ri >= ci
    att = {}
    for j, h in units:
        a = lax.dot_general(q_dec[chunk_rows[j], key_cols[h]], k_dec[chunk_rows[j], key_cols[h]],
                            (((1,), (1,)), ((), ())), preferred_element_type=F32)
        att[j, h] = jnp.where(causal, a, 0.0).astype(BF16)
    o_intra = {u: jnp.dot(att[u], v_of(v_refs, *u), preferred_element_type=F32) for u in units}
    ds = {(j, h): jnp.dot(k_state[chunk_rows[j], key_cols[h]].T.astype(BF16), v_of(v_refs, j, h),
                          preferred_element_type=F32) for j, h in units}

    entering = {}
    for h in range(B_HEADS):
        decay_t = decay_rows[:, key_cols[h]].T
        state = s_ref[h]
        for j in range(n_chunks):
            entering[j, h] = state.astype(BF16)
            state = decay_t[:, j:j + 1] * state + ds[j, h]
        s_ref[h] = state
    o_inter = {(j, h): jnp.dot(q_dec[chunk_rows[j], key_cols[h]], entering[j, h],
                               preferred_element_type=F32) for j, h in units}

    ng = ng_ref[...]
    for h in range(B_HEADS):
        o = jnp.concatenate([o_intra[j, h] + o_inter[j, h] for j in range(n_chunks)], axis=0)
        gate = v_of(og_refs, None, h).astype(F32)
        y = _rmsnorm_rows(o, ng) * (gate * jax.nn.sigmoid(gate))
        o_ref[:, h * B_VAL_DIM:(h + 1) * B_VAL_DIM] = y.astype(o_ref.dtype)


N_SWA_IN, N_GLA_IN = 7, 10


def _mixers_kernel(n_cast, *refs):
    swa_in, refs = refs[:N_SWA_IN], refs[N_SWA_IN:]
    gla_in, refs = refs[:N_GLA_IN], refs[N_GLA_IN:]
    cast_in, refs = refs[:n_cast], refs[n_cast:]
    attn_ref, gla_ref = refs[:2]
    cast_out, (bias_ref, s_ref) = refs[2:2 + n_cast], refs[2 + n_cast:]
    _cast_blocks(cast_in, cast_out)
    _swa_body(*swa_in, attn_ref, bias_ref)
    _gla_body(*gla_in, gla_ref, s_ref)


def _mixers(proj, sinks, rel_bias, wgu, bg, ng, cast_weights, batch, seq):
    t = proj.shape[0]
    rows = GLA_CHUNKS_PER_STEP * B_CHUNK
    assert rows == SWA_BLOCKS_PER_STEP * A_BLOCK and seq % rows == 0
    steps = seq // rows
    rb = lambda b, s: b * steps + s
    prev = lambda b, s: jnp.maximum(rb(b, s) * SWA_BLOCKS_PER_STEP - 1, 0)
    col_block = lambda width, off: pl.BlockSpec((rows, width), lambda b, s: (rb(b, s), off // width))
    prev_block = lambda off: pl.BlockSpec((A_BLOCK, A_KV_WIDTH), lambda b, s: (prev(b, s), off // A_KV_WIDTH))
    smem = functools.partial(pl.BlockSpec, memory_space=pltpu.SMEM)
    cast_in, cast_out, cast_shapes = _cast_specs(cast_weights, batch * steps, rb)
    swa_specs = [col_block(A_Q_WIDTH, OFF_QA), prev_block(OFF_KA), col_block(A_KV_WIDTH, OFF_KA),
                 prev_block(OFF_VA), col_block(A_KV_WIDTH, OFF_VA), smem(), smem()]
    gla_specs = [col_block(B_QK_WIDTH, OFF_QB), col_block(B_QK_WIDTH, OFF_KB),
                 col_block(HALF_V, OFF_VB), col_block(HALF_V, OFF_VB + HALF_V),
                 col_block(LANES, OFF_GLOW),
                 col_block(HALF_V, OFF_OBG), col_block(HALF_V, OFF_OBG + HALF_V),
                 _const_spec(wgu.shape), _const_spec(bg.shape), _const_spec(ng.shape)]
    assert len(swa_specs) == N_SWA_IN and len(gla_specs) == N_GLA_IN
    out_block = lambda width: pl.BlockSpec((rows, width), lambda b, s: (rb(b, s), 0))
    outs = pl.pallas_call(
        functools.partial(_mixers_kernel, len(cast_weights)),
        grid=(batch, steps),
        in_specs=swa_specs + gla_specs + cast_in,
        out_specs=[out_block(A_Q_WIDTH), out_block(B_V_WIDTH)] + cast_out,
        out_shape=[jax.ShapeDtypeStruct((t, A_Q_WIDTH), BF16),
                   jax.ShapeDtypeStruct((t, B_V_WIDTH), BF16)] + cast_shapes,
        scratch_shapes=[pltpu.VMEM((2, A_HEADS, A_BLOCK, 2 * A_BLOCK), F32),
                        pltpu.VMEM((B_HEADS, B_KEY_DIM, B_VAL_DIM), F32)],
        compiler_params=_cparams(("arbitrary", "arbitrary")),
        name="mixers",
    )(*([proj] * 5), sinks, rel_bias, *([proj] * 7), wgu, bg, ng, *cast_weights)
    return outs[0], outs[1], outs[2:]


def _merge_kernel(a_ref, b_ref, ga_ref, gb_ref, x_ref, wa_ref, wb_ref, wo_ref, gz_ref, h_ref, z_ref):
    ya = jnp.dot(a_ref[...], wa_ref[...], preferred_element_type=F32)
    yb = jnp.dot(b_ref[...], wb_ref[...], preferred_element_type=F32)
    merged = (jax.nn.sigmoid(ga_ref[...].astype(F32)) * ya
              + jax.nn.sigmoid(gb_ref[...].astype(F32)) * yb)
    h = x_ref[...] + jnp.dot(merged.astype(BF16), wo_ref[...], preferred_element_type=F32)
    h_ref[...] = h
    z_ref[...] = _rmsnorm_rows(h, gz_ref[...]).astype(z_ref.dtype)


def _merge(attn, gla, proj, x2, wa, wb, wo, gz, tm):
    t, d = x2.shape
    row_block = pl.BlockSpec((tm, d), lambda i: (i, 0))
    return pl.pallas_call(
        _merge_kernel,
        grid=(t // tm,),
        in_specs=[
            pl.BlockSpec((tm, A_Q_WIDTH), lambda i: (i, 0)),
            pl.BlockSpec((tm, B_V_WIDTH), lambda i: (i, 0)),
            pl.BlockSpec((tm, d), lambda i: (i, OFF_GATE_A // D_MODEL)),
            pl.BlockSpec((tm, d), lambda i: (i, OFF_GATE_B // D_MODEL)),
            row_block,
            _const_spec(wa.shape), _const_spec(wb.shape), _const_spec(wo.shape), _const_spec(gz.shape),
        ],
        out_specs=[row_block, row_block],
        out_shape=[jax.ShapeDtypeStruct((t, d), F32), jax.ShapeDtypeStruct((t, d), BF16)],
        compiler_params=_cparams(("parallel",)),
        name="merge",
    )(attn, gla, proj, proj, x2, wa, wb, wo, gz)


def _ffn_kernel(z_ref, h_ref, wg_ref, wu_ref, wd_ref, gf_ref, o_ref):
    f = pl.program_id(1)
    z = z_ref[...]
    g = jnp.dot(z, wg_ref[...], preferred_element_type=F32)
    u = jnp.dot(z, wu_ref[...], preferred_element_type=F32)
    act = (g * jax.nn.sigmoid(g) * u).astype(BF16)
    acc = jnp.where(f == 0, 0.0, o_ref[...])
    o_ref[...] = acc + jnp.dot(act, wd_ref[...], preferred_element_type=F32)

    @pl.when(f == pl.num_programs(1) - 1)
    def _():
        gf = gf_ref[...]

        def body(c, carry):
            rows = pl.ds(pl.multiple_of(c * NORM_ROWS, NORM_ROWS), NORM_ROWS)
            o_ref[rows, :] = _rmsnorm_rows(h_ref[rows, :] + o_ref[rows, :], gf)
            return carry

        lax.fori_loop(0, h_ref.shape[0] // NORM_ROWS, body, 0)


def _ffn(z, h, wg, wu, wd, gf, tm, tf):
    t, d = h.shape
    f = wg.shape[1]
    row_block = pl.BlockSpec((tm, d), lambda i, j: (i, 0))
    return pl.pallas_call(
        _ffn_kernel,
        grid=(t // tm, f // tf),
        in_specs=[
            row_block, row_block,
            pl.BlockSpec((d, tf), lambda i, j: (0, j)),
            pl.BlockSpec((d, tf), lambda i, j: (0, j)),
            pl.BlockSpec((tf, d), lambda i, j: (j, 0)),
            pl.BlockSpec((1, d), lambda i, j: (0, 0)),
        ],
        out_specs=row_block,
        out_shape=jax.ShapeDtypeStruct((t, d), F32),
        compiler_params=_cparams(("parallel", "arbitrary")),
        name="ffn",
    )(z, h, wg, wu, wd, gf)


def kernel(x, norm_mix_g, w_in, sinks, rel_bias, w_gate_up, b_gate, gla_norm_g, w_proj_a, w_proj_b,
           w_out, norm_ffn_g, w_ffn_gate, w_ffn_up, w_ffn_down, norm_final_g):
    batch, seq, d = x.shape
    assert d == D_MODEL and w_in.shape[0] == 1, "single-layer geometry"
    t = batch * seq
    x2 = x.reshape(t, d)

    proj = _inproj(x2, norm_mix_g, w_in[0].T, tm=1024, tn=2304)

    wgu = jnp.zeros((LANES, B_QK_WIDTH), BF16).at[:B_GATE_RANK].set(w_gate_up[0].astype(BF16))
    later_weights = (w_proj_a[0], w_proj_b[0], w_out[0], w_ffn_gate[0], w_ffn_up[0], w_ffn_down[0])
    attn, gla, (wa, wb, wo, wg, wu, wd) = _mixers(proj, sinks, rel_bias, wgu, b_gate, gla_norm_g,
                                                  later_weights, batch, seq)

    h, z = _merge(attn, gla, proj, x2, wa, wb, wo, norm_ffn_g, tm=512)

    out = _ffn(z, h, wg, wu, wd, norm_final_g.reshape(1, d), tm=1024, tf=512)
    return out.reshape(batch, seq, d)
```

```python
import functools
import math

import numpy as np
import jax
import jax.numpy as jnp
from jax import lax
from jax.experimental import pallas as pl
from jax.experimental.pallas import tpu as pltpu

F32 = jnp.float32
BF16 = jnp.bfloat16

D_MODEL = 2048
A_HEADS = 16
A_KV_HEADS = 4
A_HEAD_DIM = 64
A_GROUP = A_HEADS // A_KV_HEADS
WINDOW = 128
A_BLOCK = 128
A_Q_WIDTH = A_HEADS * A_HEAD_DIM
A_KV_WIDTH = A_KV_HEADS * A_HEAD_DIM
N_BUCKETS = 32
MAX_DISTANCE = 128
B_HEADS = 4
B_KEY_DIM = 128
B_VAL_DIM = 256
B_QK_WIDTH = B_HEADS * B_KEY_DIM
B_V_WIDTH = B_HEADS * B_VAL_DIM
B_GATE_RANK = 16
B_GATE_TAU = 16.0
B_CHUNK = 64
EPS = 1e-6
NEG_INF = -1e30

LANES = 128

OFF_GATE_A = 0
OFF_GATE_B = OFF_GATE_A + D_MODEL
OFF_QA = OFF_GATE_B + D_MODEL
OFF_KA = OFF_QA + A_Q_WIDTH
OFF_VA = OFF_KA + A_KV_WIDTH
OFF_QB = OFF_VA + A_KV_WIDTH
OFF_KB = OFF_QB + B_QK_WIDTH
OFF_VB = OFF_KB + B_QK_WIDTH
OFF_OBG = OFF_VB + B_V_WIDTH
OFF_GLOW = OFF_OBG + B_V_WIDTH
PROJ_USED = OFF_GLOW + LANES
HALF_V = B_V_WIDTH // 2

VMEM_LIMIT = 60 * 1024 * 1024


def _cparams(sem):
    return pltpu.CompilerParams(dimension_semantics=sem, vmem_limit_bytes=VMEM_LIMIT)


def _const_spec(shape):
    return pl.BlockSpec(shape, lambda *_: (0,) * len(shape), pipeline_mode=pl.Buffered(1))


def _rmsnorm_rows(x, g):
    ms = jnp.mean(x * x, axis=-1, keepdims=True)
    return x * lax.rsqrt(ms + EPS) * g


NORM_ROWS = 128


BF16_SUBLANES = 16


def _cast_specs(weights, n_chunks, chunk_of):
    in_specs, out_specs, shapes = [], [], []
    for w in weights:
        rows, rem = divmod(w.shape[0], n_chunks)
        assert rem == 0 and rows % BF16_SUBLANES == 0, (w.shape, n_chunks)
        for specs in (in_specs, out_specs):
            specs.append(pl.BlockSpec((rows, w.shape[1]), lambda *idx: (chunk_of(*idx), 0)))
        shapes.append(jax.ShapeDtypeStruct(w.shape, BF16))
    return in_specs, out_specs, shapes


def _cast_blocks(in_refs, out_refs):
    for src, dst in zip(in_refs, out_refs):
        dst[...] = src[...].astype(dst.dtype)


def _normalize_rows(x_ref, g_ref, u_ref):
    g = g_ref[...]

    def body(c, carry):
        rows = pl.ds(pl.multiple_of(c * NORM_ROWS, NORM_ROWS), NORM_ROWS)
        u_ref[rows, :] = _rmsnorm_rows(x_ref[rows, :], g).astype(BF16)
        return carry

    lax.fori_loop(0, x_ref.shape[0] // NORM_ROWS, body, 0)


def _project(u_ref, w_ref, o_ref):
    o_ref[...] = lax.dot_general(u_ref[...], w_ref[...], (((1,), (1,)), ((), ())),
                                 preferred_element_type=F32).astype(o_ref.dtype)


_N_MIX = OFF_OBG - OFF_QA
_FEATURE_RUNS = (
    (OFF_GATE_A, _N_MIX + B_GATE_RANK + B_V_WIDTH, 2 * D_MODEL),
    (OFF_QA, 0, _N_MIX),
    (OFF_OBG, _N_MIX + B_GATE_RANK, B_V_WIDTH),
    (OFF_GLOW, _N_MIX, B_GATE_RANK),
)
PACK_TILE = 1024
PACK_WINDOW = PACK_TILE + B_GATE_RANK


def _pack_plan(n_native):
    plan = []
    for tile in range(pl.cdiv(PROJ_USED, PACK_TILE)):
        lo, hi = tile * PACK_TILE, (tile + 1) * PACK_TILE
        pieces = []
        for dst, src, n in _FEATURE_RUNS:
            a, b = max(lo, dst), min(hi, dst + n)
            if a < b:
                pieces.append((a - lo, src + a - dst, b - a))
        start = min(min(p[1] for p in pieces), n_native - PACK_WINDOW)
        assert all(start <= s and s + n <= start + PACK_WINDOW for _, s, n in pieces), (tile, pieces)
        assert start % BF16_SUBLANES == 0 and all(d % BF16_SUBLANES == 0 and (s - start) % BF16_SUBLANES == 0
                                                   for d, s, _ in pieces)
        plan.append((start, [(d, s - start, n) for d, s, n in pieces]))
    return plan


def _inproj_head_kernel(plan, x_ref, g_ref, w_ref, o_ref, wt_ref, u_ref):
    j = pl.program_id(0)
    pl.when(j == 0)(lambda: _normalize_rows(x_ref, g_ref, u_ref))
    for tile, (_, pieces) in enumerate(plan):
        @pl.when(j == tile)
        def _(pieces=pieces):
            covered = 0
            for dst, src, n in sorted(pieces):
                assert dst == covered
                wt_ref[dst:dst + n, :] = w_ref[src:src + n, :].astype(BF16)
                covered += n
            if covered < PACK_TILE:
                wt_ref[covered:, :] = jnp.zeros((PACK_TILE - covered, wt_ref.shape[1]), BF16)
    _project(u_ref, wt_ref, o_ref)


def _inproj_tail_kernel(x_ref, g_ref, w_ref, _, o_ref, u_ref):
    pl.when(pl.program_id(1) == 0)(lambda: _normalize_rows(x_ref, g_ref, u_ref))
    _project(u_ref, w_ref, o_ref)


def _inproj(x2, g, w_t, tm, tn):
    t, d = x2.shape
    n_native = w_t.shape[0]
    plan = _pack_plan(n_native)
    n = len(plan) * PACK_TILE
    assert n % tn == 0 and t % tm == 0

    def window_start(j):
        units = sum(jnp.where(j == tile, start // BF16_SUBLANES, 0) for tile, (start, _) in enumerate(plan))
        return units * BF16_SUBLANES

    proj, w_p = pl.pallas_call(
        functools.partial(_inproj_head_kernel, plan),
        grid=(len(plan),),
        in_specs=[
            pl.BlockSpec((tm, d), lambda j: (0, 0), pipeline_mode=pl.Buffered(1)),
            _const_spec(g.shape),
            pl.BlockSpec((pl.Element(PACK_WINDOW), pl.Element(d)), lambda j: (window_start(j), 0)),
        ],
        out_specs=[pl.BlockSpec((tm, PACK_TILE), lambda j: (0, j)),
                   pl.BlockSpec((PACK_TILE, d), lambda j: (j, 0))],
        out_shape=[jax.ShapeDtypeStruct((t, n), BF16), jax.ShapeDtypeStruct((n, d), BF16)],
        scratch_shapes=[pltpu.VMEM((tm, d), BF16)],
        compiler_params=_cparams(("arbitrary",)),
        name="inproj_head",
    )(x2, g, w_t)

    return pl.pallas_call(
        _inproj_tail_kernel,
        grid=(t // tm - 1, n // tn),
        in_specs=[
            pl.BlockSpec((tm, d), lambda i, j: (i + 1, 0)),
            pl.BlockSpec((1, d), lambda i, j: (0, 0)),
            pl.BlockSpec((tn, d), lambda i, j: (j, 0)),
            pl.BlockSpec(memory_space=pl.ANY),
        ],
        out_specs=pl.BlockSpec((tm, tn), lambda i, j: (i + 1, j)),
        out_shape=jax.ShapeDtypeStruct((t, n), BF16),
        input_output_aliases={3: 0},
        scratch_shapes=[pltpu.VMEM((tm, d), BF16)],
        compiler_params=_cparams(("parallel", "arbitrary")),
        name="inproj_tail",
    )(x2, g, w_p, proj)


def _bucket_starts():
    max_exact = N_BUCKETS // 2
    d = np.arange(WINDOW)
    large = max_exact + (np.log(np.maximum(d, 1).astype(np.float32) / max_exact)
                         / math.log(MAX_DISTANCE / max_exact)
                         * (N_BUCKETS - max_exact)).astype(np.int32)
    bucket = np.where(d < max_exact, d, np.minimum(large, N_BUCKETS - 1))
    starts = []
    for b in range(N_BUCKETS):
        hit = np.nonzero(bucket == b)[0]
        if hit.size:
            assert np.all(np.diff(hit) == 1)
            starts.append((b, int(hit[0])))
    return starts


HEADS_PER_TILE = LANES // A_HEAD_DIM
SWA_BLOCKS_PER_STEP = 4


def _swa_body(q_ref, kp_ref, kc_ref, vp_ref, vc_ref, sink_ref, rb_ref, o_ref, bias_ref):
    n_keys = 2 * A_BLOCK

    @pl.when((pl.program_id(0) == 0) & (pl.program_id(1) == 0))
    def _():
        row = lax.broadcasted_iota(jnp.int32, (A_BLOCK, n_keys), 0)
        col = lax.broadcasted_iota(jnp.int32, (A_BLOCK, n_keys), 1)
        dist = row + A_BLOCK - col
        band = (dist >= 0) & (dist < WINDOW)
        starts = _bucket_starts()
        for h in range(A_HEADS):
            val = jnp.full(dist.shape, rb_ref[starts[0][0], h], F32)
            for b, s in starts[1:]:
                val = jnp.where(dist >= s, rb_ref[b, h], val)
            val = jnp.where(band, val, NEG_INF)
            sink = sink_ref[0, h]
            bias_ref[0, h] = jnp.where(col == 0, sink, val)
            bias_ref[1, h] = jnp.where(col == 0, sink, jnp.where(col >= A_BLOCK, val, NEG_INF))

    lane = lax.broadcasted_iota(jnp.int32, (1, LANES), 1)
    scale = A_HEAD_DIM ** -0.5
    q_keep = (jnp.where(lane < A_HEAD_DIM, scale, 0.0).astype(BF16),
              jnp.where(lane < A_HEAD_DIM, 0.0, scale).astype(BF16))
    lower_lanes = lax.broadcasted_iota(jnp.int32, (A_BLOCK, LANES), 1) < A_HEAD_DIM
    key0 = lax.broadcasted_iota(jnp.int32, (n_keys, LANES), 0) == 0
    ones = jnp.ones((n_keys, LANES), BF16)

    def attend(q_rows, prev, cur, first):
        def both_blocks(which, tile):
            cols = slice(tile * LANES, (tile + 1) * LANES)
            cat = jnp.concatenate([prev[which][:, cols], cur[which][:, cols]], axis=0).astype(F32)
            cat = jnp.where(key0, 0.0, cat)
            return cat.astype(BF16), pltpu.roll(cat, A_HEAD_DIM, 1).astype(BF16)

        stacks = []
        for tile in range(A_KV_WIDTH // LANES):
            k_cat, k_swp = both_blocks(0, tile)
            v_cat, v_swp = both_blocks(1, tile)
            q0 = tile * A_GROUP
            stacks.append((k_cat, jnp.concatenate([v_cat, ones], axis=1),
                           [(q0, 0), (q0 + 1, 0), (q0 + 2, 1), (q0 + 3, 1)]))
            stacks.append((k_swp, jnp.concatenate([v_swp, ones], axis=1),
                           [(q0, 1), (q0 + 1, 1), (q0 + 2, 0), (q0 + 3, 0)]))

        scores = []
        for k_tile, _, members in stacks:
            q4 = jnp.concatenate(
                [q_ref[q_rows, qt * LANES:(qt + 1) * LANES] * q_keep[half] for qt, half in members],
                axis=0)
            scores.append(lax.dot_general(q4, k_tile, (((1,), (1,)), ((), ())),
                                          preferred_element_type=F32))
        s = jnp.concatenate(scores, axis=0)
        s = s + jnp.concatenate(
            [bias_ref[first, qt * HEADS_PER_TILE + half]
             for _, _, members in stacks for qt, half in members], axis=0)
        p = jnp.exp(s - jnp.max(s, axis=-1, keepdims=True)).astype(BF16)

        normed = {}
        rows_per_stack = len(stacks[0][2]) * A_BLOCK
        for i, (_, v_ones, members) in enumerate(stacks):
            ov = jnp.dot(p[i * rows_per_stack:(i + 1) * rows_per_stack], v_ones,
                         preferred_element_type=F32)
            o = ov[:, :LANES] / ov[:, LANES:]
            for j, member in enumerate(members):
                normed[member] = o[j * A_BLOCK:(j + 1) * A_BLOCK]
        for qt in range(A_Q_WIDTH // LANES):
            o_ref[q_rows, qt * LANES:(qt + 1) * LANES] = jnp.where(
                lower_lanes, normed[(qt, 0)], normed[(qt, 1)]).astype(o_ref.dtype)

    blocks_per_step = q_ref.shape[0] // A_BLOCK
    kv_prev = (kp_ref[...], vp_ref[...])
    for sub in range(blocks_per_step):
        rows = slice(sub * A_BLOCK, (sub + 1) * A_BLOCK)
        kv_cur = (kc_ref[rows, :], vc_ref[rows, :])
        first = (pl.program_id(1) == 0).astype(jnp.int32) if sub == 0 else 0
        attend(rows, kv_prev, kv_cur, first)
        kv_prev = kv_cur


GLA_CHUNKS_PER_STEP = 8
GLA_PREFIX_GROUP = 1


def _split3(x):
    hi = x.astype(BF16)
    r1 = x - hi.astype(F32)
    mid = r1.astype(BF16)
    lo = (r1 - mid.astype(F32)).astype(BF16)
    return jnp.concatenate([hi, mid, lo], axis=0)


def _gla_body(q_ref, k_ref, v0_ref, v1_ref, gl_ref, og0_ref, og1_ref, wgu_ref, bg_ref, ng_ref,
              o_ref, s_ref):
    c = B_CHUNK
    n_chunks = GLA_CHUNKS_PER_STEP
    n_rows = n_chunks * c
    heads_per_half = HALF_V // B_VAL_DIM
    v_refs = (v0_ref, v1_ref)
    og_refs = (og0_ref, og1_ref)
    chunk_rows = [slice(j * c, (j + 1) * c) for j in range(n_chunks)]
    key_cols = [slice(h * B_KEY_DIM, (h + 1) * B_KEY_DIM) for h in range(B_HEADS)]
    units = [(j, h) for j in range(n_chunks) for h in range(B_HEADS)]

    def v_of(refs, j, h):
        lo = (h % heads_per_half) * B_VAL_DIM
        rows = slice(None) if j is None else chunk_rows[j]
        return refs[h // heads_per_half][rows, lo:lo + B_VAL_DIM]

    @pl.when(pl.program_id(1) == 0)
    def _():
        s_ref[...] = jnp.zeros_like(s_ref)

    glin = jnp.dot(gl_ref[...], wgu_ref[...], preferred_element_type=F32) + bg_ref[...]
    log_a = (jnp.minimum(glin, 0.0) - jnp.log(1.0 + jnp.exp(-jnp.abs(glin)))) / B_GATE_TAU

    g_rows = GLA_PREFIX_GROUP * c
    ri = lax.broadcasted_iota(jnp.int32, (g_rows, 3 * g_rows), 0)
    ci = lax.broadcasted_iota(jnp.int32, (g_rows, 3 * g_rows), 1)
    ci = ci - jnp.where(ci >= g_rows, g_rows, 0) - jnp.where(ci >= 2 * g_rows, g_rows, 0)
    shift = int(math.log2(c))
    same_chunk = lax.shift_right_logical(ri, shift) == lax.shift_right_logical(ci, shift)
    tri3 = ((ri >= ci) & same_chunk).astype(BF16)
    b = jnp.concatenate(
        [jnp.dot(tri3, _split3(log_a[r0:r0 + g_rows]), preferred_element_type=F32)
         for r0 in range(0, n_rows, g_rows)], axis=0)
    last_rows = [b[(j + 1) * c - 1:(j + 1) * c, :] for j in range(n_chunks)]
    b_last = jnp.concatenate([jnp.broadcast_to(r, (c, b.shape[1])) for r in last_rows], axis=0)

    qf = q_ref[...].astype(F32) * (B_KEY_DIM ** -0.5)
    kf = k_ref[...].astype(F32)
    q_dec = (qf * jnp.exp(b)).astype(BF16)
    k_dec = (kf * jnp.exp(-b)).astype(BF16)
    k_state = kf * jnp.exp(b_last - b)
    sublanes = 8
    pad = [jnp.zeros((sublanes - n_chunks, b.shape[1]), F32)] if n_chunks < sublanes else []
    decay_rows = jnp.exp(jnp.concatenate(last_rows + pad, axis=0))

    ri = lax.broadcasted_iota(jnp.int32, (c, c), 0)
    ci = lax.broadcasted_iota(jnp.int32, (c, c), 1)
    causal = ri >= ci
    att = {}
    for j, h in units:
        a = lax.dot_general(q_dec[chunk_rows[j], key_cols[h]], k_dec[chunk_rows[j], key_cols[h]],
                            (((1,), (1,)), ((), ())), preferred_element_type=F32)
        att[j, h] = jnp.where(causal, a, 0.0).astype(BF16)
    o_intra = {u: jnp.dot(att[u], v_of(v_refs, *u), preferred_element_type=F32) for u in units}
    ds = {(j, h): jnp.dot(k_state[chunk_rows[j], key_cols[h]].T.astype(BF16), v_of(v_refs, j, h),
                          preferred_element_type=F32) for j, h in units}

    entering = {}
    for h in range(B_HEADS):
        decay_t = decay_rows[:, key_cols[h]].T
        state = s_ref[h]
        for j in range(n_chunks):
            entering[j, h] = state.astype(BF16)
            state = decay_t[:, j:j + 1] * state + ds[j, h]
        s_ref[h] = state
    o_inter = {(j, h): jnp.dot(q_dec[chunk_rows[j], key_cols[h]], entering[j, h],
                               preferred_element_type=F32) for j, h in units}

    ng = ng_ref[...]
    for h in range(B_HEADS):
        o = jnp.concatenate([o_intra[j, h] + o_inter[j, h] for j in range(n_chunks)], axis=0)
        gate = v_of(og_refs, None, h).astype(F32)
        y = _rmsnorm_rows(o, ng) * (gate * jax.nn.sigmoid(gate))
        o_ref[:, h * B_VAL_DIM:(h + 1) * B_VAL_DIM] = y.astype(o_ref.dtype)


N_MIX_IN = 7


def _mixers_kernel(n_cast, *refs):
    mix_ref, prev_ref, sink_ref, rb_ref, wgu_ref, bg_ref, ng_ref = refs[:N_MIX_IN]
    cast_in, refs = refs[N_MIX_IN:N_MIX_IN + n_cast], refs[N_MIX_IN + n_cast:]
    attn_ref, gla_ref = refs[:2]
    cast_out, (bias_ref, s_ref) = refs[2:2 + n_cast], refs[2 + n_cast:]
    _cast_blocks(cast_in, cast_out)

    def cols(off, width):
        return mix_ref.at[:, off - OFF_QA:off - OFF_QA + width]

    kv_prev = [prev_ref.at[:, off - OFF_KA:off - OFF_KA + A_KV_WIDTH] for off in (OFF_KA, OFF_VA)]
    _swa_body(cols(OFF_QA, A_Q_WIDTH), kv_prev[0], cols(OFF_KA, A_KV_WIDTH),
              kv_prev[1], cols(OFF_VA, A_KV_WIDTH), sink_ref, rb_ref, attn_ref, bias_ref)
    _gla_body(cols(OFF_QB, B_QK_WIDTH), cols(OFF_KB, B_QK_WIDTH),
              cols(OFF_VB, HALF_V), cols(OFF_VB + HALF_V, HALF_V), cols(OFF_GLOW, LANES),
              cols(OFF_OBG, HALF_V), cols(OFF_OBG + HALF_V, HALF_V),
              wgu_ref, bg_ref, ng_ref, gla_ref, s_ref)


def _mixers(proj, sinks, rel_bias, wgu, bg, ng, cast_weights, batch, seq):
    t = proj.shape[0]
    rows = GLA_CHUNKS_PER_STEP * B_CHUNK
    assert rows == SWA_BLOCKS_PER_STEP * A_BLOCK and seq % rows == 0
    steps = seq // rows
    rb = lambda b, s: b * steps + s
    prev = lambda b, s: jnp.maximum(rb(b, s) * SWA_BLOCKS_PER_STEP - 1, 0)
    smem = functools.partial(pl.BlockSpec, memory_space=pltpu.SMEM)
    cast_in, cast_out, cast_shapes = _cast_specs(cast_weights, batch * steps, rb)
    kv_width = OFF_QB - OFF_KA
    assert OFF_KA % kv_width == 0 and OFF_VA == OFF_KA + A_KV_WIDTH
    mix_specs = [
        pl.BlockSpec((pl.Element(rows), pl.Element(PROJ_USED - OFF_QA)),
                     lambda b, s: (rb(b, s) * rows, OFF_QA)),
        pl.BlockSpec((A_BLOCK, kv_width), lambda b, s: (prev(b, s), OFF_KA // kv_width)),
        smem(), smem(), _const_spec(wgu.shape), _const_spec(bg.shape), _const_spec(ng.shape)]
    assert len(mix_specs) == N_MIX_IN
    out_block = lambda width: pl.BlockSpec((rows, width), lambda b, s: (rb(b, s), 0))
    outs = pl.pallas_call(
        functools.partial(_mixers_kernel, len(cast_weights)),
        grid=(batch, steps),
        in_specs=mix_specs + cast_in,
        out_specs=[out_block(A_Q_WIDTH), out_block(B_V_WIDTH)] + cast_out,
        out_shape=[jax.ShapeDtypeStruct((t, A_Q_WIDTH), BF16),
                   jax.ShapeDtypeStruct((t, B_V_WIDTH), BF16)] + cast_shapes,
        scratch_shapes=[pltpu.VMEM((2, A_HEADS, A_BLOCK, 2 * A_BLOCK), F32),
                        pltpu.VMEM((B_HEADS, B_KEY_DIM, B_VAL_DIM), F32)],
        compiler_params=_cparams(("arbitrary", "arbitrary")),
        name="mixers",
    )(proj, proj, sinks, rel_bias, wgu, bg, ng, *cast_weights)
    return outs[0], outs[1], outs[2:]


def _merge_kernel(a_ref, b_ref, gates_ref, x_ref, wa_ref, wb_ref, wo_ref, gz_ref, h_ref, z_ref):
    d = x_ref.shape[1]
    ya = jnp.dot(a_ref[...], wa_ref[...], preferred_element_type=F32)
    yb = jnp.dot(b_ref[...], wb_ref[...], preferred_element_type=F32)
    merged = (jax.nn.sigmoid(gates_ref[:, :d].astype(F32)) * ya
              + jax.nn.sigmoid(gates_ref[:, d:].astype(F32)) * yb)
    h = x_ref[...] + jnp.dot(merged.astype(BF16), wo_ref[...], preferred_element_type=F32)
    h_ref[...] = h
    z_ref[...] = _rmsnorm_rows(h, gz_ref[...]).astype(z_ref.dtype)


def _merge(attn, gla, proj, x2, wa, wb, wo, gz, tm):
    t, d = x2.shape
    row_block = pl.BlockSpec((tm, d), lambda i: (i, 0))
    return pl.pallas_call(
        _merge_kernel,
        grid=(t // tm,),
        in_specs=[
            pl.BlockSpec((tm, A_Q_WIDTH), lambda i: (i, 0)),
            pl.BlockSpec((tm, B_V_WIDTH), lambda i: (i, 0)),
            pl.BlockSpec((tm, 2 * d), lambda i: (i, OFF_GATE_A // (2 * d))),
            row_block,
            _const_spec(wa.shape), _const_spec(wb.shape), _const_spec(wo.shape), _const_spec(gz.shape),
        ],
        out_specs=[row_block, row_block],
        out_shape=[jax.ShapeDtypeStruct((t, d), F32), jax.ShapeDtypeStruct((t, d), BF16)],
        compiler_params=_cparams(("parallel",)),
        name="merge",
    )(attn, gla, proj, x2, wa, wb, wo, gz)


def _ffn_kernel(z_ref, h_ref, wg_ref, wu_ref, wd_ref, gf_ref, o_ref):
    f = pl.program_id(1)
    z = z_ref[...]
    g = jnp.dot(z, wg_ref[...], preferred_element_type=F32)
    u = jnp.dot(z, wu_ref[...], preferred_element_type=F32)
    act = (g * jax.nn.sigmoid(g) * u).astype(BF16)
    acc = jnp.where(f == 0, 0.0, o_ref[...])
    o_ref[...] = acc + jnp.dot(act, wd_ref[...], preferred_element_type=F32)

    @pl.when(f == pl.num_programs(1) - 1)
    def _():
        gf = gf_ref[...]

        def body(c, carry):
            rows = pl.ds(pl.multiple_of(c * NORM_ROWS, NORM_ROWS), NORM_ROWS)
            o_ref[rows, :] = _rmsnorm_rows(h_ref[rows, :] + o_ref[rows, :], gf)
            return carry

        lax.fori_loop(0, h_ref.shape[0] // NORM_ROWS, body, 0)


def _ffn(z, h, wg, wu, wd, gf, tm, tf):
    t, d = h.shape
    f = wg.shape[1]
    row_block = pl.BlockSpec((tm, d), lambda i, j: (i, 0))
    return pl.pallas_call(
        _ffn_kernel,
        grid=(t // tm, f // tf),
        in_specs=[
            row_block, row_block,
            pl.BlockSpec((d, tf), lambda i, j: (0, j)),
            pl.BlockSpec((d, tf), lambda i, j: (0, j)),
            pl.BlockSpec((tf, d), lambda i, j: (j, 0)),
            pl.BlockSpec((1, d), lambda i, j: (0, 0)),
        ],
        out_specs=row_block,
        out_shape=jax.ShapeDtypeStruct((t, d), F32),
        compiler_params=_cparams(("parallel", "arbitrary")),
        name="ffn",
    )(z, h, wg, wu, wd, gf)


def kernel(x, norm_mix_g, w_in, sinks, rel_bias, w_gate_up, b_gate, gla_norm_g, w_proj_a, w_proj_b,
           w_out, norm_ffn_g, w_ffn_gate, w_ffn_up, w_ffn_down, norm_final_g):
    batch, seq, d = x.shape
    assert d == D_MODEL and w_in.shape[0] == 1, "single-layer geometry"
    t = batch * seq
    x2 = x.reshape(t, d)

    proj = _inproj(x2, norm_mix_g, w_in[0].T, tm=1024, tn=2304)

    wgu = jnp.zeros((LANES, B_QK_WIDTH), BF16).at[:B_GATE_RANK].set(w_gate_up[0].astype(BF16))
    later_weights = (w_proj_a[0], w_proj_b[0], w_out[0], w_ffn_gate[0], w_ffn_up[0], w_ffn_down[0])
    attn, gla, (wa, wb, wo, wg, wu, wd) = _mixers(proj, sinks, rel_bias, wgu, b_gate, gla_norm_g,
                                                  later_weights, batch, seq)

    h, z = _merge(attn, gla, proj, x2, wa, wb, wo, norm_ffn_g, tm=512)

    out = _ffn(z, h, wg, wu, wd, norm_final_g.reshape(1, d), tm=1024, tf=512)
    return out.reshape(batch, seq, d)
```

```python
import functools
import math

import numpy as np
import jax
import jax.numpy as jnp
from jax import lax
from jax.experimental import pallas as pl
from jax.experimental.pallas import tpu as pltpu

F32 = jnp.float32
BF16 = jnp.bfloat16

D_MODEL = 2048
A_HEADS = 16
A_KV_HEADS = 4
A_HEAD_DIM = 64
A_GROUP = A_HEADS // A_KV_HEADS
WINDOW = 128
A_BLOCK = 128
A_Q_WIDTH = A_HEADS * A_HEAD_DIM
A_KV_WIDTH = A_KV_HEADS * A_HEAD_DIM
N_BUCKETS = 32
MAX_DISTANCE = 128
B_HEADS = 4
B_KEY_DIM = 128
B_VAL_DIM = 256
B_QK_WIDTH = B_HEADS * B_KEY_DIM
B_V_WIDTH = B_HEADS * B_VAL_DIM
B_GATE_RANK = 16
B_GATE_TAU = 16.0
B_CHUNK = 64
EPS = 1e-6
NEG_INF = -1e30

LANES = 128

OFF_GATE_A = 0
OFF_GATE_B = OFF_GATE_A + D_MODEL
OFF_QA = OFF_GATE_B + D_MODEL
OFF_KA = OFF_QA + A_Q_WIDTH
OFF_VA = OFF_KA + A_KV_WIDTH
OFF_QB = OFF_VA + A_KV_WIDTH
OFF_KB = OFF_QB + B_QK_WIDTH
OFF_VB = OFF_KB + B_QK_WIDTH
OFF_OBG = OFF_VB + B_V_WIDTH
OFF_GLOW = OFF_OBG + B_V_WIDTH
PROJ_USED = OFF_GLOW + LANES
HALF_V = B_V_WIDTH // 2

VMEM_LIMIT = 60 * 1024 * 1024


def _cparams(sem):
    return pltpu.CompilerParams(dimension_semantics=sem, vmem_limit_bytes=VMEM_LIMIT)


def _const_spec(shape):
    return pl.BlockSpec(shape, lambda *_: (0,) * len(shape), pipeline_mode=pl.Buffered(1))


def _rmsnorm_rows(x, g):
    ms = jnp.mean(x * x, axis=-1, keepdims=True)
    return x * lax.rsqrt(ms + EPS) * g


NORM_ROWS = 128


BF16_SUBLANES = 16


def _cast_specs(weights, n_chunks, chunk_of):
    in_specs, out_specs, shapes = [], [], []
    for w in weights:
        rows, rem = divmod(w.shape[0], n_chunks)
        assert rem == 0 and rows % BF16_SUBLANES == 0, (w.shape, n_chunks)
        for specs in (in_specs, out_specs):
            specs.append(pl.BlockSpec((rows, w.shape[1]), lambda *idx: (chunk_of(*idx), 0)))
        shapes.append(jax.ShapeDtypeStruct(w.shape, BF16))
    return in_specs, out_specs, shapes


def _cast_blocks(in_refs, out_refs):
    for src, dst in zip(in_refs, out_refs):
        dst[...] = src[...].astype(dst.dtype)


def _normalize_rows(x_ref, g_ref, u_ref):
    g = g_ref[...]

    def body(c, carry):
        rows = pl.ds(pl.multiple_of(c * NORM_ROWS, NORM_ROWS), NORM_ROWS)
        u_ref[rows, :] = _rmsnorm_rows(x_ref[rows, :], g).astype(BF16)
        return carry

    lax.fori_loop(0, x_ref.shape[0] // NORM_ROWS, body, 0)


def _project(u_ref, w_ref, o_ref):
    o_ref[...] = lax.dot_general(u_ref[...], w_ref[...], (((1,), (1,)), ((), ())),
                                 preferred_element_type=F32).astype(o_ref.dtype)


_N_MIX = OFF_OBG - OFF_QA
_FEATURE_RUNS = (
    (OFF_GATE_A, _N_MIX + B_GATE_RANK + B_V_WIDTH, 2 * D_MODEL),
    (OFF_QA, 0, _N_MIX),
    (OFF_OBG, _N_MIX + B_GATE_RANK, B_V_WIDTH),
    (OFF_GLOW, _N_MIX, B_GATE_RANK),
)
PACK_TILE = 512
PACK_WINDOW = PACK_TILE + B_GATE_RANK


def _pack_plan(n_native):
    plan = []
    for tile in range(pl.cdiv(PROJ_USED, PACK_TILE)):
        lo, hi = tile * PACK_TILE, (tile + 1) * PACK_TILE
        pieces = []
        for dst, src, n in _FEATURE_RUNS:
            a, b = max(lo, dst), min(hi, dst + n)
            if a < b:
                pieces.append((a - lo, src + a - dst, b - a))
        start = min(min(p[1] for p in pieces), n_native - PACK_WINDOW)
        assert all(start <= s and s + n <= start + PACK_WINDOW for _, s, n in pieces), (tile, pieces)
        assert start % BF16_SUBLANES == 0 and all(d % BF16_SUBLANES == 0 and (s - start) % BF16_SUBLANES == 0
                                                   for d, s, _ in pieces)
        plan.append((start, [(d, s - start, n) for d, s, n in pieces]))
    return plan


def _inproj_head_kernel(plan, x_ref, g_ref, w_ref, o_ref, wt_ref, u_ref):
    j = pl.program_id(0)
    pl.when(j == 0)(lambda: _normalize_rows(x_ref, g_ref, u_ref))
    for tile, (_, pieces) in enumerate(plan):
        @pl.when(j == tile)
        def _(pieces=pieces):
            covered = 0
            for dst, src, n in sorted(pieces):
                assert dst == covered
                wt_ref[dst:dst + n, :] = w_ref[src:src + n, :].astype(BF16)
                covered += n
            if covered < PACK_TILE:
                wt_ref[covered:, :] = jnp.zeros((PACK_TILE - covered, wt_ref.shape[1]), BF16)
    _project(u_ref, wt_ref, o_ref)


def _inproj_tail_kernel(x_ref, g_ref, w_ref, _, o_ref, u_ref):
    pl.when(pl.program_id(1) == 0)(lambda: _normalize_rows(x_ref, g_ref, u_ref))
    _project(u_ref, w_ref, o_ref)


def _inproj(x2, g, w_t, head_rows, tm, tn):
    t, d = x2.shape
    n_native = w_t.shape[0]
    plan = _pack_plan(n_native)
    n = len(plan) * PACK_TILE
    head_tiles = head_rows // tm
    assert n % tn == 0 and t % tm == 0 and head_rows % tm == 0

    def window_start(j):
        units = sum(jnp.where(j == tile, start // BF16_SUBLANES, 0) for tile, (start, _) in enumerate(plan))
        return units * BF16_SUBLANES

    proj, w_p = pl.pallas_call(
        functools.partial(_inproj_head_kernel, plan),
        grid=(len(plan),),
        in_specs=[
            pl.BlockSpec((head_rows, d), lambda j: (0, 0), pipeline_mode=pl.Buffered(1)),
            _const_spec(g.shape),
            pl.BlockSpec((pl.Element(PACK_WINDOW), pl.Element(d)), lambda j: (window_start(j), 0)),
        ],
        out_specs=[pl.BlockSpec((head_rows, PACK_TILE), lambda j: (0, j)),
                   pl.BlockSpec((PACK_TILE, d), lambda j: (j, 0))],
        out_shape=[jax.ShapeDtypeStruct((t, n), BF16), jax.ShapeDtypeStruct((n, d), BF16)],
        scratch_shapes=[pltpu.VMEM((head_rows, d), BF16)],
        compiler_params=_cparams(("arbitrary",)),
        name="inproj_head",
    )(x2, g, w_t)

    return pl.pallas_call(
        _inproj_tail_kernel,
        grid=(t // tm - head_tiles, n // tn),
        in_specs=[
            pl.BlockSpec((tm, d), lambda i, j: (i + head_tiles, 0)),
            pl.BlockSpec((1, d), lambda i, j: (0, 0)),
            pl.BlockSpec((tn, d), lambda i, j: (j, 0)),
            pl.BlockSpec(memory_space=pl.ANY),
        ],
        out_specs=pl.BlockSpec((tm, tn), lambda i, j: (i + head_tiles, j)),
        out_shape=jax.ShapeDtypeStruct((t, n), BF16),
        input_output_aliases={3: 0},
        scratch_shapes=[pltpu.VMEM((tm, d), BF16)],
        compiler_params=_cparams(("parallel", "arbitrary")),
        name="inproj_tail",
    )(x2, g, w_p, proj)


def _bucket_starts():
    max_exact = N_BUCKETS // 2
    d = np.arange(WINDOW)
    large = max_exact + (np.log(np.maximum(d, 1).astype(np.float32) / max_exact)
                         / math.log(MAX_DISTANCE / max_exact)
                         * (N_BUCKETS - max_exact)).astype(np.int32)
    bucket = np.where(d < max_exact, d, np.minimum(large, N_BUCKETS - 1))
    starts = []
    for b in range(N_BUCKETS):
        hit = np.nonzero(bucket == b)[0]
        if hit.size:
            assert np.all(np.diff(hit) == 1)
            starts.append((b, int(hit[0])))
    return starts


HEADS_PER_TILE = LANES // A_HEAD_DIM
SWA_BLOCKS_PER_STEP = 4


def _swa_body(q_ref, kp_ref, kc_ref, vp_ref, vc_ref, sink_ref, rb_ref, o_ref, bias_ref):
    n_keys = 2 * A_BLOCK

    @pl.when((pl.program_id(0) == 0) & (pl.program_id(1) == 0))
    def _():
        row = lax.broadcasted_iota(jnp.int32, (A_BLOCK, n_keys), 0)
        col = lax.broadcasted_iota(jnp.int32, (A_BLOCK, n_keys), 1)
        dist = row + A_BLOCK - col
        band = (dist >= 0) & (dist < WINDOW)
        starts = _bucket_starts()
        for h in range(A_HEADS):
            val = jnp.full(dist.shape, rb_ref[starts[0][0], h], F32)
            for b, s in starts[1:]:
                val = jnp.where(dist >= s, rb_ref[b, h], val)
            val = jnp.where(band, val, NEG_INF)
            sink = sink_ref[0, h]
            bias_ref[0, h] = jnp.where(col == 0, sink, val)
            bias_ref[1, h] = jnp.where(col == 0, sink, jnp.where(col >= A_BLOCK, val, NEG_INF))

    lane = lax.broadcasted_iota(jnp.int32, (1, LANES), 1)
    scale = A_HEAD_DIM ** -0.5
    q_keep = (jnp.where(lane < A_HEAD_DIM, scale, 0.0).astype(BF16),
              jnp.where(lane < A_HEAD_DIM, 0.0, scale).astype(BF16))
    lower_lanes = lax.broadcasted_iota(jnp.int32, (A_BLOCK, LANES), 1) < A_HEAD_DIM
    key0 = lax.broadcasted_iota(jnp.int32, (n_keys, LANES), 0) == 0
    ones = jnp.ones((n_keys, LANES), BF16)

    def attend(q_rows, prev, cur, first):
        def both_blocks(which, tile):
            cols = slice(tile * LANES, (tile + 1) * LANES)
            cat = jnp.concatenate([prev[which][:, cols], cur[which][:, cols]], axis=0).astype(F32)
            cat = jnp.where(key0, 0.0, cat)
            return cat.astype(BF16), pltpu.roll(cat, A_HEAD_DIM, 1).astype(BF16)

        stacks = []
        for tile in range(A_KV_WIDTH // LANES):
            k_cat, k_swp = both_blocks(0, tile)
            v_cat, v_swp = both_blocks(1, tile)
            q0 = tile * A_GROUP
            stacks.append((k_cat, jnp.concatenate([v_cat, ones], axis=1),
                           [(q0, 0), (q0 + 1, 0), (q0 + 2, 1), (q0 + 3, 1)]))
            stacks.append((k_swp, jnp.concatenate([v_swp, ones], axis=1),
                           [(q0, 1), (q0 + 1, 1), (q0 + 2, 0), (q0 + 3, 0)]))

        scores = []
        for k_tile, _, members in stacks:
            q4 = jnp.concatenate(
                [q_ref[q_rows, qt * LANES:(qt + 1) * LANES] * q_keep[half] for qt, half in members],
                axis=0)
            scores.append(lax.dot_general(q4, k_tile, (((1,), (1,)), ((), ())),
                                          preferred_element_type=F32))
        s = jnp.concatenate(scores, axis=0)
        s = s + jnp.concatenate(
            [bias_ref[first, qt * HEADS_PER_TILE + half]
             for _, _, members in stacks for qt, half in members], axis=0)
        p = jnp.exp(s - jnp.max(s, axis=-1, keepdims=True)).astype(BF16)

        normed = {}
        rows_per_stack = len(stacks[0][2]) * A_BLOCK
        for i, (_, v_ones, members) in enumerate(stacks):
            ov = jnp.dot(p[i * rows_per_stack:(i + 1) * rows_per_stack], v_ones,
                         preferred_element_type=F32)
            o = ov[:, :LANES] / ov[:, LANES:]
            for j, member in enumerate(members):
                normed[member] = o[j * A_BLOCK:(j + 1) * A_BLOCK]
        for qt in range(A_Q_WIDTH // LANES):
            o_ref[q_rows, qt * LANES:(qt + 1) * LANES] = jnp.where(
                lower_lanes, normed[(qt, 0)], normed[(qt, 1)]).astype(o_ref.dtype)

    blocks_per_step = q_ref.shape[0] // A_BLOCK
    kv_prev = (kp_ref[...], vp_ref[...])
    for sub in range(blocks_per_step):
        rows = slice(sub * A_BLOCK, (sub + 1) * A_BLOCK)
        kv_cur = (kc_ref[rows, :], vc_ref[rows, :])
        first = (pl.program_id(1) == 0).astype(jnp.int32) if sub == 0 else 0
        attend(rows, kv_prev, kv_cur, first)
        kv_prev = kv_cur


GLA_CHUNKS_PER_STEP = 8
GLA_PREFIX_GROUP = 1


def _split3(x):
    hi = x.astype(BF16)
    r1 = x - hi.astype(F32)
    mid = r1.astype(BF16)
    lo = (r1 - mid.astype(F32)).astype(BF16)
    return jnp.concatenate([hi, mid, lo], axis=0)


def _gla_body(q_ref, k_ref, v0_ref, v1_ref, gl_ref, og0_ref, og1_ref, wgu_ref, bg_ref, ng_ref,
              o_ref, s_ref):
    c = B_CHUNK
    n_chunks = GLA_CHUNKS_PER_STEP
    n_rows = n_chunks * c
    heads_per_half = HALF_V // B_VAL_DIM
    v_refs = (v0_ref, v1_ref)
    og_refs = (og0_ref, og1_ref)
    chunk_rows = [slice(j * c, (j + 1) * c) for j in range(n_chunks)]
    key_cols = [slice(h * B_KEY_DIM, (h + 1) * B_KEY_DIM) for h in range(B_HEADS)]
    units = [(j, h) for j in range(n_chunks) for h in range(B_HEADS)]

    def v_of(refs, j, h):
        lo = (h % heads_per_half) * B_VAL_DIM
        rows = slice(None) if j is None else chunk_rows[j]
        return refs[h // heads_per_half][rows, lo:lo + B_VAL_DIM]

    @pl.when(pl.program_id(1) == 0)
    def _():
        s_ref[...] = jnp.zeros_like(s_ref)

    glin = jnp.dot(gl_ref[...], wgu_ref[...], preferred_element_type=F32) + bg_ref[...]
    log_a = (jnp.minimum(glin, 0.0) - jnp.log(1.0 + jnp.exp(-jnp.abs(glin)))) / B_GATE_TAU

    g_rows = GLA_PREFIX_GROUP * c
    ri = lax.broadcasted_iota(jnp.int32, (g_rows, 3 * g_rows), 0)
    ci = lax.broadcasted_iota(jnp.int32, (g_rows, 3 * g_rows), 1)
    ci = ci - jnp.where(ci >= g_rows, g_rows, 0) - jnp.where(ci >= 2 * g_rows, g_rows, 0)
    shift = int(math.log2(c))
    same_chunk = lax.shift_right_logical(ri, shift) == lax.shift_right_logical(ci, shift)
    tri3 = ((ri >= ci) & same_chunk).astype(BF16)
    b = jnp.concatenate(
        [jnp.dot(tri3, _split3(log_a[r0:r0 + g_rows]), preferred_element_type=F32)
         for r0 in range(0, n_rows, g_rows)], axis=0)
    last_rows = [b[(j + 1) * c - 1:(j + 1) * c, :] for j in range(n_chunks)]
    b_last = jnp.concatenate([jnp.broadcast_to(r, (c, b.shape[1])) for r in last_rows], axis=0)

    qf = q_ref[...].astype(F32) * (B_KEY_DIM ** -0.5)
    kf = k_ref[...].astype(F32)
    q_dec = (qf * jnp.exp(b)).astype(BF16)
    k_dec = (kf * jnp.exp(-b)).astype(BF16)
    k_state = kf * jnp.exp(b_last - b)
    sublanes = 8
    pad = [jnp.zeros((sublanes - n_chunks, b.shape[1]), F32)] if n_chunks < sublanes else []
    decay_rows = jnp.exp(jnp.concatenate(last_rows + pad, axis=0))

    ri = lax.broadcasted_iota(jnp.int32, (c, c), 0)
    ci = lax.broadcasted_iota(jnp.int32, (c, c), 1)
    causal = ri >= ci
    att = {}
    for j, h in units:
        a = lax.dot_general(q_dec[chunk_rows[j], key_cols[h]], k_dec[chunk_rows[j], key_cols[h]],
                            (((1,), (1,)), ((), ())), preferred_element_type=F32)
        att[j, h] = jnp.where(causal, a, 0.0).astype(BF16)
    o_intra = {u: jnp.dot(att[u], v_of(v_refs, *u), preferred_element_type=F32) for u in units}
    ds = {(j, h): jnp.dot(k_state[chunk_rows[j], key_cols[h]].T.astype(BF16), v_of(v_refs, j, h),
                          preferred_element_type=F32) for j, h in units}

    entering = {}
    for h in range(B_HEADS):
        decay_t = decay_rows[:, key_cols[h]].T
        state = s_ref[h]
        for j in range(n_chunks):
            entering[j, h] = state.astype(BF16)
            state = decay_t[:, j:j + 1] * state + ds[j, h]
        s_ref[h] = state
    o_inter = {(j, h): jnp.dot(q_dec[chunk_rows[j], key_cols[h]], entering[j, h],
                               preferred_element_type=F32) for j, h in units}

    ng = ng_ref[...]
    for h in range(B_HEADS):
        o = jnp.concatenate([o_intra[j, h] + o_inter[j, h] for j in range(n_chunks)], axis=0)
        gate = v_of(og_refs, None, h).astype(F32)
        y = _rmsnorm_rows(o, ng) * (gate * jax.nn.sigmoid(gate))
        o_ref[:, h * B_VAL_DIM:(h + 1) * B_VAL_DIM] = y.astype(o_ref.dtype)


N_MIX_IN = 7


def _mixers_kernel(n_cast, *refs):
    mix_ref, prev_ref, sink_ref, rb_ref, wgu_ref, bg_ref, ng_ref = refs[:N_MIX_IN]
    cast_in, refs = refs[N_MIX_IN:N_MIX_IN + n_cast], refs[N_MIX_IN + n_cast:]
    attn_ref, gla_ref = refs[:2]
    cast_out, (bias_ref, s_ref) = refs[2:2 + n_cast], refs[2 + n_cast:]
    _cast_blocks(cast_in, cast_out)

    def cols(off, width):
        return mix_ref.at[:, off - OFF_QA:off - OFF_QA + width]

    kv_prev = [prev_ref.at[:, off - OFF_KA:off - OFF_KA + A_KV_WIDTH] for off in (OFF_KA, OFF_VA)]
    _swa_body(cols(OFF_QA, A_Q_WIDTH), kv_prev[0], cols(OFF_KA, A_KV_WIDTH),
              kv_prev[1], cols(OFF_VA, A_KV_WIDTH), sink_ref, rb_ref, attn_ref, bias_ref)
    _gla_body(cols(OFF_QB, B_QK_WIDTH), cols(OFF_KB, B_QK_WIDTH),
              cols(OFF_VB, HALF_V), cols(OFF_VB + HALF_V, HALF_V), cols(OFF_GLOW, LANES),
              cols(OFF_OBG, HALF_V), cols(OFF_OBG + HALF_V, HALF_V),
              wgu_ref, bg_ref, ng_ref, gla_ref, s_ref)


def _mixers(proj, sinks, rel_bias, wgu, bg, ng, cast_weights, batch, seq):
    t = proj.shape[0]
    rows = GLA_CHUNKS_PER_STEP * B_CHUNK
    assert rows == SWA_BLOCKS_PER_STEP * A_BLOCK and seq % rows == 0
    steps = seq // rows
    rb = lambda b, s: b * steps + s
    prev = lambda b, s: jnp.maximum(rb(b, s) * SWA_BLOCKS_PER_STEP - 1, 0)
    smem = functools.partial(pl.BlockSpec, memory_space=pltpu.SMEM)
    cast_in, cast_out, cast_shapes = _cast_specs(cast_weights, batch * steps, rb)
    kv_width = OFF_QB - OFF_KA
    assert OFF_KA % kv_width == 0 and OFF_VA == OFF_KA + A_KV_WIDTH
    mix_specs = [
        pl.BlockSpec((pl.Element(rows), pl.Element(PROJ_USED - OFF_QA)),
                     lambda b, s: (rb(b, s) * rows, OFF_QA)),
        pl.BlockSpec((A_BLOCK, kv_width), lambda b, s: (prev(b, s), OFF_KA // kv_width)),
        smem(), smem(), _const_spec(wgu.shape), _const_spec(bg.shape), _const_spec(ng.shape)]
    assert len(mix_specs) == N_MIX_IN
    out_block = lambda width: pl.BlockSpec((rows, width), lambda b, s: (rb(b, s), 0))
    outs = pl.pallas_call(
        functools.partial(_mixers_kernel, len(cast_weights)),
        grid=(batch, steps),
        in_specs=mix_specs + cast_in,
        out_specs=[out_block(A_Q_WIDTH), out_block(B_V_WIDTH)] + cast_out,
        out_shape=[jax.ShapeDtypeStruct((t, A_Q_WIDTH), BF16),
                   jax.ShapeDtypeStruct((t, B_V_WIDTH), BF16)] + cast_shapes,
        scratch_shapes=[pltpu.VMEM((2, A_HEADS, A_BLOCK, 2 * A_BLOCK), F32),
                        pltpu.VMEM((B_HEADS, B_KEY_DIM, B_VAL_DIM), F32)],
        compiler_params=_cparams(("arbitrary", "arbitrary")),
        name="mixers",
    )(proj, proj, sinks, rel_bias, wgu, bg, ng, *cast_weights)
    return outs[0], outs[1], outs[2:]


def _merge_kernel(a_ref, b_ref, gates_ref, x_ref, wa_ref, wb_ref, wo_ref, gz_ref, h_ref, z_ref):
    d = x_ref.shape[1]
    ya = jnp.dot(a_ref[...], wa_ref[...], preferred_element_type=F32)
    yb = jnp.dot(b_ref[...], wb_ref[...], preferred_element_type=F32)
    merged = (jax.nn.sigmoid(gates_ref[:, :d].astype(F32)) * ya
              + jax.nn.sigmoid(gates_ref[:, d:].astype(F32)) * yb)
    h = x_ref[...] + jnp.dot(merged.astype(BF16), wo_ref[...], preferred_element_type=F32)
    h_ref[...] = h
    z_ref[...] = _rmsnorm_rows(h, gz_ref[...]).astype(z_ref.dtype)


def _merge(attn, gla, proj, x2, wa, wb, wo, gz, tm):
    t, d = x2.shape
    row_block = pl.BlockSpec((tm, d), lambda i: (i, 0))
    return pl.pallas_call(
        _merge_kernel,
        grid=(t // tm,),
        in_specs=[
            pl.BlockSpec((tm, A_Q_WIDTH), lambda i: (i, 0)),
            pl.BlockSpec((tm, B_V_WIDTH), lambda i: (i, 0)),
            pl.BlockSpec((tm, 2 * d), lambda i: (i, OFF_GATE_A // (2 * d))),
            row_block,
            _const_spec(wa.shape), _const_spec(wb.shape), _const_spec(wo.shape), _const_spec(gz.shape),
        ],
        out_specs=[row_block, row_block],
        out_shape=[jax.ShapeDtypeStruct((t, d), F32), jax.ShapeDtypeStruct((t, d), BF16)],
        compiler_params=_cparams(("parallel",)),
        name="merge",
    )(attn, gla, proj, x2, wa, wb, wo, gz)


def _ffn_kernel(z_ref, h_ref, wg_ref, wu_ref, wd_ref, gf_ref, o_ref):
    f = pl.program_id(1)
    z = z_ref[...]
    half = wg_ref.shape[1] // 2
    for c in range(2):
        cols = slice(c * half, (c + 1) * half)
        g = jnp.dot(z, wg_ref[:, cols], preferred_element_type=F32)
        u = jnp.dot(z, wu_ref[:, cols], preferred_element_type=F32)
        act = (g * jax.nn.sigmoid(g) * u).astype(BF16)
        acc = jnp.where(f == 0, 0.0, o_ref[...]) if c == 0 else o_ref[...]
        o_ref[...] = acc + jnp.dot(act, wd_ref[cols, :], preferred_element_type=F32)

    @pl.when(f == pl.num_programs(1) - 1)
    def _():
        gf = gf_ref[...]

        def body(c, carry):
            rows = pl.ds(pl.multiple_of(c * NORM_ROWS, NORM_ROWS), NORM_ROWS)
            o_ref[rows, :] = _rmsnorm_rows(h_ref[rows, :] + o_ref[rows, :], gf)
            return carry

        lax.fori_loop(0, h_ref.shape[0] // NORM_ROWS, body, 0)


def _ffn(z, h, wg, wu, wd, gf, tm, tf):
    t, d = h.shape
    f = wg.shape[1]
    row_block = pl.BlockSpec((tm, d), lambda i, j: (i, 0))
    return pl.pallas_call(
        _ffn_kernel,
        grid=(t // tm, f // tf),
        in_specs=[
            row_block, row_block,
            pl.BlockSpec((d, tf), lambda i, j: (0, j)),
            pl.BlockSpec((d, tf), lambda i, j: (0, j)),
            pl.BlockSpec((tf, d), lambda i, j: (j, 0)),
            pl.BlockSpec((1, d), lambda i, j: (0, 0)),
        ],
        out_specs=row_block,
        out_shape=jax.ShapeDtypeStruct((t, d), F32),
        compiler_params=_cparams(("parallel", "arbitrary")),
        name="ffn",
    )(z, h, wg, wu, wd, gf)


def kernel(x, norm_mix_g, w_in, sinks, rel_bias, w_gate_up, b_gate, gla_norm_g, w_proj_a, w_proj_b,
           w_out, norm_ffn_g, w_ffn_gate, w_ffn_up, w_ffn_down, norm_final_g):
    batch, seq, d = x.shape
    assert d == D_MODEL and w_in.shape[0] == 1, "single-layer geometry"
    t = batch * seq
    x2 = x.reshape(t, d)

    proj = _inproj(x2, norm_mix_g, w_in[0].T, head_rows=2048, tm=1024, tn=2304)

    wgu = jnp.zeros((LANES, B_QK_WIDTH), BF16).at[:B_GATE_RANK].set(w_gate_up[0].astype(BF16))
    later_weights = (w_proj_a[0], w_proj_b[0], w_out[0], w_ffn_gate[0], w_ffn_up[0], w_ffn_down[0])
    attn, gla, (wa, wb, wo, wg, wu, wd) = _mixers(proj, sinks, rel_bias, wgu, b_gate, gla_norm_g,
                                                  later_weights, batch, seq)

    h, z = _merge(attn, gla, proj, x2, wa, wb, wo, norm_ffn_g, tm=512)

    out = _ffn(z, h, wg, wu, wd, norm_final_g.reshape(1, d), tm=1024, tf=512)
    return out.reshape(batch, seq, d)
```

```python
import functools
import math

import numpy as np
import jax
import jax.numpy as jnp
from jax import lax
from jax.experimental import pallas as pl
from jax.experimental.pallas import tpu as pltpu

F32 = jnp.float32
BF16 = jnp.bfloat16

D_MODEL = 2048
A_HEADS = 16
A_KV_HEADS = 4
A_HEAD_DIM = 64
A_GROUP = A_HEADS // A_KV_HEADS
WINDOW = 128
A_BLOCK = 128
A_Q_WIDTH = A_HEADS * A_HEAD_DIM
A_KV_WIDTH = A_KV_HEADS * A_HEAD_DIM
N_BUCKETS = 32
MAX_DISTANCE = 128
B_HEADS = 4
B_KEY_DIM = 128
B_VAL_DIM = 256
B_QK_WIDTH = B_HEADS * B_KEY_DIM
B_V_WIDTH = B_HEADS * B_VAL_DIM
B_GATE_RANK = 16
B_GATE_TAU = 16.0
B_CHUNK = 64
EPS = 1e-6
NEG_INF = -1e30

LANES = 128
BF16_SUBLANES = 16

OFF_GATE_A = 0
OFF_GATE_B = OFF_GATE_A + D_MODEL
OFF_QA = OFF_GATE_B + D_MODEL
OFF_KA = OFF_QA + A_Q_WIDTH
OFF_VA = OFF_KA + A_KV_WIDTH
OFF_QB = OFF_VA + A_KV_WIDTH
OFF_KB = OFF_QB + B_QK_WIDTH
OFF_VB = OFF_KB + B_QK_WIDTH
OFF_OBG = OFF_VB + B_V_WIDTH
OFF_GLOW = OFF_OBG + B_V_WIDTH
PROJ_USED = OFF_GLOW + LANES
HALF_V = B_V_WIDTH // 2

VMEM_LIMIT = 60 * 1024 * 1024


def _cparams(sem):
    return pltpu.CompilerParams(dimension_semantics=sem, vmem_limit_bytes=VMEM_LIMIT)


def _const_spec(shape):
    return pl.BlockSpec(shape, lambda *_: (0,) * len(shape), pipeline_mode=pl.Buffered(1))


def _rmsnorm_rows(x, g):
    ms = jnp.mean(x * x, axis=-1, keepdims=True)
    return x * lax.rsqrt(ms + EPS) * g


NORM_ROWS = 128


def _cast_specs(weights, n_chunks, chunk_of):
    in_specs, out_specs, shapes = [], [], []
    for w in weights:
        rows, rem = divmod(w.shape[0], n_chunks)
        assert rem == 0 and rows % BF16_SUBLANES == 0, (w.shape, n_chunks)
        for specs in (in_specs, out_specs):
            specs.append(pl.BlockSpec((rows, w.shape[1]), lambda *idx: (chunk_of(*idx), 0)))
        shapes.append(jax.ShapeDtypeStruct(w.shape, BF16))
    return in_specs, out_specs, shapes


def _cast_blocks(in_refs, out_refs):
    for src, dst in zip(in_refs, out_refs):
        dst[...] = src[...].astype(dst.dtype)


def _normalize_rows(x_ref, g_ref, u_ref):
    g = g_ref[...]

    def body(c, carry):
        rows = pl.ds(pl.multiple_of(c * NORM_ROWS, NORM_ROWS), NORM_ROWS)
        u_ref[rows, :] = _rmsnorm_rows(x_ref[rows, :], g).astype(BF16)
        return carry

    lax.fori_loop(0, x_ref.shape[0] // NORM_ROWS, body, 0)


def _project(u_ref, w_ref, o_ref):
    o_ref[...] = lax.dot_general(u_ref[...], w_ref[...], (((1,), (1,)), ((), ())),
                                 preferred_element_type=F32).astype(o_ref.dtype)


_N_MIX = OFF_OBG - OFF_QA
_FEATURE_RUNS = (
    (OFF_GATE_A, _N_MIX + B_GATE_RANK + B_V_WIDTH, 2 * D_MODEL),
    (OFF_QA, 0, _N_MIX),
    (OFF_OBG, _N_MIX + B_GATE_RANK, B_V_WIDTH),
    (OFF_GLOW, _N_MIX, B_GATE_RANK),
)
PACK_TILE = 1024
PACK_WINDOW = PACK_TILE + B_GATE_RANK


def _pack_plan(n_native):
    plan = []
    for tile in range(pl.cdiv(PROJ_USED, PACK_TILE)):
        lo, hi = tile * PACK_TILE, (tile + 1) * PACK_TILE
        pieces = []
        for dst, src, n in _FEATURE_RUNS:
            a, b = max(lo, dst), min(hi, dst + n)
            if a < b:
                pieces.append((a - lo, src + a - dst, b - a))
        start = min(min(p[1] for p in pieces), n_native - PACK_WINDOW)
        assert all(start <= s and s + n <= start + PACK_WINDOW for _, s, n in pieces), (tile, pieces)
        assert start % BF16_SUBLANES == 0 and all(d % BF16_SUBLANES == 0 and (s - start) % BF16_SUBLANES == 0
                                                   for d, s, _ in pieces)
        plan.append((start, [(d, s - start, n) for d, s, n in pieces]))
    return plan


def _inproj_head_kernel(plan, x_ref, g_ref, w_ref, o_ref, wt_ref, u_ref):
    j = pl.program_id(0)
    pl.when(j == 0)(lambda: _normalize_rows(x_ref, g_ref, u_ref))
    for tile, (_, pieces) in enumerate(plan):
        @pl.when(j == tile)
        def _(pieces=pieces):
            covered = 0
            for dst, src, n in sorted(pieces):
                assert dst == covered
                wt_ref[dst:dst + n, :] = w_ref[src:src + n, :].astype(BF16)
                covered += n
            if covered < PACK_TILE:
                wt_ref[covered:, :] = jnp.zeros((PACK_TILE - covered, wt_ref.shape[1]), BF16)
    _project(u_ref, wt_ref, o_ref)


def _inproj_tail_kernel(x_ref, g_ref, w_ref, _, o_ref, u_ref):
    pl.when(pl.program_id(1) == 0)(lambda: _normalize_rows(x_ref, g_ref, u_ref))
    _project(u_ref, w_ref, o_ref)


def _inproj(x2, g, w_t, tm, tn):
    t, d = x2.shape
    n_native = w_t.shape[0]
    plan = _pack_plan(n_native)
    n = len(plan) * PACK_TILE
    assert n % tn == 0 and t % tm == 0

    def window_start(j):
        units = sum(jnp.where(j == tile, start // BF16_SUBLANES, 0) for tile, (start, _) in enumerate(plan))
        return units * BF16_SUBLANES

    proj, w_p = pl.pallas_call(
        functools.partial(_inproj_head_kernel, plan),
        grid=(len(plan),),
        in_specs=[
            pl.BlockSpec((tm, d), lambda j: (0, 0), pipeline_mode=pl.Buffered(1)),
            _const_spec(g.shape),
            pl.BlockSpec((pl.Element(PACK_WINDOW), pl.Element(d)), lambda j: (window_start(j), 0)),
        ],
        out_specs=[pl.BlockSpec((tm, PACK_TILE), lambda j: (0, j)),
                   pl.BlockSpec((PACK_TILE, d), lambda j: (j, 0))],
        out_shape=[jax.ShapeDtypeStruct((t, n), BF16), jax.ShapeDtypeStruct((n, d), BF16)],
        scratch_shapes=[pltpu.VMEM((tm, d), BF16)],
        compiler_params=_cparams(("arbitrary",)),
        name="inproj_head",
    )(x2, g, w_t)

    return pl.pallas_call(
        _inproj_tail_kernel,
        grid=(t // tm - 1, n // tn),
        in_specs=[
            pl.BlockSpec((tm, d), lambda i, j: (i + 1, 0)),
            pl.BlockSpec((1, d), lambda i, j: (0, 0)),
            pl.BlockSpec((tn, d), lambda i, j: (j, 0)),
            pl.BlockSpec(memory_space=pl.ANY),
        ],
        out_specs=pl.BlockSpec((tm, tn), lambda i, j: (i + 1, j)),
        out_shape=jax.ShapeDtypeStruct((t, n), BF16),
        input_output_aliases={3: 0},
        scratch_shapes=[pltpu.VMEM((tm, d), BF16)],
        compiler_params=_cparams(("parallel", "arbitrary")),
        name="inproj_tail",
    )(x2, g, w_p, proj)


def _bucket_starts():
    max_exact = N_BUCKETS // 2
    d = np.arange(WINDOW)
    large = max_exact + (np.log(np.maximum(d, 1).astype(np.float32) / max_exact)
                         / math.log(MAX_DISTANCE / max_exact)
                         * (N_BUCKETS - max_exact)).astype(np.int32)
    bucket = np.where(d < max_exact, d, np.minimum(large, N_BUCKETS - 1))
    starts = []
    for b in range(N_BUCKETS):
        hit = np.nonzero(bucket == b)[0]
        if hit.size:
            assert np.all(np.diff(hit) == 1)
            starts.append((b, int(hit[0])))
    return starts


HEADS_PER_TILE = LANES // A_HEAD_DIM
SWA_BLOCKS_PER_STEP = 4


def _swa_body(q_ref, kp_ref, kc_ref, vp_ref, vc_ref, sink_ref, rb_ref, o_ref, bias_ref):
    n_keys = 2 * A_BLOCK

    @pl.when((pl.program_id(0) == 0) & (pl.program_id(1) == 0))
    def _():
        row = lax.broadcasted_iota(jnp.int32, (A_BLOCK, n_keys), 0)
        col = lax.broadcasted_iota(jnp.int32, (A_BLOCK, n_keys), 1)
        dist = row + A_BLOCK - col
        band = (dist >= 0) & (dist < WINDOW)
        starts = _bucket_starts()
        for h in range(A_HEADS):
            val = jnp.full(dist.shape, rb_ref[h, starts[0][0]], F32)
            for b, s in starts[1:]:
                val = jnp.where(dist >= s, rb_ref[h, b], val)
            val = jnp.where(band, val, NEG_INF)
            sink = sink_ref[0, h]
            bias_ref[0, h] = jnp.where(col == 0, sink, val)
            bias_ref[1, h] = jnp.where(col == 0, sink, jnp.where(col >= A_BLOCK, val, NEG_INF))

    lane = lax.broadcasted_iota(jnp.int32, (1, LANES), 1)
    scale = A_HEAD_DIM ** -0.5
    q_keep = (jnp.where(lane < A_HEAD_DIM, scale, 0.0).astype(BF16),
              jnp.where(lane < A_HEAD_DIM, 0.0, scale).astype(BF16))
    lower_lanes = lax.broadcasted_iota(jnp.int32, (A_BLOCK, LANES), 1) < A_HEAD_DIM
    key0 = lax.broadcasted_iota(jnp.int32, (n_keys, LANES), 0) == 0
    ones = jnp.ones((n_keys, LANES), BF16)

    def attend(q_rows, prev, cur, first):
        def both_blocks(which, tile):
            cols = slice(tile * LANES, (tile + 1) * LANES)
            cat = jnp.concatenate([prev[which][:, cols], cur[which][:, cols]], axis=0).astype(F32)
            cat = jnp.where(key0, 0.0, cat)
            return cat.astype(BF16), pltpu.roll(cat, A_HEAD_DIM, 1).astype(BF16)

        stacks = []
        for tile in range(A_KV_WIDTH // LANES):
            k_cat, k_swp = both_blocks(0, tile)
            v_cat, v_swp = both_blocks(1, tile)
            q0 = tile * A_GROUP
            stacks.append((k_cat, jnp.concatenate([v_cat, ones], axis=1),
                           [(q0, 0), (q0 + 1, 0), (q0 + 2, 1), (q0 + 3, 1)]))
            stacks.append((k_swp, jnp.concatenate([v_swp, ones], axis=1),
                           [(q0, 1), (q0 + 1, 1), (q0 + 2, 0), (q0 + 3, 0)]))

        scores = []
        for k_tile, _, members in stacks:
            q4 = jnp.concatenate(
                [q_ref[q_rows, qt * LANES:(qt + 1) * LANES] * q_keep[half] for qt, half in members],
                axis=0)
            scores.append(lax.dot_general(q4, k_tile, (((1,), (1,)), ((), ())),
                                          preferred_element_type=F32))
        s = jnp.concatenate(scores, axis=0)
        s = s + jnp.concatenate(
            [bias_ref[first, qt * HEADS_PER_TILE + half]
             for _, _, members in stacks for qt, half in members], axis=0)
        p = jnp.exp(s - jnp.max(s, axis=-1, keepdims=True)).astype(BF16)

        normed = {}
        rows_per_stack = len(stacks[0][2]) * A_BLOCK
        for i, (_, v_ones, members) in enumerate(stacks):
            ov = jnp.dot(p[i * rows_per_stack:(i + 1) * rows_per_stack], v_ones,
                         preferred_element_type=F32)
            o = ov[:, :LANES] / ov[:, LANES:]
            for j, member in enumerate(members):
                normed[member] = o[j * A_BLOCK:(j + 1) * A_BLOCK]
        for qt in range(A_Q_WIDTH // LANES):
            o_ref[q_rows, qt * LANES:(qt + 1) * LANES] = jnp.where(
                lower_lanes, normed[(qt, 0)], normed[(qt, 1)]).astype(o_ref.dtype)

    blocks_per_step = q_ref.shape[0] // A_BLOCK
    kv_prev = (kp_ref[...], vp_ref[...])
    for sub in range(blocks_per_step):
        rows = slice(sub * A_BLOCK, (sub + 1) * A_BLOCK)
        kv_cur = (kc_ref[rows, :], vc_ref[rows, :])
        first = (pl.program_id(1) == 0).astype(jnp.int32) if sub == 0 else 0
        attend(rows, kv_prev, kv_cur, first)
        kv_prev = kv_cur


GLA_CHUNKS_PER_STEP = 8
GLA_PREFIX_GROUP = 1


def _split3(x):
    hi = x.astype(BF16)
    r1 = x - hi.astype(F32)
    mid = r1.astype(BF16)
    lo = (r1 - mid.astype(F32)).astype(BF16)
    return jnp.concatenate([hi, mid, lo], axis=0)


def _gla_body(q_ref, k_ref, v0_ref, v1_ref, gl_ref, og0_ref, og1_ref, wgu_ref, bg_ref, ng_ref,
              o_ref, s_ref):
    c = B_CHUNK
    n_chunks = GLA_CHUNKS_PER_STEP
    n_rows = n_chunks * c
    heads_per_half = HALF_V // B_VAL_DIM
    v_refs = (v0_ref, v1_ref)
    og_refs = (og0_ref, og1_ref)
    chunk_rows = [slice(j * c, (j + 1) * c) for j in range(n_chunks)]
    key_cols = [slice(h * B_KEY_DIM, (h + 1) * B_KEY_DIM) for h in range(B_HEADS)]
    units = [(j, h) for j in range(n_chunks) for h in range(B_HEADS)]

    def v_of(refs, j, h):
        lo = (h % heads_per_half) * B_VAL_DIM
        rows = slice(None) if j is None else chunk_rows[j]
        return refs[h // heads_per_half][rows, lo:lo + B_VAL_DIM]

    @pl.when(pl.program_id(1) == 0)
    def _():
        s_ref[...] = jnp.zeros_like(s_ref)

    glin = jnp.dot(gl_ref[...], wgu_ref[...], preferred_element_type=F32) + bg_ref[...]
    log_a = (jnp.minimum(glin, 0.0) - jnp.log(1.0 + jnp.exp(-jnp.abs(glin)))) / B_GATE_TAU

    g_rows = GLA_PREFIX_GROUP * c
    ri = lax.broadcasted_iota(jnp.int32, (g_rows, 3 * g_rows), 0)
    ci = lax.broadcasted_iota(jnp.int32, (g_rows, 3 * g_rows), 1)
    ci = ci - jnp.where(ci >= g_rows, g_rows, 0) - jnp.where(ci >= 2 * g_rows, g_rows, 0)
    shift = int(math.log2(c))
    same_chunk = lax.shift_right_logical(ri, shift) == lax.shift_right_logical(ci, shift)
    tri3 = ((ri >= ci) & same_chunk).astype(BF16)
    b = jnp.concatenate(
        [jnp.dot(tri3, _split3(log_a[r0:r0 + g_rows]), preferred_element_type=F32)
         for r0 in range(0, n_rows, g_rows)], axis=0)
    last_rows = [b[(j + 1) * c - 1:(j + 1) * c, :] for j in range(n_chunks)]
    b_last = jnp.concatenate([jnp.broadcast_to(r, (c, b.shape[1])) for r in last_rows], axis=0)

    qf = q_ref[...].astype(F32) * (B_KEY_DIM ** -0.5)
    kf = k_ref[...].astype(F32)
    q_dec = (qf * jnp.exp(b)).astype(BF16)
    k_dec = (kf * jnp.exp(-b)).astype(BF16)
    k_state = kf * jnp.exp(b_last - b)
    sublanes = 8
    pad = [jnp.zeros((sublanes - n_chunks, b.shape[1]), F32)] if n_chunks < sublanes else []
    decay_rows = jnp.exp(jnp.concatenate(last_rows + pad, axis=0))

    ri = lax.broadcasted_iota(jnp.int32, (c, c), 0)
    ci = lax.broadcasted_iota(jnp.int32, (c, c), 1)
    causal = ri >= ci
    att = {}
    for j, h in units:
        a = lax.dot_general(q_dec[chunk_rows[j], key_cols[h]], k_dec[chunk_rows[j], key_cols[h]],
                            (((1,), (1,)), ((), ())), preferred_element_type=F32)
        att[j, h] = jnp.where(causal, a, 0.0).astype(BF16)
    o_intra = {u: jnp.dot(att[u], v_of(v_refs, *u), preferred_element_type=F32) for u in units}
    ds = {(j, h): jnp.dot(k_state[chunk_rows[j], key_cols[h]].T.astype(BF16), v_of(v_refs, j, h),
                          preferred_element_type=F32) for j, h in units}

    entering = {}
    for h in range(B_HEADS):
        decay_t = decay_rows[:, key_cols[h]].T
        state = s_ref[h]
        for j in range(n_chunks):
            entering[j, h] = state.astype(BF16)
            state = decay_t[:, j:j + 1] * state + ds[j, h]
        s_ref[h] = state
    o_inter = {(j, h): jnp.dot(q_dec[chunk_rows[j], key_cols[h]], entering[j, h],
                               preferred_element_type=F32) for j, h in units}

    ng = ng_ref[...]
    for h in range(B_HEADS):
        o = jnp.concatenate([o_intra[j, h] + o_inter[j, h] for j in range(n_chunks)], axis=0)
        gate = v_of(og_refs, None, h).astype(F32)
        y = _rmsnorm_rows(o, ng) * (gate * jax.nn.sigmoid(gate))
        o_ref[:, h * B_VAL_DIM:(h + 1) * B_VAL_DIM] = y.astype(o_ref.dtype)


N_MIX_IN = 7


def _mixers_kernel(n_cast, *refs):
    mix_ref, prev_ref, sink_ref, rb_ref, wgu_ref, bg_ref, ng_ref = refs[:N_MIX_IN]
    cast_in, refs = refs[N_MIX_IN:N_MIX_IN + n_cast], refs[N_MIX_IN + n_cast:]
    attn_ref, gla_ref = refs[:2]
    cast_out, (bias_ref, s_ref) = refs[2:2 + n_cast], refs[2 + n_cast:]
    _cast_blocks(cast_in, cast_out)

    def cols(off, width):
        return mix_ref.at[:, off - OFF_QA:off - OFF_QA + width]

    kv_prev = [prev_ref.at[:, off - OFF_KA:off - OFF_KA + A_KV_WIDTH] for off in (OFF_KA, OFF_VA)]
    _swa_body(cols(OFF_QA, A_Q_WIDTH), kv_prev[0], cols(OFF_KA, A_KV_WIDTH),
              kv_prev[1], cols(OFF_VA, A_KV_WIDTH), sink_ref, rb_ref, attn_ref, bias_ref)
    _gla_body(cols(OFF_QB, B_QK_WIDTH), cols(OFF_KB, B_QK_WIDTH),
              cols(OFF_VB, HALF_V), cols(OFF_VB + HALF_V, HALF_V), cols(OFF_GLOW, LANES),
              cols(OFF_OBG, HALF_V), cols(OFF_OBG + HALF_V, HALF_V),
              wgu_ref, bg_ref, ng_ref, gla_ref, s_ref)


def _mixers(proj, sinks, rel_bias, wgu, bg, ng, cast_weights, batch, seq):
    t = proj.shape[0]
    rows = GLA_CHUNKS_PER_STEP * B_CHUNK
    assert rows == SWA_BLOCKS_PER_STEP * A_BLOCK and seq % rows == 0
    steps = seq // rows
    rb = lambda b, s: b * steps + s
    prev = lambda b, s: jnp.maximum(rb(b, s) * SWA_BLOCKS_PER_STEP - 1, 0)
    smem = functools.partial(pl.BlockSpec, memory_space=pltpu.SMEM)
    cast_in, cast_out, cast_shapes = _cast_specs(cast_weights, batch * steps, rb)
    kv_width = OFF_QB - OFF_KA
    assert OFF_KA % kv_width == 0 and OFF_VA == OFF_KA + A_KV_WIDTH
    mix_specs = [
        pl.BlockSpec((pl.Element(rows), pl.Element(PROJ_USED - OFF_QA)),
                     lambda b, s: (rb(b, s) * rows, OFF_QA)),
        pl.BlockSpec((A_BLOCK, kv_width), lambda b, s: (prev(b, s), OFF_KA // kv_width)),
        smem(), smem(), _const_spec(wgu.shape), _const_spec(bg.shape), _const_spec(ng.shape)]
    assert len(mix_specs) == N_MIX_IN
    out_block = lambda width: pl.BlockSpec((rows, width), lambda b, s: (rb(b, s), 0))
    outs = pl.pallas_call(
        functools.partial(_mixers_kernel, len(cast_weights)),
        grid=(batch, steps),
        in_specs=mix_specs + cast_in,
        out_specs=[out_block(A_Q_WIDTH), out_block(B_V_WIDTH)] + cast_out,
        out_shape=[jax.ShapeDtypeStruct((t, A_Q_WIDTH), BF16),
                   jax.ShapeDtypeStruct((t, B_V_WIDTH), BF16)] + cast_shapes,
        scratch_shapes=[pltpu.VMEM((2, A_HEADS, A_BLOCK, 2 * A_BLOCK), F32),
                        pltpu.VMEM((B_HEADS, B_KEY_DIM, B_VAL_DIM), F32)],
        compiler_params=_cparams(("arbitrary", "arbitrary")),
        name="mixers",
    )(proj, proj, sinks, rel_bias.T, wgu, bg, ng, *cast_weights)
    return outs[0], outs[1], outs[2:]


def _merge_kernel(a_ref, b_ref, gates_ref, x_ref, wa_ref, wb_ref, wo_ref, gz_ref, h_ref, z_ref):
    d = x_ref.shape[1]
    ya = jnp.dot(a_ref[...], wa_ref[...], preferred_element_type=F32)
    yb = jnp.dot(b_ref[...], wb_ref[...], preferred_element_type=F32)
    merged = (jax.nn.sigmoid(gates_ref[:, :d].astype(F32)) * ya
              + jax.nn.sigmoid(gates_ref[:, d:].astype(F32)) * yb)
    h = x_ref[...] + jnp.dot(merged.astype(BF16), wo_ref[...], preferred_element_type=F32)
    h_ref[...] = h
    z_ref[...] = _rmsnorm_rows(h, gz_ref[...]).astype(z_ref.dtype)


def _merge(attn, gla, proj, x2, wa, wb, wo, gz, tm):
    t, d = x2.shape
    row_block = pl.BlockSpec((tm, d), lambda i: (i, 0))
    return pl.pallas_call(
        _merge_kernel,
        grid=(t // tm,),
        in_specs=[
            pl.BlockSpec((tm, A_Q_WIDTH), lambda i: (i, 0)),
            pl.BlockSpec((tm, B_V_WIDTH), lambda i: (i, 0)),
            pl.BlockSpec((tm, 2 * d), lambda i: (i, OFF_GATE_A // (2 * d))),
            row_block,
            _const_spec(wa.shape), _const_spec(wb.shape), _const_spec(wo.shape), _const_spec(gz.shape),
        ],
        out_specs=[row_block, row_block],
        out_shape=[jax.ShapeDtypeStruct((t, d), F32), jax.ShapeDtypeStruct((t, d), BF16)],
        compiler_params=_cparams(("parallel",)),
        name="merge",
    )(attn, gla, proj, x2, wa, wb, wo, gz)


def _ffn_kernel(z_ref, h_ref, wg_ref, wu_ref, wd_ref, gf_ref, o_ref):
    f = pl.program_id(1)
    z = z_ref[...]
    half = wg_ref.shape[1] // 2
    for c in range(2):
        cols = slice(c * half, (c + 1) * half)
        g = jnp.dot(z, wg_ref[:, cols], preferred_element_type=F32)
        u = jnp.dot(z, wu_ref[:, cols], preferred_element_type=F32)
        act = (g * jax.nn.sigmoid(g) * u).astype(BF16)
        acc = jnp.where(f == 0, 0.0, o_ref[...]) if c == 0 else o_ref[...]
        o_ref[...] = acc + jnp.dot(act, wd_ref[cols, :], preferred_element_type=F32)

    @pl.when(f == pl.num_programs(1) - 1)
    def _():
        gf = gf_ref[...]

        def body(c, carry):
            rows = pl.ds(pl.multiple_of(c * NORM_ROWS, NORM_ROWS), NORM_ROWS)
            o_ref[rows, :] = _rmsnorm_rows(h_ref[rows, :] + o_ref[rows, :], gf)
            return carry

        lax.fori_loop(0, h_ref.shape[0] // NORM_ROWS, body, 0)


def _ffn(z, h, wg, wu, wd, gf, tm, tf):
    t, d = h.shape
    f = wg.shape[1]
    row_block = pl.BlockSpec((tm, d), lambda i, j: (i, 0))
    return pl.pallas_call(
        _ffn_kernel,
        grid=(t // tm, f // tf),
        in_specs=[
            row_block, row_block,
            pl.BlockSpec((d, tf), lambda i, j: (0, j)),
            pl.BlockSpec((d, tf), lambda i, j: (0, j)),
            pl.BlockSpec((tf, d), lambda i, j: (j, 0)),
            pl.BlockSpec((1, d), lambda i, j: (0, 0)),
        ],
        out_specs=row_block,
        out_shape=jax.ShapeDtypeStruct((t, d), F32),
        compiler_params=_cparams(("parallel", "arbitrary")),
        name="ffn",
    )(z, h, wg, wu, wd, gf)


def kernel(x, norm_mix_g, w_in, sinks, rel_bias, w_gate_up, b_gate, gla_norm_g, w_proj_a, w_proj_b,
           w_out, norm_ffn_g, w_ffn_gate, w_ffn_up, w_ffn_down, norm_final_g):
    batch, seq, d = x.shape
    assert d == D_MODEL and w_in.shape[0] == 1, "single-layer geometry"
    t = batch * seq
    x2 = x.reshape(t, d)

    proj = _inproj(x2, norm_mix_g, w_in[0].T, tm=1024, tn=2304)

    wgu = jnp.zeros((LANES, B_QK_WIDTH), BF16).at[:B_GATE_RANK].set(w_gate_up[0].astype(BF16))
    later_weights = (w_proj_a[0], w_proj_b[0], w_out[0], w_ffn_gate[0], w_ffn_up[0], w_ffn_down[0])
    attn, gla, (wa, wb, wo, wg, wu, wd) = _mixers(proj, sinks, rel_bias, wgu, b_gate, gla_norm_g,
                                                  later_weights, batch, seq)

    h, z = _merge(attn, gla, proj, x2, wa, wb, wo, norm_ffn_g, tm=512)

    out = _ffn(z, h, wg, wu, wd, norm_final_g.reshape(1, d), tm=1024, tf=512)
    return out.reshape(batch, seq, d)
```

```python
import functools
import math

import numpy as np
import jax
import jax.numpy as jnp
from jax import lax
from jax.experimental import pallas as pl
from jax.experimental.pallas import tpu as pltpu

F32 = jnp.float32
BF16 = jnp.bfloat16

D_MODEL = 2048
A_HEADS = 16
A_KV_HEADS = 4
A_HEAD_DIM = 64
A_GROUP = A_HEADS // A_KV_HEADS
WINDOW = 128
A_BLOCK = 128
A_Q_WIDTH = A_HEADS * A_HEAD_DIM
A_KV_WIDTH = A_KV_HEADS * A_HEAD_DIM
N_BUCKETS = 32
MAX_DISTANCE = 128
B_HEADS = 4
B_KEY_DIM = 128
B_VAL_DIM = 256
B_QK_WIDTH = B_HEADS * B_KEY_DIM
B_V_WIDTH = B_HEADS * B_VAL_DIM
B_GATE_RANK = 16
B_GATE_TAU = 16.0
B_CHUNK = 64
EPS = 1e-6
NEG_INF = -1e30

LANES = 128
BF16_SUBLANES = 16

OFF_GATE_A = 0
OFF_GATE_B = OFF_GATE_A + D_MODEL
OFF_QA = OFF_GATE_B + D_MODEL
OFF_KA = OFF_QA + A_Q_WIDTH
OFF_VA = OFF_KA + A_KV_WIDTH
OFF_QB = OFF_VA + A_KV_WIDTH
OFF_KB = OFF_QB + B_QK_WIDTH
OFF_VB = OFF_KB + B_QK_WIDTH
OFF_OBG = OFF_VB + B_V_WIDTH
OFF_GLOW = OFF_OBG + B_V_WIDTH
PROJ_USED = OFF_GLOW + LANES
HALF_V = B_V_WIDTH // 2

VMEM_LIMIT = 60 * 1024 * 1024


def _cparams(sem):
    return pltpu.CompilerParams(dimension_semantics=sem, vmem_limit_bytes=VMEM_LIMIT)


def _const_spec(shape):
    return pl.BlockSpec(shape, lambda *_: (0,) * len(shape), pipeline_mode=pl.Buffered(1))


def _rmsnorm_rows(x, g):
    ms = jnp.mean(x * x, axis=-1, keepdims=True)
    return x * lax.rsqrt(ms + EPS) * g


NORM_ROWS = 512
OUT_NORM_ROWS = 128


def _cast_specs(weights, n_chunks, chunk_of):
    in_specs, out_specs, shapes = [], [], []
    for w in weights:
        rows, rem = divmod(w.shape[0], n_chunks)
        assert rem == 0 and rows % BF16_SUBLANES == 0, (w.shape, n_chunks)
        for specs in (in_specs, out_specs):
            specs.append(pl.BlockSpec((rows, w.shape[1]), lambda *idx: (chunk_of(*idx), 0)))
        shapes.append(jax.ShapeDtypeStruct(w.shape, BF16))
    return in_specs, out_specs, shapes


def _cast_blocks(in_refs, out_refs):
    for src, dst in zip(in_refs, out_refs):
        dst[...] = src[...].astype(dst.dtype)


def _normalize_rows(x_ref, g_ref, u_ref):
    g = g_ref[...]

    def body(c, carry):
        rows = pl.ds(pl.multiple_of(c * NORM_ROWS, NORM_ROWS), NORM_ROWS)
        u_ref[rows, :] = _rmsnorm_rows(x_ref[rows, :], g).astype(BF16)
        return carry

    lax.fori_loop(0, x_ref.shape[0] // NORM_ROWS, body, 0)


def _project(u_ref, w_ref, o_ref):
    o_ref[...] = lax.dot_general(u_ref[...], w_ref[...], (((1,), (1,)), ((), ())),
                                 preferred_element_type=F32).astype(o_ref.dtype)


_N_MIX = OFF_OBG - OFF_QA
_FEATURE_RUNS = (
    (OFF_GATE_A, _N_MIX + B_GATE_RANK + B_V_WIDTH, 2 * D_MODEL),
    (OFF_QA, 0, _N_MIX),
    (OFF_OBG, _N_MIX + B_GATE_RANK, B_V_WIDTH),
    (OFF_GLOW, _N_MIX, B_GATE_RANK),
)
PACK_TILE = 1024
PACK_WINDOW = PACK_TILE + B_GATE_RANK


def _pack_plan(n_native):
    plan = []
    for tile in range(pl.cdiv(PROJ_USED, PACK_TILE)):
        lo, hi = tile * PACK_TILE, (tile + 1) * PACK_TILE
        pieces = []
        for dst, src, n in _FEATURE_RUNS:
            a, b = max(lo, dst), min(hi, dst + n)
            if a < b:
                pieces.append((a - lo, src + a - dst, b - a))
        start = min(min(p[1] for p in pieces), n_native - PACK_WINDOW)
        assert all(start <= s and s + n <= start + PACK_WINDOW for _, s, n in pieces), (tile, pieces)
        assert start % BF16_SUBLANES == 0 and all(d % BF16_SUBLANES == 0 and (s - start) % BF16_SUBLANES == 0
                                                   for d, s, _ in pieces)
        plan.append((start, [(d, s - start, n) for d, s, n in pieces]))
    return plan


def _inproj_head_kernel(plan, x_ref, g_ref, w_ref, o_ref, wt_ref, u_ref):
    j = pl.program_id(0)
    pl.when(j == 0)(lambda: _normalize_rows(x_ref, g_ref, u_ref))
    for tile, (_, pieces) in enumerate(plan):
        @pl.when(j == tile)
        def _(pieces=pieces):
            covered = 0
            for dst, src, n in sorted(pieces):
                assert dst == covered
                wt_ref[dst:dst + n, :] = w_ref[src:src + n, :].astype(BF16)
                covered += n
            if covered < PACK_TILE:
                wt_ref[covered:, :] = jnp.zeros((PACK_TILE - covered, wt_ref.shape[1]), BF16)
    _project(u_ref, wt_ref, o_ref)


def _inproj_tail_kernel(x_ref, g_ref, w_ref, _, o_ref, u_ref):
    pl.when(pl.program_id(1) == 0)(lambda: _normalize_rows(x_ref, g_ref, u_ref))
    _project(u_ref, w_ref, o_ref)


def _inproj(x2, g, w_t, tm, tn):
    t, d = x2.shape
    n_native = w_t.shape[0]
    plan = _pack_plan(n_native)
    n = len(plan) * PACK_TILE
    assert n % tn == 0 and t % tm == 0

    def window_start(j):
        units = sum(jnp.where(j == tile, start // BF16_SUBLANES, 0) for tile, (start, _) in enumerate(plan))
        return units * BF16_SUBLANES

    proj, w_p = pl.pallas_call(
        functools.partial(_inproj_head_kernel, plan),
        grid=(len(plan),),
        in_specs=[
            pl.BlockSpec((tm, d), lambda j: (0, 0), pipeline_mode=pl.Buffered(1)),
            _const_spec(g.shape),
            pl.BlockSpec((pl.Element(PACK_WINDOW), pl.Element(d)), lambda j: (window_start(j), 0)),
        ],
        out_specs=[pl.BlockSpec((tm, PACK_TILE), lambda j: (0, j)),
                   pl.BlockSpec((PACK_TILE, d), lambda j: (j, 0))],
        out_shape=[jax.ShapeDtypeStruct((t, n), BF16), jax.ShapeDtypeStruct((n, d), BF16)],
        scratch_shapes=[pltpu.VMEM((tm, d), BF16)],
        compiler_params=_cparams(("arbitrary",)),
        name="inproj_head",
    )(x2, g, w_t)

    return pl.pallas_call(
        _inproj_tail_kernel,
        grid=(t // tm - 1, n // tn),
        in_specs=[
            pl.BlockSpec((tm, d), lambda i, j: (i + 1, 0)),
            pl.BlockSpec((1, d), lambda i, j: (0, 0)),
            pl.BlockSpec((tn, d), lambda i, j: (j, 0)),
            pl.BlockSpec(memory_space=pl.ANY),
        ],
        out_specs=pl.BlockSpec((tm, tn), lambda i, j: (i + 1, j)),
        out_shape=jax.ShapeDtypeStruct((t, n), BF16),
        input_output_aliases={3: 0},
        scratch_shapes=[pltpu.VMEM((tm, d), BF16)],
        compiler_params=_cparams(("parallel", "arbitrary")),
        name="inproj_tail",
    )(x2, g, w_p, proj)


def _bucket_starts():
    max_exact = N_BUCKETS // 2
    d = np.arange(WINDOW)
    large = max_exact + (np.log(np.maximum(d, 1).astype(np.float32) / max_exact)
                         / math.log(MAX_DISTANCE / max_exact)
                         * (N_BUCKETS - max_exact)).astype(np.int32)
    bucket = np.where(d < max_exact, d, np.minimum(large, N_BUCKETS - 1))
    starts = []
    for b in range(N_BUCKETS):
        hit = np.nonzero(bucket == b)[0]
        if hit.size:
            assert np.all(np.diff(hit) == 1)
            starts.append((b, int(hit[0])))
    return starts


HEADS_PER_TILE = LANES // A_HEAD_DIM
SWA_BLOCKS_PER_STEP = 4


def _swa_body(q_ref, kp_ref, kc_ref, vp_ref, vc_ref, sink_ref, rb_ref, o_ref, bias_ref):
    n_keys = 2 * A_BLOCK

    @pl.when((pl.program_id(0) == 0) & (pl.program_id(1) == 0))
    def _():
        row = lax.broadcasted_iota(jnp.int32, (A_BLOCK, n_keys), 0)
        col = lax.broadcasted_iota(jnp.int32, (A_BLOCK, n_keys), 1)
        dist = row + A_BLOCK - col
        band = (dist >= 0) & (dist < WINDOW)
        starts = _bucket_starts()
        for h in range(A_HEADS):
            val = jnp.full(dist.shape, rb_ref[h, starts[0][0]], F32)
            for b, s in starts[1:]:
                val = jnp.where(dist >= s, rb_ref[h, b], val)
            val = jnp.where(band, val, NEG_INF)
            sink = sink_ref[0, h]
            bias_ref[0, h] = jnp.where(col == 0, sink, val)
            bias_ref[1, h] = jnp.where(col == 0, sink, jnp.where(col >= A_BLOCK, val, NEG_INF))

    lane = lax.broadcasted_iota(jnp.int32, (1, LANES), 1)
    scale = A_HEAD_DIM ** -0.5
    q_keep = (jnp.where(lane < A_HEAD_DIM, scale, 0.0).astype(BF16),
              jnp.where(lane < A_HEAD_DIM, 0.0, scale).astype(BF16))
    lower_lanes = lax.broadcasted_iota(jnp.int32, (A_BLOCK, LANES), 1) < A_HEAD_DIM
    key0 = lax.broadcasted_iota(jnp.int32, (n_keys, LANES), 0) == 0
    ones = jnp.ones((n_keys, LANES), BF16)

    def attend(q_rows, prev, cur, first):
        def both_blocks(which, tile):
            cols = slice(tile * LANES, (tile + 1) * LANES)
            cat = jnp.concatenate([prev[which][:, cols], cur[which][:, cols]], axis=0).astype(F32)
            cat = jnp.where(key0, 0.0, cat)
            return cat.astype(BF16), pltpu.roll(cat, A_HEAD_DIM, 1).astype(BF16)

        stacks = []
        for tile in range(A_KV_WIDTH // LANES):
            k_cat, k_swp = both_blocks(0, tile)
            v_cat, v_swp = both_blocks(1, tile)
            q0 = tile * A_GROUP
            stacks.append((k_cat, jnp.concatenate([v_cat, ones], axis=1),
                           [(q0, 0), (q0 + 1, 0), (q0 + 2, 1), (q0 + 3, 1)]))
            stacks.append((k_swp, jnp.concatenate([v_swp, ones], axis=1),
                           [(q0, 1), (q0 + 1, 1), (q0 + 2, 0), (q0 + 3, 0)]))

        scores = []
        for k_tile, _, members in stacks:
            q4 = jnp.concatenate(
                [q_ref[q_rows, qt * LANES:(qt + 1) * LANES] * q_keep[half] for qt, half in members],
                axis=0)
            scores.append(lax.dot_general(q4, k_tile, (((1,), (1,)), ((), ())),
                                          preferred_element_type=F32))
        s = jnp.concatenate(scores, axis=0)
        s = s + jnp.concatenate(
            [bias_ref[first, qt * HEADS_PER_TILE + half]
             for _, _, members in stacks for qt, half in members], axis=0)
        p = jnp.exp(s - jnp.max(s, axis=-1, keepdims=True)).astype(BF16)

        normed = {}
        rows_per_stack = len(stacks[0][2]) * A_BLOCK
        for i, (_, v_ones, members) in enumerate(stacks):
            ov = jnp.dot(p[i * rows_per_stack:(i + 1) * rows_per_stack], v_ones,
                         preferred_element_type=F32)
            o = ov[:, :LANES] / ov[:, LANES:]
            for j, member in enumerate(members):
                normed[member] = o[j * A_BLOCK:(j + 1) * A_BLOCK]
        for qt in range(A_Q_WIDTH // LANES):
            o_ref[q_rows, qt * LANES:(qt + 1) * LANES] = jnp.where(
                lower_lanes, normed[(qt, 0)], normed[(qt, 1)]).astype(o_ref.dtype)

    blocks_per_step = q_ref.shape[0] // A_BLOCK
    kv_prev = (kp_ref[...], vp_ref[...])
    for sub in range(blocks_per_step):
        rows = slice(sub * A_BLOCK, (sub + 1) * A_BLOCK)
        kv_cur = (kc_ref[rows, :], vc_ref[rows, :])
        first = (pl.program_id(1) == 0).astype(jnp.int32) if sub == 0 else 0
        attend(rows, kv_prev, kv_cur, first)
        kv_prev = kv_cur


GLA_CHUNKS_PER_STEP = 8
GLA_PREFIX_GROUP = 1


def _split3(x):
    hi = x.astype(BF16)
    r1 = x - hi.astype(F32)
    mid = r1.astype(BF16)
    lo = (r1 - mid.astype(F32)).astype(BF16)
    return jnp.concatenate([hi, mid, lo], axis=0)


def _gla_body(q_ref, k_ref, v0_ref, v1_ref, gl_ref, og0_ref, og1_ref, wgu_ref, bg_ref, ng_ref,
              o_ref, s_ref):
    c = B_CHUNK
    n_chunks = GLA_CHUNKS_PER_STEP
    n_rows = n_chunks * c
    heads_per_half = HALF_V // B_VAL_DIM
    v_refs = (v0_ref, v1_ref)
    og_refs = (og0_ref, og1_ref)
    chunk_rows = [slice(j * c, (j + 1) * c) for j in range(n_chunks)]
    key_cols = [slice(h * B_KEY_DIM, (h + 1) * B_KEY_DIM) for h in range(B_HEADS)]
    units = [(j, h) for j in range(n_chunks) for h in range(B_HEADS)]

    def v_of(refs, j, h):
        lo = (h % heads_per_half) * B_VAL_DIM
        rows = slice(None) if j is None else chunk_rows[j]
        return refs[h // heads_per_half][rows, lo:lo + B_VAL_DIM]

    @pl.when(pl.program_id(1) == 0)
    def _():
        s_ref[...] = jnp.zeros_like(s_ref)

    rank = wgu_ref.shape[0]
    wgu = jnp.concatenate([wgu_ref[...].astype(BF16),
                           jnp.zeros((gl_ref.shape[1] - rank, wgu_ref.shape[1]), BF16)], axis=0)
    glin = jnp.dot(gl_ref[...], wgu, preferred_element_type=F32) + bg_ref[...]
    log_a = (jnp.minimum(glin, 0.0) - jnp.log(1.0 + jnp.exp(-jnp.abs(glin)))) / B_GATE_TAU

    g_rows = GLA_PREFIX_GROUP * c
    ri = lax.broadcasted_iota(jnp.int32, (g_rows, 3 * g_rows), 0)
    ci = lax.broadcasted_iota(jnp.int32, (g_rows, 3 * g_rows), 1)
    ci = ci - jnp.where(ci >= g_rows, g_rows, 0) - jnp.where(ci >= 2 * g_rows, g_rows, 0)
    shift = int(math.log2(c))
    same_chunk = lax.shift_right_logical(ri, shift) == lax.shift_right_logical(ci, shift)
    tri3 = ((ri >= ci) & same_chunk).astype(BF16)
    b = jnp.concatenate(
        [jnp.dot(tri3, _split3(log_a[r0:r0 + g_rows]), preferred_element_type=F32)
         for r0 in range(0, n_rows, g_rows)], axis=0)
    last_rows = [b[(j + 1) * c - 1:(j + 1) * c, :] for j in range(n_chunks)]
    b_last = jnp.concatenate([jnp.broadcast_to(r, (c, b.shape[1])) for r in last_rows], axis=0)

    qf = q_ref[...].astype(F32) * (B_KEY_DIM ** -0.5)
    kf = k_ref[...].astype(F32)
    q_dec = (qf * jnp.exp(b)).astype(BF16)
    k_dec = (kf * jnp.exp(-b)).astype(BF16)
    k_state = kf * jnp.exp(b_last - b)
    sublanes = 8
    pad = [jnp.zeros((sublanes - n_chunks, b.shape[1]), F32)] if n_chunks < sublanes else []
    decay_rows = jnp.exp(jnp.concatenate(last_rows + pad, axis=0))

    ri = lax.broadcasted_iota(jnp.int32, (c, c), 0)
    ci = lax.broadcasted_iota(jnp.int32, (c, c), 1)
    causal = ri >= ci
    att = {}
    for j, h in units:
        a = lax.dot_general(q_dec[chunk_rows[j], key_cols[h]], k_dec[chunk_rows[j], key_cols[h]],
                            (((1,), (1,)), ((), ())), preferred_element_type=F32)
        att[j, h] = jnp.where(causal, a, 0.0).astype(BF16)
    o_intra = {u: jnp.dot(att[u], v_of(v_refs, *u), preferred_element_type=F32) for u in units}
    ds = {(j, h): jnp.dot(k_state[chunk_rows[j], key_cols[h]].T.astype(BF16), v_of(v_refs, j, h),
                          preferred_element_type=F32) for j, h in units}

    entering = {}
    for h in range(B_HEADS):
        decay_t = decay_rows[:, key_cols[h]].T
        state = s_ref[h]
        for j in range(n_chunks):
            entering[j, h] = state.astype(BF16)
            state = decay_t[:, j:j + 1] * state + ds[j, h]
        s_ref[h] = state
    o_inter = {(j, h): jnp.dot(q_dec[chunk_rows[j], key_cols[h]], entering[j, h],
                               preferred_element_type=F32) for j, h in units}

    ng = ng_ref[...]
    for h in range(B_HEADS):
        o = jnp.concatenate([o_intra[j, h] + o_inter[j, h] for j in range(n_chunks)], axis=0)
        gate = v_of(og_refs, None, h).astype(F32)
        y = _rmsnorm_rows(o, ng) * (gate * jax.nn.sigmoid(gate))
        o_ref[:, h * B_VAL_DIM:(h + 1) * B_VAL_DIM] = y.astype(o_ref.dtype)


N_MIX_IN = 7


def _mixers_kernel(n_cast, *refs):
    mix_ref, prev_ref, sink_ref, rb_ref, wgu_ref, bg_ref, ng_ref = refs[:N_MIX_IN]
    cast_in, refs = refs[N_MIX_IN:N_MIX_IN + n_cast], refs[N_MIX_IN + n_cast:]
    attn_ref, gla_ref = refs[:2]
    cast_out, (bias_ref, s_ref) = refs[2:2 + n_cast], refs[2 + n_cast:]
    _cast_blocks(cast_in, cast_out)

    def cols(off, width):
        return mix_ref.at[:, off - OFF_QA:off - OFF_QA + width]

    kv_prev = [prev_ref.at[:, off - OFF_KA:off - OFF_KA + A_KV_WIDTH] for off in (OFF_KA, OFF_VA)]
    _swa_body(cols(OFF_QA, A_Q_WIDTH), kv_prev[0], cols(OFF_KA, A_KV_WIDTH),
              kv_prev[1], cols(OFF_VA, A_KV_WIDTH), sink_ref, rb_ref, attn_ref, bias_ref)
    _gla_body(cols(OFF_QB, B_QK_WIDTH), cols(OFF_KB, B_QK_WIDTH),
              cols(OFF_VB, HALF_V), cols(OFF_VB + HALF_V, HALF_V), cols(OFF_GLOW, LANES),
              cols(OFF_OBG, HALF_V), cols(OFF_OBG + HALF_V, HALF_V),
              wgu_ref, bg_ref, ng_ref, gla_ref, s_ref)


def _mixers(proj, sinks, rel_bias, wgu, bg, ng, cast_weights, batch, seq):
    t = proj.shape[0]
    rows = GLA_CHUNKS_PER_STEP * B_CHUNK
    assert rows == SWA_BLOCKS_PER_STEP * A_BLOCK and seq % rows == 0
    steps = seq // rows
    rb = lambda b, s: b * steps + s
    prev = lambda b, s: jnp.maximum(rb(b, s) * SWA_BLOCKS_PER_STEP - 1, 0)
    smem = functools.partial(pl.BlockSpec, memory_space=pltpu.SMEM)
    cast_in, cast_out, cast_shapes = _cast_specs(cast_weights, batch * steps, rb)
    kv_width = OFF_QB - OFF_KA
    assert OFF_KA % kv_width == 0 and OFF_VA == OFF_KA + A_KV_WIDTH
    mix_specs = [
        pl.BlockSpec((pl.Element(rows), pl.Element(PROJ_USED - OFF_QA)),
                     lambda b, s: (rb(b, s) * rows, OFF_QA)),
        pl.BlockSpec((A_BLOCK, kv_width), lambda b, s: (prev(b, s), OFF_KA // kv_width)),
        smem(), smem(), _const_spec(wgu.shape), _const_spec(bg.shape), _const_spec(ng.shape)]
    assert len(mix_specs) == N_MIX_IN
    out_block = lambda width: pl.BlockSpec((rows, width), lambda b, s: (rb(b, s), 0))
    outs = pl.pallas_call(
        functools.partial(_mixers_kernel, len(cast_weights)),
        grid=(batch, steps),
        in_specs=mix_specs + cast_in,
        out_specs=[out_block(A_Q_WIDTH), out_block(B_V_WIDTH)] + cast_out,
        out_shape=[jax.ShapeDtypeStruct((t, A_Q_WIDTH), BF16),
                   jax.ShapeDtypeStruct((t, B_V_WIDTH), BF16)] + cast_shapes,
        scratch_shapes=[pltpu.VMEM((2, A_HEADS, A_BLOCK, 2 * A_BLOCK), F32),
                        pltpu.VMEM((B_HEADS, B_KEY_DIM, B_VAL_DIM), F32)],
        compiler_params=_cparams(("arbitrary", "arbitrary")),
        name="mixers",
    )(proj, proj, sinks, rel_bias.T, wgu, bg, ng, *cast_weights)
    return outs[0], outs[1], outs[2:]


def _merge_kernel(a_ref, b_ref, gates_ref, x_ref, wa_ref, wb_ref, wo_ref, gz_ref, h_ref, z_ref):
    d = x_ref.shape[1]
    ya = jnp.dot(a_ref[...], wa_ref[...], preferred_element_type=F32)
    yb = jnp.dot(b_ref[...], wb_ref[...], preferred_element_type=F32)
    merged = (jax.nn.sigmoid(gates_ref[:, :d].astype(F32)) * ya
              + jax.nn.sigmoid(gates_ref[:, d:].astype(F32)) * yb)
    h = x_ref[...] + jnp.dot(merged.astype(BF16), wo_ref[...], preferred_element_type=F32)
    h_ref[...] = h
    z_ref[...] = _rmsnorm_rows(h, gz_ref[...]).astype(z_ref.dtype)


def _merge(attn, gla, proj, x2, wa, wb, wo, gz, tm):
    t, d = x2.shape
    row_block = pl.BlockSpec((tm, d), lambda i: (i, 0))
    return pl.pallas_call(
        _merge_kernel,
        grid=(t // tm,),
        in_specs=[
            pl.BlockSpec((tm, A_Q_WIDTH), lambda i: (i, 0)),
            pl.BlockSpec((tm, B_V_WIDTH), lambda i: (i, 0)),
            pl.BlockSpec((tm, 2 * d), lambda i: (i, OFF_GATE_A // (2 * d))),
            row_block,
            _const_spec(wa.shape), _const_spec(wb.shape), _const_spec(wo.shape), _const_spec(gz.shape),
        ],
        out_specs=[row_block, row_block],
        out_shape=[jax.ShapeDtypeStruct((t, d), F32), jax.ShapeDtypeStruct((t, d), BF16)],
        compiler_params=_cparams(("parallel",)),
        name="merge",
    )(attn, gla, proj, x2, wa, wb, wo, gz)


def _ffn_kernel(z_ref, h_ref, wg_ref, wu_ref, wd_ref, gf_ref, o_ref):
    f = pl.program_id(1)
    z = z_ref[...]
    half = wg_ref.shape[1] // 2
    for c in range(2):
        cols = slice(c * half, (c + 1) * half)
        g = jnp.dot(z, wg_ref[:, cols], preferred_element_type=F32)
        u = jnp.dot(z, wu_ref[:, cols], preferred_element_type=F32)
        act = (g * jax.nn.sigmoid(g) * u).astype(BF16)
        acc = jnp.where(f == 0, 0.0, o_ref[...]) if c == 0 else o_ref[...]
        o_ref[...] = acc + jnp.dot(act, wd_ref[cols, :], preferred_element_type=F32)

    @pl.when(f == pl.num_programs(1) - 1)
    def _():
        gf = gf_ref[...]

        def body(c, carry):
            rows = pl.ds(pl.multiple_of(c * OUT_NORM_ROWS, OUT_NORM_ROWS), OUT_NORM_ROWS)
            o_ref[rows, :] = _rmsnorm_rows(h_ref[rows, :] + o_ref[rows, :], gf)
            return carry

        lax.fori_loop(0, h_ref.shape[0] // OUT_NORM_ROWS, body, 0)


def _ffn(z, h, wg, wu, wd, gf, tm, tf):
    t, d = h.shape
    f = wg.shape[1]
    row_block = pl.BlockSpec((tm, d), lambda i, j: (i, 0))
    return pl.pallas_call(
        _ffn_kernel,
        grid=(t // tm, f // tf),
        in_specs=[
            row_block, row_block,
            pl.BlockSpec((d, tf), lambda i, j: (0, j)),
            pl.BlockSpec((d, tf), lambda i, j: (0, j)),
            pl.BlockSpec((tf, d), lambda i, j: (j, 0)),
            pl.BlockSpec((1, d), lambda i, j: (0, 0)),
        ],
        out_specs=row_block,
        out_shape=jax.ShapeDtypeStruct((t, d), F32),
        compiler_params=_cparams(("parallel", "arbitrary")),
        name="ffn",
    )(z, h, wg, wu, wd, gf)


def kernel(x, norm_mix_g, w_in, sinks, rel_bias, w_gate_up, b_gate, gla_norm_g, w_proj_a, w_proj_b,
           w_out, norm_ffn_g, w_ffn_gate, w_ffn_up, w_ffn_down, norm_final_g):
    batch, seq, d = x.shape
    assert d == D_MODEL and w_in.shape[0] == 1, "single-layer geometry"
    t = batch * seq
    x2 = x.reshape(t, d)

    proj = _inproj(x2, norm_mix_g, w_in[0].T, tm=1024, tn=2304)

    later_weights = (w_proj_a[0], w_proj_b[0], w_out[0], w_ffn_gate[0], w_ffn_up[0], w_ffn_down[0])
    attn, gla, (wa, wb, wo, wg, wu, wd) = _mixers(proj, sinks, rel_bias, w_gate_up[0], b_gate, gla_norm_g,
                                                  later_weights, batch, seq)

    h, z = _merge(attn, gla, proj, x2, wa, wb, wo, norm_ffn_g, tm=512)

    out = _ffn(z, h, wg, wu, wd, norm_final_g.reshape(1, d), tm=1024, tf=512)
    return out.reshape(batch, seq, d)
```

```python
import functools
import math

import numpy as np
import jax
import jax.numpy as jnp
from jax import lax
from jax.experimental import pallas as pl
from jax.experimental.pallas import tpu as pltpu

F32 = jnp.float32
BF16 = jnp.bfloat16

D_MODEL = 2048
A_HEADS = 16
A_KV_HEADS = 4
A_HEAD_DIM = 64
A_GROUP = A_HEADS // A_KV_HEADS
WINDOW = 128
A_BLOCK = 128
A_Q_WIDTH = A_HEADS * A_HEAD_DIM
A_KV_WIDTH = A_KV_HEADS * A_HEAD_DIM
N_BUCKETS = 32
MAX_DISTANCE = 128
B_HEADS = 4
B_KEY_DIM = 128
B_VAL_DIM = 256
B_QK_WIDTH = B_HEADS * B_KEY_DIM
B_V_WIDTH = B_HEADS * B_VAL_DIM
B_GATE_RANK = 16
B_GATE_TAU = 16.0
B_CHUNK = 64
EPS = 1e-6
NEG_INF = -1e30

LANES = 128
BF16_SUBLANES = 16

OFF_GATE_A = 0
OFF_GATE_B = OFF_GATE_A + D_MODEL
OFF_QA = OFF_GATE_B + D_MODEL
OFF_KA = OFF_QA + A_Q_WIDTH
OFF_VA = OFF_KA + A_KV_WIDTH
OFF_QB = OFF_VA + A_KV_WIDTH
OFF_KB = OFF_QB + B_QK_WIDTH
OFF_VB = OFF_KB + B_QK_WIDTH
OFF_OBG = OFF_VB + B_V_WIDTH
OFF_GLOW = OFF_OBG + B_V_WIDTH
PROJ_USED = OFF_GLOW + LANES
HALF_V = B_V_WIDTH // 2

VMEM_LIMIT = 60 * 1024 * 1024


def _cparams(sem):
    return pltpu.CompilerParams(dimension_semantics=sem, vmem_limit_bytes=VMEM_LIMIT)


def _const_spec(shape):
    return pl.BlockSpec(shape, lambda *_: (0,) * len(shape), pipeline_mode=pl.Buffered(1))


def _rmsnorm_rows(x, g):
    ms = jnp.mean(x * x, axis=-1, keepdims=True)
    return x * lax.rsqrt(ms + EPS) * g


NORM_ROWS = 512
OUT_NORM_ROWS = 128


def _cast_specs(weights, n_chunks, chunk_of):
    in_specs, out_specs, shapes = [], [], []
    for w in weights:
        rows, rem = divmod(w.shape[0], n_chunks)
        assert rem == 0 and rows % BF16_SUBLANES == 0, (w.shape, n_chunks)
        for specs in (in_specs, out_specs):
            specs.append(pl.BlockSpec((rows, w.shape[1]), lambda *idx: (chunk_of(*idx), 0)))
        shapes.append(jax.ShapeDtypeStruct(w.shape, BF16))
    return in_specs, out_specs, shapes


def _cast_blocks(in_refs, out_refs):
    for src, dst in zip(in_refs, out_refs):
        dst[...] = src[...].astype(dst.dtype)


def _normalize_rows(x_ref, g_ref, u_ref):
    g = g_ref[...]

    def body(c, carry):
        rows = pl.ds(pl.multiple_of(c * NORM_ROWS, NORM_ROWS), NORM_ROWS)
        u_ref[rows, :] = _rmsnorm_rows(x_ref[rows, :], g).astype(BF16)
        return carry

    lax.fori_loop(0, x_ref.shape[0] // NORM_ROWS, body, 0)


def _project(u_ref, w_ref, o_ref):
    o_ref[...] = lax.dot_general(u_ref[...], w_ref[...], (((1,), (1,)), ((), ())),
                                 preferred_element_type=F32).astype(o_ref.dtype)


_N_MIX = OFF_OBG - OFF_QA
_FEATURE_RUNS = (
    (OFF_GATE_A, _N_MIX + B_GATE_RANK + B_V_WIDTH, 2 * D_MODEL),
    (OFF_QA, 0, _N_MIX),
    (OFF_OBG, _N_MIX + B_GATE_RANK, B_V_WIDTH),
    (OFF_GLOW, _N_MIX, B_GATE_RANK),
)
PACK_TILE = 1024
PACK_WINDOW = PACK_TILE + B_GATE_RANK


def _pack_plan(n_native):
    plan = []
    for tile in range(pl.cdiv(PROJ_USED, PACK_TILE)):
        lo, hi = tile * PACK_TILE, (tile + 1) * PACK_TILE
        pieces = []
        for dst, src, n in _FEATURE_RUNS:
            a, b = max(lo, dst), min(hi, dst + n)
            if a < b:
                pieces.append((a - lo, src + a - dst, b - a))
        start = min(min(p[1] for p in pieces), n_native - PACK_WINDOW)
        assert all(start <= s and s + n <= start + PACK_WINDOW for _, s, n in pieces), (tile, pieces)
        assert start % BF16_SUBLANES == 0 and all(d % BF16_SUBLANES == 0 and (s - start) % BF16_SUBLANES == 0
                                                   for d, s, _ in pieces)
        plan.append((start, [(d, s - start, n) for d, s, n in pieces]))
    return plan


def _inproj_head_kernel(plan, x_ref, g_ref, w_ref, o_ref, wt_ref, u_ref):
    j = pl.program_id(0)
    pl.when(j == 0)(lambda: _normalize_rows(x_ref, g_ref, u_ref))
    for tile, (_, pieces) in enumerate(plan):
        @pl.when(j == tile)
        def _(pieces=pieces):
            covered = 0
            for dst, src, n in sorted(pieces):
                assert dst == covered
                wt_ref[dst:dst + n, :] = w_ref[src:src + n, :].astype(BF16)
                covered += n
            if covered < PACK_TILE:
                wt_ref[covered:, :] = jnp.zeros((PACK_TILE - covered, wt_ref.shape[1]), BF16)
    _project(u_ref, wt_ref, o_ref)


def _inproj_tail_kernel(x_ref, g_ref, w_ref, _, o_ref, u_ref):
    pl.when(pl.program_id(1) == 0)(lambda: _normalize_rows(x_ref, g_ref, u_ref))
    _project(u_ref, w_ref, o_ref)


def _inproj(x2, g, w_t, tm, tn):
    t, d = x2.shape
    n_native = w_t.shape[0]
    plan = _pack_plan(n_native)
    n = len(plan) * PACK_TILE
    assert n % tn == 0 and t % tm == 0

    def window_start(j):
        units = sum(jnp.where(j == tile, start // BF16_SUBLANES, 0) for tile, (start, _) in enumerate(plan))
        return units * BF16_SUBLANES

    proj, w_p = pl.pallas_call(
        functools.partial(_inproj_head_kernel, plan),
        grid=(len(plan),),
        in_specs=[
            pl.BlockSpec((tm, d), lambda j: (0, 0), pipeline_mode=pl.Buffered(1)),
            _const_spec(g.shape),
            pl.BlockSpec((pl.Element(PACK_WINDOW), pl.Element(d)), lambda j: (window_start(j), 0)),
        ],
        out_specs=[pl.BlockSpec((tm, PACK_TILE), lambda j: (0, j)),
                   pl.BlockSpec((PACK_TILE, d), lambda j: (j, 0))],
        out_shape=[jax.ShapeDtypeStruct((t, n), BF16), jax.ShapeDtypeStruct((n, d), BF16)],
        scratch_shapes=[pltpu.VMEM((tm, d), BF16)],
        compiler_params=_cparams(("arbitrary",)),
        name="inproj_head",
    )(x2, g, w_t)

    return pl.pallas_call(
        _inproj_tail_kernel,
        grid=(t // tm - 1, n // tn),
        in_specs=[
            pl.BlockSpec((tm, d), lambda i, j: (i + 1, 0)),
            pl.BlockSpec((1, d), lambda i, j: (0, 0)),
            pl.BlockSpec((tn, d), lambda i, j: (j, 0)),
            pl.BlockSpec(memory_space=pl.ANY),
        ],
        out_specs=pl.BlockSpec((tm, tn), lambda i, j: (i + 1, j)),
        out_shape=jax.ShapeDtypeStruct((t, n), BF16),
        input_output_aliases={3: 0},
        scratch_shapes=[pltpu.VMEM((tm, d), BF16)],
        compiler_params=_cparams(("parallel", "arbitrary")),
        name="inproj_tail",
    )(x2, g, w_p, proj)


def _bucket_starts():
    max_exact = N_BUCKETS // 2
    d = np.arange(WINDOW)
    large = max_exact + (np.log(np.maximum(d, 1).astype(np.float32) / max_exact)
                         / math.log(MAX_DISTANCE / max_exact)
                         * (N_BUCKETS - max_exact)).astype(np.int32)
    bucket = np.where(d < max_exact, d, np.minimum(large, N_BUCKETS - 1))
    starts = []
    for b in range(N_BUCKETS):
        hit = np.nonzero(bucket == b)[0]
        if hit.size:
            assert np.all(np.diff(hit) == 1)
            starts.append((b, int(hit[0])))
    return starts


HEADS_PER_TILE = LANES // A_HEAD_DIM
SWA_BLOCKS_PER_STEP = 4


def _swa_body(q_ref, kp_ref, kc_ref, vp_ref, vc_ref, sink_ref, rb_ref, o_ref, bias_ref):
    n_keys = 2 * A_BLOCK

    @pl.when((pl.program_id(0) == 0) & (pl.program_id(1) == 0))
    def _():
        row = lax.broadcasted_iota(jnp.int32, (A_BLOCK, n_keys), 0)
        col = lax.broadcasted_iota(jnp.int32, (A_BLOCK, n_keys), 1)
        dist = row + A_BLOCK - col
        band = (dist >= 0) & (dist < WINDOW)
        starts = _bucket_starts()
        for h in range(A_HEADS):
            val = jnp.full(dist.shape, rb_ref[h, starts[0][0]], F32)
            for b, s in starts[1:]:
                val = jnp.where(dist >= s, rb_ref[h, b], val)
            val = jnp.where(band, val, NEG_INF)
            sink = sink_ref[0, h]
            bias_ref[0, h] = jnp.where(col == 0, sink, val)
            bias_ref[1, h] = jnp.where(col == 0, sink, jnp.where(col >= A_BLOCK, val, NEG_INF))

    lane = lax.broadcasted_iota(jnp.int32, (1, LANES), 1)
    scale = A_HEAD_DIM ** -0.5
    q_keep = (jnp.where(lane < A_HEAD_DIM, scale, 0.0).astype(BF16),
              jnp.where(lane < A_HEAD_DIM, 0.0, scale).astype(BF16))
    lower_lanes = lax.broadcasted_iota(jnp.int32, (A_BLOCK, LANES), 1) < A_HEAD_DIM
    key0 = lax.broadcasted_iota(jnp.int32, (n_keys, LANES), 0) == 0
    ones = jnp.ones((n_keys, LANES), BF16)

    def attend(q_rows, prev, cur, first):
        def both_blocks(which, tile):
            cols = slice(tile * LANES, (tile + 1) * LANES)
            cat = jnp.concatenate([prev[which][:, cols], cur[which][:, cols]], axis=0).astype(F32)
            cat = jnp.where(key0, 0.0, cat)
            return cat.astype(BF16), pltpu.roll(cat, A_HEAD_DIM, 1).astype(BF16)

        stacks = []
        for tile in range(A_KV_WIDTH // LANES):
            k_cat, k_swp = both_blocks(0, tile)
            v_cat, v_swp = both_blocks(1, tile)
            q0 = tile * A_GROUP
            stacks.append((k_cat, jnp.concatenate([v_cat, ones], axis=1),
                           [(q0, 0), (q0 + 1, 0), (q0 + 2, 1), (q0 + 3, 1)]))
            stacks.append((k_swp, jnp.concatenate([v_swp, ones], axis=1),
                           [(q0, 1), (q0 + 1, 1), (q0 + 2, 0), (q0 + 3, 0)]))

        scores = []
        for k_tile, _, members in stacks:
            q4 = jnp.concatenate(
                [q_ref[q_rows, qt * LANES:(qt + 1) * LANES] * q_keep[half] for qt, half in members],
                axis=0)
            scores.append(lax.dot_general(q4, k_tile, (((1,), (1,)), ((), ())),
                                          preferred_element_type=F32))
        s = jnp.concatenate(scores, axis=0)
        s = s + jnp.concatenate(
            [bias_ref[first, qt * HEADS_PER_TILE + half]
             for _, _, members in stacks for qt, half in members], axis=0)
        p = jnp.exp(s - jnp.max(s, axis=-1, keepdims=True)).astype(BF16)

        normed = {}
        rows_per_stack = len(stacks[0][2]) * A_BLOCK
        for i, (_, v_ones, members) in enumerate(stacks):
            ov = jnp.dot(p[i * rows_per_stack:(i + 1) * rows_per_stack], v_ones,
                         preferred_element_type=F32)
            o = ov[:, :LANES] / ov[:, LANES:]
            for j, member in enumerate(members):
                normed[member] = o[j * A_BLOCK:(j + 1) * A_BLOCK]
        for qt in range(A_Q_WIDTH // LANES):
            o_ref[q_rows, qt * LANES:(qt + 1) * LANES] = jnp.where(
                lower_lanes, normed[(qt, 0)], normed[(qt, 1)]).astype(o_ref.dtype)

    blocks_per_step = q_ref.shape[0] // A_BLOCK
    kv_prev = (kp_ref[...], vp_ref[...])
    for sub in range(blocks_per_step):
        rows = slice(sub * A_BLOCK, (sub + 1) * A_BLOCK)
        kv_cur = (kc_ref[rows, :], vc_ref[rows, :])
        first = (pl.program_id(1) == 0).astype(jnp.int32) if sub == 0 else 0
        attend(rows, kv_prev, kv_cur, first)
        kv_prev = kv_cur


GLA_CHUNKS_PER_STEP = 8
GLA_PREFIX_GROUP = 1


def _split3(x):
    hi = x.astype(BF16)
    r1 = x - hi.astype(F32)
    mid = r1.astype(BF16)
    lo = (r1 - mid.astype(F32)).astype(BF16)
    return jnp.concatenate([hi, mid, lo], axis=0)


def _gla_body(q_ref, k_ref, v0_ref, v1_ref, gl_ref, og0_ref, og1_ref, wgu_ref, bg_ref, ng_ref,
              o_ref, s_ref):
    c = B_CHUNK
    n_chunks = GLA_CHUNKS_PER_STEP
    n_rows = n_chunks * c
    heads_per_half = HALF_V // B_VAL_DIM
    v_refs = (v0_ref, v1_ref)
    og_refs = (og0_ref, og1_ref)
    chunk_rows = [slice(j * c, (j + 1) * c) for j in range(n_chunks)]
    key_cols = [slice(h * B_KEY_DIM, (h + 1) * B_KEY_DIM) for h in range(B_HEADS)]
    units = [(j, h) for j in range(n_chunks) for h in range(B_HEADS)]

    def v_of(refs, j, h):
        lo = (h % heads_per_half) * B_VAL_DIM
        rows = slice(None) if j is None else chunk_rows[j]
        return refs[h // heads_per_half][rows, lo:lo + B_VAL_DIM]

    @pl.when(pl.program_id(1) == 0)
    def _():
        s_ref[...] = jnp.zeros_like(s_ref)

    rank = wgu_ref.shape[0]
    wgu = jnp.concatenate([wgu_ref[...].astype(BF16),
                           jnp.zeros((gl_ref.shape[1] - rank, wgu_ref.shape[1]), BF16)], axis=0)
    glin = jnp.dot(gl_ref[...], wgu, preferred_element_type=F32) + bg_ref[...]
    log_a = (jnp.minimum(glin, 0.0) - jnp.log(1.0 + jnp.exp(-jnp.abs(glin)))) / B_GATE_TAU

    g_rows = GLA_PREFIX_GROUP * c
    ri = lax.broadcasted_iota(jnp.int32, (g_rows, 3 * g_rows), 0)
    ci = lax.broadcasted_iota(jnp.int32, (g_rows, 3 * g_rows), 1)
    ci = ci - jnp.where(ci >= g_rows, g_rows, 0) - jnp.where(ci >= 2 * g_rows, g_rows, 0)
    shift = int(math.log2(c))
    same_chunk = lax.shift_right_logical(ri, shift) == lax.shift_right_logical(ci, shift)
    tri3 = ((ri >= ci) & same_chunk).astype(BF16)
    b = jnp.concatenate(
        [jnp.dot(tri3, _split3(log_a[r0:r0 + g_rows]), preferred_element_type=F32)
         for r0 in range(0, n_rows, g_rows)], axis=0)
    last_rows = [b[(j + 1) * c - 1:(j + 1) * c, :] for j in range(n_chunks)]
    b_last = jnp.concatenate([jnp.broadcast_to(r, (c, b.shape[1])) for r in last_rows], axis=0)

    qf = q_ref[...].astype(F32) * (B_KEY_DIM ** -0.5)
    kf = k_ref[...].astype(F32)
    q_dec = (qf * jnp.exp(b)).astype(BF16)
    k_dec = (kf * jnp.exp(-b)).astype(BF16)
    k_state = kf * jnp.exp(b_last - b)
    sublanes = 8
    pad = [jnp.zeros((sublanes - n_chunks, b.shape[1]), F32)] if n_chunks < sublanes else []
    decay_rows = jnp.exp(jnp.concatenate(last_rows + pad, axis=0))

    ri = lax.broadcasted_iota(jnp.int32, (c, c), 0)
    ci = lax.broadcasted_iota(jnp.int32, (c, c), 1)
    causal = ri >= ci
    att = {}
    for j, h in units:
        a = lax.dot_general(q_dec[chunk_rows[j], key_cols[h]], k_dec[chunk_rows[j], key_cols[h]],
                            (((1,), (1,)), ((), ())), preferred_element_type=F32)
        att[j, h] = jnp.where(causal, a, 0.0).astype(BF16)
    o_intra = {u: jnp.dot(att[u], v_of(v_refs, *u), preferred_element_type=F32) for u in units}
    ds = {(j, h): jnp.dot(k_state[chunk_rows[j], key_cols[h]].T.astype(BF16), v_of(v_refs, j, h),
                          preferred_element_type=F32) for j, h in units}

    entering = {}
    for h in range(B_HEADS):
        decay_t = decay_rows[:, key_cols[h]].T
        state = s_ref[h]
        for j in range(n_chunks):
            entering[j, h] = state.astype(BF16)
            state = decay_t[:, j:j + 1] * state + ds[j, h]
        s_ref[h] = state
    o_inter = {(j, h): jnp.dot(q_dec[chunk_rows[j], key_cols[h]], entering[j, h],
                               preferred_element_type=F32) for j, h in units}

    ng = ng_ref[...]
    for h in range(B_HEADS):
        o = jnp.concatenate([o_intra[j, h] + o_inter[j, h] for j in range(n_chunks)], axis=0)
        gate = v_of(og_refs, None, h).astype(F32)
        y = _rmsnorm_rows(o, ng) * (gate * jax.nn.sigmoid(gate))
        o_ref[:, h * B_VAL_DIM:(h + 1) * B_VAL_DIM] = y.astype(o_ref.dtype)


N_MIX_IN = 7


def _mixers_kernel(n_cast, *refs):
    mix_ref, prev_ref, sink_ref, rb_ref, wgu_ref, bg_ref, ng_ref = refs[:N_MIX_IN]
    cast_in, refs = refs[N_MIX_IN:N_MIX_IN + n_cast], refs[N_MIX_IN + n_cast:]
    attn_ref, gla_ref = refs[:2]
    cast_out, (bias_ref, s_ref) = refs[2:2 + n_cast], refs[2 + n_cast:]
    _cast_blocks(cast_in, cast_out)

    def cols(off, width):
        return mix_ref.at[:, off - OFF_QA:off - OFF_QA + width]

    kv_prev = [prev_ref.at[:, off - OFF_KA:off - OFF_KA + A_KV_WIDTH] for off in (OFF_KA, OFF_VA)]
    _swa_body(cols(OFF_QA, A_Q_WIDTH), kv_prev[0], cols(OFF_KA, A_KV_WIDTH),
              kv_prev[1], cols(OFF_VA, A_KV_WIDTH), sink_ref, rb_ref, attn_ref, bias_ref)
    _gla_body(cols(OFF_QB, B_QK_WIDTH), cols(OFF_KB, B_QK_WIDTH),
              cols(OFF_VB, HALF_V), cols(OFF_VB + HALF_V, HALF_V), cols(OFF_GLOW, LANES),
              cols(OFF_OBG, HALF_V), cols(OFF_OBG + HALF_V, HALF_V),
              wgu_ref, bg_ref, ng_ref, gla_ref, s_ref)


def _mixers(proj, sinks, rel_bias, wgu, bg, ng, cast_weights, batch, seq):
    t = proj.shape[0]
    rows = GLA_CHUNKS_PER_STEP * B_CHUNK
    assert rows == SWA_BLOCKS_PER_STEP * A_BLOCK and seq % rows == 0
    steps = seq // rows
    rb = lambda b, s: b * steps + s
    prev = lambda b, s: jnp.maximum(rb(b, s) * SWA_BLOCKS_PER_STEP - 1, 0)
    smem = functools.partial(pl.BlockSpec, memory_space=pltpu.SMEM)
    cast_in, cast_out, cast_shapes = _cast_specs(cast_weights, batch * steps, rb)
    kv_width = OFF_QB - OFF_KA
    assert OFF_KA % kv_width == 0 and OFF_VA == OFF_KA + A_KV_WIDTH
    mix_specs = [
        pl.BlockSpec((pl.Element(rows), pl.Element(PROJ_USED - OFF_QA)),
                     lambda b, s: (rb(b, s) * rows, OFF_QA)),
        pl.BlockSpec((A_BLOCK, kv_width), lambda b, s: (prev(b, s), OFF_KA // kv_width)),
        smem(), smem(), _const_spec(wgu.shape), _const_spec(bg.shape), _const_spec(ng.shape)]
    assert len(mix_specs) == N_MIX_IN
    out_block = lambda width: pl.BlockSpec((rows, width), lambda b, s: (rb(b, s), 0))
    outs = pl.pallas_call(
        functools.partial(_mixers_kernel, len(cast_weights)),
        grid=(batch, steps),
        in_specs=mix_specs + cast_in,
        out_specs=[out_block(A_Q_WIDTH), out_block(B_V_WIDTH)] + cast_out,
        out_shape=[jax.ShapeDtypeStruct((t, A_Q_WIDTH), BF16),
                   jax.ShapeDtypeStruct((t, B_V_WIDTH), BF16)] + cast_shapes,
        scratch_shapes=[pltpu.VMEM((2, A_HEADS, A_BLOCK, 2 * A_BLOCK), F32),
                        pltpu.VMEM((B_HEADS, B_KEY_DIM, B_VAL_DIM), F32)],
        compiler_params=_cparams(("arbitrary", "arbitrary")),
        name="mixers",
    )(proj, proj, sinks, rel_bias.T, wgu, bg, ng, *cast_weights)
    return outs[0], outs[1], outs[2:]


def _merge_kernel(a_ref, b_ref, gates_ref, x_ref, wa_ref, wb_ref, wo_ref, gz_ref, h_ref, z_ref):
    d = x_ref.shape[1]
    ya = jnp.dot(a_ref[...], wa_ref[...], preferred_element_type=F32)
    yb = jnp.dot(b_ref[...], wb_ref[...], preferred_element_type=F32)
    merged = (jax.nn.sigmoid(gates_ref[:, :d].astype(F32)) * ya
              + jax.nn.sigmoid(gates_ref[:, d:].astype(F32)) * yb)
    h = x_ref[...] + jnp.dot(merged.astype(BF16), wo_ref[...], preferred_element_type=F32)
    h_ref[...] = h
    z_ref[...] = _rmsnorm_rows(h, gz_ref[...]).astype(z_ref.dtype)


def _merge(attn, gla, proj, x2, wa, wb, wo, gz, tm):
    t, d = x2.shape
    row_block = pl.BlockSpec((tm, d), lambda i: (i, 0))
    return pl.pallas_call(
        _merge_kernel,
        grid=(t // tm,),
        in_specs=[
            pl.BlockSpec((tm, A_Q_WIDTH), lambda i: (i, 0)),
            pl.BlockSpec((tm, B_V_WIDTH), lambda i: (i, 0)),
            pl.BlockSpec((tm, 2 * d), lambda i: (i, OFF_GATE_A // (2 * d))),
            row_block,
            _const_spec(wa.shape), _const_spec(wb.shape), _const_spec(wo.shape), _const_spec(gz.shape),
        ],
        out_specs=[row_block, row_block],
        out_shape=[jax.ShapeDtypeStruct((t, d), F32), jax.ShapeDtypeStruct((t, d), BF16)],
        compiler_params=_cparams(("parallel",)),
        name="merge",
    )(attn, gla, proj, x2, wa, wb, wo, gz)


def _ffn_kernel(z_ref, h_ref, wg_ref, wu_ref, wd_ref, gf_ref, o_ref):
    f = pl.program_id(1)
    z = z_ref[...]
    half = wg_ref.shape[1] // 2
    halves = [slice(c * half, (c + 1) * half) for c in range(2)]
    gu = [(jnp.dot(z, wg_ref[:, cols], preferred_element_type=F32),
           jnp.dot(z, wu_ref[:, cols], preferred_element_type=F32)) for cols in halves]
    for c, cols in enumerate(halves):
        g, u = gu[c]
        act = (g * jax.nn.sigmoid(g) * u).astype(BF16)
        acc = jnp.where(f == 0, 0.0, o_ref[...]) if c == 0 else o_ref[...]
        o_ref[...] = acc + jnp.dot(act, wd_ref[cols, :], preferred_element_type=F32)

    @pl.when(f == pl.num_programs(1) - 1)
    def _():
        gf = gf_ref[...]

        def body(c, carry):
            rows = pl.ds(pl.multiple_of(c * OUT_NORM_ROWS, OUT_NORM_ROWS), OUT_NORM_ROWS)
            o_ref[rows, :] = _rmsnorm_rows(h_ref[rows, :] + o_ref[rows, :], gf)
            return carry

        lax.fori_loop(0, h_ref.shape[0] // OUT_NORM_ROWS, body, 0)


def _ffn(z, h, wg, wu, wd, gf, tm, tf):
    t, d = h.shape
    f = wg.shape[1]
    row_block = pl.BlockSpec((tm, d), lambda i, j: (i, 0))
    return pl.pallas_call(
        _ffn_kernel,
        grid=(t // tm, f // tf),
        in_specs=[
            row_block, row_block,
            pl.BlockSpec((d, tf), lambda i, j: (0, j)),
            pl.BlockSpec((d, tf), lambda i, j: (0, j)),
            pl.BlockSpec((tf, d), lambda i, j: (j, 0)),
            pl.BlockSpec((1, d), lambda i, j: (0, 0)),
        ],
        out_specs=row_block,
        out_shape=jax.ShapeDtypeStruct((t, d), F32),
        compiler_params=_cparams(("parallel", "arbitrary")),
        name="ffn",
    )(z, h, wg, wu, wd, gf)


def kernel(x, norm_mix_g, w_in, sinks, rel_bias, w_gate_up, b_gate, gla_norm_g, w_proj_a, w_proj_b,
           w_out, norm_ffn_g, w_ffn_gate, w_ffn_up, w_ffn_down, norm_final_g):
    batch, seq, d = x.shape
    assert d == D_MODEL and w_in.shape[0] == 1, "single-layer geometry"
    t = batch * seq
    x2 = x.reshape(t, d)

    proj = _inproj(x2, norm_mix_g, w_in[0].T, tm=1024, tn=2304)

    later_weights = (w_proj_a[0], w_proj_b[0], w_out[0], w_ffn_gate[0], w_ffn_up[0], w_ffn_down[0])
    attn, gla, (wa, wb, wo, wg, wu, wd) = _mixers(proj, sinks, rel_bias, w_gate_up[0], b_gate, gla_norm_g,
                                                  later_weights, batch, seq)

    h, z = _merge(attn, gla, proj, x2, wa, wb, wo, norm_ffn_g, tm=512)

    out = _ffn(z, h, wg, wu, wd, norm_final_g.reshape(1, d), tm=1024, tf=512)
    return out.reshape(batch, seq, d)
```

```python
import functools
import math

import numpy as np
import jax
import jax.numpy as jnp
from jax import lax
from jax.experimental import pallas as pl
from jax.experimental.pallas import tpu as pltpu

F32 = jnp.float32
BF16 = jnp.bfloat16

D_MODEL = 2048
A_HEADS = 16
A_KV_HEADS = 4
A_HEAD_DIM = 64
A_GROUP = A_HEADS // A_KV_HEADS
WINDOW = 128
A_BLOCK = 128
A_Q_WIDTH = A_HEADS * A_HEAD_DIM
A_KV_WIDTH = A_KV_HEADS * A_HEAD_DIM
N_BUCKETS = 32
MAX_DISTANCE = 128
B_HEADS = 4
B_KEY_DIM = 128
B_VAL_DIM = 256
B_QK_WIDTH = B_HEADS * B_KEY_DIM
B_V_WIDTH = B_HEADS * B_VAL_DIM
B_GATE_RANK = 16
B_GATE_TAU = 16.0
B_CHUNK = 64
EPS = 1e-6
NEG_INF = -1e30

LANES = 128
BF16_SUBLANES = 16

OFF_GATE_A = 0
OFF_GATE_B = OFF_GATE_A + D_MODEL
OFF_QA = OFF_GATE_B + D_MODEL
OFF_KA = OFF_QA + A_Q_WIDTH
OFF_VA = OFF_KA + A_KV_WIDTH
OFF_QB = OFF_VA + A_KV_WIDTH
OFF_KB = OFF_QB + B_QK_WIDTH
OFF_VB = OFF_KB + B_QK_WIDTH
OFF_OBG = OFF_VB + B_V_WIDTH
OFF_GLOW = OFF_OBG + B_V_WIDTH
PROJ_USED = OFF_GLOW + LANES
HALF_V = B_V_WIDTH // 2

VMEM_LIMIT = 60 * 1024 * 1024


def _cparams(sem):
    return pltpu.CompilerParams(dimension_semantics=sem, vmem_limit_bytes=VMEM_LIMIT)


def _const_spec(shape):
    return pl.BlockSpec(shape, lambda *_: (0,) * len(shape), pipeline_mode=pl.Buffered(1))


def _rmsnorm_rows(x, g):
    ms = jnp.mean(x * x, axis=-1, keepdims=True)
    return x * lax.rsqrt(ms + EPS) * g


NORM_ROWS = 512
OUT_NORM_ROWS = 128


def _cast_specs(weights, n_chunks, chunk_of):
    in_specs, out_specs, shapes = [], [], []
    for w in weights:
        rows, rem = divmod(w.shape[0], n_chunks)
        assert rem == 0 and rows % BF16_SUBLANES == 0, (w.shape, n_chunks)
        for specs in (in_specs, out_specs):
            specs.append(pl.BlockSpec((rows, w.shape[1]), lambda *idx: (chunk_of(*idx), 0)))
        shapes.append(jax.ShapeDtypeStruct(w.shape, BF16))
    return in_specs, out_specs, shapes


def _cast_blocks(in_refs, out_refs):
    for src, dst in zip(in_refs, out_refs):
        dst[...] = src[...].astype(dst.dtype)


def _normalize_rows(x_ref, g_ref, u_ref):
    g = g_ref[...]

    def body(c, carry):
        rows = pl.ds(pl.multiple_of(c * NORM_ROWS, NORM_ROWS), NORM_ROWS)
        u_ref[rows, :] = _rmsnorm_rows(x_ref[rows, :], g).astype(BF16)
        return carry

    lax.fori_loop(0, x_ref.shape[0] // NORM_ROWS, body, 0)


def _project(u_ref, w_ref, o_ref):
    o_ref[...] = lax.dot_general(u_ref[...], w_ref[...], (((1,), (1,)), ((), ())),
                                 preferred_element_type=F32).astype(o_ref.dtype)


_N_MIX = OFF_OBG - OFF_QA
_FEATURE_RUNS = (
    (OFF_GATE_A, _N_MIX + B_GATE_RANK + B_V_WIDTH, 2 * D_MODEL),
    (OFF_QA, 0, _N_MIX),
    (OFF_OBG, _N_MIX + B_GATE_RANK, B_V_WIDTH),
    (OFF_GLOW, _N_MIX, B_GATE_RANK),
)
PACK_TILE = 1024
PACK_WINDOW = PACK_TILE + B_GATE_RANK


def _pack_plan(n_native):
    plan = []
    for tile in range(pl.cdiv(PROJ_USED, PACK_TILE)):
        lo, hi = tile * PACK_TILE, (tile + 1) * PACK_TILE
        pieces = []
        for dst, src, n in _FEATURE_RUNS:
            a, b = max(lo, dst), min(hi, dst + n)
            if a < b:
                pieces.append((a - lo, src + a - dst, b - a))
        start = min(min(p[1] for p in pieces), n_native - PACK_WINDOW)
        assert all(start <= s and s + n <= start + PACK_WINDOW for _, s, n in pieces), (tile, pieces)
        assert start % BF16_SUBLANES == 0 and all(d % BF16_SUBLANES == 0 and (s - start) % BF16_SUBLANES == 0
                                                   for d, s, _ in pieces)
        plan.append((start, [(d, s - start, n) for d, s, n in pieces]))
    return plan


def _inproj_head_kernel(plan, x_ref, g_ref, w_ref, o_ref, wt_ref, u_ref):
    j = pl.program_id(0)
    pl.when(j == 0)(lambda: _normalize_rows(x_ref, g_ref, u_ref))
    for tile, (_, pieces) in enumerate(plan):
        @pl.when(j == tile)
        def _(pieces=pieces):
            covered = 0
            for dst, src, n in sorted(pieces):
                assert dst == covered
                wt_ref[dst:dst + n, :] = w_ref[src:src + n, :].astype(BF16)
                covered += n
            if covered < PACK_TILE:
                wt_ref[covered:, :] = jnp.zeros((PACK_TILE - covered, wt_ref.shape[1]), BF16)
    _project(u_ref, wt_ref, o_ref)


def _inproj_tail_kernel(x_ref, g_ref, w_ref, _, o_ref, u_ref):
    pl.when(pl.program_id(1) == 0)(lambda: _normalize_rows(x_ref, g_ref, u_ref))
    _project(u_ref, w_ref, o_ref)


def _inproj(x2, g, w_t, tm, tn):
    t, d = x2.shape
    n_native = w_t.shape[0]
    plan = _pack_plan(n_native)
    n = len(plan) * PACK_TILE
    assert n % tn == 0 and t % tm == 0

    def window_start(j):
        units = sum(jnp.where(j == tile, start // BF16_SUBLANES, 0) for tile, (start, _) in enumerate(plan))
        return units * BF16_SUBLANES

    proj, w_p = pl.pallas_call(
        functools.partial(_inproj_head_kernel, plan),
        grid=(len(plan),),
        in_specs=[
            pl.BlockSpec((tm, d), lambda j: (0, 0), pipeline_mode=pl.Buffered(1)),
            _const_spec(g.shape),
            pl.BlockSpec((pl.Element(PACK_WINDOW), pl.Element(d)), lambda j: (window_start(j), 0)),
        ],
        out_specs=[pl.BlockSpec((tm, PACK_TILE), lambda j: (0, j)),
                   pl.BlockSpec((PACK_TILE, d), lambda j: (j, 0))],
        out_shape=[jax.ShapeDtypeStruct((t, n), BF16), jax.ShapeDtypeStruct((n, d), BF16)],
        scratch_shapes=[pltpu.VMEM((tm, d), BF16)],
        compiler_params=_cparams(("arbitrary",)),
        name="inproj_head",
    )(x2, g, w_t)

    return pl.pallas_call(
        _inproj_tail_kernel,
        grid=(t // tm - 1, n // tn),
        in_specs=[
            pl.BlockSpec((tm, d), lambda i, j: (i + 1, 0)),
            pl.BlockSpec((1, d), lambda i, j: (0, 0)),
            pl.BlockSpec((tn, d), lambda i, j: (j, 0)),
            pl.BlockSpec(memory_space=pl.ANY),
        ],
        out_specs=pl.BlockSpec((tm, tn), lambda i, j: (i + 1, j)),
        out_shape=jax.ShapeDtypeStruct((t, n), BF16),
        input_output_aliases={3: 0},
        scratch_shapes=[pltpu.VMEM((tm, d), BF16)],
        compiler_params=_cparams(("parallel", "arbitrary")),
        name="inproj_tail",
    )(x2, g, w_p, proj)


def _bucket_starts():
    max_exact = N_BUCKETS // 2
    d = np.arange(WINDOW)
    large = max_exact + (np.log(np.maximum(d, 1).astype(np.float32) / max_exact)
                         / math.log(MAX_DISTANCE / max_exact)
                         * (N_BUCKETS - max_exact)).astype(np.int32)
    bucket = np.where(d < max_exact, d, np.minimum(large, N_BUCKETS - 1))
    starts = []
    for b in range(N_BUCKETS):
        hit = np.nonzero(bucket == b)[0]
        if hit.size:
            assert np.all(np.diff(hit) == 1)
            starts.append((b, int(hit[0])))
    return starts


HEADS_PER_TILE = LANES // A_HEAD_DIM
SWA_STAGES, GLA_STAGES = 3, 4
SWA_BLOCKS_PER_STEP = 4


def _swa_body(q_ref, kp_ref, kc_ref, vp_ref, vc_ref, sink_ref, rb_ref, o_ref, bias_ref):
    n_keys = 2 * A_BLOCK

    @pl.when((pl.program_id(0) == 0) & (pl.program_id(1) == 0))
    def _():
        row = lax.broadcasted_iota(jnp.int32, (A_BLOCK, n_keys), 0)
        col = lax.broadcasted_iota(jnp.int32, (A_BLOCK, n_keys), 1)
        dist = row + A_BLOCK - col
        band = (dist >= 0) & (dist < WINDOW)
        starts = _bucket_starts()
        for h in range(A_HEADS):
            val = jnp.full(dist.shape, rb_ref[h, starts[0][0]], F32)
            for b, s in starts[1:]:
                val = jnp.where(dist >= s, rb_ref[h, b], val)
            val = jnp.where(band, val, NEG_INF)
            sink = sink_ref[0, h]
            bias_ref[0, h] = jnp.where(col == 0, sink, val)
            bias_ref[1, h] = jnp.where(col == 0, sink, jnp.where(col >= A_BLOCK, val, NEG_INF))

    lane = lax.broadcasted_iota(jnp.int32, (1, LANES), 1)
    scale = A_HEAD_DIM ** -0.5
    q_keep = (jnp.where(lane < A_HEAD_DIM, scale, 0.0).astype(BF16),
              jnp.where(lane < A_HEAD_DIM, 0.0, scale).astype(BF16))
    lower_lanes = lax.broadcasted_iota(jnp.int32, (A_BLOCK, LANES), 1) < A_HEAD_DIM
    key0 = lax.broadcasted_iota(jnp.int32, (n_keys, LANES), 0) == 0
    ones = jnp.ones((n_keys, LANES), BF16)

    def attend(q_rows, prev, cur, first):
        def both_blocks(which, tile):
            cols = slice(tile * LANES, (tile + 1) * LANES)
            cat = jnp.concatenate([prev[which][:, cols], cur[which][:, cols]], axis=0).astype(F32)
            cat = jnp.where(key0, 0.0, cat)
            return cat.astype(BF16), pltpu.roll(cat, A_HEAD_DIM, 1).astype(BF16)

        stacks = []
        for tile in range(A_KV_WIDTH // LANES):
            k_cat, k_swp = both_blocks(0, tile)
            v_cat, v_swp = both_blocks(1, tile)
            q0 = tile * A_GROUP
            stacks.append((k_cat, jnp.concatenate([v_cat, ones], axis=1),
                           [(q0, 0), (q0 + 1, 0), (q0 + 2, 1), (q0 + 3, 1)]))
            stacks.append((k_swp, jnp.concatenate([v_swp, ones], axis=1),
                           [(q0, 1), (q0 + 1, 1), (q0 + 2, 0), (q0 + 3, 0)]))

        scores = []
        for k_tile, _, members in stacks:
            q4 = jnp.concatenate(
                [q_ref[q_rows, qt * LANES:(qt + 1) * LANES] * q_keep[half] for qt, half in members],
                axis=0)
            scores.append(lax.dot_general(q4, k_tile, (((1,), (1,)), ((), ())),
                                          preferred_element_type=F32))
        yield
        s = jnp.concatenate(scores, axis=0)
        s = s + jnp.concatenate(
            [bias_ref[first, qt * HEADS_PER_TILE + half]
             for _, _, members in stacks for qt, half in members], axis=0)
        p = jnp.exp(s - jnp.max(s, axis=-1, keepdims=True)).astype(BF16)
        yield

        normed = {}
        rows_per_stack = len(stacks[0][2]) * A_BLOCK
        for i, (_, v_ones, members) in enumerate(stacks):
            ov = jnp.dot(p[i * rows_per_stack:(i + 1) * rows_per_stack], v_ones,
                         preferred_element_type=F32)
            o = ov[:, :LANES] / ov[:, LANES:]
            for j, member in enumerate(members):
                normed[member] = o[j * A_BLOCK:(j + 1) * A_BLOCK]
        for qt in range(A_Q_WIDTH // LANES):
            o_ref[q_rows, qt * LANES:(qt + 1) * LANES] = jnp.where(
                lower_lanes, normed[(qt, 0)], normed[(qt, 1)]).astype(o_ref.dtype)

    blocks_per_step = q_ref.shape[0] // A_BLOCK
    kv_prev = (kp_ref[...], vp_ref[...])
    stages = []
    for sub in range(blocks_per_step):
        rows = slice(sub * A_BLOCK, (sub + 1) * A_BLOCK)
        kv_cur = (kc_ref[rows, :], vc_ref[rows, :])
        first = (pl.program_id(1) == 0).astype(jnp.int32) if sub == 0 else 0
        stages.append(attend(rows, kv_prev, kv_cur, first))
        kv_prev = kv_cur
    for _ in range(SWA_STAGES):
        for stage in stages:
            next(stage, None)
        yield


GLA_CHUNKS_PER_STEP = 8
GLA_PREFIX_GROUP = 1


def _split3(x):
    hi = x.astype(BF16)
    r1 = x - hi.astype(F32)
    mid = r1.astype(BF16)
    lo = (r1 - mid.astype(F32)).astype(BF16)
    return jnp.concatenate([hi, mid, lo], axis=0)


def _gla_body(q_ref, k_ref, v0_ref, v1_ref, gl_ref, og0_ref, og1_ref, wgu_ref, bg_ref, ng_ref,
              o_ref, s_ref):
    c = B_CHUNK
    n_chunks = GLA_CHUNKS_PER_STEP
    n_rows = n_chunks * c
    heads_per_half = HALF_V // B_VAL_DIM
    v_refs = (v0_ref, v1_ref)
    og_refs = (og0_ref, og1_ref)
    chunk_rows = [slice(j * c, (j + 1) * c) for j in range(n_chunks)]
    key_cols = [slice(h * B_KEY_DIM, (h + 1) * B_KEY_DIM) for h in range(B_HEADS)]
    units = [(j, h) for j in range(n_chunks) for h in range(B_HEADS)]

    def v_of(refs, j, h):
        lo = (h % heads_per_half) * B_VAL_DIM
        rows = slice(None) if j is None else chunk_rows[j]
        return refs[h // heads_per_half][rows, lo:lo + B_VAL_DIM]

    @pl.when(pl.program_id(1) == 0)
    def _():
        s_ref[...] = jnp.zeros_like(s_ref)

    rank = wgu_ref.shape[0]
    wgu = jnp.concatenate([wgu_ref[...].astype(BF16),
                           jnp.zeros((gl_ref.shape[1] - rank, wgu_ref.shape[1]), BF16)], axis=0)
    glin = jnp.dot(gl_ref[...], wgu, preferred_element_type=F32) + bg_ref[...]
    log_a = (jnp.minimum(glin, 0.0) - jnp.log(1.0 + jnp.exp(-jnp.abs(glin)))) / B_GATE_TAU

    g_rows = GLA_PREFIX_GROUP * c
    ri = lax.broadcasted_iota(jnp.int32, (g_rows, 3 * g_rows), 0)
    ci = lax.broadcasted_iota(jnp.int32, (g_rows, 3 * g_rows), 1)
    ci = ci - jnp.where(ci >= g_rows, g_rows, 0) - jnp.where(ci >= 2 * g_rows, g_rows, 0)
    shift = int(math.log2(c))
    same_chunk = lax.shift_right_logical(ri, shift) == lax.shift_right_logical(ci, shift)
    tri3 = ((ri >= ci) & same_chunk).astype(BF16)
    b = jnp.concatenate(
        [jnp.dot(tri3, _split3(log_a[r0:r0 + g_rows]), preferred_element_type=F32)
         for r0 in range(0, n_rows, g_rows)], axis=0)
    last_rows = [b[(j + 1) * c - 1:(j + 1) * c, :] for j in range(n_chunks)]
    b_last = jnp.concatenate([jnp.broadcast_to(r, (c, b.shape[1])) for r in last_rows], axis=0)

    qf = q_ref[...].astype(F32) * (B_KEY_DIM ** -0.5)
    kf = k_ref[...].astype(F32)
    q_dec = (qf * jnp.exp(b)).astype(BF16)
    k_dec = (kf * jnp.exp(-b)).astype(BF16)
    k_state = kf * jnp.exp(b_last - b)
    sublanes = 8
    pad = [jnp.zeros((sublanes - n_chunks, b.shape[1]), F32)] if n_chunks < sublanes else []
    decay_rows = jnp.exp(jnp.concatenate(last_rows + pad, axis=0))

    yield
    ri = lax.broadcasted_iota(jnp.int32, (c, c), 0)
    ci = lax.broadcasted_iota(jnp.int32, (c, c), 1)
    causal = ri >= ci
    att = {}
    for j, h in units:
        a = lax.dot_general(q_dec[chunk_rows[j], key_cols[h]], k_dec[chunk_rows[j], key_cols[h]],
                            (((1,), (1,)), ((), ())), preferred_element_type=F32)
        att[j, h] = jnp.where(causal, a, 0.0).astype(BF16)
    o_intra = {u: jnp.dot(att[u], v_of(v_refs, *u), preferred_element_type=F32) for u in units}
    ds = {(j, h): jnp.dot(k_state[chunk_rows[j], key_cols[h]].T.astype(BF16), v_of(v_refs, j, h),
                          preferred_element_type=F32) for j, h in units}

    yield
    entering = {}
    for h in range(B_HEADS):
        decay_t = decay_rows[:, key_cols[h]].T
        state = s_ref[h]
        for j in range(n_chunks):
            entering[j, h] = state.astype(BF16)
            state = decay_t[:, j:j + 1] * state + ds[j, h]
        s_ref[h] = state
    o_inter = {(j, h): jnp.dot(q_dec[chunk_rows[j], key_cols[h]], entering[j, h],
                               preferred_element_type=F32) for j, h in units}

    yield
    ng = ng_ref[...]
    for h in range(B_HEADS):
        o = jnp.concatenate([o_intra[j, h] + o_inter[j, h] for j in range(n_chunks)], axis=0)
        gate = v_of(og_refs, None, h).astype(F32)
        y = _rmsnorm_rows(o, ng) * (gate * jax.nn.sigmoid(gate))
        o_ref[:, h * B_VAL_DIM:(h + 1) * B_VAL_DIM] = y.astype(o_ref.dtype)


N_MIX_IN = 7


def _mixers_kernel(n_cast, *refs):
    mix_ref, prev_ref, sink_ref, rb_ref, wgu_ref, bg_ref, ng_ref = refs[:N_MIX_IN]
    cast_in, refs = refs[N_MIX_IN:N_MIX_IN + n_cast], refs[N_MIX_IN + n_cast:]
    attn_ref, gla_ref = refs[:2]
    cast_out, (bias_ref, s_ref) = refs[2:2 + n_cast], refs[2 + n_cast:]
    _cast_blocks(cast_in, cast_out)

    def cols(off, width):
        return mix_ref.at[:, off - OFF_QA:off - OFF_QA + width]

    kv_prev = [prev_ref.at[:, off - OFF_KA:off - OFF_KA + A_KV_WIDTH] for off in (OFF_KA, OFF_VA)]
    swa = _swa_body(cols(OFF_QA, A_Q_WIDTH), kv_prev[0], cols(OFF_KA, A_KV_WIDTH),
                    kv_prev[1], cols(OFF_VA, A_KV_WIDTH), sink_ref, rb_ref, attn_ref, bias_ref)
    gla = _gla_body(cols(OFF_QB, B_QK_WIDTH), cols(OFF_KB, B_QK_WIDTH),
                    cols(OFF_VB, HALF_V), cols(OFF_VB + HALF_V, HALF_V), cols(OFF_GLOW, LANES),
                    cols(OFF_OBG, HALF_V), cols(OFF_OBG + HALF_V, HALF_V),
                    wgu_ref, bg_ref, ng_ref, gla_ref, s_ref)
    for body in (swa,) * SWA_STAGES + (gla,) * GLA_STAGES:
        next(body, None)


def _mixers(proj, sinks, rel_bias, wgu, bg, ng, cast_weights, batch, seq):
    t = proj.shape[0]
    rows = GLA_CHUNKS_PER_STEP * B_CHUNK
    assert rows == SWA_BLOCKS_PER_STEP * A_BLOCK and seq % rows == 0
    steps = seq // rows
    rb = lambda b, s: b * steps + s
    prev = lambda b, s: jnp.maximum(rb(b, s) * SWA_BLOCKS_PER_STEP - 1, 0)
    smem = functools.partial(pl.BlockSpec, memory_space=pltpu.SMEM)
    cast_in, cast_out, cast_shapes = _cast_specs(cast_weights, batch * steps, rb)
    kv_width = OFF_QB - OFF_KA
    assert OFF_KA % kv_width == 0 and OFF_VA == OFF_KA + A_KV_WIDTH
    mix_specs = [
        pl.BlockSpec((pl.Element(rows), pl.Element(PROJ_USED - OFF_QA)),
                     lambda b, s: (rb(b, s) * rows, OFF_QA)),
        pl.BlockSpec((A_BLOCK, kv_width), lambda b, s: (prev(b, s), OFF_KA // kv_width)),
        smem(), smem(), _const_spec(wgu.shape), _const_spec(bg.shape), _const_spec(ng.shape)]
    assert len(mix_specs) == N_MIX_IN
    out_block = lambda width: pl.BlockSpec((rows, width), lambda b, s: (rb(b, s), 0))
    outs = pl.pallas_call(
        functools.partial(_mixers_kernel, len(cast_weights)),
        grid=(batch, steps),
        in_specs=mix_specs + cast_in,
        out_specs=[out_block(A_Q_WIDTH), out_block(B_V_WIDTH)] + cast_out,
        out_shape=[jax.ShapeDtypeStruct((t, A_Q_WIDTH), BF16),
                   jax.ShapeDtypeStruct((t, B_V_WIDTH), BF16)] + cast_shapes,
        scratch_shapes=[pltpu.VMEM((2, A_HEADS, A_BLOCK, 2 * A_BLOCK), F32),
                        pltpu.VMEM((B_HEADS, B_KEY_DIM, B_VAL_DIM), F32)],
        compiler_params=_cparams(("arbitrary", "arbitrary")),
        name="mixers",
    )(proj, proj, sinks, rel_bias.T, wgu, bg, ng, *cast_weights)
    return outs[0], outs[1], outs[2:]


def _merge_kernel(a_ref, b_ref, gates_ref, x_ref, wa_ref, wb_ref, wo_ref, gz_ref, h_ref, z_ref):
    d = x_ref.shape[1]
    ya = jnp.dot(a_ref[...], wa_ref[...], preferred_element_type=F32)
    yb = jnp.dot(b_ref[...], wb_ref[...], preferred_element_type=F32)
    merged = (jax.nn.sigmoid(gates_ref[:, :d].astype(F32)) * ya
              + jax.nn.sigmoid(gates_ref[:, d:].astype(F32)) * yb)
    h = x_ref[...] + jnp.dot(merged.astype(BF16), wo_ref[...], preferred_element_type=F32)
    h_ref[...] = h
    z_ref[...] = _rmsnorm_rows(h, gz_ref[...]).astype(z_ref.dtype)


def _merge(attn, gla, proj, x2, wa, wb, wo, gz, tm):
    t, d = x2.shape
    row_block = pl.BlockSpec((tm, d), lambda i: (i, 0))
    return pl.pallas_call(
        _merge_kernel,
        grid=(t // tm,),
        in_specs=[
            pl.BlockSpec((tm, A_Q_WIDTH), lambda i: (i, 0)),
            pl.BlockSpec((tm, B_V_WIDTH), lambda i: (i, 0)),
            pl.BlockSpec((tm, 2 * d), lambda i: (i, OFF_GATE_A // (2 * d))),
            row_block,
            _const_spec(wa.shape), _const_spec(wb.shape), _const_spec(wo.shape), _const_spec(gz.shape),
        ],
        out_specs=[row_block, row_block],
        out_shape=[jax.ShapeDtypeStruct((t, d), F32), jax.ShapeDtypeStruct((t, d), BF16)],
        compiler_params=_cparams(("parallel",)),
        name="merge",
    )(attn, gla, proj, x2, wa, wb, wo, gz)


def _ffn_kernel(z_ref, h_ref, wg_ref, wu_ref, wd_ref, gf_ref, o_ref):
    f = pl.program_id(1)
    z = z_ref[...]
    half = wg_ref.shape[1] // 2
    halves = [slice(c * half, (c + 1) * half) for c in range(2)]
    gu = [(jnp.dot(z, wg_ref[:, cols], preferred_element_type=F32),
           jnp.dot(z, wu_ref[:, cols], preferred_element_type=F32)) for cols in halves]
    for c, cols in enumerate(halves):
        g, u = gu[c]
        act = (g * jax.nn.sigmoid(g) * u).astype(BF16)
        acc = jnp.where(f == 0, 0.0, o_ref[...]) if c == 0 else o_ref[...]
        o_ref[...] = acc + jnp.dot(act, wd_ref[cols, :], preferred_element_type=F32)

    @pl.when(f == pl.num_programs(1) - 1)
    def _():
        gf = gf_ref[...]

        def body(c, carry):
            rows = pl.ds(pl.multiple_of(c * OUT_NORM_ROWS, OUT_NORM_ROWS), OUT_NORM_ROWS)
            o_ref[rows, :] = _rmsnorm_rows(h_ref[rows, :] + o_ref[rows, :], gf)
            return carry

        lax.fori_loop(0, h_ref.shape[0] // OUT_NORM_ROWS, body, 0)


def _ffn(z, h, wg, wu, wd, gf, tm, tf):
    t, d = h.shape
    f = wg.shape[1]
    row_block = pl.BlockSpec((tm, d), lambda i, j: (i, 0))
    return pl.pallas_call(
        _ffn_kernel,
        grid=(t // tm, f // tf),
        in_specs=[
            row_block, row_block,
            pl.BlockSpec((d, tf), lambda i, j: (0, j)),
            pl.BlockSpec((d, tf), lambda i, j: (0, j)),
            pl.BlockSpec((tf, d), lambda i, j: (j, 0)),
            pl.BlockSpec((1, d), lambda i, j: (0, 0)),
        ],
        out_specs=row_block,
        out_shape=jax.ShapeDtypeStruct((t, d), F32),
        compiler_params=_cparams(("parallel", "arbitrary")),
        name="ffn",
    )(z, h, wg, wu, wd, gf)


def kernel(x, norm_mix_g, w_in, sinks, rel_bias, w_gate_up, b_gate, gla_norm_g, w_proj_a, w_proj_b,
           w_out, norm_ffn_g, w_ffn_gate, w_ffn_up, w_ffn_down, norm_final_g):
    batch, seq, d = x.shape
    assert d == D_MODEL and w_in.shape[0] == 1, "single-layer geometry"
    t = batch * seq
    x2 = x.reshape(t, d)

    proj = _inproj(x2, norm_mix_g, w_in[0].T, tm=1024, tn=2304)

    later_weights = (w_proj_a[0], w_proj_b[0], w_out[0], w_ffn_gate[0], w_ffn_up[0], w_ffn_down[0])
    attn, gla, (wa, wb, wo, wg, wu, wd) = _mixers(proj, sinks, rel_bias, w_gate_up[0], b_gate, gla_norm_g,
                                                  later_weights, batch, seq)

    h, z = _merge(attn, gla, proj, x2, wa, wb, wo, norm_ffn_g, tm=512)

    out = _ffn(z, h, wg, wu, wd, norm_final_g.reshape(1, d), tm=1024, tf=512)
    return out.reshape(batch, seq, d)
```

```python
import functools
import math

import numpy as np
import jax
import jax.numpy as jnp
from jax import lax
from jax.experimental import pallas as pl
from jax.experimental.pallas import tpu as pltpu

F32 = jnp.float32
BF16 = jnp.bfloat16

D_MODEL = 2048
A_HEADS = 16
A_KV_HEADS = 4
A_HEAD_DIM = 64
A_GROUP = A_HEADS // A_KV_HEADS
WINDOW = 128
A_BLOCK = 128
A_Q_WIDTH = A_HEADS * A_HEAD_DIM
A_KV_WIDTH = A_KV_HEADS * A_HEAD_DIM
N_BUCKETS = 32
MAX_DISTANCE = 128
B_HEADS = 4
B_KEY_DIM = 128
B_VAL_DIM = 256
B_QK_WIDTH = B_HEADS * B_KEY_DIM
B_V_WIDTH = B_HEADS * B_VAL_DIM
B_GATE_RANK = 16
B_GATE_TAU = 16.0
B_CHUNK = 64
EPS = 1e-6
NEG_INF = -1e30

LANES = 128
BF16_SUBLANES = 16

OFF_GATE_A = 0
OFF_GATE_B = OFF_GATE_A + D_MODEL
OFF_QA = OFF_GATE_B + D_MODEL
OFF_KA = OFF_QA + A_Q_WIDTH
OFF_VA = OFF_KA + A_KV_WIDTH
OFF_QB = OFF_VA + A_KV_WIDTH
OFF_KB = OFF_QB + B_QK_WIDTH
OFF_VB = OFF_KB + B_QK_WIDTH
OFF_OBG = OFF_VB + B_V_WIDTH
OFF_GLOW = OFF_OBG + B_V_WIDTH
PROJ_USED = OFF_GLOW + LANES
HALF_V = B_V_WIDTH // 2

VMEM_LIMIT = 60 * 1024 * 1024


def _cparams(sem):
    return pltpu.CompilerParams(dimension_semantics=sem, vmem_limit_bytes=VMEM_LIMIT)


def _const_spec(shape):
    return pl.BlockSpec(shape, lambda *_: (0,) * len(shape), pipeline_mode=pl.Buffered(1))


def _rmsnorm_rows(x, g):
    ms = jnp.mean(x * x, axis=-1, keepdims=True)
    return x * lax.rsqrt(ms + EPS) * g


NORM_ROWS = 512
OUT_NORM_ROWS = 128


def _cast_specs(weights, n_chunks, chunk_of):
    in_specs, out_specs, shapes = [], [], []
    for w in weights:
        rows, rem = divmod(w.shape[0], n_chunks)
        assert rem == 0 and rows % BF16_SUBLANES == 0, (w.shape, n_chunks)
        for specs in (in_specs, out_specs):
            specs.append(pl.BlockSpec((rows, w.shape[1]), lambda *idx: (chunk_of(*idx), 0)))
        shapes.append(jax.ShapeDtypeStruct(w.shape, BF16))
    return in_specs, out_specs, shapes


def _cast_blocks(in_refs, out_refs):
    for src, dst in zip(in_refs, out_refs):
        dst[...] = src[...].astype(dst.dtype)


def _normalize_rows(x_ref, g_ref, u_ref):
    g = g_ref[...]

    def body(c, carry):
        rows = pl.ds(pl.multiple_of(c * NORM_ROWS, NORM_ROWS), NORM_ROWS)
        u_ref[rows, :] = _rmsnorm_rows(x_ref[rows, :], g).astype(BF16)
        return carry

    lax.fori_loop(0, x_ref.shape[0] // NORM_ROWS, body, 0)


def _project(u_ref, w_ref, o_ref):
    o_ref[...] = lax.dot_general(u_ref[...], w_ref[...], (((1,), (1,)), ((), ())),
                                 preferred_element_type=F32).astype(o_ref.dtype)


_N_MIX = OFF_OBG - OFF_QA
_FEATURE_RUNS = (
    (OFF_GATE_A, _N_MIX + B_GATE_RANK + B_V_WIDTH, 2 * D_MODEL),
    (OFF_QA, 0, _N_MIX),
    (OFF_OBG, _N_MIX + B_GATE_RANK, B_V_WIDTH),
    (OFF_GLOW, _N_MIX, B_GATE_RANK),
)
PACK_TILE = 1024
PACK_WINDOW = PACK_TILE + B_GATE_RANK


def _pack_plan(n_native):
    plan = []
    for tile in range(pl.cdiv(PROJ_USED, PACK_TILE)):
        lo, hi = tile * PACK_TILE, (tile + 1) * PACK_TILE
        pieces = []
        for dst, src, n in _FEATURE_RUNS:
            a, b = max(lo, dst), min(hi, dst + n)
            if a < b:
                pieces.append((a - lo, src + a - dst, b - a))
        start = min(min(p[1] for p in pieces), n_native - PACK_WINDOW)
        assert all(start <= s and s + n <= start + PACK_WINDOW for _, s, n in pieces), (tile, pieces)
        assert start % BF16_SUBLANES == 0 and all(d % BF16_SUBLANES == 0 and (s - start) % BF16_SUBLANES == 0
                                                   for d, s, _ in pieces)
        plan.append((start, [(d, s - start, n) for d, s, n in pieces]))
    return plan


def _inproj_head_kernel(plan, x_ref, g_ref, w_ref, o_ref, wt_ref, u_ref):
    j = pl.program_id(0)
    pl.when(j == 0)(lambda: _normalize_rows(x_ref, g_ref, u_ref))
    for tile, (_, pieces) in enumerate(plan):
        @pl.when(j == tile)
        def _(pieces=pieces):
            covered = 0
            for dst, src, n in sorted(pieces):
                assert dst == covered
                wt_ref[dst:dst + n, :] = w_ref[src:src + n, :].astype(BF16)
                covered += n
            if covered < PACK_TILE:
                wt_ref[covered:, :] = jnp.zeros((PACK_TILE - covered, wt_ref.shape[1]), BF16)
    _project(u_ref, wt_ref, o_ref)


def _inproj_tail_kernel(x_ref, g_ref, w_ref, _, o_ref, u_ref):
    pl.when(pl.program_id(1) == 0)(lambda: _normalize_rows(x_ref, g_ref, u_ref))
    _project(u_ref, w_ref, o_ref)


def _inproj(x2, g, w_t, tm, tn):
    t, d = x2.shape
    n_native = w_t.shape[0]
    plan = _pack_plan(n_native)
    n = len(plan) * PACK_TILE
    assert n % tn == 0 and t % tm == 0

    def window_start(j):
        units = sum(jnp.where(j == tile, start // BF16_SUBLANES, 0) for tile, (start, _) in enumerate(plan))
        return units * BF16_SUBLANES

    proj, w_p = pl.pallas_call(
        functools.partial(_inproj_head_kernel, plan),
        grid=(len(plan),),
        in_specs=[
            pl.BlockSpec((tm, d), lambda j: (0, 0), pipeline_mode=pl.Buffered(1)),
            _const_spec(g.shape),
            pl.BlockSpec((pl.Element(PACK_WINDOW), pl.Element(d)), lambda j: (window_start(j), 0)),
        ],
        out_specs=[pl.BlockSpec((tm, PACK_TILE), lambda j: (0, j)),
                   pl.BlockSpec((PACK_TILE, d), lambda j: (j, 0))],
        out_shape=[jax.ShapeDtypeStruct((t, n), BF16), jax.ShapeDtypeStruct((n, d), BF16)],
        scratch_shapes=[pltpu.VMEM((tm, d), BF16)],
        compiler_params=_cparams(("arbitrary",)),
        name="inproj_head",
    )(x2, g, w_t)

    return pl.pallas_call(
        _inproj_tail_kernel,
        grid=(t // tm - 1, n // tn),
        in_specs=[
            pl.BlockSpec((tm, d), lambda i, j: (i + 1, 0)),
            pl.BlockSpec((1, d), lambda i, j: (0, 0)),
            pl.BlockSpec((tn, d), lambda i, j: (j, 0)),
            pl.BlockSpec(memory_space=pl.ANY),
        ],
        out_specs=pl.BlockSpec((tm, tn), lambda i, j: (i + 1, j)),
        out_shape=jax.ShapeDtypeStruct((t, n), BF16),
        input_output_aliases={3: 0},
        scratch_shapes=[pltpu.VMEM((tm, d), BF16)],
        compiler_params=_cparams(("parallel", "arbitrary")),
        name="inproj_tail",
    )(x2, g, w_p, proj)


def _bucket_starts():
    max_exact = N_BUCKETS // 2
    d = np.arange(WINDOW)
    large = max_exact + (np.log(np.maximum(d, 1).astype(np.float32) / max_exact)
                         / math.log(MAX_DISTANCE / max_exact)
                         * (N_BUCKETS - max_exact)).astype(np.int32)
    bucket = np.where(d < max_exact, d, np.minimum(large, N_BUCKETS - 1))
    starts = []
    for b in range(N_BUCKETS):
        hit = np.nonzero(bucket == b)[0]
        if hit.size:
            assert np.all(np.diff(hit) == 1)
            starts.append((b, int(hit[0])))
    return starts


HEADS_PER_TILE = LANES // A_HEAD_DIM
SWA_STAGES = 3
SWA_BLOCKS_PER_STEP = 4


def _swa_body(q_ref, kp_ref, kc_ref, vp_ref, vc_ref, sink_ref, rb_ref, o_ref, bias_ref):
    n_keys = 2 * A_BLOCK

    @pl.when((pl.program_id(0) == 0) & (pl.program_id(1) == 0))
    def _():
        row = lax.broadcasted_iota(jnp.int32, (A_BLOCK, n_keys), 0)
        col = lax.broadcasted_iota(jnp.int32, (A_BLOCK, n_keys), 1)
        dist = row + A_BLOCK - col
        band = (dist >= 0) & (dist < WINDOW)
        starts = _bucket_starts()
        for h in range(A_HEADS):
            val = jnp.full(dist.shape, rb_ref[h, starts[0][0]], F32)
            for b, s in starts[1:]:
                val = jnp.where(dist >= s, rb_ref[h, b], val)
            val = jnp.where(band, val, NEG_INF)
            sink = sink_ref[0, h]
            bias_ref[0, h] = jnp.where(col == 0, sink, val)
            bias_ref[1, h] = jnp.where(col == 0, sink, jnp.where(col >= A_BLOCK, val, NEG_INF))

    lane = lax.broadcasted_iota(jnp.int32, (1, LANES), 1)
    scale = A_HEAD_DIM ** -0.5
    q_keep = (jnp.where(lane < A_HEAD_DIM, scale, 0.0).astype(BF16),
              jnp.where(lane < A_HEAD_DIM, 0.0, scale).astype(BF16))
    lower_lanes = lax.broadcasted_iota(jnp.int32, (A_BLOCK, LANES), 1) < A_HEAD_DIM
    key0 = lax.broadcasted_iota(jnp.int32, (n_keys, LANES), 0) == 0
    ones = jnp.ones((n_keys, LANES), BF16)

    def attend(q_rows, prev, cur, first):
        def both_blocks(which, tile):
            cols = slice(tile * LANES, (tile + 1) * LANES)
            cat = jnp.concatenate([prev[which][:, cols], cur[which][:, cols]], axis=0).astype(F32)
            cat = jnp.where(key0, 0.0, cat)
            return cat.astype(BF16), pltpu.roll(cat, A_HEAD_DIM, 1).astype(BF16)

        stacks = []
        for tile in range(A_KV_WIDTH // LANES):
            k_cat, k_swp = both_blocks(0, tile)
            v_cat, v_swp = both_blocks(1, tile)
            q0 = tile * A_GROUP
            stacks.append((k_cat, jnp.concatenate([v_cat, ones], axis=1),
                           [(q0, 0), (q0 + 1, 0), (q0 + 2, 1), (q0 + 3, 1)]))
            stacks.append((k_swp, jnp.concatenate([v_swp, ones], axis=1),
                           [(q0, 1), (q0 + 1, 1), (q0 + 2, 0), (q0 + 3, 0)]))

        scores = []
        for k_tile, _, members in stacks:
            q4 = jnp.concatenate(
                [q_ref[q_rows, qt * LANES:(qt + 1) * LANES] * q_keep[half] for qt, half in members],
                axis=0)
            scores.append(lax.dot_general(q4, k_tile, (((1,), (1,)), ((), ())),
                                          preferred_element_type=F32))
        yield
        s = jnp.concatenate(scores, axis=0)
        s = s + jnp.concatenate(
            [bias_ref[first, qt * HEADS_PER_TILE + half]
             for _, _, members in stacks for qt, half in members], axis=0)
        p = jnp.exp(s - jnp.max(s, axis=-1, keepdims=True)).astype(BF16)
        yield

        normed = {}
        rows_per_stack = len(stacks[0][2]) * A_BLOCK
        for i, (_, v_ones, members) in enumerate(stacks):
            ov = jnp.dot(p[i * rows_per_stack:(i + 1) * rows_per_stack], v_ones,
                         preferred_element_type=F32)
            o = ov[:, :LANES] / ov[:, LANES:]
            for j, member in enumerate(members):
                normed[member] = o[j * A_BLOCK:(j + 1) * A_BLOCK]
        for qt in range(A_Q_WIDTH // LANES):
            o_ref[q_rows, qt * LANES:(qt + 1) * LANES] = jnp.where(
                lower_lanes, normed[(qt, 0)], normed[(qt, 1)]).astype(o_ref.dtype)

    blocks_per_step = q_ref.shape[0] // A_BLOCK
    kv_prev = (kp_ref[...], vp_ref[...])
    stages = []
    for sub in range(blocks_per_step):
        rows = slice(sub * A_BLOCK, (sub + 1) * A_BLOCK)
        kv_cur = (kc_ref[rows, :], vc_ref[rows, :])
        first = (pl.program_id(1) == 0).astype(jnp.int32) if sub == 0 else 0
        stages.append(attend(rows, kv_prev, kv_cur, first))
        kv_prev = kv_cur
    for _ in range(SWA_STAGES):
        for stage in stages:
            next(stage, None)


GLA_CHUNKS_PER_STEP = 8
GLA_PREFIX_GROUP = 1


def _split3(x):
    hi = x.astype(BF16)
    r1 = x - hi.astype(F32)
    mid = r1.astype(BF16)
    lo = (r1 - mid.astype(F32)).astype(BF16)
    return jnp.concatenate([hi, mid, lo], axis=0)


def _gla_body(q_ref, k_ref, v0_ref, v1_ref, gl_ref, og0_ref, og1_ref, wgu_ref, bg_ref, ng_ref,
              o_ref, s_ref):
    c = B_CHUNK
    n_chunks = GLA_CHUNKS_PER_STEP
    n_rows = n_chunks * c
    heads_per_half = HALF_V // B_VAL_DIM
    v_refs = (v0_ref, v1_ref)
    og_refs = (og0_ref, og1_ref)
    chunk_rows = [slice(j * c, (j + 1) * c) for j in range(n_chunks)]
    key_cols = [slice(h * B_KEY_DIM, (h + 1) * B_KEY_DIM) for h in range(B_HEADS)]
    units = [(j, h) for j in range(n_chunks) for h in range(B_HEADS)]

    def v_of(refs, j, h):
        lo = (h % heads_per_half) * B_VAL_DIM
        rows = slice(None) if j is None else chunk_rows[j]
        return refs[h // heads_per_half][rows, lo:lo + B_VAL_DIM]

    @pl.when(pl.program_id(1) == 0)
    def _():
        s_ref[...] = jnp.zeros_like(s_ref)

    rank = wgu_ref.shape[0]
    wgu = jnp.concatenate([wgu_ref[...].astype(BF16),
                           jnp.zeros((gl_ref.shape[1] - rank, wgu_ref.shape[1]), BF16)], axis=0)
    glin = jnp.dot(gl_ref[...], wgu, preferred_element_type=F32) + bg_ref[...]
    log_a = (jnp.minimum(glin, 0.0) - jnp.log(1.0 + jnp.exp(-jnp.abs(glin)))) / B_GATE_TAU

    g_rows = GLA_PREFIX_GROUP * c
    ri = lax.broadcasted_iota(jnp.int32, (g_rows, 3 * g_rows), 0)
    ci = lax.broadcasted_iota(jnp.int32, (g_rows, 3 * g_rows), 1)
    ci = ci - jnp.where(ci >= g_rows, g_rows, 0) - jnp.where(ci >= 2 * g_rows, g_rows, 0)
    shift = int(math.log2(c))
    same_chunk = lax.shift_right_logical(ri, shift) == lax.shift_right_logical(ci, shift)
    tri3 = ((ri >= ci) & same_chunk).astype(BF16)
    b = jnp.concatenate(
        [jnp.dot(tri3, _split3(log_a[r0:r0 + g_rows]), preferred_element_type=F32)
         for r0 in range(0, n_rows, g_rows)], axis=0)
    last_rows = [b[(j + 1) * c - 1:(j + 1) * c, :] for j in range(n_chunks)]
    b_last = jnp.concatenate([jnp.broadcast_to(r, (c, b.shape[1])) for r in last_rows], axis=0)

    qf = q_ref[...].astype(F32) * (B_KEY_DIM ** -0.5)
    kf = k_ref[...].astype(F32)
    q_dec = (qf * jnp.exp(b)).astype(BF16)
    k_dec = (kf * jnp.exp(-b)).astype(BF16)
    k_state = kf * jnp.exp(b_last - b)
    sublanes = 8
    pad = [jnp.zeros((sublanes - n_chunks, b.shape[1]), F32)] if n_chunks < sublanes else []
    decay_rows = jnp.exp(jnp.concatenate(last_rows + pad, axis=0))

    ri = lax.broadcasted_iota(jnp.int32, (c, c), 0)
    ci = lax.broadcasted_iota(jnp.int32, (c, c), 1)
    causal = ri >= ci
    att = {}
    for j, h in units:
        a = lax.dot_general(q_dec[chunk_rows[j], key_cols[h]], k_dec[chunk_rows[j], key_cols[h]],
                            (((1,), (1,)), ((), ())), preferred_element_type=F32)
        att[j, h] = jnp.where(causal, a, 0.0).astype(BF16)
    o_intra = {u: jnp.dot(att[u], v_of(v_refs, *u), preferred_element_type=F32) for u in units}
    ds = {(j, h): jnp.dot(k_state[chunk_rows[j], key_cols[h]].T.astype(BF16), v_of(v_refs, j, h),
                          preferred_element_type=F32) for j, h in units}

    entering = {}
    for h in range(B_HEADS):
        decay_t = decay_rows[:, key_cols[h]].T
        state = s_ref[h]
        for j in range(n_chunks):
            entering[j, h] = state.astype(BF16)
            state = decay_t[:, j:j + 1] * state + ds[j, h]
        s_ref[h] = state
    o_inter = {(j, h): jnp.dot(q_dec[chunk_rows[j], key_cols[h]], entering[j, h],
                               preferred_element_type=F32) for j, h in units}

    ng = ng_ref[...]
    for h in range(B_HEADS):
        o = jnp.concatenate([o_intra[j, h] + o_inter[j, h] for j in range(n_chunks)], axis=0)
        gate = v_of(og_refs, None, h).astype(F32)
        y = _rmsnorm_rows(o, ng) * (gate * jax.nn.sigmoid(gate))
        o_ref[:, h * B_VAL_DIM:(h + 1) * B_VAL_DIM] = y.astype(o_ref.dtype)


N_MIX_IN = 7


def _mixers_kernel(n_cast, *refs):
    mix_ref, prev_ref, sink_ref, rb_ref, wgu_ref, bg_ref, ng_ref = refs[:N_MIX_IN]
    cast_in, refs = refs[N_MIX_IN:N_MIX_IN + n_cast], refs[N_MIX_IN + n_cast:]
    attn_ref, gla_ref = refs[:2]
    cast_out, (bias_ref, s_ref) = refs[2:2 + n_cast], refs[2 + n_cast:]
    _cast_blocks(cast_in, cast_out)

    def cols(off, width):
        return mix_ref.at[:, off - OFF_QA:off - OFF_QA + width]

    kv_prev = [prev_ref.at[:, off - OFF_KA:off - OFF_KA + A_KV_WIDTH] for off in (OFF_KA, OFF_VA)]
    _swa_body(cols(OFF_QA, A_Q_WIDTH), kv_prev[0], cols(OFF_KA, A_KV_WIDTH),
              kv_prev[1], cols(OFF_VA, A_KV_WIDTH), sink_ref, rb_ref, attn_ref, bias_ref)
    _gla_body(cols(OFF_QB, B_QK_WIDTH), cols(OFF_KB, B_QK_WIDTH),
              cols(OFF_VB, HALF_V), cols(OFF_VB + HALF_V, HALF_V), cols(OFF_GLOW, LANES),
              cols(OFF_OBG, HALF_V), cols(OFF_OBG + HALF_V, HALF_V),
              wgu_ref, bg_ref, ng_ref, gla_ref, s_ref)


def _mixers(proj, sinks, rel_bias, wgu, bg, ng, cast_weights, batch, seq):
    t = proj.shape[0]
    rows = GLA_CHUNKS_PER_STEP * B_CHUNK
    assert rows == SWA_BLOCKS_PER_STEP * A_BLOCK and seq % rows == 0
    steps = seq // rows
    rb = lambda b, s: b * steps + s
    prev = lambda b, s: jnp.maximum(rb(b, s) * SWA_BLOCKS_PER_STEP - 1, 0)
    smem = functools.partial(pl.BlockSpec, memory_space=pltpu.SMEM)
    cast_in, cast_out, cast_shapes = _cast_specs(cast_weights, batch * steps, rb)
    kv_width = OFF_QB - OFF_KA
    assert OFF_KA % kv_width == 0 and OFF_VA == OFF_KA + A_KV_WIDTH
    mix_specs = [
        pl.BlockSpec((pl.Element(rows), pl.Element(PROJ_USED - OFF_QA)),
                     lambda b, s: (rb(b, s) * rows, OFF_QA)),
        pl.BlockSpec((A_BLOCK, kv_width), lambda b, s: (prev(b, s), OFF_KA // kv_width)),
        smem(), smem(), _const_spec(wgu.shape), _const_spec(bg.shape), _const_spec(ng.shape)]
    assert len(mix_specs) == N_MIX_IN
    out_block = lambda width: pl.BlockSpec((rows, width), lambda b, s: (rb(b, s), 0))
    outs = pl.pallas_call(
        functools.partial(_mixers_kernel, len(cast_weights)),
        grid=(batch, steps),
        in_specs=mix_specs + cast_in,
        out_specs=[out_block(A_Q_WIDTH), out_block(B_V_WIDTH)] + cast_out,
        out_shape=[jax.ShapeDtypeStruct((t, A_Q_WIDTH), BF16),
                   jax.ShapeDtypeStruct((t, B_V_WIDTH), BF16)] + cast_shapes,
        scratch_shapes=[pltpu.VMEM((2, A_HEADS, A_BLOCK, 2 * A_BLOCK), F32),
                        pltpu.VMEM((B_HEADS, B_KEY_DIM, B_VAL_DIM), F32)],
        compiler_params=_cparams(("arbitrary", "arbitrary")),
        name="mixers",
    )(proj, proj, sinks, rel_bias.T, wgu, bg, ng, *cast_weights)
    return outs[0], outs[1], outs[2:]


def _merge_kernel(a_ref, b_ref, gates_ref, x_ref, wa_ref, wb_ref, wo_ref, gz_ref, h_ref, z_ref):
    d = x_ref.shape[1]
    ya = jnp.dot(a_ref[...], wa_ref[...], preferred_element_type=F32)
    yb = jnp.dot(b_ref[...], wb_ref[...], preferred_element_type=F32)
    merged = (jax.nn.sigmoid(gates_ref[:, :d].astype(F32)) * ya
              + jax.nn.sigmoid(gates_ref[:, d:].astype(F32)) * yb)
    h = x_ref[...] + jnp.dot(merged.astype(BF16), wo_ref[...], preferred_element_type=F32)
    h_ref[...] = h
    z_ref[...] = _rmsnorm_rows(h, gz_ref[...]).astype(z_ref.dtype)


def _merge(attn, gla, proj, x2, wa, wb, wo, gz, tm):
    t, d = x2.shape
    row_block = pl.BlockSpec((tm, d), lambda i: (i, 0))
    return pl.pallas_call(
        _merge_kernel,
        grid=(t // tm,),
        in_specs=[
            pl.BlockSpec((tm, A_Q_WIDTH), lambda i: (i, 0)),
            pl.BlockSpec((tm, B_V_WIDTH), lambda i: (i, 0)),
            pl.BlockSpec((tm, 2 * d), lambda i: (i, OFF_GATE_A // (2 * d))),
            row_block,
            _const_spec(wa.shape), _const_spec(wb.shape), _const_spec(wo.shape), _const_spec(gz.shape),
        ],
        out_specs=[row_block, row_block],
        out_shape=[jax.ShapeDtypeStruct((t, d), F32), jax.ShapeDtypeStruct((t, d), BF16)],
        compiler_params=_cparams(("parallel",)),
        name="merge",
    )(attn, gla, proj, x2, wa, wb, wo, gz)


def _ffn_kernel(z_ref, h_ref, wg_ref, wu_ref, wd_ref, gf_ref, o_ref):
    f = pl.program_id(1)
    z = z_ref[...]
    half = wg_ref.shape[1] // 2
    halves = [slice(c * half, (c + 1) * half) for c in range(2)]
    gu = [(jnp.dot(z, wg_ref[:, cols], preferred_element_type=F32),
           jnp.dot(z, wu_ref[:, cols], preferred_element_type=F32)) for cols in halves]
    for c, cols in enumerate(halves):
        g, u = gu[c]
        act = (g * jax.nn.sigmoid(g) * u).astype(BF16)
        acc = jnp.where(f == 0, 0.0, o_ref[...]) if c == 0 else o_ref[...]
        o_ref[...] = acc + jnp.dot(act, wd_ref[cols, :], preferred_element_type=F32)

    @pl.when(f == pl.num_programs(1) - 1)
    def _():
        gf = gf_ref[...]

        def body(c, carry):
            rows = pl.ds(pl.multiple_of(c * OUT_NORM_ROWS, OUT_NORM_ROWS), OUT_NORM_ROWS)
            o_ref[rows, :] = _rmsnorm_rows(h_ref[rows, :] + o_ref[rows, :], gf)
            return carry

        lax.fori_loop(0, h_ref.shape[0] // OUT_NORM_ROWS, body, 0)


def _ffn(z, h, wg, wu, wd, gf, tm, tf):
    t, d = h.shape
    f = wg.shape[1]
    row_block = pl.BlockSpec((tm, d), lambda i, j: (i, 0))
    return pl.pallas_call(
        _ffn_kernel,
        grid=(t // tm, f // tf),
        in_specs=[
            row_block, row_block,
            pl.BlockSpec((d, tf), lambda i, j: (0, j)),
            pl.BlockSpec((d, tf), lambda i, j: (0, j)),
            pl.BlockSpec((tf, d), lambda i, j: (j, 0)),
            pl.BlockSpec((1, d), lambda i, j: (0, 0)),
        ],
        out_specs=row_block,
        out_shape=jax.ShapeDtypeStruct((t, d), F32),
        compiler_params=_cparams(("parallel", "arbitrary")),
        name="ffn",
    )(z, h, wg, wu, wd, gf)


def kernel(x, norm_mix_g, w_in, sinks, rel_bias, w_gate_up, b_gate, gla_norm_g, w_proj_a, w_proj_b,
           w_out, norm_ffn_g, w_ffn_gate, w_ffn_up, w_ffn_down, norm_final_g):
    batch, seq, d = x.shape
    assert d == D_MODEL and w_in.shape[0] == 1, "single-layer geometry"
    t = batch * seq
    x2 = x.reshape(t, d)

    proj = _inproj(x2, norm_mix_g, w_in[0].T, tm=1024, tn=2304)

    later_weights = (w_proj_a[0], w_proj_b[0], w_out[0], w_ffn_gate[0], w_ffn_up[0], w_ffn_down[0])
    attn, gla, (wa, wb, wo, wg, wu, wd) = _mixers(proj, sinks, rel_bias, w_gate_up[0], b_gate, gla_norm_g,
                                                  later_weights, batch, seq)

    h, z = _merge(attn, gla, proj, x2, wa, wb, wo, norm_ffn_g, tm=512)

    out = _ffn(z, h, wg, wu, wd, norm_final_g.reshape(1, d), tm=1024, tf=512)
    return out.reshape(batch, seq, d)
```

```python
import functools
import math

import numpy as np
import jax
import jax.numpy as jnp
from jax import lax
from jax.experimental import pallas as pl
from jax.experimental.pallas import tpu as pltpu

F32 = jnp.float32
BF16 = jnp.bfloat16

D_MODEL = 2048
A_HEADS = 16
A_KV_HEADS = 4
A_HEAD_DIM = 64
A_GROUP = A_HEADS // A_KV_HEADS
WINDOW = 128
A_BLOCK = 128
A_Q_WIDTH = A_HEADS * A_HEAD_DIM
A_KV_WIDTH = A_KV_HEADS * A_HEAD_DIM
N_BUCKETS = 32
MAX_DISTANCE = 128
B_HEADS = 4
B_KEY_DIM = 128
B_VAL_DIM = 256
B_QK_WIDTH = B_HEADS * B_KEY_DIM
B_V_WIDTH = B_HEADS * B_VAL_DIM
B_GATE_RANK = 16
B_GATE_TAU = 16.0
B_CHUNK = 64
EPS = 1e-6
NEG_INF = -1e30

LANES = 128
BF16_SUBLANES = 16

OFF_GATE_A = 0
OFF_GATE_B = OFF_GATE_A + D_MODEL
OFF_QA = OFF_GATE_B + D_MODEL
OFF_KA = OFF_QA + A_Q_WIDTH
OFF_VA = OFF_KA + A_KV_WIDTH
OFF_QB = OFF_VA + A_KV_WIDTH
OFF_KB = OFF_QB + B_QK_WIDTH
OFF_VB = OFF_KB + B_QK_WIDTH
OFF_OBG = OFF_VB + B_V_WIDTH
OFF_GLOW = OFF_OBG + B_V_WIDTH
PROJ_USED = OFF_GLOW + LANES
HALF_V = B_V_WIDTH // 2

VMEM_LIMIT = 60 * 1024 * 1024


def _cparams(sem):
    return pltpu.CompilerParams(dimension_semantics=sem, vmem_limit_bytes=VMEM_LIMIT)


def _const_spec(shape):
    return pl.BlockSpec(shape, lambda *_: (0,) * len(shape), pipeline_mode=pl.Buffered(1))


def _rmsnorm_rows(x, g):
    ms = jnp.mean(x * x, axis=-1, keepdims=True)
    return x * lax.rsqrt(ms + EPS) * g


NORM_ROWS = 512
OUT_NORM_ROWS = 128


def _cast_specs(weights, n_chunks, chunk_of):
    in_specs, out_specs, shapes = [], [], []
    for w in weights:
        rows, rem = divmod(w.shape[0], n_chunks)
        assert rem == 0 and rows % BF16_SUBLANES == 0, (w.shape, n_chunks)
        for specs in (in_specs, out_specs):
            specs.append(pl.BlockSpec((rows, w.shape[1]), lambda *idx: (chunk_of(*idx), 0)))
        shapes.append(jax.ShapeDtypeStruct(w.shape, BF16))
    return in_specs, out_specs, shapes


def _cast_blocks(in_refs, out_refs):
    for src, dst in zip(in_refs, out_refs):
        dst[...] = src[...].astype(dst.dtype)


def _normalize_rows(x_ref, g_ref, u_ref):
    g = g_ref[...]

    def body(c, carry):
        rows = pl.ds(pl.multiple_of(c * NORM_ROWS, NORM_ROWS), NORM_ROWS)
        u_ref[rows, :] = _rmsnorm_rows(x_ref[rows, :], g).astype(BF16)
        return carry

    lax.fori_loop(0, x_ref.shape[0] // NORM_ROWS, body, 0)


def _project(u_ref, w_ref, o_ref):
    o_ref[...] = lax.dot_general(u_ref[...], w_ref[...], (((1,), (1,)), ((), ())),
                                 preferred_element_type=F32).astype(o_ref.dtype)


_N_MIX = OFF_OBG - OFF_QA
_FEATURE_RUNS = (
    (OFF_GATE_A, _N_MIX + B_GATE_RANK + B_V_WIDTH, 2 * D_MODEL),
    (OFF_QA, 0, _N_MIX),
    (OFF_OBG, _N_MIX + B_GATE_RANK, B_V_WIDTH),
    (OFF_GLOW, _N_MIX, B_GATE_RANK),
)
PACK_TILE = 1024
PACK_WINDOW = PACK_TILE + B_GATE_RANK


def _pack_plan(n_native):
    plan = []
    for tile in range(pl.cdiv(PROJ_USED, PACK_TILE)):
        lo, hi = tile * PACK_TILE, (tile + 1) * PACK_TILE
        pieces = []
        for dst, src, n in _FEATURE_RUNS:
            a, b = max(lo, dst), min(hi, dst + n)
            if a < b:
                pieces.append((a - lo, src + a - dst, b - a))
        start = min(min(p[1] for p in pieces), n_native - PACK_WINDOW)
        assert all(start <= s and s + n <= start + PACK_WINDOW for _, s, n in pieces), (tile, pieces)
        assert start % BF16_SUBLANES == 0 and all(d % BF16_SUBLANES == 0 and (s - start) % BF16_SUBLANES == 0
                                                   for d, s, _ in pieces)
        plan.append((start, [(d, s - start, n) for d, s, n in pieces]))
    return plan


def _inproj_head_kernel(plan, x_ref, g_ref, w_ref, o_ref, wt_ref, u_ref):
    j = pl.program_id(0)
    pl.when(j == 0)(lambda: _normalize_rows(x_ref, g_ref, u_ref))
    for tile, (_, pieces) in enumerate(plan):
        @pl.when(j == tile)
        def _(pieces=pieces):
            covered = 0
            for dst, src, n in sorted(pieces):
                assert dst == covered
                wt_ref[dst:dst + n, :] = w_ref[src:src + n, :].astype(BF16)
                covered += n
            if covered < PACK_TILE:
                wt_ref[covered:, :] = jnp.zeros((PACK_TILE - covered, wt_ref.shape[1]), BF16)
    _project(u_ref, wt_ref, o_ref)


def _inproj_tail_kernel(n_cast, x_ref, g_ref, w_ref, _, *rest):
    cast_in, o_ref, cast_out, u_ref = rest[:n_cast], rest[n_cast], rest[n_cast + 1:-1], rest[-1]
    _cast_blocks(cast_in, cast_out)
    pl.when(pl.program_id(1) == 0)(lambda: _normalize_rows(x_ref, g_ref, u_ref))
    _project(u_ref, w_ref, o_ref)


TAIL_CAST_CHUNKS = 16


def _inproj(x2, g, w_t, cast_weights, tm, tn):
    t, d = x2.shape
    n_native = w_t.shape[0]
    plan = _pack_plan(n_native)
    n = len(plan) * PACK_TILE
    assert n % tn == 0 and t % tm == 0

    def window_start(j):
        units = sum(jnp.where(j == tile, start // BF16_SUBLANES, 0) for tile, (start, _) in enumerate(plan))
        return units * BF16_SUBLANES

    proj, w_p = pl.pallas_call(
        functools.partial(_inproj_head_kernel, plan),
        grid=(len(plan),),
        in_specs=[
            pl.BlockSpec((tm, d), lambda j: (0, 0), pipeline_mode=pl.Buffered(1)),
            _const_spec(g.shape),
            pl.BlockSpec((pl.Element(PACK_WINDOW), pl.Element(d)), lambda j: (window_start(j), 0)),
        ],
        out_specs=[pl.BlockSpec((tm, PACK_TILE), lambda j: (0, j)),
                   pl.BlockSpec((PACK_TILE, d), lambda j: (j, 0))],
        out_shape=[jax.ShapeDtypeStruct((t, n), BF16), jax.ShapeDtypeStruct((n, d), BF16)],
        scratch_shapes=[pltpu.VMEM((tm, d), BF16)],
        compiler_params=_cparams(("arbitrary",)),
        name="inproj_head",
    )(x2, g, w_t)

    n_feat_tiles = n // tn
    assert (t // tm - 1) * n_feat_tiles >= TAIL_CAST_CHUNKS
    cast_in, cast_out, cast_shapes = _cast_specs(
        cast_weights, TAIL_CAST_CHUNKS,
        lambda i, j: jnp.minimum(i * n_feat_tiles + j, TAIL_CAST_CHUNKS - 1))
    outs = pl.pallas_call(
        functools.partial(_inproj_tail_kernel, len(cast_weights)),
        grid=(t // tm - 1, n_feat_tiles),
        in_specs=[
            pl.BlockSpec((tm, d), lambda i, j: (i + 1, 0)),
            pl.BlockSpec((1, d), lambda i, j: (0, 0)),
            pl.BlockSpec((tn, d), lambda i, j: (j, 0)),
            pl.BlockSpec(memory_space=pl.ANY),
        ] + cast_in,
        out_specs=[pl.BlockSpec((tm, tn), lambda i, j: (i + 1, j))] + cast_out,
        out_shape=[jax.ShapeDtypeStruct((t, n), BF16)] + cast_shapes,
        input_output_aliases={3: 0},
        scratch_shapes=[pltpu.VMEM((tm, d), BF16)],
        compiler_params=_cparams(("arbitrary", "arbitrary")),
        name="inproj_tail",
    )(x2, g, w_p, proj, *cast_weights)
    return outs[0], outs[1:]


def _bucket_starts():
    max_exact = N_BUCKETS // 2
    d = np.arange(WINDOW)
    large = max_exact + (np.log(np.maximum(d, 1).astype(np.float32) / max_exact)
                         / math.log(MAX_DISTANCE / max_exact)
                         * (N_BUCKETS - max_exact)).astype(np.int32)
    bucket = np.where(d < max_exact, d, np.minimum(large, N_BUCKETS - 1))
    starts = []
    for b in range(N_BUCKETS):
        hit = np.nonzero(bucket == b)[0]
        if hit.size:
            assert np.all(np.diff(hit) == 1)
            starts.append((b, int(hit[0])))
    return starts


HEADS_PER_TILE = LANES // A_HEAD_DIM
SWA_STAGES = 3
SWA_BLOCKS_PER_STEP = 4


def _swa_body(q_ref, kp_ref, kc_ref, vp_ref, vc_ref, sink_ref, rb_ref, o_ref, bias_ref):
    n_keys = 2 * A_BLOCK

    @pl.when((pl.program_id(0) == 0) & (pl.program_id(1) == 0))
    def _():
        row = lax.broadcasted_iota(jnp.int32, (A_BLOCK, n_keys), 0)
        col = lax.broadcasted_iota(jnp.int32, (A_BLOCK, n_keys), 1)
        dist = row + A_BLOCK - col
        band = (dist >= 0) & (dist < WINDOW)
        starts = _bucket_starts()
        for h in range(A_HEADS):
            val = jnp.full(dist.shape, rb_ref[h, starts[0][0]], F32)
            for b, s in starts[1:]:
                val = jnp.where(dist >= s, rb_ref[h, b], val)
            val = jnp.where(band, val, NEG_INF)
            sink = sink_ref[0, h]
            bias_ref[0, h] = jnp.where(col == 0, sink, val)
            bias_ref[1, h] = jnp.where(col == 0, sink, jnp.where(col >= A_BLOCK, val, NEG_INF))

    lane = lax.broadcasted_iota(jnp.int32, (1, LANES), 1)
    scale = A_HEAD_DIM ** -0.5
    q_keep = (jnp.where(lane < A_HEAD_DIM, scale, 0.0).astype(BF16),
              jnp.where(lane < A_HEAD_DIM, 0.0, scale).astype(BF16))
    lower_lanes = lax.broadcasted_iota(jnp.int32, (A_BLOCK, LANES), 1) < A_HEAD_DIM
    key0 = lax.broadcasted_iota(jnp.int32, (n_keys, LANES), 0) == 0
    ones = jnp.ones((n_keys, LANES), BF16)

    def attend(q_rows, prev, cur, first):
        def both_blocks(which, tile):
            cols = slice(tile * LANES, (tile + 1) * LANES)
            cat = jnp.concatenate([prev[which][:, cols], cur[which][:, cols]], axis=0).astype(F32)
            cat = jnp.where(key0, 0.0, cat)
            return cat.astype(BF16), pltpu.roll(cat, A_HEAD_DIM, 1).astype(BF16)

        stacks = []
        for tile in range(A_KV_WIDTH // LANES):
            k_cat, k_swp = both_blocks(0, tile)
            v_cat, v_swp = both_blocks(1, tile)
            q0 = tile * A_GROUP
            stacks.append((k_cat, jnp.concatenate([v_cat, ones], axis=1),
                           [(q0, 0), (q0 + 1, 0), (q0 + 2, 1), (q0 + 3, 1)]))
            stacks.append((k_swp, jnp.concatenate([v_swp, ones], axis=1),
                           [(q0, 1), (q0 + 1, 1), (q0 + 2, 0), (q0 + 3, 0)]))

        scores = []
        for k_tile, _, members in stacks:
            q4 = jnp.concatenate(
                [q_ref[q_rows, qt * LANES:(qt + 1) * LANES] * q_keep[half] for qt, half in members],
                axis=0)
            scores.append(lax.dot_general(q4, k_tile, (((1,), (1,)), ((), ())),
                                          preferred_element_type=F32))
        yield
        s = jnp.concatenate(scores, axis=0)
        s = s + jnp.concatenate(
            [bias_ref[first, qt * HEADS_PER_TILE + half]
             for _, _, members in stacks for qt, half in members], axis=0)
        p = jnp.exp(s - jnp.max(s, axis=-1, keepdims=True)).astype(BF16)
        yield

        normed = {}
        rows_per_stack = len(stacks[0][2]) * A_BLOCK
        for i, (_, v_ones, members) in enumerate(stacks):
            ov = jnp.dot(p[i * rows_per_stack:(i + 1) * rows_per_stack], v_ones,
                         preferred_element_type=F32)
            o = ov[:, :LANES] / ov[:, LANES:]
            for j, member in enumerate(members):
                normed[member] = o[j * A_BLOCK:(j + 1) * A_BLOCK]
        for qt in range(A_Q_WIDTH // LANES):
            o_ref[q_rows, qt * LANES:(qt + 1) * LANES] = jnp.where(
                lower_lanes, normed[(qt, 0)], normed[(qt, 1)]).astype(o_ref.dtype)

    blocks_per_step = q_ref.shape[0] // A_BLOCK
    kv_prev = (kp_ref[...], vp_ref[...])
    stages = []
    for sub in range(blocks_per_step):
        rows = slice(sub * A_BLOCK, (sub + 1) * A_BLOCK)
        kv_cur = (kc_ref[rows, :], vc_ref[rows, :])
        first = (pl.program_id(1) == 0).astype(jnp.int32) if sub == 0 else 0
        stages.append(attend(rows, kv_prev, kv_cur, first))
        kv_prev = kv_cur
    for _ in range(SWA_STAGES):
        for stage in stages:
            next(stage, None)


GLA_CHUNKS_PER_STEP = 8
GLA_PREFIX_GROUP = 1


def _split3(x):
    hi = x.astype(BF16)
    r1 = x - hi.astype(F32)
    mid = r1.astype(BF16)
    lo = (r1 - mid.astype(F32)).astype(BF16)
    return jnp.concatenate([hi, mid, lo], axis=0)


def _gla_body(q_ref, k_ref, v0_ref, v1_ref, gl_ref, og0_ref, og1_ref, wgu_ref, bg_ref, ng_ref,
              o_ref, s_ref):
    c = B_CHUNK
    n_chunks = GLA_CHUNKS_PER_STEP
    n_rows = n_chunks * c
    heads_per_half = HALF_V // B_VAL_DIM
    v_refs = (v0_ref, v1_ref)
    og_refs = (og0_ref, og1_ref)
    chunk_rows = [slice(j * c, (j + 1) * c) for j in range(n_chunks)]
    key_cols = [slice(h * B_KEY_DIM, (h + 1) * B_KEY_DIM) for h in range(B_HEADS)]
    units = [(j, h) for j in range(n_chunks) for h in range(B_HEADS)]

    def v_of(refs, j, h):
        lo = (h % heads_per_half) * B_VAL_DIM
        rows = slice(None) if j is None else chunk_rows[j]
        return refs[h // heads_per_half][rows, lo:lo + B_VAL_DIM]

    @pl.when(pl.program_id(1) == 0)
    def _():
        s_ref[...] = jnp.zeros_like(s_ref)

    rank = wgu_ref.shape[0]
    wgu = jnp.concatenate([wgu_ref[...].astype(BF16),
                           jnp.zeros((gl_ref.shape[1] - rank, wgu_ref.shape[1]), BF16)], axis=0)
    glin = jnp.dot(gl_ref[...], wgu, preferred_element_type=F32) + bg_ref[...]
    log_a = (jnp.minimum(glin, 0.0) - jnp.log(1.0 + jnp.exp(-jnp.abs(glin)))) / B_GATE_TAU

    g_rows = GLA_PREFIX_GROUP * c
    ri = lax.broadcasted_iota(jnp.int32, (g_rows, 3 * g_rows), 0)
    ci = lax.broadcasted_iota(jnp.int32, (g_rows, 3 * g_rows), 1)
    ci = ci - jnp.where(ci >= g_rows, g_rows, 0) - jnp.where(ci >= 2 * g_rows, g_rows, 0)
    shift = int(math.log2(c))
    same_chunk = lax.shift_right_logical(ri, shift) == lax.shift_right_logical(ci, shift)
    tri3 = ((ri >= ci) & same_chunk).astype(BF16)
    b = jnp.concatenate(
        [jnp.dot(tri3, _split3(log_a[r0:r0 + g_rows]), preferred_element_type=F32)
         for r0 in range(0, n_rows, g_rows)], axis=0)
    last_rows = [b[(j + 1) * c - 1:(j + 1) * c, :] for j in range(n_chunks)]
    b_last = jnp.concatenate([jnp.broadcast_to(r, (c, b.shape[1])) for r in last_rows], axis=0)

    qf = q_ref[...].astype(F32) * (B_KEY_DIM ** -0.5)
    kf = k_ref[...].astype(F32)
    q_dec = (qf * jnp.exp(b)).astype(BF16)
    k_dec = (kf * jnp.exp(-b)).astype(BF16)
    k_state = kf * jnp.exp(b_last - b)
    sublanes = 8
    pad = [jnp.zeros((sublanes - n_chunks, b.shape[1]), F32)] if n_chunks < sublanes else []
    decay_rows = jnp.exp(jnp.concatenate(last_rows + pad, axis=0))

    ri = lax.broadcasted_iota(jnp.int32, (c, c), 0)
    ci = lax.broadcasted_iota(jnp.int32, (c, c), 1)
    causal = ri >= ci
    att = {}
    for j, h in units:
        a = lax.dot_general(q_dec[chunk_rows[j], key_cols[h]], k_dec[chunk_rows[j], key_cols[h]],
                            (((1,), (1,)), ((), ())), preferred_element_type=F32)
        att[j, h] = jnp.where(causal, a, 0.0).astype(BF16)
    o_intra = {u: jnp.dot(att[u], v_of(v_refs, *u), preferred_element_type=F32) for u in units}
    ds = {(j, h): jnp.dot(k_state[chunk_rows[j], key_cols[h]].T.astype(BF16), v_of(v_refs, j, h),
                          preferred_element_type=F32) for j, h in units}

    entering = {}
    for h in range(B_HEADS):
        decay_t = decay_rows[:, key_cols[h]].T
        state = s_ref[h]
        for j in range(n_chunks):
            entering[j, h] = state.astype(BF16)
            state = decay_t[:, j:j + 1] * state + ds[j, h]
        s_ref[h] = state
    o_inter = {(j, h): jnp.dot(q_dec[chunk_rows[j], key_cols[h]], entering[j, h],
                               preferred_element_type=F32) for j, h in units}

    ng = ng_ref[...]
    for h in range(B_HEADS):
        o = jnp.concatenate([o_intra[j, h] + o_inter[j, h] for j in range(n_chunks)], axis=0)
        gate = v_of(og_refs, None, h).astype(F32)
        y = _rmsnorm_rows(o, ng) * (gate * jax.nn.sigmoid(gate))
        o_ref[:, h * B_VAL_DIM:(h + 1) * B_VAL_DIM] = y.astype(o_ref.dtype)


N_MIX_IN = 7


def _mixers_kernel(n_cast, *refs):
    mix_ref, prev_ref, sink_ref, rb_ref, wgu_ref, bg_ref, ng_ref = refs[:N_MIX_IN]
    cast_in, refs = refs[N_MIX_IN:N_MIX_IN + n_cast], refs[N_MIX_IN + n_cast:]
    attn_ref, gla_ref = refs[:2]
    cast_out, (bias_ref, s_ref) = refs[2:2 + n_cast], refs[2 + n_cast:]
    _cast_blocks(cast_in, cast_out)

    def cols(off, width):
        return mix_ref.at[:, off - OFF_QA:off - OFF_QA + width]

    kv_prev = [prev_ref.at[:, off - OFF_KA:off - OFF_KA + A_KV_WIDTH] for off in (OFF_KA, OFF_VA)]
    _swa_body(cols(OFF_QA, A_Q_WIDTH), kv_prev[0], cols(OFF_KA, A_KV_WIDTH),
              kv_prev[1], cols(OFF_VA, A_KV_WIDTH), sink_ref, rb_ref, attn_ref, bias_ref)
    _gla_body(cols(OFF_QB, B_QK_WIDTH), cols(OFF_KB, B_QK_WIDTH),
              cols(OFF_VB, HALF_V), cols(OFF_VB + HALF_V, HALF_V), cols(OFF_GLOW, LANES),
              cols(OFF_OBG, HALF_V), cols(OFF_OBG + HALF_V, HALF_V),
              wgu_ref, bg_ref, ng_ref, gla_ref, s_ref)


def _mixers(proj, sinks, rel_bias, wgu, bg, ng, cast_weights, batch, seq):
    t = proj.shape[0]
    rows = GLA_CHUNKS_PER_STEP * B_CHUNK
    assert rows == SWA_BLOCKS_PER_STEP * A_BLOCK and seq % rows == 0
    steps = seq // rows
    rb = lambda b, s: b * steps + s
    prev = lambda b, s: jnp.maximum(rb(b, s) * SWA_BLOCKS_PER_STEP - 1, 0)
    smem = functools.partial(pl.BlockSpec, memory_space=pltpu.SMEM)
    cast_in, cast_out, cast_shapes = _cast_specs(cast_weights, batch * steps, rb)
    kv_width = OFF_QB - OFF_KA
    assert OFF_KA % kv_width == 0 and OFF_VA == OFF_KA + A_KV_WIDTH
    mix_specs = [
        pl.BlockSpec((pl.Element(rows), pl.Element(PROJ_USED - OFF_QA)),
                     lambda b, s: (rb(b, s) * rows, OFF_QA)),
        pl.BlockSpec((A_BLOCK, kv_width), lambda b, s: (prev(b, s), OFF_KA // kv_width)),
        smem(), smem(), _const_spec(wgu.shape), _const_spec(bg.shape), _const_spec(ng.shape)]
    assert len(mix_specs) == N_MIX_IN
    out_block = lambda width: pl.BlockSpec((rows, width), lambda b, s: (rb(b, s), 0))
    outs = pl.pallas_call(
        functools.partial(_mixers_kernel, len(cast_weights)),
        grid=(batch, steps),
        in_specs=mix_specs + cast_in,
        out_specs=[out_block(A_Q_WIDTH), out_block(B_V_WIDTH)] + cast_out,
        out_shape=[jax.ShapeDtypeStruct((t, A_Q_WIDTH), BF16),
                   jax.ShapeDtypeStruct((t, B_V_WIDTH), BF16)] + cast_shapes,
        scratch_shapes=[pltpu.VMEM((2, A_HEADS, A_BLOCK, 2 * A_BLOCK), F32),
                        pltpu.VMEM((B_HEADS, B_KEY_DIM, B_VAL_DIM), F32)],
        compiler_params=_cparams(("arbitrary", "arbitrary")),
        name="mixers",
    )(proj, proj, sinks, rel_bias.T, wgu, bg, ng, *cast_weights)
    return outs[0], outs[1], outs[2:]


def _merge_kernel(a_ref, b_ref, gates_ref, x_ref, wa_ref, wb_ref, wo_ref, gz_ref, h_ref, z_ref):
    d = x_ref.shape[1]
    ya = jnp.dot(a_ref[...], wa_ref[...], preferred_element_type=F32)
    yb = jnp.dot(b_ref[...], wb_ref[...], preferred_element_type=F32)
    merged = (jax.nn.sigmoid(gates_ref[:, :d].astype(F32)) * ya
              + jax.nn.sigmoid(gates_ref[:, d:].astype(F32)) * yb)
    h = x_ref[...] + jnp.dot(merged.astype(BF16), wo_ref[...], preferred_element_type=F32)
    h_ref[...] = h
    z_ref[...] = _rmsnorm_rows(h, gz_ref[...]).astype(z_ref.dtype)


def _merge(attn, gla, proj, x2, wa, wb, wo, gz, tm):
    t, d = x2.shape
    row_block = pl.BlockSpec((tm, d), lambda i: (i, 0))
    return pl.pallas_call(
        _merge_kernel,
        grid=(t // tm,),
        in_specs=[
            pl.BlockSpec((tm, A_Q_WIDTH), lambda i: (i, 0)),
            pl.BlockSpec((tm, B_V_WIDTH), lambda i: (i, 0)),
            pl.BlockSpec((tm, 2 * d), lambda i: (i, OFF_GATE_A // (2 * d))),
            row_block,
            _const_spec(wa.shape), _const_spec(wb.shape), _const_spec(wo.shape), _const_spec(gz.shape),
        ],
        out_specs=[row_block, row_block],
        out_shape=[jax.ShapeDtypeStruct((t, d), F32), jax.ShapeDtypeStruct((t, d), BF16)],
        compiler_params=_cparams(("parallel",)),
        name="merge",
    )(attn, gla, proj, x2, wa, wb, wo, gz)


def _ffn_kernel(z_ref, h_ref, wg_ref, wu_ref, wd_ref, gf_ref, o_ref):
    f = pl.program_id(1)
    z = z_ref[...]
    half = wg_ref.shape[1] // 2
    halves = [slice(c * half, (c + 1) * half) for c in range(2)]
    gu = [(jnp.dot(z, wg_ref[:, cols], preferred_element_type=F32),
           jnp.dot(z, wu_ref[:, cols], preferred_element_type=F32)) for cols in halves]
    for c, cols in enumerate(halves):
        g, u = gu[c]
        act = (g * jax.nn.sigmoid(g) * u).astype(BF16)
        acc = jnp.where(f == 0, 0.0, o_ref[...]) if c == 0 else o_ref[...]
        o_ref[...] = acc + jnp.dot(act, wd_ref[cols, :], preferred_element_type=F32)

    @pl.when(f == pl.num_programs(1) - 1)
    def _():
        gf = gf_ref[...]

        def body(c, carry):
            rows = pl.ds(pl.multiple_of(c * OUT_NORM_ROWS, OUT_NORM_ROWS), OUT_NORM_ROWS)
            o_ref[rows, :] = _rmsnorm_rows(h_ref[rows, :] + o_ref[rows, :], gf)
            return carry

        lax.fori_loop(0, h_ref.shape[0] // OUT_NORM_ROWS, body, 0)


def _ffn(z, h, wg, wu, wd, gf, tm, tf):
    t, d = h.shape
    f = wg.shape[1]
    row_block = pl.BlockSpec((tm, d), lambda i, j: (i, 0))
    return pl.pallas_call(
        _ffn_kernel,
        grid=(t // tm, f // tf),
        in_specs=[
            row_block, row_block,
            pl.BlockSpec((d, tf), lambda i, j: (0, j)),
            pl.BlockSpec((d, tf), lambda i, j: (0, j)),
            pl.BlockSpec((tf, d), lambda i, j: (j, 0)),
            pl.BlockSpec((1, d), lambda i, j: (0, 0)),
        ],
        out_specs=row_block,
        out_shape=jax.ShapeDtypeStruct((t, d), F32),
        compiler_params=_cparams(("parallel", "arbitrary")),
        name="ffn",
    )(z, h, wg, wu, wd, gf)


def kernel(x, norm_mix_g, w_in, sinks, rel_bias, w_gate_up, b_gate, gla_norm_g, w_proj_a, w_proj_b,
           w_out, norm_ffn_g, w_ffn_gate, w_ffn_up, w_ffn_down, norm_final_g):
    batch, seq, d = x.shape
    assert d == D_MODEL and w_in.shape[0] == 1, "single-layer geometry"
    t = batch * seq
    x2 = x.reshape(t, d)

    proj, (wa, wb, wo) = _inproj(x2, norm_mix_g, w_in[0].T, (w_proj_a[0], w_proj_b[0], w_out[0]),
                                 tm=1024, tn=2304)
    attn, gla, (wg, wu, wd) = _mixers(proj, sinks, rel_bias, w_gate_up[0], b_gate, gla_norm_g,
                                      (w_ffn_gate[0], w_ffn_up[0], w_ffn_down[0]), batch, seq)

    h, z = _merge(attn, gla, proj, x2, wa, wb, wo, norm_ffn_g, tm=512)

    out = _ffn(z, h, wg, wu, wd, norm_final_g.reshape(1, d), tm=1024, tf=512)
    return out.reshape(batch, seq, d)
```

```python
import functools
import math

import numpy as np
import jax
import jax.numpy as jnp
from jax import lax
from jax.experimental import pallas as pl
from jax.experimental.pallas import tpu as pltpu

F32 = jnp.float32
BF16 = jnp.bfloat16

D_MODEL = 2048
A_HEADS = 16
A_KV_HEADS = 4
A_HEAD_DIM = 64
A_GROUP = A_HEADS // A_KV_HEADS
WINDOW = 128
A_BLOCK = 128
A_Q_WIDTH = A_HEADS * A_HEAD_DIM
A_KV_WIDTH = A_KV_HEADS * A_HEAD_DIM
N_BUCKETS = 32
MAX_DISTANCE = 128
B_HEADS = 4
B_KEY_DIM = 128
B_VAL_DIM = 256
B_QK_WIDTH = B_HEADS * B_KEY_DIM
B_V_WIDTH = B_HEADS * B_VAL_DIM
B_GATE_RANK = 16
B_GATE_TAU = 16.0
B_CHUNK = 64
EPS = 1e-6
NEG_INF = -1e30

LANES = 128
BF16_SUBLANES = 16

OFF_GATE_A = 0
OFF_GATE_B = OFF_GATE_A + D_MODEL
OFF_QA = OFF_GATE_B + D_MODEL
OFF_KA = OFF_QA + A_Q_WIDTH
OFF_VA = OFF_KA + A_KV_WIDTH
OFF_QB = OFF_VA + A_KV_WIDTH
OFF_KB = OFF_QB + B_QK_WIDTH
OFF_VB = OFF_KB + B_QK_WIDTH
OFF_OBG = OFF_VB + B_V_WIDTH
OFF_GLOW = OFF_OBG + B_V_WIDTH
PROJ_USED = OFF_GLOW + LANES
HALF_V = B_V_WIDTH // 2

VMEM_LIMIT = 60 * 1024 * 1024


def _cparams(sem):
    return pltpu.CompilerParams(dimension_semantics=sem, vmem_limit_bytes=VMEM_LIMIT)


def _const_spec(shape):
    return pl.BlockSpec(shape, lambda *_: (0,) * len(shape), pipeline_mode=pl.Buffered(1))


def _rmsnorm_rows(x, g):
    ms = jnp.mean(x * x, axis=-1, keepdims=True)
    return x * lax.rsqrt(ms + EPS) * g


NORM_ROWS = 512
OUT_NORM_ROWS = 128


def _cast_specs(weights, n_chunks, chunk_of):
    in_specs, out_specs, shapes = [], [], []
    for w in weights:
        rows, rem = divmod(w.shape[0], n_chunks)
        assert rem == 0 and rows % BF16_SUBLANES == 0, (w.shape, n_chunks)
        for specs in (in_specs, out_specs):
            specs.append(pl.BlockSpec((rows, w.shape[1]), lambda *idx: (chunk_of(*idx), 0)))
        shapes.append(jax.ShapeDtypeStruct(w.shape, BF16))
    return in_specs, out_specs, shapes


def _cast_blocks(in_refs, out_refs):
    for src, dst in zip(in_refs, out_refs):
        dst[...] = src[...].astype(dst.dtype)


def _normalize_rows(x_ref, g_ref, u_ref):
    g = g_ref[...]

    def body(c, carry):
        rows = pl.ds(pl.multiple_of(c * NORM_ROWS, NORM_ROWS), NORM_ROWS)
        u_ref[rows, :] = _rmsnorm_rows(x_ref[rows, :], g).astype(BF16)
        return carry

    lax.fori_loop(0, x_ref.shape[0] // NORM_ROWS, body, 0)


def _project(u_ref, w_ref, o_ref):
    o_ref[...] = lax.dot_general(u_ref[...], w_ref[...], (((1,), (1,)), ((), ())),
                                 preferred_element_type=F32).astype(o_ref.dtype)


_N_MIX = OFF_OBG - OFF_QA
_FEATURE_RUNS = (
    (OFF_GATE_A, _N_MIX + B_GATE_RANK + B_V_WIDTH, 2 * D_MODEL),
    (OFF_QA, 0, _N_MIX),
    (OFF_OBG, _N_MIX + B_GATE_RANK, B_V_WIDTH),
    (OFF_GLOW, _N_MIX, B_GATE_RANK),
)
PACK_TILE = 1024
PACK_WINDOW = PACK_TILE + B_GATE_RANK


def _pack_plan(n_native):
    plan = []
    for tile in range(pl.cdiv(PROJ_USED, PACK_TILE)):
        lo, hi = tile * PACK_TILE, (tile + 1) * PACK_TILE
        pieces = []
        for dst, src, n in _FEATURE_RUNS:
            a, b = max(lo, dst), min(hi, dst + n)
            if a < b:
                pieces.append((a - lo, src + a - dst, b - a))
        start = min(min(p[1] for p in pieces), n_native - PACK_WINDOW)
        assert all(start <= s and s + n <= start + PACK_WINDOW for _, s, n in pieces), (tile, pieces)
        assert start % BF16_SUBLANES == 0 and all(d % BF16_SUBLANES == 0 and (s - start) % BF16_SUBLANES == 0
                                                   for d, s, _ in pieces)
        plan.append((start, [(d, s - start, n) for d, s, n in pieces]))
    return plan


HEAD_RING = 3


def _inproj_head_kernel(plan, x_ref, g_ref, w_hbm, o_ref, wt_ref, u_ref, ring_ref, sem_ref):
    j = pl.program_id(0)

    def window_copy(tile):
        slot = tile % HEAD_RING
        return pltpu.make_async_copy(w_hbm.at[pl.ds(plan[tile][0], PACK_WINDOW), :],
                                     ring_ref.at[slot], sem_ref.at[slot])

    for tile, (_, pieces) in enumerate(plan):
        @pl.when(j == tile)
        def _(tile=tile, pieces=pieces):
            if tile == 0:
                for ahead in range(min(HEAD_RING - 1, len(plan))):
                    window_copy(ahead).start()
            if tile + HEAD_RING - 1 < len(plan):
                window_copy(tile + HEAD_RING - 1).start()
            if tile == 0:
                _normalize_rows(x_ref, g_ref, u_ref)
            window_copy(tile).wait()
            w_ref = ring_ref.at[tile % HEAD_RING]
            covered = 0
            for dst, src, n in sorted(pieces):
                assert dst == covered
                wt_ref[dst:dst + n, :] = w_ref[src:src + n, :].astype(BF16)
                covered += n
            if covered < PACK_TILE:
                wt_ref[covered:, :] = jnp.zeros((PACK_TILE - covered, wt_ref.shape[1]), BF16)
    _project(u_ref, wt_ref, o_ref)


def _inproj_tail_kernel(x_ref, g_ref, w_ref, _, o_ref, u_ref):
    pl.when(pl.program_id(1) == 0)(lambda: _normalize_rows(x_ref, g_ref, u_ref))
    _project(u_ref, w_ref, o_ref)


def _inproj(x2, g, w_t, tm, tn):
    t, d = x2.shape
    n_native = w_t.shape[0]
    plan = _pack_plan(n_native)
    n = len(plan) * PACK_TILE
    assert n % tn == 0 and t % tm == 0

    proj, w_p = pl.pallas_call(
        functools.partial(_inproj_head_kernel, plan),
        grid=(len(plan),),
        in_specs=[
            pl.BlockSpec((tm, d), lambda j: (0, 0), pipeline_mode=pl.Buffered(1)),
            _const_spec(g.shape),
            pl.BlockSpec(memory_space=pl.ANY),
        ],
        out_specs=[pl.BlockSpec((tm, PACK_TILE), lambda j: (0, j)),
                   pl.BlockSpec((PACK_TILE, d), lambda j: (j, 0))],
        out_shape=[jax.ShapeDtypeStruct((t, n), BF16), jax.ShapeDtypeStruct((n, d), BF16)],
        scratch_shapes=[pltpu.VMEM((tm, d), BF16),
                        pltpu.VMEM((HEAD_RING, PACK_WINDOW, d), F32),
                        pltpu.SemaphoreType.DMA((HEAD_RING,))],
        compiler_params=_cparams(("arbitrary",)),
        name="inproj_head",
    )(x2, g, w_t)

    return pl.pallas_call(
        _inproj_tail_kernel,
        grid=(t // tm - 1, n // tn),
        in_specs=[
            pl.BlockSpec((tm, d), lambda i, j: (i + 1, 0)),
            pl.BlockSpec((1, d), lambda i, j: (0, 0)),
            pl.BlockSpec((tn, d), lambda i, j: (j, 0)),
            pl.BlockSpec(memory_space=pl.ANY),
        ],
        out_specs=pl.BlockSpec((tm, tn), lambda i, j: (i + 1, j)),
        out_shape=jax.ShapeDtypeStruct((t, n), BF16),
        input_output_aliases={3: 0},
        scratch_shapes=[pltpu.VMEM((tm, d), BF16)],
        compiler_params=_cparams(("parallel", "arbitrary")),
        name="inproj_tail",
    )(x2, g, w_p, proj)


def _bucket_starts():
    max_exact = N_BUCKETS // 2
    d = np.arange(WINDOW)
    large = max_exact + (np.log(np.maximum(d, 1).astype(np.float32) / max_exact)
                         / math.log(MAX_DISTANCE / max_exact)
                         * (N_BUCKETS - max_exact)).astype(np.int32)
    bucket = np.where(d < max_exact, d, np.minimum(large, N_BUCKETS - 1))
    starts = []
    for b in range(N_BUCKETS):
        hit = np.nonzero(bucket == b)[0]
        if hit.size:
            assert np.all(np.diff(hit) == 1)
            starts.append((b, int(hit[0])))
    return starts


HEADS_PER_TILE = LANES // A_HEAD_DIM
SWA_STAGES = 3
SWA_BLOCKS_PER_STEP = 4


def _swa_body(q_ref, kp_ref, kc_ref, vp_ref, vc_ref, sink_ref, rb_ref, o_ref, bias_ref):
    n_keys = 2 * A_BLOCK

    @pl.when((pl.program_id(0) == 0) & (pl.program_id(1) == 0))
    def _():
        row = lax.broadcasted_iota(jnp.int32, (A_BLOCK, n_keys), 0)
        col = lax.broadcasted_iota(jnp.int32, (A_BLOCK, n_keys), 1)
        dist = row + A_BLOCK - col
        band = (dist >= 0) & (dist < WINDOW)
        starts = _bucket_starts()
        for h in range(A_HEADS):
            val = jnp.full(dist.shape, rb_ref[h, starts[0][0]], F32)
            for b, s in starts[1:]:
                val = jnp.where(dist >= s, rb_ref[h, b], val)
            val = jnp.where(band, val, NEG_INF)
            sink = sink_ref[0, h]
            bias_ref[0, h] = jnp.where(col == 0, sink, val)
            bias_ref[1, h] = jnp.where(col == 0, sink, jnp.where(col >= A_BLOCK, val, NEG_INF))

    lane = lax.broadcasted_iota(jnp.int32, (1, LANES), 1)
    scale = A_HEAD_DIM ** -0.5
    q_keep = (jnp.where(lane < A_HEAD_DIM, scale, 0.0).astype(BF16),
              jnp.where(lane < A_HEAD_DIM, 0.0, scale).astype(BF16))
    lower_lanes = lax.broadcasted_iota(jnp.int32, (A_BLOCK, LANES), 1) < A_HEAD_DIM
    key0 = lax.broadcasted_iota(jnp.int32, (n_keys, LANES), 0) == 0
    ones = jnp.ones((n_keys, LANES), BF16)

    def attend(q_rows, prev, cur, first):
        def both_blocks(which, tile):
            cols = slice(tile * LANES, (tile + 1) * LANES)
            cat = jnp.concatenate([prev[which][:, cols], cur[which][:, cols]], axis=0).astype(F32)
            cat = jnp.where(key0, 0.0, cat)
            return cat.astype(BF16), pltpu.roll(cat, A_HEAD_DIM, 1).astype(BF16)

        stacks = []
        for tile in range(A_KV_WIDTH // LANES):
            k_cat, k_swp = both_blocks(0, tile)
            v_cat, v_swp = both_blocks(1, tile)
            q0 = tile * A_GROUP
            stacks.append((k_cat, jnp.concatenate([v_cat, ones], axis=1),
                           [(q0, 0), (q0 + 1, 0), (q0 + 2, 1), (q0 + 3, 1)]))
            stacks.append((k_swp, jnp.concatenate([v_swp, ones], axis=1),
                           [(q0, 1), (q0 + 1, 1), (q0 + 2, 0), (q0 + 3, 0)]))

        scores = []
        for k_tile, _, members in stacks:
            q4 = jnp.concatenate(
                [q_ref[q_rows, qt * LANES:(qt + 1) * LANES] * q_keep[half] for qt, half in members],
                axis=0)
            scores.append(lax.dot_general(q4, k_tile, (((1,), (1,)), ((), ())),
                                          preferred_element_type=F32))
        yield
        s = jnp.concatenate(scores, axis=0)
        s = s + jnp.concatenate(
            [bias_ref[first, qt * HEADS_PER_TILE + half]
             for _, _, members in stacks for qt, half in members], axis=0)
        p = jnp.exp(s - jnp.max(s, axis=-1, keepdims=True)).astype(BF16)
        yield

        normed = {}
        rows_per_stack = len(stacks[0][2]) * A_BLOCK
        for i, (_, v_ones, members) in enumerate(stacks):
            ov = jnp.dot(p[i * rows_per_stack:(i + 1) * rows_per_stack], v_ones,
                         preferred_element_type=F32)
            o = ov[:, :LANES] / ov[:, LANES:]
            for j, member in enumerate(members):
                normed[member] = o[j * A_BLOCK:(j + 1) * A_BLOCK]
        for qt in range(A_Q_WIDTH // LANES):
            o_ref[q_rows, qt * LANES:(qt + 1) * LANES] = jnp.where(
                lower_lanes, normed[(qt, 0)], normed[(qt, 1)]).astype(o_ref.dtype)

    blocks_per_step = q_ref.shape[0] // A_BLOCK
    kv_prev = (kp_ref[...], vp_ref[...])
    stages = []
    for sub in range(blocks_per_step):
        rows = slice(sub * A_BLOCK, (sub + 1) * A_BLOCK)
        kv_cur = (kc_ref[rows, :], vc_ref[rows, :])
        first = (pl.program_id(1) == 0).astype(jnp.int32) if sub == 0 else 0
        stages.append(attend(rows, kv_prev, kv_cur, first))
        kv_prev = kv_cur
    for _ in range(SWA_STAGES):
        for stage in stages:
            next(stage, None)


GLA_CHUNKS_PER_STEP = 8
GLA_PREFIX_GROUP = 1


def _split3(x):
    hi = x.astype(BF16)
    r1 = x - hi.astype(F32)
    mid = r1.astype(BF16)
    lo = (r1 - mid.astype(F32)).astype(BF16)
    return jnp.concatenate([hi, mid, lo], axis=0)


def _gla_body(q_ref, k_ref, v0_ref, v1_ref, gl_ref, og0_ref, og1_ref, wgu_ref, bg_ref, ng_ref,
              o_ref, s_ref):
    c = B_CHUNK
    n_chunks = GLA_CHUNKS_PER_STEP
    n_rows = n_chunks * c
    heads_per_half = HALF_V // B_VAL_DIM
    v_refs = (v0_ref, v1_ref)
    og_refs = (og0_ref, og1_ref)
    chunk_rows = [slice(j * c, (j + 1) * c) for j in range(n_chunks)]
    key_cols = [slice(h * B_KEY_DIM, (h + 1) * B_KEY_DIM) for h in range(B_HEADS)]
    units = [(j, h) for j in range(n_chunks) for h in range(B_HEADS)]

    def v_of(refs, j, h):
        lo = (h % heads_per_half) * B_VAL_DIM
        rows = slice(None) if j is None else chunk_rows[j]
        return refs[h // heads_per_half][rows, lo:lo + B_VAL_DIM]

    @pl.when(pl.program_id(1) == 0)
    def _():
        s_ref[...] = jnp.zeros_like(s_ref)

    rank = wgu_ref.shape[0]
    wgu = jnp.concatenate([wgu_ref[...].astype(BF16),
                           jnp.zeros((gl_ref.shape[1] - rank, wgu_ref.shape[1]), BF16)], axis=0)
    glin = jnp.dot(gl_ref[...], wgu, preferred_element_type=F32) + bg_ref[...]
    log_a = (jnp.minimum(glin, 0.0) - jnp.log(1.0 + jnp.exp(-jnp.abs(glin)))) / B_GATE_TAU

    g_rows = GLA_PREFIX_GROUP * c
    ri = lax.broadcasted_iota(jnp.int32, (g_rows, 3 * g_rows), 0)
    ci = lax.broadcasted_iota(jnp.int32, (g_rows, 3 * g_rows), 1)
    ci = ci - jnp.where(ci >= g_rows, g_rows, 0) - jnp.where(ci >= 2 * g_rows, g_rows, 0)
    shift = int(math.log2(c))
    same_chunk = lax.shift_right_logical(ri, shift) == lax.shift_right_logical(ci, shift)
    tri3 = ((ri >= ci) & same_chunk).astype(BF16)
    b = jnp.concatenate(
        [jnp.dot(tri3, _split3(log_a[r0:r0 + g_rows]), preferred_element_type=F32)
         for r0 in range(0, n_rows, g_rows)], axis=0)
    last_rows = [b[(j + 1) * c - 1:(j + 1) * c, :] for j in range(n_chunks)]
    b_last = jnp.concatenate([jnp.broadcast_to(r, (c, b.shape[1])) for r in last_rows], axis=0)

    qf = q_ref[...].astype(F32) * (B_KEY_DIM ** -0.5)
    kf = k_ref[...].astype(F32)
    q_dec = (qf * jnp.exp(b)).astype(BF16)
    k_dec = (kf * jnp.exp(-b)).astype(BF16)
    k_state = kf * jnp.exp(b_last - b)
    sublanes = 8
    pad = [jnp.zeros((sublanes - n_chunks, b.shape[1]), F32)] if n_chunks < sublanes else []
    decay_rows = jnp.exp(jnp.concatenate(last_rows + pad, axis=0))

    ri = lax.broadcasted_iota(jnp.int32, (c, c), 0)
    ci = lax.broadcasted_iota(jnp.int32, (c, c), 1)
    causal = ri >= ci
    att = {}
    for j, h in units:
        a = lax.dot_general(q_dec[chunk_rows[j], key_cols[h]], k_dec[chunk_rows[j], key_cols[h]],
                            (((1,), (1,)), ((), ())), preferred_element_type=F32)
        att[j, h] = jnp.where(causal, a, 0.0).astype(BF16)
    o_intra = {u: jnp.dot(att[u], v_of(v_refs, *u), preferred_element_type=F32) for u in units}
    ds = {(j, h): jnp.dot(k_state[chunk_rows[j], key_cols[h]].T.astype(BF16), v_of(v_refs, j, h),
                          preferred_element_type=F32) for j, h in units}

    entering = {}
    for h in range(B_HEADS):
        decay_t = decay_rows[:, key_cols[h]].T
        state = s_ref[h]
        for j in range(n_chunks):
            entering[j, h] = state.astype(BF16)
            state = decay_t[:, j:j + 1] * state + ds[j, h]
        s_ref[h] = state
    o_inter = {(j, h): jnp.dot(q_dec[chunk_rows[j], key_cols[h]], entering[j, h],
                               preferred_element_type=F32) for j, h in units}

    ng = ng_ref[...]
    for h in range(B_HEADS):
        o = jnp.concatenate([o_intra[j, h] + o_inter[j, h] for j in range(n_chunks)], axis=0)
        gate = v_of(og_refs, None, h).astype(F32)
        y = _rmsnorm_rows(o, ng) * (gate * jax.nn.sigmoid(gate))
        o_ref[:, h * B_VAL_DIM:(h + 1) * B_VAL_DIM] = y.astype(o_ref.dtype)


N_MIX_IN = 7


def _mixers_kernel(n_cast, *refs):
    mix_ref, prev_ref, sink_ref, rb_ref, wgu_ref, bg_ref, ng_ref = refs[:N_MIX_IN]
    cast_in, refs = refs[N_MIX_IN:N_MIX_IN + n_cast], refs[N_MIX_IN + n_cast:]
    attn_ref, gla_ref = refs[:2]
    cast_out, (bias_ref, s_ref) = refs[2:2 + n_cast], refs[2 + n_cast:]
    _cast_blocks(cast_in, cast_out)

    def cols(off, width):
        return mix_ref.at[:, off - OFF_QA:off - OFF_QA + width]

    kv_prev = [prev_ref.at[:, off - OFF_KA:off - OFF_KA + A_KV_WIDTH] for off in (OFF_KA, OFF_VA)]
    _swa_body(cols(OFF_QA, A_Q_WIDTH), kv_prev[0], cols(OFF_KA, A_KV_WIDTH),
              kv_prev[1], cols(OFF_VA, A_KV_WIDTH), sink_ref, rb_ref, attn_ref, bias_ref)
    _gla_body(cols(OFF_QB, B_QK_WIDTH), cols(OFF_KB, B_QK_WIDTH),
              cols(OFF_VB, HALF_V), cols(OFF_VB + HALF_V, HALF_V), cols(OFF_GLOW, LANES),
              cols(OFF_OBG, HALF_V), cols(OFF_OBG + HALF_V, HALF_V),
              wgu_ref, bg_ref, ng_ref, gla_ref, s_ref)


def _mixers(proj, sinks, rel_bias, wgu, bg, ng, cast_weights, batch, seq):
    t = proj.shape[0]
    rows = GLA_CHUNKS_PER_STEP * B_CHUNK
    assert rows == SWA_BLOCKS_PER_STEP * A_BLOCK and seq % rows == 0
    steps = seq // rows
    rb = lambda b, s: b * steps + s
    prev = lambda b, s: jnp.maximum(rb(b, s) * SWA_BLOCKS_PER_STEP - 1, 0)
    smem = functools.partial(pl.BlockSpec, memory_space=pltpu.SMEM)
    cast_in, cast_out, cast_shapes = _cast_specs(cast_weights, batch * steps, rb)
    kv_width = OFF_QB - OFF_KA
    assert OFF_KA % kv_width == 0 and OFF_VA == OFF_KA + A_KV_WIDTH
    mix_specs = [
        pl.BlockSpec((pl.Element(rows), pl.Element(PROJ_USED - OFF_QA)),
                     lambda b, s: (rb(b, s) * rows, OFF_QA)),
        pl.BlockSpec((A_BLOCK, kv_width), lambda b, s: (prev(b, s), OFF_KA // kv_width)),
        smem(), smem(), _const_spec(wgu.shape), _const_spec(bg.shape), _const_spec(ng.shape)]
    assert len(mix_specs) == N_MIX_IN
    out_block = lambda width: pl.BlockSpec((rows, width), lambda b, s: (rb(b, s), 0))
    outs = pl.pallas_call(
        functools.partial(_mixers_kernel, len(cast_weights)),
        grid=(batch, steps),
        in_specs=mix_specs + cast_in,
        out_specs=[out_block(A_Q_WIDTH), out_block(B_V_WIDTH)] + cast_out,
        out_shape=[jax.ShapeDtypeStruct((t, A_Q_WIDTH), BF16),
                   jax.ShapeDtypeStruct((t, B_V_WIDTH), BF16)] + cast_shapes,
        scratch_shapes=[pltpu.VMEM((2, A_HEADS, A_BLOCK, 2 * A_BLOCK), F32),
                        pltpu.VMEM((B_HEADS, B_KEY_DIM, B_VAL_DIM), F32)],
        compiler_params=_cparams(("arbitrary", "arbitrary")),
        name="mixers",
    )(proj, proj, sinks, rel_bias.T, wgu, bg, ng, *cast_weights)
    return outs[0], outs[1], outs[2:]


def _merge_kernel(a_ref, b_ref, gates_ref, x_ref, wa_ref, wb_ref, wo_ref, gz_ref, h_ref, z_ref):
    d = x_ref.shape[1]
    ya = jnp.dot(a_ref[...], wa_ref[...], preferred_element_type=F32)
    yb = jnp.dot(b_ref[...], wb_ref[...], preferred_element_type=F32)
    merged = (jax.nn.sigmoid(gates_ref[:, :d].astype(F32)) * ya
              + jax.nn.sigmoid(gates_ref[:, d:].astype(F32)) * yb)
    h = x_ref[...] + jnp.dot(merged.astype(BF16), wo_ref[...], preferred_element_type=F32)
    h_ref[...] = h
    z_ref[...] = _rmsnorm_rows(h, gz_ref[...]).astype(z_ref.dtype)


def _merge(attn, gla, proj, x2, wa, wb, wo, gz, tm):
    t, d = x2.shape
    row_block = pl.BlockSpec((tm, d), lambda i: (i, 0))
    return pl.pallas_call(
        _merge_kernel,
        grid=(t // tm,),
        in_specs=[
            pl.BlockSpec((tm, A_Q_WIDTH), lambda i: (i, 0)),
            pl.BlockSpec((tm, B_V_WIDTH), lambda i: (i, 0)),
            pl.BlockSpec((tm, 2 * d), lambda i: (i, OFF_GATE_A // (2 * d))),
            row_block,
            _const_spec(wa.shape), _const_spec(wb.shape), _const_spec(wo.shape), _const_spec(gz.shape),
        ],
        out_specs=[row_block, row_block],
        out_shape=[jax.ShapeDtypeStruct((t, d), F32), jax.ShapeDtypeStruct((t, d), BF16)],
        compiler_params=_cparams(("parallel",)),
        name="merge",
    )(attn, gla, proj, x2, wa, wb, wo, gz)


def _ffn_kernel(z_ref, h_ref, wg_ref, wu_ref, wd_ref, gf_ref, o_ref):
    f = pl.program_id(1)
    z = z_ref[...]
    half = wg_ref.shape[1] // 2
    halves = [slice(c * half, (c + 1) * half) for c in range(2)]
    gu = [(jnp.dot(z, wg_ref[:, cols], preferred_element_type=F32),
           jnp.dot(z, wu_ref[:, cols], preferred_element_type=F32)) for cols in halves]
    for c, cols in enumerate(halves):
        g, u = gu[c]
        act = (g * jax.nn.sigmoid(g) * u).astype(BF16)
        acc = jnp.where(f == 0, 0.0, o_ref[...]) if c == 0 else o_ref[...]
        o_ref[...] = acc + jnp.dot(act, wd_ref[cols, :], preferred_element_type=F32)

    @pl.when(f == pl.num_programs(1) - 1)
    def _():
        gf = gf_ref[...]

        def body(c, carry):
            rows = pl.ds(pl.multiple_of(c * OUT_NORM_ROWS, OUT_NORM_ROWS), OUT_NORM_ROWS)
            o_ref[rows, :] = _rmsnorm_rows(h_ref[rows, :] + o_ref[rows, :], gf)
            return carry

        lax.fori_loop(0, h_ref.shape[0] // OUT_NORM_ROWS, body, 0)


def _ffn(z, h, wg, wu, wd, gf, tm, tf):
    t, d = h.shape
    f = wg.shape[1]
    row_block = pl.BlockSpec((tm, d), lambda i, j: (i, 0))
    return pl.pallas_call(
        _ffn_kernel,
        grid=(t // tm, f // tf),
        in_specs=[
            row_block, row_block,
            pl.BlockSpec((d, tf), lambda i, j: (0, j)),
            pl.BlockSpec((d, tf), lambda i, j: (0, j)),
            pl.BlockSpec((tf, d), lambda i, j: (j, 0)),
            pl.BlockSpec((1, d), lambda i, j: (0, 0)),
        ],
        out_specs=row_block,
        out_shape=jax.ShapeDtypeStruct((t, d), F32),
        compiler_params=_cparams(("parallel", "arbitrary")),
        name="ffn",
    )(z, h, wg, wu, wd, gf)


def kernel(x, norm_mix_g, w_in, sinks, rel_bias, w_gate_up, b_gate, gla_norm_g, w_proj_a, w_proj_b,
           w_out, norm_ffn_g, w_ffn_gate, w_ffn_up, w_ffn_down, norm_final_g):
    batch, seq, d = x.shape
    assert d == D_MODEL and w_in.shape[0] == 1, "single-layer geometry"
    t = batch * seq
    x2 = x.reshape(t, d)

    proj = _inproj(x2, norm_mix_g, w_in[0].T, tm=1024, tn=2304)

    later_weights = (w_proj_a[0], w_proj_b[0], w_out[0], w_ffn_gate[0], w_ffn_up[0], w_ffn_down[0])
    attn, gla, (wa, wb, wo, wg, wu, wd) = _mixers(proj, sinks, rel_bias, w_gate_up[0], b_gate, gla_norm_g,
                                                  later_weights, batch, seq)

    h, z = _merge(attn, gla, proj, x2, wa, wb, wo, norm_ffn_g, tm=512)

    out = _ffn(z, h, wg, wu, wd, norm_final_g.reshape(1, d), tm=1024, tf=512)
    return out.reshape(batch, seq, d)
```

```python
import functools
import math

import numpy as np
import jax
import jax.numpy as jnp
from jax import lax
from jax.experimental import pallas as pl
from jax.experimental.pallas import tpu as pltpu

F32 = jnp.float32
BF16 = jnp.bfloat16

D_MODEL = 2048
A_HEADS = 16
A_KV_HEADS = 4
A_HEAD_DIM = 64
A_GROUP = A_HEADS // A_KV_HEADS
WINDOW = 128
A_BLOCK = 128
A_Q_WIDTH = A_HEADS * A_HEAD_DIM
A_KV_WIDTH = A_KV_HEADS * A_HEAD_DIM
N_BUCKETS = 32
MAX_DISTANCE = 128
B_HEADS = 4
B_KEY_DIM = 128
B_VAL_DIM = 256
B_QK_WIDTH = B_HEADS * B_KEY_DIM
B_V_WIDTH = B_HEADS * B_VAL_DIM
B_GATE_RANK = 16
B_GATE_TAU = 16.0
B_CHUNK = 64
EPS = 1e-6
NEG_INF = -1e30

LANES = 128
BF16_SUBLANES = 16

OFF_GATE_A = 0
OFF_GATE_B = OFF_GATE_A + D_MODEL
OFF_QA = OFF_GATE_B + D_MODEL
OFF_KA = OFF_QA + A_Q_WIDTH
OFF_VA = OFF_KA + A_KV_WIDTH
OFF_QB = OFF_VA + A_KV_WIDTH
OFF_KB = OFF_QB + B_QK_WIDTH
OFF_VB = OFF_KB + B_QK_WIDTH
OFF_OBG = OFF_VB + B_V_WIDTH
OFF_GLOW = OFF_OBG + B_V_WIDTH
PROJ_USED = OFF_GLOW + LANES
HALF_V = B_V_WIDTH // 2

VMEM_LIMIT = 60 * 1024 * 1024


def _cparams(sem):
    return pltpu.CompilerParams(dimension_semantics=sem, vmem_limit_bytes=VMEM_LIMIT)


def _const_spec(shape):
    return pl.BlockSpec(shape, lambda *_: (0,) * len(shape), pipeline_mode=pl.Buffered(1))


def _rmsnorm_rows(x, g):
    ms = jnp.mean(x * x, axis=-1, keepdims=True)
    return x * lax.rsqrt(ms + EPS) * g


NORM_ROWS = 512
OUT_NORM_ROWS = 128


def _cast_specs(weights, n_chunks, chunk_of):
    in_specs, out_specs, shapes = [], [], []
    for w in weights:
        rows, rem = divmod(w.shape[0], n_chunks)
        assert rem == 0 and rows % BF16_SUBLANES == 0, (w.shape, n_chunks)
        for specs in (in_specs, out_specs):
            specs.append(pl.BlockSpec((rows, w.shape[1]), lambda *idx: (chunk_of(*idx), 0)))
        shapes.append(jax.ShapeDtypeStruct(w.shape, BF16))
    return in_specs, out_specs, shapes


def _cast_blocks(in_refs, out_refs):
    for src, dst in zip(in_refs, out_refs):
        dst[...] = src[...].astype(dst.dtype)


def _normalize_rows(x_ref, g_ref, u_ref):
    g = g_ref[...]

    def body(c, carry):
        rows = pl.ds(pl.multiple_of(c * NORM_ROWS, NORM_ROWS), NORM_ROWS)
        u_ref[rows, :] = _rmsnorm_rows(x_ref[rows, :], g).astype(BF16)
        return carry

    lax.fori_loop(0, x_ref.shape[0] // NORM_ROWS, body, 0)


def _project(u_ref, w_ref, o_ref):
    o_ref[...] = lax.dot_general(u_ref[...], w_ref[...], (((1,), (1,)), ((), ())),
                                 preferred_element_type=F32).astype(o_ref.dtype)


_N_MIX = OFF_OBG - OFF_QA
_FEATURE_RUNS = (
    (OFF_GATE_A, _N_MIX + B_GATE_RANK + B_V_WIDTH, 2 * D_MODEL),
    (OFF_QA, 0, _N_MIX),
    (OFF_OBG, _N_MIX + B_GATE_RANK, B_V_WIDTH),
    (OFF_GLOW, _N_MIX, B_GATE_RANK),
)
PACK_TILE = 1024
PACK_WINDOW = PACK_TILE + B_GATE_RANK


def _pack_plan(n_native):
    plan = []
    for tile in range(pl.cdiv(PROJ_USED, PACK_TILE)):
        lo, hi = tile * PACK_TILE, (tile + 1) * PACK_TILE
        pieces = []
        for dst, src, n in _FEATURE_RUNS:
            a, b = max(lo, dst), min(hi, dst + n)
            if a < b:
                pieces.append((a - lo, src + a - dst, b - a))
        start = min(min(p[1] for p in pieces), n_native - PACK_WINDOW)
        assert all(start <= s and s + n <= start + PACK_WINDOW for _, s, n in pieces), (tile, pieces)
        assert start % BF16_SUBLANES == 0 and all(d % BF16_SUBLANES == 0 and (s - start) % BF16_SUBLANES == 0
                                                   for d, s, _ in pieces)
        plan.append((start, [(d, s - start, n) for d, s, n in pieces]))
    return plan


HEAD_RING = 3


def _inproj_head_kernel(plan, x_ref, g_ref, w_hbm, o_ref, wt_ref, u_ref, ring_ref, sem_ref):
    j = pl.program_id(0)

    def window_copy(tile):
        slot = tile % HEAD_RING
        return pltpu.make_async_copy(w_hbm.at[pl.ds(plan[tile][0], PACK_WINDOW), :],
                                     ring_ref.at[slot], sem_ref.at[slot])

    for tile, (_, pieces) in enumerate(plan):
        @pl.when(j == tile)
        def _(tile=tile, pieces=pieces):
            if tile == 0:
                for ahead in range(min(HEAD_RING - 1, len(plan))):
                    window_copy(ahead).start()
            if tile + HEAD_RING - 1 < len(plan):
                window_copy(tile + HEAD_RING - 1).start()
            if tile == 0:
                _normalize_rows(x_ref, g_ref, u_ref)
            window_copy(tile).wait()
            w_ref = ring_ref.at[tile % HEAD_RING]
            covered = 0
            for dst, src, n in sorted(pieces):
                assert dst == covered
                wt_ref[dst:dst + n, :] = w_ref[src:src + n, :].astype(BF16)
                covered += n
            if covered < PACK_TILE:
                wt_ref[covered:, :] = jnp.zeros((PACK_TILE - covered, wt_ref.shape[1]), BF16)
    _project(u_ref, wt_ref, o_ref)


def _inproj_tail_kernel(x_ref, g_ref, w_ref, _, o_ref, u_ref):
    pl.when(pl.program_id(1) == 0)(lambda: _normalize_rows(x_ref, g_ref, u_ref))
    _project(u_ref, w_ref, o_ref)


def _inproj(x2, g, w_t, tm, tn):
    t, d = x2.shape
    n_native = w_t.shape[0]
    plan = _pack_plan(n_native)
    n = len(plan) * PACK_TILE
    n_tail = pl.cdiv(PROJ_USED, tn) * tn
    assert n_tail <= n and t % tm == 0

    proj, w_p = pl.pallas_call(
        functools.partial(_inproj_head_kernel, plan),
        grid=(len(plan),),
        in_specs=[
            pl.BlockSpec((tm, d), lambda j: (0, 0), pipeline_mode=pl.Buffered(1)),
            _const_spec(g.shape),
            pl.BlockSpec(memory_space=pl.ANY),
        ],
        out_specs=[pl.BlockSpec((tm, PACK_TILE), lambda j: (0, j)),
                   pl.BlockSpec((PACK_TILE, d), lambda j: (j, 0))],
        out_shape=[jax.ShapeDtypeStruct((t, n), BF16), jax.ShapeDtypeStruct((n, d), BF16)],
        scratch_shapes=[pltpu.VMEM((tm, d), BF16),
                        pltpu.VMEM((HEAD_RING, PACK_WINDOW, d), F32),
                        pltpu.SemaphoreType.DMA((HEAD_RING,))],
        compiler_params=_cparams(("arbitrary",)),
        name="inproj_head",
    )(x2, g, w_t)

    return pl.pallas_call(
        _inproj_tail_kernel,
        grid=(t // tm - 1, n_tail // tn),
        in_specs=[
            pl.BlockSpec((tm, d), lambda i, j: (i + 1, 0)),
            pl.BlockSpec((1, d), lambda i, j: (0, 0)),
            pl.BlockSpec((tn, d), lambda i, j: (j, 0)),
            pl.BlockSpec(memory_space=pl.ANY),
        ],
        out_specs=pl.BlockSpec((tm, tn), lambda i, j: (i + 1, j)),
        out_shape=jax.ShapeDtypeStruct((t, n), BF16),
        input_output_aliases={3: 0},
        scratch_shapes=[pltpu.VMEM((tm, d), BF16)],
        compiler_params=_cparams(("parallel", "arbitrary")),
        name="inproj_tail",
    )(x2, g, w_p, proj)


def _bucket_starts():
    max_exact = N_BUCKETS // 2
    d = np.arange(WINDOW)
    large = max_exact + (np.log(np.maximum(d, 1).astype(np.float32) / max_exact)
                         / math.log(MAX_DISTANCE / max_exact)
                         * (N_BUCKETS - max_exact)).astype(np.int32)
    bucket = np.where(d < max_exact, d, np.minimum(large, N_BUCKETS - 1))
    starts = []
    for b in range(N_BUCKETS):
        hit = np.nonzero(bucket == b)[0]
        if hit.size:
            assert np.all(np.diff(hit) == 1)
            starts.append((b, int(hit[0])))
    return starts


HEADS_PER_TILE = LANES // A_HEAD_DIM
SWA_STAGES = 3
SWA_BLOCKS_PER_STEP = 4


def _swa_body(q_ref, kp_ref, kc_ref, vp_ref, vc_ref, sink_ref, rb_ref, o_ref, bias_ref):
    n_keys = 2 * A_BLOCK

    @pl.when((pl.program_id(0) == 0) & (pl.program_id(1) == 0))
    def _():
        row = lax.broadcasted_iota(jnp.int32, (A_BLOCK, n_keys), 0)
        col = lax.broadcasted_iota(jnp.int32, (A_BLOCK, n_keys), 1)
        dist = row + A_BLOCK - col
        band = (dist >= 0) & (dist < WINDOW)
        starts = _bucket_starts()
        for h in range(A_HEADS):
            val = jnp.full(dist.shape, rb_ref[h, starts[0][0]], F32)
            for b, s in starts[1:]:
                val = jnp.where(dist >= s, rb_ref[h, b], val)
            val = jnp.where(band, val, NEG_INF)
            sink = sink_ref[0, h]
            bias_ref[0, h] = jnp.where(col == 0, sink, val)
            bias_ref[1, h] = jnp.where(col == 0, sink, jnp.where(col >= A_BLOCK, val, NEG_INF))

    lane = lax.broadcasted_iota(jnp.int32, (1, LANES), 1)
    scale = A_HEAD_DIM ** -0.5
    q_keep = (jnp.where(lane < A_HEAD_DIM, scale, 0.0).astype(BF16),
              jnp.where(lane < A_HEAD_DIM, 0.0, scale).astype(BF16))
    lower_lanes = lax.broadcasted_iota(jnp.int32, (A_BLOCK, LANES), 1) < A_HEAD_DIM
    key0 = lax.broadcasted_iota(jnp.int32, (n_keys, LANES), 0) == 0
    ones = jnp.ones((n_keys, LANES), BF16)

    def attend(q_rows, prev, cur, first):
        def both_blocks(which, tile):
            cols = slice(tile * LANES, (tile + 1) * LANES)
            cat = jnp.concatenate([prev[which][:, cols], cur[which][:, cols]], axis=0).astype(F32)
            cat = jnp.where(key0, 0.0, cat)
            return cat.astype(BF16), pltpu.roll(cat, A_HEAD_DIM, 1).astype(BF16)

        stacks = []
        for tile in range(A_KV_WIDTH // LANES):
            k_cat, k_swp = both_blocks(0, tile)
            v_cat, v_swp = both_blocks(1, tile)
            q0 = tile * A_GROUP
            stacks.append((k_cat, jnp.concatenate([v_cat, ones], axis=1),
                           [(q0, 0), (q0 + 1, 0), (q0 + 2, 1), (q0 + 3, 1)]))
            stacks.append((k_swp, jnp.concatenate([v_swp, ones], axis=1),
                           [(q0, 1), (q0 + 1, 1), (q0 + 2, 0), (q0 + 3, 0)]))

        scores = []
        for k_tile, _, members in stacks:
            q4 = jnp.concatenate(
                [q_ref[q_rows, qt * LANES:(qt + 1) * LANES] * q_keep[half] for qt, half in members],
                axis=0)
            scores.append(lax.dot_general(q4, k_tile, (((1,), (1,)), ((), ())),
                                          preferred_element_type=F32))
        yield
        s = jnp.concatenate(scores, axis=0)
        s = s + jnp.concatenate(
            [bias_ref[first, qt * HEADS_PER_TILE + half]
             for _, _, members in stacks for qt, half in members], axis=0)
        p = jnp.exp(s - jnp.max(s, axis=-1, keepdims=True)).astype(BF16)
        yield

        normed = {}
        rows_per_stack = len(stacks[0][2]) * A_BLOCK
        for i, (_, v_ones, members) in enumerate(stacks):
            ov = jnp.dot(p[i * rows_per_stack:(i + 1) * rows_per_stack], v_ones,
                         preferred_element_type=F32)
            o = ov[:, :LANES] / ov[:, LANES:]
            for j, member in enumerate(members):
                normed[member] = o[j * A_BLOCK:(j + 1) * A_BLOCK]
        for qt in range(A_Q_WIDTH // LANES):
            o_ref[q_rows, qt * LANES:(qt + 1) * LANES] = jnp.where(
                lower_lanes, normed[(qt, 0)], normed[(qt, 1)]).astype(o_ref.dtype)

    blocks_per_step = q_ref.shape[0] // A_BLOCK
    kv_prev = (kp_ref[...], vp_ref[...])
    stages = []
    for sub in range(blocks_per_step):
        rows = slice(sub * A_BLOCK, (sub + 1) * A_BLOCK)
        kv_cur = (kc_ref[rows, :], vc_ref[rows, :])
        first = (pl.program_id(1) == 0).astype(jnp.int32) if sub == 0 else 0
        stages.append(attend(rows, kv_prev, kv_cur, first))
        kv_prev = kv_cur
    for _ in range(SWA_STAGES):
        for stage in stages:
            next(stage, None)


GLA_CHUNKS_PER_STEP = 8
GLA_PREFIX_GROUP = 1


def _split3(x):
    hi = x.astype(BF16)
    r1 = x - hi.astype(F32)
    mid = r1.astype(BF16)
    lo = (r1 - mid.astype(F32)).astype(BF16)
    return jnp.concatenate([hi, mid, lo], axis=0)


def _gla_body(q_ref, k_ref, v0_ref, v1_ref, gl_ref, og0_ref, og1_ref, wgu_ref, bg_ref, ng_ref,
              o_ref, s_ref):
    c = B_CHUNK
    n_chunks = GLA_CHUNKS_PER_STEP
    n_rows = n_chunks * c
    heads_per_half = HALF_V // B_VAL_DIM
    v_refs = (v0_ref, v1_ref)
    og_refs = (og0_ref, og1_ref)
    chunk_rows = [slice(j * c, (j + 1) * c) for j in range(n_chunks)]
    key_cols = [slice(h * B_KEY_DIM, (h + 1) * B_KEY_DIM) for h in range(B_HEADS)]
    units = [(j, h) for j in range(n_chunks) for h in range(B_HEADS)]

    def v_of(refs, j, h):
        lo = (h % heads_per_half) * B_VAL_DIM
        rows = slice(None) if j is None else chunk_rows[j]
        return refs[h // heads_per_half][rows, lo:lo + B_VAL_DIM]

    @pl.when(pl.program_id(1) == 0)
    def _():
        s_ref[...] = jnp.zeros_like(s_ref)

    rank = wgu_ref.shape[0]
    wgu = jnp.concatenate([wgu_ref[...].astype(BF16),
                           jnp.zeros((gl_ref.shape[1] - rank, wgu_ref.shape[1]), BF16)], axis=0)
    glin = jnp.dot(gl_ref[...], wgu, preferred_element_type=F32) + bg_ref[...]
    log_a = (jnp.minimum(glin, 0.0) - jnp.log(1.0 + jnp.exp(-jnp.abs(glin)))) / B_GATE_TAU

    g_rows = GLA_PREFIX_GROUP * c
    ri = lax.broadcasted_iota(jnp.int32, (g_rows, 3 * g_rows), 0)
    ci = lax.broadcasted_iota(jnp.int32, (g_rows, 3 * g_rows), 1)
    ci = ci - jnp.where(ci >= g_rows, g_rows, 0) - jnp.where(ci >= 2 * g_rows, g_rows, 0)
    shift = int(math.log2(c))
    same_chunk = lax.shift_right_logical(ri, shift) == lax.shift_right_logical(ci, shift)
    tri3 = ((ri >= ci) & same_chunk).astype(BF16)
    b = jnp.concatenate(
        [jnp.dot(tri3, _split3(log_a[r0:r0 + g_rows]), preferred_element_type=F32)
         for r0 in range(0, n_rows, g_rows)], axis=0)
    last_rows = [b[(j + 1) * c - 1:(j + 1) * c, :] for j in range(n_chunks)]
    b_last = jnp.concatenate([jnp.broadcast_to(r, (c, b.shape[1])) for r in last_rows], axis=0)

    qf = q_ref[...].astype(F32) * (B_KEY_DIM ** -0.5)
    kf = k_ref[...].astype(F32)
    q_dec = (qf * jnp.exp(b)).astype(BF16)
    k_dec = (kf * jnp.exp(-b)).astype(BF16)
    k_state = kf * jnp.exp(b_last - b)
    sublanes = 8
    pad = [jnp.zeros((sublanes - n_chunks, b.shape[1]), F32)] if n_chunks < sublanes else []
    decay_rows = jnp.exp(jnp.concatenate(last_rows + pad, axis=0))

    ri = lax.broadcasted_iota(jnp.int32, (c, c), 0)
    ci = lax.broadcasted_iota(jnp.int32, (c, c), 1)
    causal = ri >= ci
    att = {}
    for j, h in units:
        a = lax.dot_general(q_dec[chunk_rows[j], key_cols[h]], k_dec[chunk_rows[j], key_cols[h]],
                            (((1,), (1,)), ((), ())), preferred_element_type=F32)
        att[j, h] = jnp.where(causal, a, 0.0).astype(BF16)
    o_intra = {u: jnp.dot(att[u], v_of(v_refs, *u), preferred_element_type=F32) for u in units}
    ds = {(j, h): jnp.dot(k_state[chunk_rows[j], key_cols[h]].T.astype(BF16), v_of(v_refs, j, h),
                          preferred_element_type=F32) for j, h in units}

    entering = {}
    for h in range(B_HEADS):
        decay_t = decay_rows[:, key_cols[h]].T
        state = s_ref[h]
        for j in range(n_chunks):
            entering[j, h] = state.astype(BF16)
            state = decay_t[:, j:j + 1] * state + ds[j, h]
        s_ref[h] = state
    o_inter = {(j, h): jnp.dot(q_dec[chunk_rows[j], key_cols[h]], entering[j, h],
                               preferred_element_type=F32) for j, h in units}

    ng = ng_ref[...]
    for h in range(B_HEADS):
        o = jnp.concatenate([o_intra[j, h] + o_inter[j, h] for j in range(n_chunks)], axis=0)
        gate = v_of(og_refs, None, h).astype(F32)
        y = _rmsnorm_rows(o, ng) * (gate * jax.nn.sigmoid(gate))
        o_ref[:, h * B_VAL_DIM:(h + 1) * B_VAL_DIM] = y.astype(o_ref.dtype)


N_MIX_IN = 7


def _mixers_kernel(n_cast, *refs):
    mix_ref, prev_ref, sink_ref, rb_ref, wgu_ref, bg_ref, ng_ref = refs[:N_MIX_IN]
    cast_in, refs = refs[N_MIX_IN:N_MIX_IN + n_cast], refs[N_MIX_IN + n_cast:]
    attn_ref, gla_ref = refs[:2]
    cast_out, (bias_ref, s_ref) = refs[2:2 + n_cast], refs[2 + n_cast:]
    _cast_blocks(cast_in, cast_out)

    def cols(off, width):
        return mix_ref.at[:, off - OFF_QA:off - OFF_QA + width]

    kv_prev = [prev_ref.at[:, off - OFF_KA:off - OFF_KA + A_KV_WIDTH] for off in (OFF_KA, OFF_VA)]
    _swa_body(cols(OFF_QA, A_Q_WIDTH), kv_prev[0], cols(OFF_KA, A_KV_WIDTH),
              kv_prev[1], cols(OFF_VA, A_KV_WIDTH), sink_ref, rb_ref, attn_ref, bias_ref)
    _gla_body(cols(OFF_QB, B_QK_WIDTH), cols(OFF_KB, B_QK_WIDTH),
              cols(OFF_VB, HALF_V), cols(OFF_VB + HALF_V, HALF_V), cols(OFF_GLOW, LANES),
              cols(OFF_OBG, HALF_V), cols(OFF_OBG + HALF_V, HALF_V),
              wgu_ref, bg_ref, ng_ref, gla_ref, s_ref)


def _mixers(proj, sinks, rel_bias, wgu, bg, ng, cast_weights, batch, seq):
    t = proj.shape[0]
    rows = GLA_CHUNKS_PER_STEP * B_CHUNK
    assert rows == SWA_BLOCKS_PER_STEP * A_BLOCK and seq % rows == 0
    steps = seq // rows
    rb = lambda b, s: b * steps + s
    prev = lambda b, s: jnp.maximum(rb(b, s) * SWA_BLOCKS_PER_STEP - 1, 0)
    smem = functools.partial(pl.BlockSpec, memory_space=pltpu.SMEM)
    cast_in, cast_out, cast_shapes = _cast_specs(cast_weights, batch * steps, rb)
    kv_width = OFF_QB - OFF_KA
    assert OFF_KA % kv_width == 0 and OFF_VA == OFF_KA + A_KV_WIDTH
    mix_specs = [
        pl.BlockSpec((pl.Element(rows), pl.Element(PROJ_USED - OFF_QA)),
                     lambda b, s: (rb(b, s) * rows, OFF_QA)),
        pl.BlockSpec((A_BLOCK, kv_width), lambda b, s: (prev(b, s), OFF_KA // kv_width)),
        smem(), smem(), _const_spec(wgu.shape), _const_spec(bg.shape), _const_spec(ng.shape)]
    assert len(mix_specs) == N_MIX_IN
    out_block = lambda width: pl.BlockSpec((rows, width), lambda b, s: (rb(b, s), 0))
    outs = pl.pallas_call(
        functools.partial(_mixers_kernel, len(cast_weights)),
        grid=(batch, steps),
        in_specs=mix_specs + cast_in,
        out_specs=[out_block(A_Q_WIDTH), out_block(B_V_WIDTH)] + cast_out,
        out_shape=[jax.ShapeDtypeStruct((t, A_Q_WIDTH), BF16),
                   jax.ShapeDtypeStruct((t, B_V_WIDTH), BF16)] + cast_shapes,
        scratch_shapes=[pltpu.VMEM((2, A_HEADS, A_BLOCK, 2 * A_BLOCK), F32),
                        pltpu.VMEM((B_HEADS, B_KEY_DIM, B_VAL_DIM), F32)],
        compiler_params=_cparams(("arbitrary", "arbitrary")),
        name="mixers",
    )(proj, proj, sinks, rel_bias.T, wgu, bg, ng, *cast_weights)
    return outs[0], outs[1], outs[2:]


def _merge_kernel(a_ref, b_ref, gates_ref, x_ref, wa_ref, wb_ref, wo_ref, gz_ref, h_ref, z_ref):
    d = x_ref.shape[1]
    ya = jnp.dot(a_ref[...], wa_ref[...], preferred_element_type=F32)
    yb = jnp.dot(b_ref[...], wb_ref[...], preferred_element_type=F32)
    merged = (jax.nn.sigmoid(gates_ref[:, :d].astype(F32)) * ya
              + jax.nn.sigmoid(gates_ref[:, d:].astype(F32)) * yb)
    h = x_ref[...] + jnp.dot(merged.astype(BF16), wo_ref[...], preferred_element_type=F32)
    h_ref[...] = h
    z_ref[...] = _rmsnorm_rows(h, gz_ref[...]).astype(z_ref.dtype)


def _merge(attn, gla, proj, x2, wa, wb, wo, gz, tm):
    t, d = x2.shape
    row_block = pl.BlockSpec((tm, d), lambda i: (i, 0))
    return pl.pallas_call(
        _merge_kernel,
        grid=(t // tm,),
        in_specs=[
            pl.BlockSpec((tm, A_Q_WIDTH), lambda i: (i, 0)),
            pl.BlockSpec((tm, B_V_WIDTH), lambda i: (i, 0)),
            pl.BlockSpec((tm, 2 * d), lambda i: (i, OFF_GATE_A // (2 * d))),
            row_block,
            _const_spec(wa.shape), _const_spec(wb.shape), _const_spec(wo.shape), _const_spec(gz.shape),
        ],
        out_specs=[row_block, row_block],
        out_shape=[jax.ShapeDtypeStruct((t, d), F32), jax.ShapeDtypeStruct((t, d), BF16)],
        compiler_params=_cparams(("parallel",)),
        name="merge",
    )(attn, gla, proj, x2, wa, wb, wo, gz)


def _ffn_kernel(z_ref, h_ref, wg_ref, wu_ref, wd_ref, gf_ref, o_ref):
    f = pl.program_id(1)
    z = z_ref[...]
    half = wg_ref.shape[1] // 2
    halves = [slice(c * half, (c + 1) * half) for c in range(2)]
    gu = [(jnp.dot(z, wg_ref[:, cols], preferred_element_type=F32),
           jnp.dot(z, wu_ref[:, cols], preferred_element_type=F32)) for cols in halves]
    for c, cols in enumerate(halves):
        g, u = gu[c]
        act = (g * jax.nn.sigmoid(g) * u).astype(BF16)
        acc = jnp.where(f == 0, 0.0, o_ref[...]) if c == 0 else o_ref[...]
        o_ref[...] = acc + jnp.dot(act, wd_ref[cols, :], preferred_element_type=F32)

    @pl.when(f == pl.num_programs(1) - 1)
    def _():
        gf = gf_ref[...]

        def body(c, carry):
            rows = pl.ds(pl.multiple_of(c * OUT_NORM_ROWS, OUT_NORM_ROWS), OUT_NORM_ROWS)
            o_ref[rows, :] = _rmsnorm_rows(h_ref[rows, :] + o_ref[rows, :], gf)
            return carry

        lax.fori_loop(0, h_ref.shape[0] // OUT_NORM_ROWS, body, 0)


def _ffn(z, h, wg, wu, wd, gf, tm, tf):
    t, d = h.shape
    f = wg.shape[1]
    row_block = pl.BlockSpec((tm, d), lambda i, j: (i, 0))
    return pl.pallas_call(
        _ffn_kernel,
        grid=(t // tm, f // tf),
        in_specs=[
            row_block, row_block,
            pl.BlockSpec((d, tf), lambda i, j: (0, j)),
            pl.BlockSpec((d, tf), lambda i, j: (0, j)),
            pl.BlockSpec((tf, d), lambda i, j: (j, 0)),
            pl.BlockSpec((1, d), lambda i, j: (0, 0)),
        ],
        out_specs=row_block,
        out_shape=jax.ShapeDtypeStruct((t, d), F32),
        compiler_params=_cparams(("parallel", "arbitrary")),
        name="ffn",
    )(z, h, wg, wu, wd, gf)


def kernel(x, norm_mix_g, w_in, sinks, rel_bias, w_gate_up, b_gate, gla_norm_g, w_proj_a, w_proj_b,
           w_out, norm_ffn_g, w_ffn_gate, w_ffn_up, w_ffn_down, norm_final_g):
    batch, seq, d = x.shape
    assert d == D_MODEL and w_in.shape[0] == 1, "single-layer geometry"
    t = batch * seq
    x2 = x.reshape(t, d)

    proj = _inproj(x2, norm_mix_g, w_in[0].T, tm=1024, tn=1792)

    later_weights = (w_proj_a[0], w_proj_b[0], w_out[0], w_ffn_gate[0], w_ffn_up[0], w_ffn_down[0])
    attn, gla, (wa, wb, wo, wg, wu, wd) = _mixers(proj, sinks, rel_bias, w_gate_up[0], b_gate, gla_norm_g,
                                                  later_weights, batch, seq)

    h, z = _merge(attn, gla, proj, x2, wa, wb, wo, norm_ffn_g, tm=512)

    out = _ffn(z, h, wg, wu, wd, norm_final_g.reshape(1, d), tm=1024, tf=512)
    return out.reshape(batch, seq, d)
```

```python
import functools
import math

import numpy as np
import jax
import jax.numpy as jnp
from jax import lax
from jax.experimental import pallas as pl
from jax.experimental.pallas import tpu as pltpu

F32 = jnp.float32
BF16 = jnp.bfloat16

D_MODEL = 2048
A_HEADS = 16
A_KV_HEADS = 4
A_HEAD_DIM = 64
A_GROUP = A_HEADS // A_KV_HEADS
WINDOW = 128
A_BLOCK = 128
A_Q_WIDTH = A_HEADS * A_HEAD_DIM
A_KV_WIDTH = A_KV_HEADS * A_HEAD_DIM
N_BUCKETS = 32
MAX_DISTANCE = 128
B_HEADS = 4
B_KEY_DIM = 128
B_VAL_DIM = 256
B_QK_WIDTH = B_HEADS * B_KEY_DIM
B_V_WIDTH = B_HEADS * B_VAL_DIM
B_GATE_RANK = 16
B_GATE_TAU = 16.0
B_CHUNK = 64
EPS = 1e-6
NEG_INF = -1e30

LANES = 128
BF16_SUBLANES = 16

OFF_GATE_A = 0
OFF_GATE_B = OFF_GATE_A + D_MODEL
OFF_QA = OFF_GATE_B + D_MODEL
OFF_KA = OFF_QA + A_Q_WIDTH
OFF_VA = OFF_KA + A_KV_WIDTH
OFF_QB = OFF_VA + A_KV_WIDTH
OFF_KB = OFF_QB + B_QK_WIDTH
OFF_VB = OFF_KB + B_QK_WIDTH
OFF_OBG = OFF_VB + B_V_WIDTH
OFF_GLOW = OFF_OBG + B_V_WIDTH
PROJ_USED = OFF_GLOW + LANES
HALF_V = B_V_WIDTH // 2

VMEM_LIMIT = 60 * 1024 * 1024


def _cparams(sem):
    return pltpu.CompilerParams(dimension_semantics=sem, vmem_limit_bytes=VMEM_LIMIT)


def _const_spec(shape):
    return pl.BlockSpec(shape, lambda *_: (0,) * len(shape), pipeline_mode=pl.Buffered(1))


def _rmsnorm_rows(x, g):
    ms = jnp.mean(x * x, axis=-1, keepdims=True)
    return x * lax.rsqrt(ms + EPS) * g


NORM_ROWS = 512
OUT_NORM_ROWS = 128


def _cast_specs(weights, n_chunks, chunk_of):
    in_specs, out_specs, shapes = [], [], []
    for w in weights:
        rows, rem = divmod(w.shape[0], n_chunks)
        assert rem == 0 and rows % BF16_SUBLANES == 0, (w.shape, n_chunks)
        for specs in (in_specs, out_specs):
            specs.append(pl.BlockSpec((rows, w.shape[1]), lambda *idx: (chunk_of(*idx), 0)))
        shapes.append(jax.ShapeDtypeStruct(w.shape, BF16))
    return in_specs, out_specs, shapes


def _cast_blocks(in_refs, out_refs):
    for src, dst in zip(in_refs, out_refs):
        dst[...] = src[...].astype(dst.dtype)


def _normalize_rows(x_ref, g_ref, u_ref):
    g = g_ref[...]

    chunk = math.gcd(x_ref.shape[0], NORM_ROWS)

    def body(c, carry):
        rows = pl.ds(pl.multiple_of(c * chunk, chunk), chunk)
        u_ref[rows, :] = _rmsnorm_rows(x_ref[rows, :], g).astype(BF16)
        return carry

    lax.fori_loop(0, x_ref.shape[0] // chunk, body, 0)


def _project(u_ref, w_ref, o_ref):
    o_ref[...] = lax.dot_general(u_ref[...], w_ref[...], (((1,), (1,)), ((), ())),
                                 preferred_element_type=F32).astype(o_ref.dtype)


_N_MIX = OFF_OBG - OFF_QA
_FEATURE_RUNS = (
    (OFF_GATE_A, _N_MIX + B_GATE_RANK + B_V_WIDTH, 2 * D_MODEL),
    (OFF_QA, 0, _N_MIX),
    (OFF_OBG, _N_MIX + B_GATE_RANK, B_V_WIDTH),
    (OFF_GLOW, _N_MIX, B_GATE_RANK),
)
PACK_TILE = 1024
PACK_WINDOW = PACK_TILE + B_GATE_RANK


def _pack_plan(n_native):
    plan = []
    for tile in range(pl.cdiv(PROJ_USED, PACK_TILE)):
        lo, hi = tile * PACK_TILE, (tile + 1) * PACK_TILE
        pieces = []
        for dst, src, n in _FEATURE_RUNS:
            a, b = max(lo, dst), min(hi, dst + n)
            if a < b:
                pieces.append((a - lo, src + a - dst, b - a))
        start = min(min(p[1] for p in pieces), n_native - PACK_WINDOW)
        assert all(start <= s and s + n <= start + PACK_WINDOW for _, s, n in pieces), (tile, pieces)
        assert start % BF16_SUBLANES == 0 and all(d % BF16_SUBLANES == 0 and (s - start) % BF16_SUBLANES == 0
                                                   for d, s, _ in pieces)
        plan.append((start, [(d, s - start, n) for d, s, n in pieces]))
    return plan


HEAD_RING = 3


def _inproj_head_kernel(plan, x_ref, g_ref, w_hbm, o_ref, wt_ref, u_ref, ring_ref, sem_ref):
    j = pl.program_id(0)

    def window_copy(tile):
        slot = tile % HEAD_RING
        return pltpu.make_async_copy(w_hbm.at[pl.ds(plan[tile][0], PACK_WINDOW), :],
                                     ring_ref.at[slot], sem_ref.at[slot])

    for tile, (_, pieces) in enumerate(plan):
        @pl.when(j == tile)
        def _(tile=tile, pieces=pieces):
            if tile == 0:
                for ahead in range(min(HEAD_RING - 1, len(plan))):
                    window_copy(ahead).start()
            if tile + HEAD_RING - 1 < len(plan):
                window_copy(tile + HEAD_RING - 1).start()
            if tile == 0:
                _normalize_rows(x_ref, g_ref, u_ref)
            window_copy(tile).wait()
            w_ref = ring_ref.at[tile % HEAD_RING]
            covered = 0
            for dst, src, n in sorted(pieces):
                assert dst == covered
                wt_ref[dst:dst + n, :] = w_ref[src:src + n, :].astype(BF16)
                covered += n
            if covered < PACK_TILE:
                wt_ref[covered:, :] = jnp.zeros((PACK_TILE - covered, wt_ref.shape[1]), BF16)
    _project(u_ref, wt_ref, o_ref)


def _inproj_tail_kernel(x_hbm, g_ref, w_ref, _, o_ref, u_ref, x_ref, sem_ref):
    i, j = pl.program_id(0), pl.program_id(1)
    tm = x_ref.shape[0]

    def x_copy(tile):
        return pltpu.make_async_copy(x_hbm.at[pl.ds(pl.multiple_of(tile * tm, tm), tm), :], x_ref, sem_ref.at[0])

    @pl.when(j == 0)
    def _():
        pl.when(i == 0)(lambda: x_copy(0).start())
        x_copy(i).wait()
        _normalize_rows(x_ref, g_ref, u_ref)
        pl.when(i + 1 < pl.num_programs(0))(lambda: x_copy(i + 1).start())

    _project(u_ref, w_ref, o_ref)


def _inproj(x2, g, w_t, tm_head, tm, tn):
    t, d = x2.shape
    n_native = w_t.shape[0]
    plan = _pack_plan(n_native)
    n = len(plan) * PACK_TILE
    n_tail = pl.cdiv(PROJ_USED, tn) * tn
    t_tail = t - tm_head
    assert n_tail <= n and t % tm_head == 0 and t_tail % tm == 0
    head_block = t // tm_head - 1

    proj, w_p = pl.pallas_call(
        functools.partial(_inproj_head_kernel, plan),
        grid=(len(plan),),
        in_specs=[
            pl.BlockSpec((tm_head, d), lambda j: (head_block, 0), pipeline_mode=pl.Buffered(1)),
            _const_spec(g.shape),
            pl.BlockSpec(memory_space=pl.ANY),
        ],
        out_specs=[pl.BlockSpec((tm_head, PACK_TILE), lambda j: (head_block, j)),
                   pl.BlockSpec((PACK_TILE, d), lambda j: (j, 0))],
        out_shape=[jax.ShapeDtypeStruct((t, n), BF16), jax.ShapeDtypeStruct((n, d), BF16)],
        scratch_shapes=[pltpu.VMEM((tm_head, d), BF16),
                        pltpu.VMEM((HEAD_RING, PACK_WINDOW, d), F32),
                        pltpu.SemaphoreType.DMA((HEAD_RING,))],
        compiler_params=_cparams(("arbitrary",)),
        name="inproj_head",
    )(x2, g, w_t)

    return pl.pallas_call(
        _inproj_tail_kernel,
        grid=(t_tail // tm, n_tail // tn),
        in_specs=[
            pl.BlockSpec(memory_space=pl.ANY),
            pl.BlockSpec((1, d), lambda i, j: (0, 0)),
            pl.BlockSpec((tn, d), lambda i, j: (j, 0)),
            pl.BlockSpec(memory_space=pl.ANY),
        ],
        out_specs=pl.BlockSpec((tm, tn), lambda i, j: (i, j)),
        out_shape=jax.ShapeDtypeStruct((t, n), BF16),
        input_output_aliases={3: 0},
        scratch_shapes=[pltpu.VMEM((tm, d), BF16), pltpu.VMEM((tm, d), F32), pltpu.SemaphoreType.DMA((1,))],
        compiler_params=_cparams(("arbitrary", "arbitrary")),
        name="inproj_tail",
    )(x2, g, w_p, proj)


def _bucket_starts():
    max_exact = N_BUCKETS // 2
    d = np.arange(WINDOW)
    large = max_exact + (np.log(np.maximum(d, 1).astype(np.float32) / max_exact)
                         / math.log(MAX_DISTANCE / max_exact)
                         * (N_BUCKETS - max_exact)).astype(np.int32)
    bucket = np.where(d < max_exact, d, np.minimum(large, N_BUCKETS - 1))
    starts = []
    for b in range(N_BUCKETS):
        hit = np.nonzero(bucket == b)[0]
        if hit.size:
            assert np.all(np.diff(hit) == 1)
            starts.append((b, int(hit[0])))
    return starts


HEADS_PER_TILE = LANES // A_HEAD_DIM
SWA_STAGES = 3
SWA_BLOCKS_PER_STEP = 4


def _swa_body(q_ref, kp_ref, kc_ref, vp_ref, vc_ref, sink_ref, rb_ref, o_ref, bias_ref):
    n_keys = 2 * A_BLOCK

    @pl.when((pl.program_id(0) == 0) & (pl.program_id(1) == 0))
    def _():
        row = lax.broadcasted_iota(jnp.int32, (A_BLOCK, n_keys), 0)
        col = lax.broadcasted_iota(jnp.int32, (A_BLOCK, n_keys), 1)
        dist = row + A_BLOCK - col
        band = (dist >= 0) & (dist < WINDOW)
        starts = _bucket_starts()
        for h in range(A_HEADS):
            val = jnp.full(dist.shape, rb_ref[h, starts[0][0]], F32)
            for b, s in starts[1:]:
                val = jnp.where(dist >= s, rb_ref[h, b], val)
            val = jnp.where(band, val, NEG_INF)
            sink = sink_ref[0, h]
            bias_ref[0, h] = jnp.where(col == 0, sink, val)
            bias_ref[1, h] = jnp.where(col == 0, sink, jnp.where(col >= A_BLOCK, val, NEG_INF))

    lane = lax.broadcasted_iota(jnp.int32, (1, LANES), 1)
    scale = A_HEAD_DIM ** -0.5
    q_keep = (jnp.where(lane < A_HEAD_DIM, scale, 0.0).astype(BF16),
              jnp.where(lane < A_HEAD_DIM, 0.0, scale).astype(BF16))
    lower_lanes = lax.broadcasted_iota(jnp.int32, (A_BLOCK, LANES), 1) < A_HEAD_DIM
    key0 = lax.broadcasted_iota(jnp.int32, (n_keys, LANES), 0) == 0
    ones = jnp.ones((n_keys, LANES), BF16)

    def attend(q_rows, prev, cur, first):
        def both_blocks(which, tile):
            cols = slice(tile * LANES, (tile + 1) * LANES)
            cat = jnp.concatenate([prev[which][:, cols], cur[which][:, cols]], axis=0).astype(F32)
            cat = jnp.where(key0, 0.0, cat)
            return cat.astype(BF16), pltpu.roll(cat, A_HEAD_DIM, 1).astype(BF16)

        stacks = []
        for tile in range(A_KV_WIDTH // LANES):
            k_cat, k_swp = both_blocks(0, tile)
            v_cat, v_swp = both_blocks(1, tile)
            q0 = tile * A_GROUP
            stacks.append((k_cat, jnp.concatenate([v_cat, ones], axis=1),
                           [(q0, 0), (q0 + 1, 0), (q0 + 2, 1), (q0 + 3, 1)]))
            stacks.append((k_swp, jnp.concatenate([v_swp, ones], axis=1),
                           [(q0, 1), (q0 + 1, 1), (q0 + 2, 0), (q0 + 3, 0)]))

        scores = []
        for k_tile, _, members in stacks:
            q4 = jnp.concatenate(
                [q_ref[q_rows, qt * LANES:(qt + 1) * LANES] * q_keep[half] for qt, half in members],
                axis=0)
            scores.append(lax.dot_general(q4, k_tile, (((1,), (1,)), ((), ())),
                                          preferred_element_type=F32))
        yield
        s = jnp.concatenate(scores, axis=0)
        s = s + jnp.concatenate(
            [bias_ref[first, qt * HEADS_PER_TILE + half]
             for _, _, members in stacks for qt, half in members], axis=0)
        p = jnp.exp(s - jnp.max(s, axis=-1, keepdims=True)).astype(BF16)
        yield

        normed = {}
        rows_per_stack = len(stacks[0][2]) * A_BLOCK
        for i, (_, v_ones, members) in enumerate(stacks):
            ov = jnp.dot(p[i * rows_per_stack:(i + 1) * rows_per_stack], v_ones,
                         preferred_element_type=F32)
            o = ov[:, :LANES] / ov[:, LANES:]
            for j, member in enumerate(members):
                normed[member] = o[j * A_BLOCK:(j + 1) * A_BLOCK]
        for qt in range(A_Q_WIDTH // LANES):
            o_ref[q_rows, qt * LANES:(qt + 1) * LANES] = jnp.where(
                lower_lanes, normed[(qt, 0)], normed[(qt, 1)]).astype(o_ref.dtype)

    blocks_per_step = q_ref.shape[0] // A_BLOCK
    kv_prev = (kp_ref[...], vp_ref[...])
    stages = []
    for sub in range(blocks_per_step):
        rows = slice(sub * A_BLOCK, (sub + 1) * A_BLOCK)
        kv_cur = (kc_ref[rows, :], vc_ref[rows, :])
        first = (pl.program_id(1) == 0).astype(jnp.int32) if sub == 0 else 0
        stages.append(attend(rows, kv_prev, kv_cur, first))
        kv_prev = kv_cur
    for _ in range(SWA_STAGES):
        for stage in stages:
            next(stage, None)


GLA_CHUNKS_PER_STEP = 8
GLA_PREFIX_GROUP = 1


def _split3(x):
    hi = x.astype(BF16)
    r1 = x - hi.astype(F32)
    mid = r1.astype(BF16)
    lo = (r1 - mid.astype(F32)).astype(BF16)
    return jnp.concatenate([hi, mid, lo], axis=0)


def _gla_body(q_ref, k_ref, v0_ref, v1_ref, gl_ref, og0_ref, og1_ref, wgu_ref, bg_ref, ng_ref,
              o_ref, s_ref):
    c = B_CHUNK
    n_chunks = GLA_CHUNKS_PER_STEP
    n_rows = n_chunks * c
    heads_per_half = HALF_V // B_VAL_DIM
    v_refs = (v0_ref, v1_ref)
    og_refs = (og0_ref, og1_ref)
    chunk_rows = [slice(j * c, (j + 1) * c) for j in range(n_chunks)]
    key_cols = [slice(h * B_KEY_DIM, (h + 1) * B_KEY_DIM) for h in range(B_HEADS)]
    units = [(j, h) for j in range(n_chunks) for h in range(B_HEADS)]

    def v_of(refs, j, h):
        lo = (h % heads_per_half) * B_VAL_DIM
        rows = slice(None) if j is None else chunk_rows[j]
        return refs[h // heads_per_half][rows, lo:lo + B_VAL_DIM]

    @pl.when(pl.program_id(1) == 0)
    def _():
        s_ref[...] = jnp.zeros_like(s_ref)

    rank = wgu_ref.shape[0]
    wgu = jnp.concatenate([wgu_ref[...].astype(BF16),
                           jnp.zeros((gl_ref.shape[1] - rank, wgu_ref.shape[1]), BF16)], axis=0)
    glin = jnp.dot(gl_ref[...], wgu, preferred_element_type=F32) + bg_ref[...]
    log_a = (jnp.minimum(glin, 0.0) - jnp.log(1.0 + jnp.exp(-jnp.abs(glin)))) / B_GATE_TAU

    g_rows = GLA_PREFIX_GROUP * c
    ri = lax.broadcasted_iota(jnp.int32, (g_rows, 3 * g_rows), 0)
    ci = lax.broadcasted_iota(jnp.int32, (g_rows, 3 * g_rows), 1)
    ci = ci - jnp.where(ci >= g_rows, g_rows, 0) - jnp.where(ci >= 2 * g_rows, g_rows, 0)
    shift = int(math.log2(c))
    same_chunk = lax.shift_right_logical(ri, shift) == lax.shift_right_logical(ci, shift)
    tri3 = ((ri >= ci) & same_chunk).astype(BF16)
    b = jnp.concatenate(
        [jnp.dot(tri3, _split3(log_a[r0:r0 + g_rows]), preferred_element_type=F32)
         for r0 in range(0, n_rows, g_rows)], axis=0)
    last_rows = [b[(j + 1) * c - 1:(j + 1) * c, :] for j in range(n_chunks)]
    b_last = jnp.concatenate([jnp.broadcast_to(r, (c, b.shape[1])) for r in last_rows], axis=0)

    qf = q_ref[...].astype(F32) * (B_KEY_DIM ** -0.5)
    kf = k_ref[...].astype(F32)
    q_dec = (qf * jnp.exp(b)).astype(BF16)
    k_dec = (kf * jnp.exp(-b)).astype(BF16)
    k_state = kf * jnp.exp(b_last - b)
    sublanes = 8
    pad = [jnp.zeros((sublanes - n_chunks, b.shape[1]), F32)] if n_chunks < sublanes else []
    decay_rows = jnp.exp(jnp.concatenate(last_rows + pad, axis=0))

    ri = lax.broadcasted_iota(jnp.int32, (c, c), 0)
    ci = lax.broadcasted_iota(jnp.int32, (c, c), 1)
    causal = ri >= ci
    att = {}
    for j, h in units:
        a = lax.dot_general(q_dec[chunk_rows[j], key_cols[h]], k_dec[chunk_rows[j], key_cols[h]],
                            (((1,), (1,)), ((), ())), preferred_element_type=F32)
        att[j, h] = jnp.where(causal, a, 0.0).astype(BF16)
    o_intra = {u: jnp.dot(att[u], v_of(v_refs, *u), preferred_element_type=F32) for u in units}
    ds = {(j, h): jnp.dot(k_state[chunk_rows[j], key_cols[h]].T.astype(BF16), v_of(v_refs, j, h),
                          preferred_element_type=F32) for j, h in units}

    entering = {}
    for h in range(B_HEADS):
        decay_t = decay_rows[:, key_cols[h]].T
        state = s_ref[h]
        for j in range(n_chunks):
            entering[j, h] = state.astype(BF16)
            state = decay_t[:, j:j + 1] * state + ds[j, h]
        s_ref[h] = state
    o_inter = {(j, h): jnp.dot(q_dec[chunk_rows[j], key_cols[h]], entering[j, h],
                               preferred_element_type=F32) for j, h in units}

    ng = ng_ref[...]
    for h in range(B_HEADS):
        o = jnp.concatenate([o_intra[j, h] + o_inter[j, h] for j in range(n_chunks)], axis=0)
        gate = v_of(og_refs, None, h).astype(F32)
        y = _rmsnorm_rows(o, ng) * (gate * jax.nn.sigmoid(gate))
        o_ref[:, h * B_VAL_DIM:(h + 1) * B_VAL_DIM] = y.astype(o_ref.dtype)


N_MIX_IN = 7


def _mixers_kernel(n_cast, *refs):
    mix_ref, prev_ref, sink_ref, rb_ref, wgu_ref, bg_ref, ng_ref = refs[:N_MIX_IN]
    cast_in, refs = refs[N_MIX_IN:N_MIX_IN + n_cast], refs[N_MIX_IN + n_cast:]
    attn_ref, gla_ref = refs[:2]
    cast_out, (bias_ref, s_ref) = refs[2:2 + n_cast], refs[2 + n_cast:]
    _cast_blocks(cast_in, cast_out)

    def cols(off, width):
        return mix_ref.at[:, off - OFF_QA:off - OFF_QA + width]

    kv_prev = [prev_ref.at[:, off - OFF_KA:off - OFF_KA + A_KV_WIDTH] for off in (OFF_KA, OFF_VA)]
    _swa_body(cols(OFF_QA, A_Q_WIDTH), kv_prev[0], cols(OFF_KA, A_KV_WIDTH),
              kv_prev[1], cols(OFF_VA, A_KV_WIDTH), sink_ref, rb_ref, attn_ref, bias_ref)
    _gla_body(cols(OFF_QB, B_QK_WIDTH), cols(OFF_KB, B_QK_WIDTH),
              cols(OFF_VB, HALF_V), cols(OFF_VB + HALF_V, HALF_V), cols(OFF_GLOW, LANES),
              cols(OFF_OBG, HALF_V), cols(OFF_OBG + HALF_V, HALF_V),
              wgu_ref, bg_ref, ng_ref, gla_ref, s_ref)


def _mixers(proj, sinks, rel_bias, wgu, bg, ng, cast_weights, batch, seq):
    t = proj.shape[0]
    rows = GLA_CHUNKS_PER_STEP * B_CHUNK
    assert rows == SWA_BLOCKS_PER_STEP * A_BLOCK and seq % rows == 0
    steps = seq // rows
    rb = lambda b, s: b * steps + s
    prev = lambda b, s: jnp.maximum(rb(b, s) * SWA_BLOCKS_PER_STEP - 1, 0)
    smem = functools.partial(pl.BlockSpec, memory_space=pltpu.SMEM)
    cast_in, cast_out, cast_shapes = _cast_specs(cast_weights, batch * steps, rb)
    kv_width = OFF_QB - OFF_KA
    assert OFF_KA % kv_width == 0 and OFF_VA == OFF_KA + A_KV_WIDTH
    mix_specs = [
        pl.BlockSpec((pl.Element(rows), pl.Element(PROJ_USED - OFF_QA)),
                     lambda b, s: (rb(b, s) * rows, OFF_QA)),
        pl.BlockSpec((A_BLOCK, kv_width), lambda b, s: (prev(b, s), OFF_KA // kv_width)),
        smem(), smem(), _const_spec(wgu.shape), _const_spec(bg.shape), _const_spec(ng.shape)]
    assert len(mix_specs) == N_MIX_IN
    out_block = lambda width: pl.BlockSpec((rows, width), lambda b, s: (rb(b, s), 0))
    outs = pl.pallas_call(
        functools.partial(_mixers_kernel, len(cast_weights)),
        grid=(batch, steps),
        in_specs=mix_specs + cast_in,
        out_specs=[out_block(A_Q_WIDTH), out_block(B_V_WIDTH)] + cast_out,
        out_shape=[jax.ShapeDtypeStruct((t, A_Q_WIDTH), BF16),
                   jax.ShapeDtypeStruct((t, B_V_WIDTH), BF16)] + cast_shapes,
        scratch_shapes=[pltpu.VMEM((2, A_HEADS, A_BLOCK, 2 * A_BLOCK), F32),
                        pltpu.VMEM((B_HEADS, B_KEY_DIM, B_VAL_DIM), F32)],
        compiler_params=_cparams(("arbitrary", "arbitrary")),
        name="mixers",
    )(proj, proj, sinks, rel_bias.T, wgu, bg, ng, *cast_weights)
    return outs[0], outs[1], outs[2:]


def _merge_kernel(a_ref, b_ref, gates_ref, x_ref, wa_ref, wb_ref, wo_ref, gz_ref, h_ref, z_ref):
    d = x_ref.shape[1]
    ya = jnp.dot(a_ref[...], wa_ref[...], preferred_element_type=F32)
    yb = jnp.dot(b_ref[...], wb_ref[...], preferred_element_type=F32)
    merged = (jax.nn.sigmoid(gates_ref[:, :d].astype(F32)) * ya
              + jax.nn.sigmoid(gates_ref[:, d:].astype(F32)) * yb)
    h = x_ref[...] + jnp.dot(merged.astype(BF16), wo_ref[...], preferred_element_type=F32)
    h_ref[...] = h
    z_ref[...] = _rmsnorm_rows(h, gz_ref[...]).astype(z_ref.dtype)


def _merge(attn, gla, proj, x2, wa, wb, wo, gz, tm):
    t, d = x2.shape
    row_block = pl.BlockSpec((tm, d), lambda i: (i, 0))
    return pl.pallas_call(
        _merge_kernel,
        grid=(t // tm,),
        in_specs=[
            pl.BlockSpec((tm, A_Q_WIDTH), lambda i: (i, 0)),
            pl.BlockSpec((tm, B_V_WIDTH), lambda i: (i, 0)),
            pl.BlockSpec((tm, 2 * d), lambda i: (i, OFF_GATE_A // (2 * d))),
            row_block,
            _const_spec(wa.shape), _const_spec(wb.shape), _const_spec(wo.shape), _const_spec(gz.shape),
        ],
        out_specs=[row_block, row_block],
        out_shape=[jax.ShapeDtypeStruct((t, d), F32), jax.ShapeDtypeStruct((t, d), BF16)],
        compiler_params=_cparams(("parallel",)),
        name="merge",
    )(attn, gla, proj, x2, wa, wb, wo, gz)


def _ffn_kernel(z_ref, h_ref, wg_ref, wu_ref, wd_ref, gf_ref, o_ref):
    f = pl.program_id(1)
    z = z_ref[...]
    half = wg_ref.shape[1] // 2
    halves = [slice(c * half, (c + 1) * half) for c in range(2)]
    gu = [(jnp.dot(z, wg_ref[:, cols], preferred_element_type=F32),
           jnp.dot(z, wu_ref[:, cols], preferred_element_type=F32)) for cols in halves]
    for c, cols in enumerate(halves):
        g, u = gu[c]
        act = (g * jax.nn.sigmoid(g) * u).astype(BF16)
        acc = jnp.where(f == 0, 0.0, o_ref[...]) if c == 0 else o_ref[...]
        o_ref[...] = acc + jnp.dot(act, wd_ref[cols, :], preferred_element_type=F32)

    @pl.when(f == pl.num_programs(1) - 1)
    def _():
        gf = gf_ref[...]

        def body(c, carry):
            rows = pl.ds(pl.multiple_of(c * OUT_NORM_ROWS, OUT_NORM_ROWS), OUT_NORM_ROWS)
            o_ref[rows, :] = _rmsnorm_rows(h_ref[rows, :] + o_ref[rows, :], gf)
            return carry

        lax.fori_loop(0, h_ref.shape[0] // OUT_NORM_ROWS, body, 0)


def _ffn(z, h, wg, wu, wd, gf, tm, tf):
    t, d = h.shape
    f = wg.shape[1]
    row_block = pl.BlockSpec((tm, d), lambda i, j: (i, 0))
    return pl.pallas_call(
        _ffn_kernel,
        grid=(t // tm, f // tf),
        in_specs=[
            row_block, row_block,
            pl.BlockSpec((d, tf), lambda i, j: (0, j)),
            pl.BlockSpec((d, tf), lambda i, j: (0, j)),
            pl.BlockSpec((tf, d), lambda i, j: (j, 0)),
            pl.BlockSpec((1, d), lambda i, j: (0, 0)),
        ],
        out_specs=row_block,
        out_shape=jax.ShapeDtypeStruct((t, d), F32),
        compiler_params=_cparams(("parallel", "arbitrary")),
        name="ffn",
    )(z, h, wg, wu, wd, gf)


def kernel(x, norm_mix_g, w_in, sinks, rel_bias, w_gate_up, b_gate, gla_norm_g, w_proj_a, w_proj_b,
           w_out, norm_ffn_g, w_ffn_gate, w_ffn_up, w_ffn_down, norm_final_g):
    batch, seq, d = x.shape
    assert d == D_MODEL and w_in.shape[0] == 1, "single-layer geometry"
    t = batch * seq
    x2 = x.reshape(t, d)

    proj = _inproj(x2, norm_mix_g, w_in[0].T, tm_head=1024, tm=1792, tn=1792)

    later_weights = (w_proj_a[0], w_proj_b[0], w_out[0], w_ffn_gate[0], w_ffn_up[0], w_ffn_down[0])
    attn, gla, (wa, wb, wo, wg, wu, wd) = _mixers(proj, sinks, rel_bias, w_gate_up[0], b_gate, gla_norm_g,
                                                  later_weights, batch, seq)

    h, z = _merge(attn, gla, proj, x2, wa, wb, wo, norm_ffn_g, tm=512)

    out = _ffn(z, h, wg, wu, wd, norm_final_g.reshape(1, d), tm=1024, tf=512)
    return out.reshape(batch, seq, d)
```

```python
import functools
import math

import numpy as np
import jax
import jax.numpy as jnp
from jax import lax
from jax.experimental import pallas as pl
from jax.experimental.pallas import tpu as pltpu

F32 = jnp.float32
BF16 = jnp.bfloat16

D_MODEL = 2048
A_HEADS = 16
A_KV_HEADS = 4
A_HEAD_DIM = 64
A_GROUP = A_HEADS // A_KV_HEADS
WINDOW = 128
A_BLOCK = 128
A_Q_WIDTH = A_HEADS * A_HEAD_DIM
A_KV_WIDTH = A_KV_HEADS * A_HEAD_DIM
N_BUCKETS = 32
MAX_DISTANCE = 128
B_HEADS = 4
B_KEY_DIM = 128
B_VAL_DIM = 256
B_QK_WIDTH = B_HEADS * B_KEY_DIM
B_V_WIDTH = B_HEADS * B_VAL_DIM
B_GATE_RANK = 16
B_GATE_TAU = 16.0
B_CHUNK = 64
EPS = 1e-6
NEG_INF = -1e30

LANES = 128
BF16_SUBLANES = 16

OFF_GATE_A = 0
OFF_GATE_B = OFF_GATE_A + D_MODEL
OFF_QA = OFF_GATE_B + D_MODEL
OFF_KA = OFF_QA + A_Q_WIDTH
OFF_VA = OFF_KA + A_KV_WIDTH
OFF_QB = OFF_VA + A_KV_WIDTH
OFF_KB = OFF_QB + B_QK_WIDTH
OFF_VB = OFF_KB + B_QK_WIDTH
OFF_OBG = OFF_VB + B_V_WIDTH
OFF_GLOW = OFF_OBG + B_V_WIDTH
PROJ_USED = OFF_GLOW + LANES
HALF_V = B_V_WIDTH // 2

VMEM_LIMIT = 60 * 1024 * 1024


def _cparams(sem):
    return pltpu.CompilerParams(dimension_semantics=sem, vmem_limit_bytes=VMEM_LIMIT)


def _const_spec(shape):
    return pl.BlockSpec(shape, lambda *_: (0,) * len(shape), pipeline_mode=pl.Buffered(1))


def _rmsnorm_rows(x, g):
    ms = jnp.mean(x * x, axis=-1, keepdims=True)
    return x * lax.rsqrt(ms + EPS) * g


NORM_ROWS = 512
OUT_NORM_ROWS = 128


def _cast_specs(weights, n_chunks, chunk_of):
    in_specs, out_specs, shapes = [], [], []
    for w in weights:
        rows, rem = divmod(w.shape[0], n_chunks)
        assert rem == 0 and rows % BF16_SUBLANES == 0, (w.shape, n_chunks)
        for specs in (in_specs, out_specs):
            specs.append(pl.BlockSpec((rows, w.shape[1]), lambda *idx: (chunk_of(*idx), 0)))
        shapes.append(jax.ShapeDtypeStruct(w.shape, BF16))
    return in_specs, out_specs, shapes


def _cast_blocks(in_refs, out_refs):
    for src, dst in zip(in_refs, out_refs):
        dst[...] = src[...].astype(dst.dtype)


def _normalize_rows(x_ref, g_ref, u_ref):
    g = g_ref[...]

    def body(c, carry):
        rows = pl.ds(pl.multiple_of(c * NORM_ROWS, NORM_ROWS), NORM_ROWS)
        u_ref[rows, :] = _rmsnorm_rows(x_ref[rows, :], g).astype(BF16)
        return carry

    lax.fori_loop(0, x_ref.shape[0] // NORM_ROWS, body, 0)


def _project(u_ref, w_ref, o_ref):
    o_ref[...] = lax.dot_general(u_ref[...], w_ref[...], (((1,), (1,)), ((), ())),
                                 preferred_element_type=F32).astype(o_ref.dtype)


_N_MIX = OFF_OBG - OFF_QA
_FEATURE_RUNS = (
    (OFF_GATE_A, _N_MIX + B_GATE_RANK + B_V_WIDTH, 2 * D_MODEL),
    (OFF_QA, 0, _N_MIX),
    (OFF_OBG, _N_MIX + B_GATE_RANK, B_V_WIDTH),
    (OFF_GLOW, _N_MIX, B_GATE_RANK),
)
PACK_TILE = 1024
PACK_WINDOW = PACK_TILE + B_GATE_RANK


def _pack_plan(n_native):
    plan = []
    for tile in range(pl.cdiv(PROJ_USED, PACK_TILE)):
        lo, hi = tile * PACK_TILE, (tile + 1) * PACK_TILE
        pieces = []
        for dst, src, n in _FEATURE_RUNS:
            a, b = max(lo, dst), min(hi, dst + n)
            if a < b:
                pieces.append((a - lo, src + a - dst, b - a))
        start = min(min(p[1] for p in pieces), n_native - PACK_WINDOW)
        assert all(start <= s and s + n <= start + PACK_WINDOW for _, s, n in pieces), (tile, pieces)
        assert start % BF16_SUBLANES == 0 and all(d % BF16_SUBLANES == 0 and (s - start) % BF16_SUBLANES == 0
                                                   for d, s, _ in pieces)
        plan.append((start, [(d, s - start, n) for d, s, n in pieces]))
    return plan


HEAD_RING = 3


def _inproj_head_kernel(plan, x_ref, g_ref, w_hbm, o_ref, wt_ref, u_ref, ring_ref, sem_ref):
    j = pl.program_id(0)

    def window_copy(tile):
        slot = tile % HEAD_RING
        return pltpu.make_async_copy(w_hbm.at[pl.ds(plan[tile][0], PACK_WINDOW), :],
                                     ring_ref.at[slot], sem_ref.at[slot])

    for tile, (_, pieces) in enumerate(plan):
        @pl.when(j == tile)
        def _(tile=tile, pieces=pieces):
            if tile == 0:
                for ahead in range(min(HEAD_RING - 1, len(plan))):
                    window_copy(ahead).start()
            if tile + HEAD_RING - 1 < len(plan):
                window_copy(tile + HEAD_RING - 1).start()
            if tile == 0:
                _normalize_rows(x_ref, g_ref, u_ref)
            window_copy(tile).wait()
            w_ref = ring_ref.at[tile % HEAD_RING]
            covered = 0
            for dst, src, n in sorted(pieces):
                assert dst == covered
                wt_ref[dst:dst + n, :] = w_ref[src:src + n, :].astype(BF16)
                covered += n
            if covered < PACK_TILE:
                wt_ref[covered:, :] = jnp.zeros((PACK_TILE - covered, wt_ref.shape[1]), BF16)
    _project(u_ref, wt_ref, o_ref)


def _inproj_tail_kernel(x_ref, g_ref, w_ref, _, o_ref, u_ref):
    pl.when(pl.program_id(1) == 0)(lambda: _normalize_rows(x_ref, g_ref, u_ref))
    _project(u_ref, w_ref, o_ref)


def _inproj(x2, g, w_t, tm, tn):
    t, d = x2.shape
    n_native = w_t.shape[0]
    plan = _pack_plan(n_native)
    n = len(plan) * PACK_TILE
    n_tail = pl.cdiv(PROJ_USED, tn) * tn
    assert n_tail <= n and t % tm == 0

    proj, w_p = pl.pallas_call(
        functools.partial(_inproj_head_kernel, plan),
        grid=(len(plan),),
        in_specs=[
            pl.BlockSpec((tm, d), lambda j: (0, 0), pipeline_mode=pl.Buffered(1)),
            _const_spec(g.shape),
            pl.BlockSpec(memory_space=pl.ANY),
        ],
        out_specs=[pl.BlockSpec((tm, PACK_TILE), lambda j: (0, j)),
                   pl.BlockSpec((PACK_TILE, d), lambda j: (j, 0))],
        out_shape=[jax.ShapeDtypeStruct((t, n), BF16), jax.ShapeDtypeStruct((n, d), BF16)],
        scratch_shapes=[pltpu.VMEM((tm, d), BF16),
                        pltpu.VMEM((HEAD_RING, PACK_WINDOW, d), F32),
                        pltpu.SemaphoreType.DMA((HEAD_RING,))],
        compiler_params=_cparams(("arbitrary",)),
        name="inproj_head",
    )(x2, g, w_t)

    return pl.pallas_call(
        _inproj_tail_kernel,
        grid=(t // tm - 1, n_tail // tn),
        in_specs=[
            pl.BlockSpec((tm, d), lambda i, j: (i + 1, 0)),
            pl.BlockSpec((1, d), lambda i, j: (0, 0)),
            pl.BlockSpec((tn, d), lambda i, j: (j, 0)),
            pl.BlockSpec(memory_space=pl.ANY),
        ],
        out_specs=pl.BlockSpec((tm, tn), lambda i, j: (i + 1, j)),
        out_shape=jax.ShapeDtypeStruct((t, n), BF16),
        input_output_aliases={3: 0},
        scratch_shapes=[pltpu.VMEM((tm, d), BF16)],
        compiler_params=_cparams(("parallel", "arbitrary")),
        name="inproj_tail",
    )(x2, g, w_p, proj)


def _bucket_starts():
    max_exact = N_BUCKETS // 2
    d = np.arange(WINDOW)
    large = max_exact + (np.log(np.maximum(d, 1).astype(np.float32) / max_exact)
                         / math.log(MAX_DISTANCE / max_exact)
                         * (N_BUCKETS - max_exact)).astype(np.int32)
    bucket = np.where(d < max_exact, d, np.minimum(large, N_BUCKETS - 1))
    starts = []
    for b in range(N_BUCKETS):
        hit = np.nonzero(bucket == b)[0]
        if hit.size:
            assert np.all(np.diff(hit) == 1)
            starts.append((b, int(hit[0])))
    return starts


HEADS_PER_TILE = LANES // A_HEAD_DIM
SWA_STAGES = 3
SWA_BLOCKS_PER_STEP = 4


def _swa_body(q_ref, kp_ref, kc_ref, vp_ref, vc_ref, sink_ref, rb_ref, o_ref, bias_ref):
    n_keys = 2 * A_BLOCK

    @pl.when((pl.program_id(0) == 0) & (pl.program_id(1) == 0))
    def _():
        row = lax.broadcasted_iota(jnp.int32, (A_BLOCK, n_keys), 0)
        col = lax.broadcasted_iota(jnp.int32, (A_BLOCK, n_keys), 1)
        dist = row + A_BLOCK - col
        band = (dist >= 0) & (dist < WINDOW)
        starts = _bucket_starts()
        for h in range(A_HEADS):
            val = jnp.full(dist.shape, rb_ref[h, starts[0][0]], F32)
            for b, s in starts[1:]:
                val = jnp.where(dist >= s, rb_ref[h, b], val)
            val = jnp.where(band, val, NEG_INF)
            sink = sink_ref[0, h]
            bias_ref[0, h] = jnp.where(col == 0, sink, val)
            bias_ref[1, h] = jnp.where(col == 0, sink, jnp.where(col >= A_BLOCK, val, NEG_INF))

    lane = lax.broadcasted_iota(jnp.int32, (1, LANES), 1)
    scale = A_HEAD_DIM ** -0.5
    q_keep = (jnp.where(lane < A_HEAD_DIM, scale, 0.0).astype(BF16),
              jnp.where(lane < A_HEAD_DIM, 0.0, scale).astype(BF16))
    lower_lanes = lax.broadcasted_iota(jnp.int32, (A_BLOCK, LANES), 1) < A_HEAD_DIM
    key0 = lax.broadcasted_iota(jnp.int32, (n_keys, LANES), 0) == 0
    ones = jnp.ones((n_keys, LANES), BF16)

    def attend(q_rows, prev, cur, first):
        def both_blocks(which, tile):
            cols = slice(tile * LANES, (tile + 1) * LANES)
            cat = jnp.concatenate([prev[which][:, cols], cur[which][:, cols]], axis=0).astype(F32)
            cat = jnp.where(key0, 0.0, cat)
            return cat.astype(BF16), pltpu.roll(cat, A_HEAD_DIM, 1).astype(BF16)

        stacks = []
        for tile in range(A_KV_WIDTH // LANES):
            k_cat, k_swp = both_blocks(0, tile)
            v_cat, v_swp = both_blocks(1, tile)
            q0 = tile * A_GROUP
            stacks.append((k_cat, jnp.concatenate([v_cat, ones], axis=1),
                           [(q0, 0), (q0 + 1, 0), (q0 + 2, 1), (q0 + 3, 1)]))
            stacks.append((k_swp, jnp.concatenate([v_swp, ones], axis=1),
                           [(q0, 1), (q0 + 1, 1), (q0 + 2, 0), (q0 + 3, 0)]))

        scores = []
        for k_tile, _, members in stacks:
            q4 = jnp.concatenate(
                [q_ref[q_rows, qt * LANES:(qt + 1) * LANES] * q_keep[half] for qt, half in members],
                axis=0)
            scores.append(lax.dot_general(q4, k_tile, (((1,), (1,)), ((), ())),
                                          preferred_element_type=F32))
        yield
        s = jnp.concatenate(scores, axis=0)
        s = s + jnp.concatenate(
            [bias_ref[first, qt * HEADS_PER_TILE + half]
             for _, _, members in stacks for qt, half in members], axis=0)
        p = jnp.exp(s - jnp.max(s, axis=-1, keepdims=True)).astype(BF16)
        yield

        normed = {}
        rows_per_stack = len(stacks[0][2]) * A_BLOCK
        for i, (_, v_ones, members) in enumerate(stacks):
            ov = jnp.dot(p[i * rows_per_stack:(i + 1) * rows_per_stack], v_ones,
                         preferred_element_type=F32)
            o = ov[:, :LANES] / ov[:, LANES:]
            for j, member in enumerate(members):
                normed[member] = o[j * A_BLOCK:(j + 1) * A_BLOCK]
        for qt in range(A_Q_WIDTH // LANES):
            o_ref[q_rows, qt * LANES:(qt + 1) * LANES] = jnp.where(
                lower_lanes, normed[(qt, 0)], normed[(qt, 1)]).astype(o_ref.dtype)

    blocks_per_step = q_ref.shape[0] // A_BLOCK
    kv_prev = (kp_ref[...], vp_ref[...])
    stages = []
    for sub in range(blocks_per_step):
        rows = slice(sub * A_BLOCK, (sub + 1) * A_BLOCK)
        kv_cur = (kc_ref[rows, :], vc_ref[rows, :])
        first = (pl.program_id(1) == 0).astype(jnp.int32) if sub == 0 else 0
        stages.append(attend(rows, kv_prev, kv_cur, first))
        kv_prev = kv_cur
    for _ in range(SWA_STAGES):
        for stage in stages:
            next(stage, None)


GLA_CHUNKS_PER_STEP = 8
GLA_PREFIX_GROUP = 1


def _split3(x):
    hi = x.astype(BF16)
    r1 = x - hi.astype(F32)
    mid = r1.astype(BF16)
    lo = (r1 - mid.astype(F32)).astype(BF16)
    return jnp.concatenate([hi, mid, lo], axis=0)


def _gla_body(q_ref, k_ref, v0_ref, v1_ref, gl_ref, og0_ref, og1_ref, wgu_ref, bg_ref, ng_ref,
              o_ref, s_ref):
    c = B_CHUNK
    n_chunks = GLA_CHUNKS_PER_STEP
    n_rows = n_chunks * c
    heads_per_half = HALF_V // B_VAL_DIM
    v_refs = (v0_ref, v1_ref)
    og_refs = (og0_ref, og1_ref)
    chunk_rows = [slice(j * c, (j + 1) * c) for j in range(n_chunks)]
    key_cols = [slice(h * B_KEY_DIM, (h + 1) * B_KEY_DIM) for h in range(B_HEADS)]
    units = [(j, h) for j in range(n_chunks) for h in range(B_HEADS)]

    def v_of(refs, j, h):
        lo = (h % heads_per_half) * B_VAL_DIM
        rows = slice(None) if j is None else chunk_rows[j]
        return refs[h // heads_per_half][rows, lo:lo + B_VAL_DIM]

    @pl.when(pl.program_id(1) == 0)
    def _():
        s_ref[...] = jnp.zeros_like(s_ref)

    rank = wgu_ref.shape[0]
    wgu = jnp.concatenate([wgu_ref[...].astype(BF16),
                           jnp.zeros((gl_ref.shape[1] - rank, wgu_ref.shape[1]), BF16)], axis=0)
    glin = jnp.dot(gl_ref[...], wgu, preferred_element_type=F32) + bg_ref[...]
    log_a = (jnp.minimum(glin, 0.0) - jnp.log(1.0 + jnp.exp(-jnp.abs(glin)))) / B_GATE_TAU

    g_rows = GLA_PREFIX_GROUP * c
    ri = lax.broadcasted_iota(jnp.int32, (g_rows, 3 * g_rows), 0)
    ci = lax.broadcasted_iota(jnp.int32, (g_rows, 3 * g_rows), 1)
    ci = ci - jnp.where(ci >= g_rows, g_rows, 0) - jnp.where(ci >= 2 * g_rows, g_rows, 0)
    shift = int(math.log2(c))
    same_chunk = lax.shift_right_logical(ri, shift) == lax.shift_right_logical(ci, shift)
    tri3 = ((ri >= ci) & same_chunk).astype(BF16)
    b = jnp.concatenate(
        [jnp.dot(tri3, _split3(log_a[r0:r0 + g_rows]), preferred_element_type=F32)
         for r0 in range(0, n_rows, g_rows)], axis=0)
    last_rows = [b[(j + 1) * c - 1:(j + 1) * c, :] for j in range(n_chunks)]
    b_last = jnp.concatenate([jnp.broadcast_to(r, (c, b.shape[1])) for r in last_rows], axis=0)

    qf = q_ref[...].astype(F32) * (B_KEY_DIM ** -0.5)
    kf = k_ref[...].astype(F32)
    q_dec = (qf * jnp.exp(b)).astype(BF16)
    k_dec = (kf * jnp.exp(-b)).astype(BF16)
    k_state = kf * jnp.exp(b_last - b)
    sublanes = 8
    pad = [jnp.zeros((sublanes - n_chunks, b.shape[1]), F32)] if n_chunks < sublanes else []
    decay_rows = jnp.exp(jnp.concatenate(last_rows + pad, axis=0))

    ri = lax.broadcasted_iota(jnp.int32, (c, c), 0)
    ci = lax.broadcasted_iota(jnp.int32, (c, c), 1)
    causal = ri >= ci
    att = {}
    for j, h in units:
        a = lax.dot_general(q_dec[chunk_rows[j], key_cols[h]], k_dec[chunk_rows[j], key_cols[h]],
                            (((1,), (1,)), ((), ())), preferred_element_type=F32)
        att[j, h] = jnp.where(causal, a, 0.0).astype(BF16)
    o_intra = {u: jnp.dot(att[u], v_of(v_refs, *u), preferred_element_type=F32) for u in units}
    ds = {(j, h): jnp.dot(k_state[chunk_rows[j], key_cols[h]].T.astype(BF16), v_of(v_refs, j, h),
                          preferred_element_type=F32) for j, h in units}

    entering = {}
    for h in range(B_HEADS):
        decay_t = decay_rows[:, key_cols[h]].T
        state = s_ref[h]
        for j in range(n_chunks):
            entering[j, h] = state.astype(BF16)
            state = decay_t[:, j:j + 1] * state + ds[j, h]
        s_ref[h] = state
    o_inter = {(j, h): jnp.dot(q_dec[chunk_rows[j], key_cols[h]], entering[j, h],
                               preferred_element_type=F32) for j, h in units}

    ng = ng_ref[...]
    for h in range(B_HEADS):
        o = jnp.concatenate([o_intra[j, h] + o_inter[j, h] for j in range(n_chunks)], axis=0)
        gate = v_of(og_refs, None, h).astype(F32)
        y = _rmsnorm_rows(o, ng) * (gate * jax.nn.sigmoid(gate))
        o_ref[:, h * B_VAL_DIM:(h + 1) * B_VAL_DIM] = y.astype(o_ref.dtype)


N_MIX_IN = 7


def _mixers_kernel(n_cast, *refs):
    mix_ref, prev_ref, sink_ref, rb_ref, wgu_ref, bg_ref, ng_ref = refs[:N_MIX_IN]
    cast_in, refs = refs[N_MIX_IN:N_MIX_IN + n_cast], refs[N_MIX_IN + n_cast:]
    attn_ref, gla_ref = refs[:2]
    cast_out, (bias_ref, s_ref) = refs[2:2 + n_cast], refs[2 + n_cast:]
    _cast_blocks(cast_in, cast_out)

    def cols(off, width):
        return mix_ref.at[:, off - OFF_QA:off - OFF_QA + width]

    kv_prev = [prev_ref.at[:, off - OFF_KA:off - OFF_KA + A_KV_WIDTH] for off in (OFF_KA, OFF_VA)]
    _swa_body(cols(OFF_QA, A_Q_WIDTH), kv_prev[0], cols(OFF_KA, A_KV_WIDTH),
              kv_prev[1], cols(OFF_VA, A_KV_WIDTH), sink_ref, rb_ref, attn_ref, bias_ref)
    _gla_body(cols(OFF_QB, B_QK_WIDTH), cols(OFF_KB, B_QK_WIDTH),
              cols(OFF_VB, HALF_V), cols(OFF_VB + HALF_V, HALF_V), cols(OFF_GLOW, LANES),
              cols(OFF_OBG, HALF_V), cols(OFF_OBG + HALF_V, HALF_V),
              wgu_ref, bg_ref, ng_ref, gla_ref, s_ref)


def _mixers(proj, sinks, rel_bias, wgu, bg, ng, cast_weights, batch, seq):
    t = proj.shape[0]
    rows = GLA_CHUNKS_PER_STEP * B_CHUNK
    assert rows == SWA_BLOCKS_PER_STEP * A_BLOCK and seq % rows == 0
    steps = seq // rows
    rb = lambda b, s: b * steps + s
    prev = lambda b, s: jnp.maximum(rb(b, s) * SWA_BLOCKS_PER_STEP - 1, 0)
    smem = functools.partial(pl.BlockSpec, memory_space=pltpu.SMEM)
    cast_in, cast_out, cast_shapes = _cast_specs(cast_weights, batch * steps, rb)
    kv_width = OFF_QB - OFF_KA
    assert OFF_KA % kv_width == 0 and OFF_VA == OFF_KA + A_KV_WIDTH
    mix_specs = [
        pl.BlockSpec((pl.Element(rows), pl.Element(PROJ_USED - OFF_QA)),
                     lambda b, s: (rb(b, s) * rows, OFF_QA)),
        pl.BlockSpec((A_BLOCK, kv_width), lambda b, s: (prev(b, s), OFF_KA // kv_width)),
        smem(), smem(), _const_spec(wgu.shape), _const_spec(bg.shape), _const_spec(ng.shape)]
    assert len(mix_specs) == N_MIX_IN
    out_block = lambda width: pl.BlockSpec((rows, width), lambda b, s: (rb(b, s), 0))
    outs = pl.pallas_call(
        functools.partial(_mixers_kernel, len(cast_weights)),
        grid=(batch, steps),
        in_specs=mix_specs + cast_in,
        out_specs=[out_block(A_Q_WIDTH), out_block(B_V_WIDTH)] + cast_out,
        out_shape=[jax.ShapeDtypeStruct((t, A_Q_WIDTH), BF16),
                   jax.ShapeDtypeStruct((t, B_V_WIDTH), BF16)] + cast_shapes,
        scratch_shapes=[pltpu.VMEM((2, A_HEADS, A_BLOCK, 2 * A_BLOCK), F32),
                        pltpu.VMEM((B_HEADS, B_KEY_DIM, B_VAL_DIM), F32)],
        compiler_params=_cparams(("arbitrary", "arbitrary")),
        name="mixers",
    )(proj, proj, sinks, rel_bias.T, wgu, bg, ng, *cast_weights)
    return outs[0], outs[1], outs[2:]


def _merge_kernel(a_ref, b_ref, gates_ref, x_ref, wa_ref, wb_ref, wo_hbm, gz_ref, h_ref, z_ref,
                  wo_ref, sem_ref):
    d = x_ref.shape[1]
    first = pl.program_id(0) == 0
    wo_copy = pltpu.make_async_copy(wo_hbm, wo_ref, sem_ref.at[0])
    pl.when(first)(wo_copy.start)
    ya = jnp.dot(a_ref[...], wa_ref[...], preferred_element_type=F32)
    yb = jnp.dot(b_ref[...], wb_ref[...], preferred_element_type=F32)
    merged = (jax.nn.sigmoid(gates_ref[:, :d].astype(F32)) * ya
              + jax.nn.sigmoid(gates_ref[:, d:].astype(F32)) * yb)
    pl.when(first)(wo_copy.wait)
    h = x_ref[...] + jnp.dot(merged.astype(BF16), wo_ref[...], preferred_element_type=F32)
    h_ref[...] = h
    z_ref[...] = _rmsnorm_rows(h, gz_ref[...]).astype(z_ref.dtype)


def _merge(attn, gla, proj, x2, wa, wb, wo, gz, tm):
    t, d = x2.shape
    row_block = pl.BlockSpec((tm, d), lambda i: (i, 0))
    return pl.pallas_call(
        _merge_kernel,
        grid=(t // tm,),
        in_specs=[
            pl.BlockSpec((tm, A_Q_WIDTH), lambda i: (i, 0)),
            pl.BlockSpec((tm, B_V_WIDTH), lambda i: (i, 0)),
            pl.BlockSpec((tm, 2 * d), lambda i: (i, OFF_GATE_A // (2 * d))),
            row_block,
            _const_spec(wa.shape), _const_spec(wb.shape),
            pl.BlockSpec(memory_space=pl.ANY),
            _const_spec(gz.shape),
        ],
        out_specs=[row_block, row_block],
        out_shape=[jax.ShapeDtypeStruct((t, d), F32), jax.ShapeDtypeStruct((t, d), BF16)],
        scratch_shapes=[pltpu.VMEM(wo.shape, wo.dtype), pltpu.SemaphoreType.DMA((1,))],
        compiler_params=_cparams(("arbitrary",)),
        name="merge",
    )(attn, gla, proj, x2, wa, wb, wo, gz)


def _ffn_kernel(z_ref, h_ref, wg_ref, wu_ref, wd_ref, gf_ref, o_ref):
    f = pl.program_id(1)
    z = z_ref[...]
    half = wg_ref.shape[1] // 2
    halves = [slice(c * half, (c + 1) * half) for c in range(2)]
    gu = [(jnp.dot(z, wg_ref[:, cols], preferred_element_type=F32),
           jnp.dot(z, wu_ref[:, cols], preferred_element_type=F32)) for cols in halves]
    for c, cols in enumerate(halves):
        g, u = gu[c]
        act = (g * jax.nn.sigmoid(g) * u).astype(BF16)
        acc = jnp.where(f == 0, 0.0, o_ref[...]) if c == 0 else o_ref[...]
        o_ref[...] = acc + jnp.dot(act, wd_ref[cols, :], preferred_element_type=F32)

    @pl.when(f == pl.num_programs(1) - 1)
    def _():
        gf = gf_ref[...]

        def body(c, carry):
            rows = pl.ds(pl.multiple_of(c * OUT_NORM_ROWS, OUT_NORM_ROWS), OUT_NORM_ROWS)
            o_ref[rows, :] = _rmsnorm_rows(h_ref[rows, :] + o_ref[rows, :], gf)
            return carry

        lax.fori_loop(0, h_ref.shape[0] // OUT_NORM_ROWS, body, 0)


def _ffn(z, h, wg, wu, wd, gf, tm, tf):
    t, d = h.shape
    f = wg.shape[1]
    row_block = pl.BlockSpec((tm, d), lambda i, j: (i, 0))
    return pl.pallas_call(
        _ffn_kernel,
        grid=(t // tm, f // tf),
        in_specs=[
            row_block, row_block,
            pl.BlockSpec((d, tf), lambda i, j: (0, j)),
            pl.BlockSpec((d, tf), lambda i, j: (0, j)),
            pl.BlockSpec((tf, d), lambda i, j: (j, 0)),
            pl.BlockSpec((1, d), lambda i, j: (0, 0)),
        ],
        out_specs=row_block,
        out_shape=jax.ShapeDtypeStruct((t, d), F32),
        compiler_params=_cparams(("parallel", "arbitrary")),
        name="ffn",
    )(z, h, wg, wu, wd, gf)


def kernel(x, norm_mix_g, w_in, sinks, rel_bias, w_gate_up, b_gate, gla_norm_g, w_proj_a, w_proj_b,
           w_out, norm_ffn_g, w_ffn_gate, w_ffn_up, w_ffn_down, norm_final_g):
    batch, seq, d = x.shape
    assert d == D_MODEL and w_in.shape[0] == 1, "single-layer geometry"
    t = batch * seq
    x2 = x.reshape(t, d)

    proj = _inproj(x2, norm_mix_g, w_in[0].T, tm=1024, tn=1792)

    later_weights = (w_proj_a[0], w_proj_b[0], w_out[0], w_ffn_gate[0], w_ffn_up[0], w_ffn_down[0])
    attn, gla, (wa, wb, wo, wg, wu, wd) = _mixers(proj, sinks, rel_bias, w_gate_up[0], b_gate, gla_norm_g,
                                                  later_weights, batch, seq)

    h, z = _merge(attn, gla, proj, x2, wa, wb, wo, norm_ffn_g, tm=512)

    out = _ffn(z, h, wg, wu, wd, norm_final_g.reshape(1, d), tm=1024, tf=512)
    return out.reshape(batch, seq, d)
```

```python
import functools
import math

import numpy as np
import jax
import jax.numpy as jnp
from jax import lax
from jax.experimental import pallas as pl
from jax.experimental.pallas import tpu as pltpu

F32 = jnp.float32
BF16 = jnp.bfloat16

D_MODEL = 2048
A_HEADS = 16
A_KV_HEADS = 4
A_HEAD_DIM = 64
A_GROUP = A_HEADS // A_KV_HEADS
WINDOW = 128
A_BLOCK = 128
A_Q_WIDTH = A_HEADS * A_HEAD_DIM
A_KV_WIDTH = A_KV_HEADS * A_HEAD_DIM
N_BUCKETS = 32
MAX_DISTANCE = 128
B_HEADS = 4
B_KEY_DIM = 128
B_VAL_DIM = 256
B_QK_WIDTH = B_HEADS * B_KEY_DIM
B_V_WIDTH = B_HEADS * B_VAL_DIM
B_GATE_RANK = 16
B_GATE_TAU = 16.0
B_CHUNK = 64
EPS = 1e-6
NEG_INF = -1e30

LANES = 128
BF16_SUBLANES = 16

OFF_GATE_A = 0
OFF_GATE_B = OFF_GATE_A + D_MODEL
OFF_QA = OFF_GATE_B + D_MODEL
OFF_KA = OFF_QA + A_Q_WIDTH
OFF_VA = OFF_KA + A_KV_WIDTH
OFF_QB = OFF_VA + A_KV_WIDTH
OFF_KB = OFF_QB + B_QK_WIDTH
OFF_VB = OFF_KB + B_QK_WIDTH
OFF_OBG = OFF_VB + B_V_WIDTH
OFF_GLOW = OFF_OBG + B_V_WIDTH
PROJ_USED = OFF_GLOW + LANES
HALF_V = B_V_WIDTH // 2

VMEM_LIMIT = 60 * 1024 * 1024


def _cparams(sem):
    return pltpu.CompilerParams(dimension_semantics=sem, vmem_limit_bytes=VMEM_LIMIT)


def _const_spec(shape):
    return pl.BlockSpec(shape, lambda *_: (0,) * len(shape), pipeline_mode=pl.Buffered(1))


def _rmsnorm_rows(x, g):
    ms = jnp.mean(x * x, axis=-1, keepdims=True)
    return x * lax.rsqrt(ms + EPS) * g


NORM_ROWS = 512
OUT_NORM_ROWS = 128


def _cast_specs(weights, n_chunks, chunk_of):
    in_specs, out_specs, shapes = [], [], []
    for w in weights:
        rows, rem = divmod(w.shape[0], n_chunks)
        assert rem == 0 and rows % BF16_SUBLANES == 0, (w.shape, n_chunks)
        for specs in (in_specs, out_specs):
            specs.append(pl.BlockSpec((rows, w.shape[1]), lambda *idx: (chunk_of(*idx), 0)))
        shapes.append(jax.ShapeDtypeStruct(w.shape, BF16))
    return in_specs, out_specs, shapes


def _cast_blocks(in_refs, out_refs):
    for src, dst in zip(in_refs, out_refs):
        dst[...] = src[...].astype(dst.dtype)


def _normalize_rows(x_ref, g_ref, u_ref):
    g = g_ref[...]

    def body(c, carry):
        rows = pl.ds(pl.multiple_of(c * NORM_ROWS, NORM_ROWS), NORM_ROWS)
        u_ref[rows, :] = _rmsnorm_rows(x_ref[rows, :], g).astype(BF16)
        return carry

    lax.fori_loop(0, x_ref.shape[0] // NORM_ROWS, body, 0)


def _project(u_ref, w_ref, o_ref):
    o_ref[...] = lax.dot_general(u_ref[...], w_ref[...], (((1,), (1,)), ((), ())),
                                 preferred_element_type=F32).astype(o_ref.dtype)


_N_MIX = OFF_OBG - OFF_QA
_FEATURE_RUNS = (
    (OFF_GATE_A, _N_MIX + B_GATE_RANK + B_V_WIDTH, 2 * D_MODEL),
    (OFF_QA, 0, _N_MIX),
    (OFF_OBG, _N_MIX + B_GATE_RANK, B_V_WIDTH),
    (OFF_GLOW, _N_MIX, B_GATE_RANK),
)
PACK_TILE = 1024
PACK_WINDOW = PACK_TILE + B_GATE_RANK


def _pack_plan(n_native):
    plan = []
    for tile in range(pl.cdiv(PROJ_USED, PACK_TILE)):
        lo, hi = tile * PACK_TILE, (tile + 1) * PACK_TILE
        pieces = []
        for dst, src, n in _FEATURE_RUNS:
            a, b = max(lo, dst), min(hi, dst + n)
            if a < b:
                pieces.append((a - lo, src + a - dst, b - a))
        start = min(min(p[1] for p in pieces), n_native - PACK_WINDOW)
        assert all(start <= s and s + n <= start + PACK_WINDOW for _, s, n in pieces), (tile, pieces)
        assert start % BF16_SUBLANES == 0 and all(d % BF16_SUBLANES == 0 and (s - start) % BF16_SUBLANES == 0
                                                   for d, s, _ in pieces)
        plan.append((start, [(d, s - start, n) for d, s, n in pieces]))
    return plan


HEAD_RING = 3


def _inproj_head_kernel(plan, x_ref, g_ref, w_hbm, o_ref, wt_ref, u_ref, ring_ref, sem_ref):
    j = pl.program_id(0)

    def window_copy(tile):
        slot = tile % HEAD_RING
        return pltpu.make_async_copy(w_hbm.at[pl.ds(plan[tile][0], PACK_WINDOW), :],
                                     ring_ref.at[slot], sem_ref.at[slot])

    for tile, (_, pieces) in enumerate(plan):
        @pl.when(j == tile)
        def _(tile=tile, pieces=pieces):
            if tile == 0:
                for ahead in range(min(HEAD_RING - 1, len(plan))):
                    window_copy(ahead).start()
            if tile + HEAD_RING - 1 < len(plan):
                window_copy(tile + HEAD_RING - 1).start()
            if tile == 0:
                _normalize_rows(x_ref, g_ref, u_ref)
            window_copy(tile).wait()
            w_ref = ring_ref.at[tile % HEAD_RING]
            covered = 0
            for dst, src, n in sorted(pieces):
                assert dst == covered
                wt_ref[dst:dst + n, :] = w_ref[src:src + n, :].astype(BF16)
                covered += n
            if covered < PACK_TILE:
                wt_ref[covered:, :] = jnp.zeros((PACK_TILE - covered, wt_ref.shape[1]), BF16)
    _project(u_ref, wt_ref, o_ref)


def _inproj_tail_kernel(x_ref, g_ref, w_ref, _, o_ref, u_ref):
    pl.when(pl.program_id(1) == 0)(lambda: _normalize_rows(x_ref, g_ref, u_ref))
    _project(u_ref, w_ref, o_ref)


def _inproj(x2, g, w_t, tm, tn):
    t, d = x2.shape
    n_native = w_t.shape[0]
    plan = _pack_plan(n_native)
    n = len(plan) * PACK_TILE
    n_tail = pl.cdiv(PROJ_USED, tn) * tn
    assert n_tail <= n and t % tm == 0

    proj, w_p = pl.pallas_call(
        functools.partial(_inproj_head_kernel, plan),
        grid=(len(plan),),
        in_specs=[
            pl.BlockSpec((tm, d), lambda j: (0, 0), pipeline_mode=pl.Buffered(1)),
            _const_spec(g.shape),
            pl.BlockSpec(memory_space=pl.ANY),
        ],
        out_specs=[pl.BlockSpec((tm, PACK_TILE), lambda j: (0, j)),
                   pl.BlockSpec((PACK_TILE, d), lambda j: (j, 0))],
        out_shape=[jax.ShapeDtypeStruct((t, n), BF16), jax.ShapeDtypeStruct((n, d), BF16)],
        scratch_shapes=[pltpu.VMEM((tm, d), BF16),
                        pltpu.VMEM((HEAD_RING, PACK_WINDOW, d), F32),
                        pltpu.SemaphoreType.DMA((HEAD_RING,))],
        compiler_params=_cparams(("arbitrary",)),
        name="inproj_head",
    )(x2, g, w_t)

    return pl.pallas_call(
        _inproj_tail_kernel,
        grid=(t // tm - 1, n_tail // tn),
        in_specs=[
            pl.BlockSpec((tm, d), lambda i, j: (i + 1, 0)),
            pl.BlockSpec((1, d), lambda i, j: (0, 0)),
            pl.BlockSpec((tn, d), lambda i, j: (j, 0)),
            pl.BlockSpec(memory_space=pl.ANY),
        ],
        out_specs=pl.BlockSpec((tm, tn), lambda i, j: (i + 1, j)),
        out_shape=jax.ShapeDtypeStruct((t, n), BF16),
        input_output_aliases={3: 0},
        scratch_shapes=[pltpu.VMEM((tm, d), BF16)],
        compiler_params=_cparams(("parallel", "arbitrary")),
        name="inproj_tail",
    )(x2, g, w_p, proj)


def _bucket_starts():
    max_exact = N_BUCKETS // 2
    d = np.arange(WINDOW)
    large = max_exact + (np.log(np.maximum(d, 1).astype(np.float32) / max_exact)
                         / math.log(MAX_DISTANCE / max_exact)
                         * (N_BUCKETS - max_exact)).astype(np.int32)
    bucket = np.where(d < max_exact, d, np.minimum(large, N_BUCKETS - 1))
    starts = []
    for b in range(N_BUCKETS):
        hit = np.nonzero(bucket == b)[0]
        if hit.size:
            assert np.all(np.diff(hit) == 1)
            starts.append((b, int(hit[0])))
    return starts


HEADS_PER_TILE = LANES // A_HEAD_DIM
SWA_STAGES = 3
SWA_BLOCKS_PER_STEP = 4


def _swa_body(q_ref, kp_ref, kc_ref, vp_ref, vc_ref, sink_ref, rb_ref, o_ref, bias_ref):
    n_keys = 2 * A_BLOCK

    @pl.when((pl.program_id(0) == 0) & (pl.program_id(1) == 0))
    def _():
        row = lax.broadcasted_iota(jnp.int32, (A_BLOCK, n_keys), 0)
        col = lax.broadcasted_iota(jnp.int32, (A_BLOCK, n_keys), 1)
        dist = row + A_BLOCK - col
        band = (dist >= 0) & (dist < WINDOW)
        starts = _bucket_starts()
        for h in range(A_HEADS):
            val = jnp.full(dist.shape, rb_ref[h, starts[0][0]], F32)
            for b, s in starts[1:]:
                val = jnp.where(dist >= s, rb_ref[h, b], val)
            val = jnp.where(band, val, NEG_INF)
            sink = sink_ref[0, h]
            bias_ref[0, h] = jnp.where(col == 0, sink, val)
            bias_ref[1, h] = jnp.where(col == 0, sink, jnp.where(col >= A_BLOCK, val, NEG_INF))

    lane = lax.broadcasted_iota(jnp.int32, (1, LANES), 1)
    scale = A_HEAD_DIM ** -0.5
    q_keep = (jnp.where(lane < A_HEAD_DIM, scale, 0.0).astype(BF16),
              jnp.where(lane < A_HEAD_DIM, 0.0, scale).astype(BF16))
    lower_lanes = lax.broadcasted_iota(jnp.int32, (A_BLOCK, LANES), 1) < A_HEAD_DIM
    key0 = lax.broadcasted_iota(jnp.int32, (n_keys, LANES), 0) == 0
    ones = jnp.ones((n_keys, LANES), BF16)

    def attend(q_rows, prev, cur, first):
        def both_blocks(which, tile):
            cols = slice(tile * LANES, (tile + 1) * LANES)
            cat = jnp.concatenate([prev[which][:, cols], cur[which][:, cols]], axis=0).astype(F32)
            cat = jnp.where(key0, 0.0, cat)
            return cat.astype(BF16), pltpu.roll(cat, A_HEAD_DIM, 1).astype(BF16)

        stacks = []
        for tile in range(A_KV_WIDTH // LANES):
            k_cat, k_swp = both_blocks(0, tile)
            v_cat, v_swp = both_blocks(1, tile)
            q0 = tile * A_GROUP
            stacks.append((k_cat, jnp.concatenate([v_cat, ones], axis=1),
                           [(q0, 0), (q0 + 1, 0), (q0 + 2, 1), (q0 + 3, 1)]))
            stacks.append((k_swp, jnp.concatenate([v_swp, ones], axis=1),
                           [(q0, 1), (q0 + 1, 1), (q0 + 2, 0), (q0 + 3, 0)]))

        scores = []
        for k_tile, _, members in stacks:
            q4 = jnp.concatenate(
                [q_ref[q_rows, qt * LANES:(qt + 1) * LANES] * q_keep[half] for qt, half in members],
                axis=0)
            scores.append(lax.dot_general(q4, k_tile, (((1,), (1,)), ((), ())),
                                          preferred_element_type=F32))
        yield
        s = jnp.concatenate(scores, axis=0)
        s = s + jnp.concatenate(
            [bias_ref[first, qt * HEADS_PER_TILE + half]
             for _, _, members in stacks for qt, half in members], axis=0)
        p = jnp.exp(s - jnp.max(s, axis=-1, keepdims=True)).astype(BF16)
        yield

        normed = {}
        rows_per_stack = len(stacks[0][2]) * A_BLOCK
        for i, (_, v_ones, members) in enumerate(stacks):
            ov = jnp.dot(p[i * rows_per_stack:(i + 1) * rows_per_stack], v_ones,
                         preferred_element_type=F32)
            o = ov[:, :LANES] / ov[:, LANES:]
            for j, member in enumerate(members):
                normed[member] = o[j * A_BLOCK:(j + 1) * A_BLOCK]
        for qt in range(A_Q_WIDTH // LANES):
            o_ref[q_rows, qt * LANES:(qt + 1) * LANES] = jnp.where(
                lower_lanes, normed[(qt, 0)], normed[(qt, 1)]).astype(o_ref.dtype)

    blocks_per_step = q_ref.shape[0] // A_BLOCK
    kv_prev = (kp_ref[...], vp_ref[...])
    stages = []
    for sub in range(blocks_per_step):
        rows = slice(sub * A_BLOCK, (sub + 1) * A_BLOCK)
        kv_cur = (kc_ref[rows, :], vc_ref[rows, :])
        first = (pl.program_id(1) == 0).astype(jnp.int32) if sub == 0 else 0
        stages.append(attend(rows, kv_prev, kv_cur, first))
        kv_prev = kv_cur
    for _ in range(SWA_STAGES):
        for stage in stages:
            next(stage, None)


GLA_CHUNKS_PER_STEP = 8
GLA_PREFIX_GROUP = 1


def _split3(x):
    hi = x.astype(BF16)
    r1 = x - hi.astype(F32)
    mid = r1.astype(BF16)
    lo = (r1 - mid.astype(F32)).astype(BF16)
    return jnp.concatenate([hi, mid, lo], axis=0)


def _gla_body(q_ref, k_ref, v0_ref, v1_ref, gl_ref, og0_ref, og1_ref, wgu_ref, bg_ref, ng_ref,
              o_ref, s_ref):
    c = B_CHUNK
    n_chunks = GLA_CHUNKS_PER_STEP
    n_rows = n_chunks * c
    heads_per_half = HALF_V // B_VAL_DIM
    v_refs = (v0_ref, v1_ref)
    og_refs = (og0_ref, og1_ref)
    chunk_rows = [slice(j * c, (j + 1) * c) for j in range(n_chunks)]
    key_cols = [slice(h * B_KEY_DIM, (h + 1) * B_KEY_DIM) for h in range(B_HEADS)]
    units = [(j, h) for j in range(n_chunks) for h in range(B_HEADS)]

    def v_of(refs, j, h):
        lo = (h % heads_per_half) * B_VAL_DIM
        rows = slice(None) if j is None else chunk_rows[j]
        return refs[h // heads_per_half][rows, lo:lo + B_VAL_DIM]

    @pl.when(pl.program_id(1) == 0)
    def _():
        s_ref[...] = jnp.zeros_like(s_ref)

    rank = wgu_ref.shape[0]
    wgu = jnp.concatenate([wgu_ref[...].astype(BF16),
                           jnp.zeros((gl_ref.shape[1] - rank, wgu_ref.shape[1]), BF16)], axis=0)
    glin = jnp.dot(gl_ref[...], wgu, preferred_element_type=F32) + bg_ref[...]
    log_a = (jnp.minimum(glin, 0.0) - jnp.log(1.0 + jnp.exp(-jnp.abs(glin)))) / B_GATE_TAU

    g_rows = GLA_PREFIX_GROUP * c
    ri = lax.broadcasted_iota(jnp.int32, (g_rows, 3 * g_rows), 0)
    ci = lax.broadcasted_iota(jnp.int32, (g_rows, 3 * g_rows), 1)
    ci = ci - jnp.where(ci >= g_rows, g_rows, 0) - jnp.where(ci >= 2 * g_rows, g_rows, 0)
    shift = int(math.log2(c))
    same_chunk = lax.shift_right_logical(ri, shift) == lax.shift_right_logical(ci, shift)
    tri3 = ((ri >= ci) & same_chunk).astype(BF16)
    b = jnp.concatenate(
        [jnp.dot(tri3, _split3(log_a[r0:r0 + g_rows]), preferred_element_type=F32)
         for r0 in range(0, n_rows, g_rows)], axis=0)
    last_rows = [b[(j + 1) * c - 1:(j + 1) * c, :] for j in range(n_chunks)]
    b_last = jnp.concatenate([jnp.broadcast_to(r, (c, b.shape[1])) for r in last_rows], axis=0)

    qf = q_ref[...].astype(F32) * (B_KEY_DIM ** -0.5)
    kf = k_ref[...].astype(F32)
    q_dec = (qf * jnp.exp(b)).astype(BF16)
    k_dec = (kf * jnp.exp(-b)).astype(BF16)
    k_state = kf * jnp.exp(b_last - b)
    sublanes = 8
    pad = [jnp.zeros((sublanes - n_chunks, b.shape[1]), F32)] if n_chunks < sublanes else []
    decay_rows = jnp.exp(jnp.concatenate(last_rows + pad, axis=0))

    ri = lax.broadcasted_iota(jnp.int32, (c, c), 0)
    ci = lax.broadcasted_iota(jnp.int32, (c, c), 1)
    causal = ri >= ci
    att = {}
    for j, h in units:
        a = lax.dot_general(q_dec[chunk_rows[j], key_cols[h]], k_dec[chunk_rows[j], key_cols[h]],
                            (((1,), (1,)), ((), ())), preferred_element_type=F32)
        att[j, h] = jnp.where(causal, a, 0.0).astype(BF16)
    o_intra = {u: jnp.dot(att[u], v_of(v_refs, *u), preferred_element_type=F32) for u in units}
    ds = {(j, h): jnp.dot(k_state[chunk_rows[j], key_cols[h]].T.astype(BF16), v_of(v_refs, j, h),
                          preferred_element_type=F32) for j, h in units}

    entering = {}
    for h in range(B_HEADS):
        decay_t = decay_rows[:, key_cols[h]].T
        state = s_ref[h]
        for j in range(n_chunks):
            entering[j, h] = state.astype(BF16)
            state = decay_t[:, j:j + 1] * state + ds[j, h]
        s_ref[h] = state
    o_inter = {(j, h): jnp.dot(q_dec[chunk_rows[j], key_cols[h]], entering[j, h],
                               preferred_element_type=F32) for j, h in units}

    ng = ng_ref[...]
    for h in range(B_HEADS):
        o = jnp.concatenate([o_intra[j, h] + o_inter[j, h] for j in range(n_chunks)], axis=0)
        gate = v_of(og_refs, None, h).astype(F32)
        y = _rmsnorm_rows(o, ng) * (gate * jax.nn.sigmoid(gate))
        o_ref[:, h * B_VAL_DIM:(h + 1) * B_VAL_DIM] = y.astype(o_ref.dtype)


N_MIX_IN = 7


def _mixers_kernel(n_cast, *refs):
    mix_ref, prev_ref, sink_ref, rb_ref, wgu_ref, bg_ref, ng_ref = refs[:N_MIX_IN]
    cast_in, refs = refs[N_MIX_IN:N_MIX_IN + n_cast], refs[N_MIX_IN + n_cast:]
    attn_ref, gla_ref = refs[:2]
    cast_out, (bias_ref, s_ref) = refs[2:2 + n_cast], refs[2 + n_cast:]
    _cast_blocks(cast_in, cast_out)

    def cols(off, width):
        return mix_ref.at[:, off - OFF_QA:off - OFF_QA + width]

    kv_prev = [prev_ref.at[:, off - OFF_KA:off - OFF_KA + A_KV_WIDTH] for off in (OFF_KA, OFF_VA)]
    _swa_body(cols(OFF_QA, A_Q_WIDTH), kv_prev[0], cols(OFF_KA, A_KV_WIDTH),
              kv_prev[1], cols(OFF_VA, A_KV_WIDTH), sink_ref, rb_ref, attn_ref, bias_ref)
    _gla_body(cols(OFF_QB, B_QK_WIDTH), cols(OFF_KB, B_QK_WIDTH),
              cols(OFF_VB, HALF_V), cols(OFF_VB + HALF_V, HALF_V), cols(OFF_GLOW, LANES),
              cols(OFF_OBG, HALF_V), cols(OFF_OBG + HALF_V, HALF_V),
              wgu_ref, bg_ref, ng_ref, gla_ref, s_ref)


def _mixers(proj, sinks, rel_bias, wgu, bg, ng, cast_weights, batch, seq):
    t = proj.shape[0]
    rows = GLA_CHUNKS_PER_STEP * B_CHUNK
    assert rows == SWA_BLOCKS_PER_STEP * A_BLOCK and seq % rows == 0
    steps = seq // rows
    rb = lambda b, s: b * steps + s
    prev = lambda b, s: jnp.maximum(rb(b, s) * SWA_BLOCKS_PER_STEP - 1, 0)
    smem = functools.partial(pl.BlockSpec, memory_space=pltpu.SMEM)
    cast_in, cast_out, cast_shapes = _cast_specs(cast_weights, batch * steps, rb)
    kv_width = OFF_QB - OFF_KA
    assert OFF_KA % kv_width == 0 and OFF_VA == OFF_KA + A_KV_WIDTH
    mix_specs = [
        pl.BlockSpec((pl.Element(rows), pl.Element(PROJ_USED - OFF_QA)),
                     lambda b, s: (rb(b, s) * rows, OFF_QA)),
        pl.BlockSpec((A_BLOCK, kv_width), lambda b, s: (prev(b, s), OFF_KA // kv_width)),
        smem(), smem(), _const_spec(wgu.shape), _const_spec(bg.shape), _const_spec(ng.shape)]
    assert len(mix_specs) == N_MIX_IN
    out_block = lambda width: pl.BlockSpec((rows, width), lambda b, s: (rb(b, s), 0))
    outs = pl.pallas_call(
        functools.partial(_mixers_kernel, len(cast_weights)),
        grid=(batch, steps),
        in_specs=mix_specs + cast_in,
        out_specs=[out_block(A_Q_WIDTH), out_block(B_V_WIDTH)] + cast_out,
        out_shape=[jax.ShapeDtypeStruct((t, A_Q_WIDTH), BF16),
                   jax.ShapeDtypeStruct((t, B_V_WIDTH), BF16)] + cast_shapes,
        scratch_shapes=[pltpu.VMEM((2, A_HEADS, A_BLOCK, 2 * A_BLOCK), F32),
                        pltpu.VMEM((B_HEADS, B_KEY_DIM, B_VAL_DIM), F32)],
        compiler_params=_cparams(("arbitrary", "arbitrary")),
        name="mixers",
    )(proj, proj, sinks, rel_bias.T, wgu, bg, ng, *cast_weights)
    return outs[0], outs[1], outs[2:]


def _merge_kernel(a_ref, b_ref, gates_ref, x_ref, wa_ref, wb_ref, wo_ref, gz_ref, h_ref, z_ref):
    d = x_ref.shape[1]
    ya = jnp.dot(a_ref[...], wa_ref[...], preferred_element_type=F32)
    yb = jnp.dot(b_ref[...], wb_ref[...], preferred_element_type=F32)
    merged = (jax.nn.sigmoid(gates_ref[:, :d].astype(F32)) * ya
              + jax.nn.sigmoid(gates_ref[:, d:].astype(F32)) * yb)
    h = x_ref[...] + jnp.dot(merged.astype(BF16), wo_ref[...], preferred_element_type=F32)
    h_ref[...] = h
    z_ref[...] = _rmsnorm_rows(h, gz_ref[...]).astype(z_ref.dtype)


def _merge(attn, gla, proj, x2, wa, wb, wo, gz, tm):
    t, d = x2.shape
    row_block = pl.BlockSpec((tm, d), lambda i: (i, 0))
    return pl.pallas_call(
        _merge_kernel,
        grid=(t // tm,),
        in_specs=[
            pl.BlockSpec((tm, A_Q_WIDTH), lambda i: (i, 0)),
            pl.BlockSpec((tm, B_V_WIDTH), lambda i: (i, 0)),
            pl.BlockSpec((tm, 2 * d), lambda i: (i, OFF_GATE_A // (2 * d))),
            row_block,
            _const_spec(wa.shape), _const_spec(wb.shape), _const_spec(wo.shape), _const_spec(gz.shape),
        ],
        out_specs=[row_block, row_block],
        out_shape=[jax.ShapeDtypeStruct((t, d), F32), jax.ShapeDtypeStruct((t, d), BF16)],
        compiler_params=_cparams(("parallel",)),
        name="merge",
    )(attn, gla, proj, x2, wa, wb, wo, gz)


def _ffn_kernel(z_ref, h_ref, wg_ref, wu_ref, wd_ref, gf_ref, o_ref):
    f = pl.program_id(1)
    z = z_ref[...]
    half = wg_ref.shape[1] // 2
    halves = [slice(c * half, (c + 1) * half) for c in range(2)]
    gu = [(jnp.dot(z, wg_ref[:, cols], preferred_element_type=F32),
           jnp.dot(z, wu_ref[:, cols], preferred_element_type=F32)) for cols in halves]
    for c, cols in enumerate(halves):
        g, u = gu[c]
        act = (g * jax.nn.sigmoid(g) * u).astype(BF16)
        acc = jnp.where(f == 0, h_ref[...], o_ref[...]) if c == 0 else o_ref[...]
        o_ref[...] = acc + jnp.dot(act, wd_ref[cols, :], preferred_element_type=F32)

    @pl.when(f == pl.num_programs(1) - 1)
    def _():
        gf = gf_ref[...]

        def body(c, carry):
            rows = pl.ds(pl.multiple_of(c * OUT_NORM_ROWS, OUT_NORM_ROWS), OUT_NORM_ROWS)
            o_ref[rows, :] = _rmsnorm_rows(o_ref[rows, :], gf)
            return carry

        lax.fori_loop(0, h_ref.shape[0] // OUT_NORM_ROWS, body, 0)


def _ffn(z, h, wg, wu, wd, gf, tm, tf):
    t, d = h.shape
    f = wg.shape[1]
    row_block = pl.BlockSpec((tm, d), lambda i, j: (i, 0))
    return pl.pallas_call(
        _ffn_kernel,
        grid=(t // tm, f // tf),
        in_specs=[
            row_block, row_block,
            pl.BlockSpec((d, tf), lambda i, j: (0, j)),
            pl.BlockSpec((d, tf), lambda i, j: (0, j)),
            pl.BlockSpec((tf, d), lambda i, j: (j, 0)),
            pl.BlockSpec((1, d), lambda i, j: (0, 0)),
        ],
        out_specs=row_block,
        out_shape=jax.ShapeDtypeStruct((t, d), F32),
        compiler_params=_cparams(("parallel", "arbitrary")),
        name="ffn",
    )(z, h, wg, wu, wd, gf)


def kernel(x, norm_mix_g, w_in, sinks, rel_bias, w_gate_up, b_gate, gla_norm_g, w_proj_a, w_proj_b,
           w_out, norm_ffn_g, w_ffn_gate, w_ffn_up, w_ffn_down, norm_final_g):
    batch, seq, d = x.shape
    assert d == D_MODEL and w_in.shape[0] == 1, "single-layer geometry"
    t = batch * seq
    x2 = x.reshape(t, d)

    proj = _inproj(x2, norm_mix_g, w_in[0].T, tm=1024, tn=1792)

    later_weights = (w_proj_a[0], w_proj_b[0], w_out[0], w_ffn_gate[0], w_ffn_up[0], w_ffn_down[0])
    attn, gla, (wa, wb, wo, wg, wu, wd) = _mixers(proj, sinks, rel_bias, w_gate_up[0], b_gate, gla_norm_g,
                                                  later_weights, batch, seq)

    h, z = _merge(attn, gla, proj, x2, wa, wb, wo, norm_ffn_g, tm=512)

    out = _ffn(z, h, wg, wu, wd, norm_final_g.reshape(1, d), tm=1024, tf=512)
    return out.reshape(batch, seq, d)
```
